```python
import math
import jax
import jax.numpy as jnp
from jax import lax
import numpy as np

D_MODEL = 1024
BATCH = 8
SEQ = 4096
DEPTH = 1

MEM_LEN = 256
NORM_EPS = 1e-6
NEG_INF = -1e30
FORCE_SCORE = 1e9

SSD_D_INNER = 2 * D_MODEL
SSD_HEAD_DIM = 64
SSD_HEADS = SSD_D_INNER // SSD_HEAD_DIM
SSD_GROUPS = 8
SSD_STATE = 128
SSD_CONV = 4
SSD_CHUNK = 128
SSD_CONV_DIM = SSD_D_INNER + 2 * SSD_GROUPS * SSD_STATE

NSA_HEADS = 16
NSA_KV_HEADS = 4
NSA_HEAD_DIM = 64
NSA_REP = NSA_HEADS // NSA_KV_HEADS
NSA_WIDTH = NSA_HEADS * NSA_HEAD_DIM
NSA_KV_WIDTH = NSA_KV_HEADS * NSA_HEAD_DIM
CMP_BLOCK = 32
CMP_STRIDE = 16
CMP_HIDDEN = 256
SLC_BLOCK = 64
SLC_TOPK = 16
WINDOW = 512
NSA_Q_CHUNK = 32

REL_BUCKETS = 32
REL_MAX_DIST = 128

X_HEADS = 4
X_HEAD_DIM = 128
X_WIDTH = X_HEADS * X_HEAD_DIM

D_FF = 4 * D_MODEL

IN_PROJ_SIZES = (SSD_D_INNER, SSD_CONV_DIM, SSD_HEADS, NSA_WIDTH, 6 * NSA_KV_WIDTH, 3 * NSA_HEADS, 2 * D_MODEL)
IN_PROJ_DIM = SSD_D_INNER + SSD_CONV_DIM + SSD_HEADS + NSA_WIDTH + 6 * NSA_KV_WIDTH + 3 * NSA_HEADS + 2 * D_MODEL

kernel_name = "hybrid_ssd_nsa_gated_layer"


def rms_norm(x, g):
    xf = x.astype(jnp.float32)
    y = xf * lax.rsqrt(jnp.mean(xf * xf, axis=-1, keepdims=True) + NORM_EPS)
    return (y * g.astype(jnp.float32)).astype(x.dtype)


def t5_bucket(dist):
    n = jnp.maximum(dist, 0)
    max_exact = REL_BUCKETS // 2
    nf = jnp.maximum(n, 1).astype(jnp.float32)
    large = max_exact + (jnp.log(nf / max_exact) / math.log(REL_MAX_DIST / max_exact)
                         * (REL_BUCKETS - max_exact)).astype(jnp.int32)
    large = jnp.minimum(large, REL_BUCKETS - 1)
    return jnp.where(n < max_exact, n, large)


def causal_depthwise_conv(u, w, b):
    k = w.shape[0]
    out = lax.conv_general_dilated(
        u, w[:, None, :].astype(u.dtype), window_strides=(1,), padding=[(k - 1, 0)],
        dimension_numbers=("NWC", "WIO", "NWC"), feature_group_count=u.shape[-1])
    return out + b.astype(u.dtype)


def _segsum_from_cumsum(cs):
    t = cs.shape[-1]
    causal = jnp.tril(jnp.ones((t, t), dtype=bool))
    return jnp.where(causal, cs[..., :, None] - cs[..., None, :], -jnp.inf)


def ssd_chunked_scan(xh, dt, a_neg, bm, cm):
    b, s, h, p = xh.shape
    g, n = bm.shape[2], bm.shape[3]
    r = h // g
    nc, l = s // SSD_CHUNK, SSD_CHUNK
    xd = (xh * dt[..., None]).reshape(b, nc, l, g, r, p)
    a = (dt * a_neg).reshape(b, nc, l, g, r).transpose(0, 3, 4, 1, 2)
    bc = bm.reshape(b, nc, l, g, n)
    cc = cm.reshape(b, nc, l, g, n)
    a_cs = jnp.cumsum(a, axis=-1)
    decay_in = jnp.exp(_segsum_from_cumsum(a_cs))
    cb = jnp.einsum("bclgn,bcsgn->bgcls", cc, bc)
    y_diag = jnp.einsum("bgrcls,bcsgrp->bclgrp", cb[:, :, None] * decay_in, xd)
    decay_to_end = jnp.exp(a_cs[..., -1:] - a_cs).transpose(0, 3, 4, 1, 2)
    states = jnp.einsum("bclgn,bclgrp->bcgrpn", bc, xd * decay_to_end[..., None])
    chunk_cs = jnp.cumsum(jnp.pad(a_cs[..., -1], ((0, 0), (0, 0), (0, 0), (1, 0))), axis=-1)
    decay_chunk = jnp.exp(_segsum_from_cumsum(chunk_cs))
    states = jnp.concatenate([jnp.zeros_like(states[:, :1]), states], axis=1)
    carried = jnp.einsum("bgrzc,bcgrpn->bzgrpn", decay_chunk, states)[:, :-1]
    decay_out = jnp.exp(a_cs).transpose(0, 3, 4, 1, 2)
    y_off = jnp.einsum("bclgn,bcgrpn->bclgrp", cc, carried) * decay_out[..., None]
    return (y_diag + y_off).reshape(b, s, h, p)


def ssd_mixer(z, xbc, dt_raw, conv_w, conv_b, dt_bias, a_log, d_skip, norm_w):
    f32 = jnp.float32
    b, s, _ = z.shape
    xbc = jax.nn.silu(causal_depthwise_conv(xbc, conv_w, conv_b))
    xs, bm, cm = jnp.split(xbc, [SSD_D_INNER, SSD_D_INNER + SSD_GROUPS * SSD_STATE], axis=-1)
    dt = jax.nn.softplus(dt_raw.astype(f32) + dt_bias.astype(f32))
    a_neg = -jnp.exp(a_log.astype(f32))
    xh = xs.astype(f32).reshape(b, s, SSD_HEADS, SSD_HEAD_DIM)
    y = ssd_chunked_scan(xh, dt, a_neg,
                         bm.astype(f32).reshape(b, s, SSD_GROUPS, SSD_STATE),
                         cm.astype(f32).reshape(b, s, SSD_GROUPS, SSD_STATE))
    y = y + xh * d_skip.astype(f32)[:, None]
    y = y.reshape(b, s, SSD_D_INNER) * jax.nn.silu(z.astype(f32))
    yg = y.reshape(b, s, SSD_GROUPS, SSD_D_INNER // SSD_GROUPS)
    yg = yg * lax.rsqrt(jnp.mean(yg * yg, axis=-1, keepdims=True) + NORM_EPS)
    return (yg.reshape(b, s, SSD_D_INNER) * norm_w.astype(f32)).astype(z.dtype)


def nsa_mixer(q, kv, gate_logits, cmp_pos, cmp_w1, cmp_w2, rel_bias):
    f32 = jnp.float32
    b, s, _ = q.shape
    G, R, hd = NSA_KV_HEADS, NSA_REP, NSA_HEAD_DIM
    scale = hd ** -0.5
    qh = q.astype(f32).reshape(b, s, G, R, hd)
    kv = kv.astype(f32).reshape(b, s, 6, G, hd)
    k_cmp, v_cmp, k_slc, v_slc, k_win, v_win = [kv[:, :, i] for i in range(6)]
    table = rel_bias.astype(f32).reshape(REL_BUCKETS, G, R)

    n_cmp = (s - CMP_BLOCK) // CMP_STRIDE + 1
    blk_idx = CMP_STRIDE * jnp.arange(n_cmp)[:, None] + jnp.arange(CMP_BLOCK)[None, :]

    def compress(u, pos, w1, w2):
        ub = u[:, blk_idx] + pos.astype(f32)[:, None, :]
        ub = ub.transpose(0, 1, 3, 2, 4).reshape(b, n_cmp, G, CMP_BLOCK * hd)
        return jax.nn.silu(ub @ w1.astype(f32)) @ w2.astype(f32)

    kc = compress(k_cmp, cmp_pos[0], cmp_w1[0], cmp_w2[0])
    vc = compress(v_cmp, cmp_pos[1], cmp_w1[1], cmp_w2[1])
    cmp_start = CMP_STRIDE * jnp.arange(n_cmp)
    cmp_end = cmp_start + CMP_BLOCK - 1

    n_slc = s // SLC_BLOCK
    top_k = min(SLC_TOPK, n_slc)
    slc_start = SLC_BLOCK * jnp.arange(n_slc)
    overlap = ((cmp_start[:, None] <= slc_start[None, :] + SLC_BLOCK - 1)
               & (cmp_end[:, None] >= slc_start[None, :])).astype(f32)
    ks_blocks = k_slc.reshape(b, n_slc, SLC_BLOCK, G, hd).transpose(0, 3, 1, 2, 4)
    vs_blocks = v_slc.reshape(b, n_slc, SLC_BLOCK, G, hd).transpose(0, 3, 1, 2, 4)
    gather_blocks = jax.vmap(jax.vmap(lambda kb, ix: kb[ix]))
    bias_by_group = jax.vmap(lambda tbl, bk: tbl[bk], in_axes=(1, 1), out_axes=1)

    kw_pad = jnp.pad(k_win, ((0, 0), (WINDOW, 0), (0, 0), (0, 0)))
    vw_pad = jnp.pad(v_win, ((0, 0), (WINDOW, 0), (0, 0), (0, 0)))
    jb = jnp.arange(n_slc)

    def chunk(ci):
        t0 = ci * NSA_Q_CHUNK
        tq = t0 + jnp.arange(NSA_Q_CHUNK)
        qc = lax.dynamic_slice_in_dim(qh, t0, NSA_Q_CHUNK, axis=1) * scale

        lg = jnp.einsum("btgrd,bcgd->bgrtc", qc, kc)
        lg = lg + table[t5_bucket(tq[:, None] - cmp_end[None, :])].transpose(2, 3, 0, 1)
        valid = cmp_end[None, :] <= tq[:, None]
        p_cmp = jax.nn.softmax(jnp.where(valid, lg, NEG_INF), axis=-1) * valid
        o_cmp = jnp.einsum("bgrtc,bcgd->btgrd", p_cmp, vc)

        imp = jnp.einsum("bgrtc,cn->bgtn", p_cmp, overlap)
        tb = tq // SLC_BLOCK
        forced = (jb[None, :] == 0) | (jb[None, :] == tb[:, None]) | (jb[None, :] == tb[:, None] - 1)
        imp = jnp.where(forced, FORCE_SCORE, imp)
        imp = jnp.where(jb[None, :] > tb[:, None], NEG_INF, imp)
        sc, idx = lax.top_k(imp, top_k)
        sel_ok = sc > 0.5 * NEG_INF
        kg = gather_blocks(ks_blocks, idx)
        vg = gather_blocks(vs_blocks, idx)
        kpos = idx[..., None] * SLC_BLOCK + jnp.arange(SLC_BLOCK)
        dist = tq[None, None, :, None, None] - kpos
        m = sel_ok[..., None] & (dist >= 0)
        lg = jnp.einsum("btgrd,bgtkld->bgrtkl", qc, kg)
        lg = lg + jnp.moveaxis(bias_by_group(table, t5_bucket(dist)), -1, 2)
        lg = jnp.where(m[:, :, None], lg, NEG_INF)
        lg = lg.reshape(b, G, R, NSA_Q_CHUNK, top_k * SLC_BLOCK)
        p = jax.nn.softmax(lg, axis=-1).reshape(b, G, R, NSA_Q_CHUNK, top_k, SLC_BLOCK)
        o_slc = jnp.einsum("bgrtkl,bgtkld->btgrd", p, vg)

        kw = lax.dynamic_slice_in_dim(kw_pad, t0, WINDOW + NSA_Q_CHUNK, axis=1)
        vw = lax.dynamic_slice_in_dim(vw_pad, t0, WINDOW + NSA_Q_CHUNK, axis=1)
        kpos_w = t0 - WINDOW + jnp.arange(WINDOW + NSA_Q_CHUNK)
        dist_w = tq[:, None] - kpos_w[None, :]
        m_w = (kpos_w[None, :] >= 0) & (dist_w >= 0) & (dist_w < WINDOW)
        lg = jnp.einsum("btgrd,bkgd->bgrtk", qc, kw)
        lg = lg + table[t5_bucket(dist_w)].transpose(2, 3, 0, 1)
        p = jax.nn.softmax(jnp.where(m_w, lg, NEG_INF), axis=-1)
        o_win = jnp.einsum("bgrtk,bkgd->btgrd", p, vw)
        return o_cmp, o_slc, o_win

    o_cmp, o_slc, o_win = lax.map(chunk, jnp.arange(s // NSA_Q_CHUNK))

    def unchunk(o):
        return jnp.moveaxis(o, 0, 1).reshape(b, s, G, R, hd)

    gates = jax.nn.sigmoid(gate_logits.astype(f32)).reshape(b, s, 3, G, R)[..., None]
    o = gates[:, :, 0] * unchunk(o_cmp) + gates[:, :, 1] * unchunk(o_slc) + gates[:, :, 2] * unchunk(o_win)
    return o.reshape(b, s, NSA_WIDTH).astype(q.dtype)


def memory_cross_attention(h, m, w_q, w_kv, w_o):
    b, s, _ = h.shape
    ml = m.shape[1]
    q = (h @ w_q).reshape(b, s, X_HEADS, X_HEAD_DIM).astype(jnp.float32)
    kv = (m @ w_kv).reshape(b, ml, 2, X_HEADS, X_HEAD_DIM).astype(jnp.float32)
    lg = jnp.einsum("bshd,bmhd->bhsm", q, kv[:, :, 0]) * (X_HEAD_DIM ** -0.5)
    p = jax.nn.softmax(lg, axis=-1)
    o = jnp.einsum("bhsm,bmhd->bshd", p, kv[:, :, 1]).reshape(b, s, X_WIDTH).astype(h.dtype)
    return o @ w_o


def setup_inputs(seed: int = 0) -> dict:
    key = jax.random.key(seed)
    ks = jax.random.split(key, 32)
    f32 = jnp.float32

    def nrm(k, shape, scale):
        return jax.random.normal(k, shape, f32) * scale

    def gain(k, width=D_MODEL):
        return 1.0 + 0.05 * jax.random.normal(k, (DEPTH, width), f32)

    dt = jnp.exp(jax.random.uniform(ks[5], (DEPTH, SSD_HEADS), f32, math.log(1e-3), math.log(1e-1)))
    return {
        "x": nrm(ks[0], (BATCH, SEQ, D_MODEL), 1.0),
        "mem": nrm(ks[1], (BATCH, MEM_LEN, D_MODEL), 1.0),
        "w_in": nrm(ks[2], (DEPTH, D_MODEL, IN_PROJ_DIM), D_MODEL ** -0.5),
        "ssd_conv_w": nrm(ks[3], (DEPTH, SSD_CONV, SSD_CONV_DIM), SSD_CONV ** -0.5),
        "ssd_conv_b": nrm(ks[4], (DEPTH, SSD_CONV_DIM), 0.01),
        "ssd_dt_bias": dt + jnp.log(-jnp.expm1(-dt)),
        "ssd_a_log": jnp.log(jax.random.uniform(ks[6], (DEPTH, SSD_HEADS), f32, 1.0, 16.0)),
        "ssd_d_skip": 1.0 + 0.1 * jax.random.normal(ks[7], (DEPTH, SSD_HEADS), f32),
        "ssd_norm": gain(ks[8], SSD_D_INNER),
        "cmp_pos": nrm(ks[9], (DEPTH, 2, CMP_BLOCK, NSA_HEAD_DIM), 0.1),
        "cmp_w1": nrm(ks[10], (DEPTH, 2, CMP_BLOCK * NSA_HEAD_DIM, CMP_HIDDEN), (CMP_BLOCK * NSA_HEAD_DIM) ** -0.5),
        "cmp_w2": nrm(ks[11], (DEPTH, 2, CMP_HIDDEN, NSA_HEAD_DIM), CMP_HIDDEN ** -0.5),
        "rel_bias": nrm(ks[12], (REL_BUCKETS, NSA_HEADS), 0.5),
        "w_br_ssd": nrm(ks[13], (DEPTH, SSD_D_INNER, D_MODEL), SSD_D_INNER ** -0.5),
        "w_br_nsa": nrm(ks[14], (DEPTH, NSA_WIDTH, D_MODEL), NSA_WIDTH ** -0.5),
        "w_out": nrm(ks[15], (DEPTH, D_MODEL, D_MODEL), D_MODEL ** -0.5),
        "w_xq": nrm(ks[16], (DEPTH, D_MODEL, X_WIDTH), D_MODEL ** -0.5),
        "w_xkv": nrm(ks[17], (DEPTH, D_MODEL, 2 * X_WIDTH), D_MODEL ** -0.5),
        "w_xo": nrm(ks[18], (DEPTH, X_WIDTH, D_MODEL), X_WIDTH ** -0.5),
        "w_ff1": nrm(ks[19], (DEPTH, D_MODEL, D_FF), D_MODEL ** -0.5),
        "w_ff2": nrm(ks[20], (DEPTH, D_FF, D_MODEL), D_FF ** -0.5),
        "norm_mix_pre": gain(ks[21]),
        "norm_mix_post": gain(ks[22]),
        "norm_x_pre": gain(ks[23]),
        "norm_x_post": gain(ks[24]),
        "norm_mem": gain(ks[25]),
        "norm_ffn_pre": gain(ks[26]),
        "norm_ffn_post": gain(ks[27]),
    }


def reference(x, mem, w_in, ssd_conv_w, ssd_conv_b, ssd_dt_bias, ssd_a_log, ssd_d_skip, ssd_norm,
              cmp_pos, cmp_w1, cmp_w2, rel_bias, w_br_ssd, w_br_nsa, w_out, w_xq, w_xkv, w_xo,
              w_ff1, w_ff2, norm_mix_pre, norm_mix_post, norm_x_pre, norm_x_post, norm_mem,
              norm_ffn_pre, norm_ffn_post):
    split_points = np.cumsum(IN_PROJ_SIZES)[:-1].tolist()
    for l in range(DEPTH):
        h = rms_norm(x, norm_mix_pre[l])
        proj = h @ w_in[l]
        z, xbc, dt_raw, q, kv, nsa_gate, merge_gate = jnp.split(proj, split_points, axis=-1)
        y_ssd = ssd_mixer(z, xbc, dt_raw, ssd_conv_w[l], ssd_conv_b[l], ssd_dt_bias[l],
                          ssd_a_log[l], ssd_d_skip[l], ssd_norm[l])
        y_nsa = nsa_mixer(q, kv, nsa_gate, cmp_pos[l], cmp_w1[l], cmp_w2[l], rel_bias)
        g_ssd, g_nsa = jnp.split(jax.nn.sigmoid(merge_gate), 2, axis=-1)
        mixed = g_ssd * (y_ssd @ w_br_ssd[l]) + g_nsa * (y_nsa @ w_br_nsa[l])
        x = x + rms_norm(mixed @ w_out[l], norm_mix_post[l])
        h = rms_norm(x, norm_x_pre[l])
        m = rms_norm(mem, norm_mem[l])
        x = x + rms_norm(memory_cross_attention(h, m, w_xq[l], w_xkv[l], w_xo[l]), norm_x_post[l])
        h = rms_norm(x, norm_ffn_pre[l])
        ff = jnp.square(jax.nn.relu(h @ w_ff1[l])) @ w_ff2[l]
        x = x + rms_norm(ff, norm_ffn_post[l])
    return x
```

```python
import functools
import math

import numpy as np
import jax
import jax.numpy as jnp
from jax import lax
from jax.experimental import pallas as pl
from jax.experimental.pallas import tpu as pltpu

F32 = jnp.float32
BF16 = jnp.bfloat16

NORM_EPS = 1e-6
NEG_INF = -1e30
FORCE_SCORE = 1e9
SEL_PENALTY = -1e9

LANES = 128
VMEM_LIMIT = 56 * 1024 * 1024

SSD_HEAD_DIM = 64
SSD_GROUPS = 8
SSD_STATE = 128
SSD_CONV = 4
SSD_CHUNK = 128
NSA_HEADS = 16
NSA_KV_HEADS = 4
NSA_HEAD_DIM = 64
NSA_REP = NSA_HEADS // NSA_KV_HEADS
CMP_BLOCK = 32
CMP_STRIDE = 16
SLC_BLOCK = 64
SLC_TOPK = 16
WINDOW = 512
NSA_TQ = 128
REL_BUCKETS = 32
REL_MAX_DIST = 128
X_HEADS = 4
X_HEAD_DIM = 128


def _cparams(n_grid):
    return pltpu.CompilerParams(dimension_semantics=("arbitrary",) * n_grid,
                                vmem_limit_bytes=VMEM_LIMIT)


def _sigmoid(x):
    return 1.0 / (1.0 + jnp.exp(-x))


def _rms(x, g):
    return x * lax.rsqrt(jnp.mean(x * x, axis=-1, keepdims=True) + NORM_EPS) * g


def _dot(a, b):
    return jnp.dot(a, b, preferred_element_type=F32)


def _dot_nt(a, b):
    return lax.dot_general(a, b, (((1,), (1,)), ((), ())), preferred_element_type=F32)


def _split3(x):
    x1 = x.astype(BF16)
    r1 = x - x1.astype(F32)
    x2 = r1.astype(BF16)
    x3 = (r1 - x2.astype(F32)).astype(BF16)
    return x1, x2, x3


def _norm_matmul_kernel(x_ref, g_ref, *refs, n_out, n_chunk):
    w_refs, o_refs = refs[:n_out], refs[n_out:]
    h = _rms(x_ref[...], g_ref[...]).astype(BF16)
    for w_ref, o_ref in zip(w_refs, o_refs):
        n = w_ref.shape[1]
        step = min(n, n_chunk)
        for n0 in range(0, n, step):
            o_ref[:, n0:n0 + step] = _dot(h, w_ref[:, n0:n0 + step]).astype(o_ref.dtype)


def _norm_matmul(x2d, g, ws, out_dtypes, tm=512):
    m, k = x2d.shape
    tm = min(tm, m)
    in_specs = [pl.BlockSpec((tm, k), lambda i: (i, 0)), pl.BlockSpec((1, k), lambda i: (0, 0))]
    in_specs += [pl.BlockSpec(w.shape, lambda i: (0, 0)) for w in ws]
    out_specs = [pl.BlockSpec((tm, w.shape[1]), lambda i: (i, 0)) for w in ws]
    out_shape = [jax.ShapeDtypeStruct((m, w.shape[1]), dt) for w, dt in zip(ws, out_dtypes)]
    return pl.pallas_call(
        functools.partial(_norm_matmul_kernel, n_out=len(ws), n_chunk=1024),
        grid=(m // tm,), in_specs=in_specs, out_specs=out_specs, out_shape=out_shape,
        compiler_params=_cparams(1), name="norm_proj",
    )(x2d, g.reshape(1, k), *ws)


def _softplus(x):
    return jnp.maximum(x, 0.0) + jnp.log(1.0 + jnp.exp(-jnp.abs(x)))


def _ssd_kernel(xbc_ref, prev_ref, z_ref, dt_ref, dtT_ref, cw_ref, cb_ref, dtb_ref, dtbT_ref,
                alog_ref, alogT_ref, dskip_ref, nw_ref, y_ref, state_ref, xc_ref, ybuf_ref,
                *, n_heads, d_inner):
    L, P, N, G = SSD_CHUNK, SSD_HEAD_DIM, SSD_STATE, SSD_GROUPS
    conv_dim = xc_ref.shape[1]
    c = pl.program_id(1)

    @pl.when(c == 0)
    def _():
        state_ref[...] = jnp.zeros_like(state_ref)

    cw = 512
    prev_rows = prev_ref.shape[0]
    row = lax.broadcasted_iota(jnp.int32, (prev_rows, cw), 0)
    for j in range(conv_dim // cw):
        sl = slice(j * cw, (j + 1) * cw)
        cur = xbc_ref[:, sl].astype(F32)
        prev = jnp.where(c == 0, 0.0, prev_ref[:, sl].astype(F32))
        acc = cb_ref[:, sl] + cw_ref[SSD_CONV - 1:SSD_CONV, sl] * cur
        for k in range(1, SSD_CONV):
            rc = pltpu.roll(cur, k, 0)
            rp = pltpu.roll(prev, k, 0)
            head = jnp.where(row < k, rp, rc[:prev_rows])
            shifted = jnp.concatenate([head, rc[prev_rows:]], axis=0)
            acc = acc + cw_ref[SSD_CONV - 1 - k:SSD_CONV - k, sl] * shifted
        xc_ref[:, sl] = acc * _sigmoid(acc)

    dt = _softplus(dt_ref[:, 0:n_heads] + dtb_ref[...])
    a = dt * (-jnp.exp(alog_ref[...]))
    dtT = _softplus(dtT_ref[...] + dtbT_ref[...])
    aT = dtT * (-jnp.exp(alogT_ref[...]))
    ri = lax.broadcasted_iota(jnp.int32, (L, L), 0)
    ci = lax.broadcasted_iota(jnp.int32, (L, L), 1)
    causal = ci <= ri
    tri = jnp.where(causal, 1.0, 0.0).astype(BF16)
    triu = jnp.where(ri <= ci, 1.0, 0.0).astype(BF16)
    a_cs = sum(_dot(tri, ai) for ai in _split3(a))
    a_csT = sum(_dot(ai, triu) for ai in _split3(aT))
    lane = lax.broadcasted_iota(jnp.int32, (L, 2 * P), 1)
    lo_half = lane < P
    lane1 = lax.broadcasted_iota(jnp.int32, (1, 2 * P), 1) < P

    heads_per_group = n_heads // G
    for g in range(G):
        b_g = xc_ref[:, d_inner + g * N:d_inner + (g + 1) * N]
        c_g = xc_ref[:, d_inner + G * N + g * N:d_inner + G * N + (g + 1) * N]
        c_gb = c_g.astype(BF16)
        cb = _dot_nt(c_gb, b_g.astype(BF16))
        b_gT = b_g.T.astype(BF16)
        for pp in range(heads_per_group // 2):
            pair = g * (heads_per_group // 2) + pp
            h0, h1 = 2 * pair, 2 * pair + 1
            xs = xc_ref[:, pair * 2 * P:(pair + 1) * 2 * P]
            dt_pair = jnp.where(lo_half, dt[:, h0:h0 + 1], dt[:, h1:h1 + 1])
            xd = xs * dt_pair
            xd_b = xd.astype(BF16)
            ys = []
            for h in (h0, h1):
                seg = jnp.where(causal, a_cs[:, h:h + 1] - a_csT[h:h + 1, :], NEG_INF)
                mat = (cb * jnp.exp(seg)).astype(BF16)
                ys.append(_dot(mat, xd_b))
            y = jnp.where(lo_half, ys[0], ys[1])
            cs_pair = jnp.where(lo_half, a_cs[:, h0:h0 + 1], a_cs[:, h1:h1 + 1])
            st = state_ref[pair]
            y = y + _dot(c_gb, st.astype(BF16)) * jnp.exp(cs_pair)
            tot = jnp.where(lane1, a_csT[h0:h0 + 1, L - 1:L], a_csT[h1:h1 + 1, L - 1:L])
            xdd = (xd * jnp.exp(tot - cs_pair)).astype(BF16)
            state_ref[pair] = st * jnp.exp(tot) + _dot(b_gT, xdd)
            ybuf_ref[:, pair * 2 * P:(pair + 1) * 2 * P] = y + xs * dskip_ref[:, pair * 2 * P:(pair + 1) * 2 * P]

    gw = d_inner // G
    for g in range(G):
        sl = slice(g * gw, (g + 1) * gw)
        zz = z_ref[:, sl].astype(F32)
        yg = ybuf_ref[:, sl] * (zz * _sigmoid(zz))
        y_ref[:, sl] = _rms(yg, nw_ref[:, sl]).astype(y_ref.dtype)


def _ssd_mixer(z, xbc, small, conv_w, conv_b, dt_bias, a_log, d_skip, norm_w):
    b, s, d_inner = z.shape
    conv_dim = xbc.shape[-1]
    n_heads = d_inner // SSD_HEAD_DIM
    L = SSD_CHUNK
    nc = s // L
    prev_rows = 16
    xbc_prev = xbc.reshape(b, s // prev_rows, prev_rows, conv_dim)
    dtT = jnp.swapaxes(small[:, :, :n_heads], 1, 2)
    blocks_per_chunk = L // prev_rows
    kern = functools.partial(_ssd_kernel, n_heads=n_heads, d_inner=d_inner)
    const = lambda shape: pl.BlockSpec(shape, lambda i, j: (0,) * len(shape))
    return pl.pallas_call(
        kern, grid=(b, nc),
        in_specs=[
            pl.BlockSpec((None, L, conv_dim), lambda i, j: (i, j, 0)),
            pl.BlockSpec((None, None, prev_rows, conv_dim),
                         lambda i, j: (i, jnp.maximum(j * blocks_per_chunk - 1, 0), 0, 0)),
            pl.BlockSpec((None, L, d_inner), lambda i, j: (i, j, 0)),
            pl.BlockSpec((None, L, small.shape[-1]), lambda i, j: (i, j, 0)),
            pl.BlockSpec((None, n_heads, L), lambda i, j: (i, 0, j)),
            const((SSD_CONV, conv_dim)), const((1, conv_dim)),
            const((1, n_heads)), const((n_heads, 1)),
            const((1, n_heads)), const((n_heads, 1)),
            const((1, d_inner)), const((1, d_inner)),
        ],
        out_specs=pl.BlockSpec((None, L, d_inner), lambda i, j: (i, j, 0)),
        out_shape=jax.ShapeDtypeStruct((b, s, d_inner), BF16),
        scratch_shapes=[pltpu.VMEM((n_heads // 2, SSD_STATE, 2 * SSD_HEAD_DIM), F32),
                        pltpu.VMEM((L, conv_dim), F32),
                        pltpu.VMEM((L, d_inner), F32)],
        compiler_params=_cparams(2), name="ssd",
    )(xbc, xbc_prev, z, small, dtT, conv_w.astype(F32), conv_b.reshape(1, -1).astype(F32),
      dt_bias.reshape(1, -1), dt_bias.reshape(-1, 1), a_log.reshape(1, -1), a_log.reshape(-1, 1),
      jnp.repeat(d_skip, SSD_HEAD_DIM).reshape(1, -1), norm_w.reshape(1, -1))


def _compress_kernel(kr_ref, pos_ref, w1_ref, w2_ref, o_ref, *, front):
    nr, half = kr_ref.shape
    kr = kr_ref[...]
    p1 = _dot(kr, w1_ref[0:half, :])
    p2 = _dot(kr, w1_ref[half:2 * half, :])
    pb = _dot(pos_ref[...], w1_ref[...])[0:1]
    hid = p1 + pltpu.roll(p2, nr - 1, 0) + pb
    hid = hid * _sigmoid(hid)
    out = _dot(hid.astype(BF16), w2_ref[...])
    rows = lax.broadcasted_iota(jnp.int32, out.shape, 0)
    out = jnp.where(rows < nr - 1, out, 0.0)
    o_ref[0:front, :] = jnp.zeros((front, out.shape[1]), o_ref.dtype)
    o_ref[front:front + nr, :] = out.astype(o_ref.dtype)


def _compress(kr, pos, w1, w2p):
    two, b, g, nr, half = kr.shape
    front = nr - NSA_TQ // CMP_STRIDE
    hidden = w1.shape[-1]
    return pl.pallas_call(
        functools.partial(_compress_kernel, front=front),
        grid=(two, b, g),
        in_specs=[
            pl.BlockSpec((None, None, None, nr, half), lambda t, i, j: (t, i, j, 0, 0)),
            pl.BlockSpec((None, 8, 2 * half), lambda t, i, j: (t, 0, 0)),
            pl.BlockSpec((None, 2 * half, hidden), lambda t, i, j: (t, 0, 0)),
            pl.BlockSpec((None, hidden, LANES), lambda t, i, j: (t, 0, 0)),
        ],
        out_specs=pl.BlockSpec((None, None, None, front + nr, LANES), lambda t, i, j: (t, i, j, 0, 0)),
        out_shape=jax.ShapeDtypeStruct((two, b, g, front + nr, LANES), F32),
        compiler_params=_cparams(3), name="nsa_compress",
    )(kr, pos, w1, w2p)


def _nsa_kernel(q_ref, kc_ref, vc_ref, ksl_ref, vsl_ref, kw_ref, vw_ref, cb_ref, tp_ref, gl_ref,
                o_ref, m_ref, l_ref, acc_ref, *, n_slc, top_k):
    T, R, HD = NSA_TQ, NSA_REP, NSA_HEAD_DIM
    nr = cb_ref.shape[1]
    front = nr - T // CMP_STRIDE
    qi = pl.program_id(2)

    q = jnp.concatenate([q_ref[:, LANES * r:LANES * (r + 1)] for r in range(R)], axis=0)

    off = pl.multiple_of(qi * (T // CMP_STRIDE), 8)
    kc = kc_ref[pl.ds(off, nr), :].astype(BF16)
    vc = vc_ref[pl.ds(off, nr), :].astype(BF16)
    lg = _dot_nt(q, kc) + cb_ref[...]
    jcol = lax.broadcasted_iota(jnp.int32, (R * T, nr), 1)
    lg = jnp.where(jcol >= front - qi * (T // CMP_STRIDE), lg, NEG_INF)
    valid = jnp.where(lg > 0.5 * NEG_INF, 1.0, 0.0)
    e = jnp.exp(lg - jnp.max(lg, axis=-1, keepdims=True))
    p = e / jnp.sum(e, axis=-1, keepdims=True) * valid
    o_cmp = _dot(p.astype(BF16), vc)

    psum = p[0:T]
    for r in range(1, R):
        psum = psum + p[r * T:(r + 1) * T]
    half = LANES // 2
    nrow = lax.broadcasted_iota(jnp.int32, (LANES, nr), 0) - half
    cabs = lax.broadcasted_iota(jnp.int32, (LANES, nr), 1) - front + qi * (T // CMP_STRIDE)
    lo = CMP_STRIDE * cabs
    ov = jnp.where(lo <= SLC_BLOCK * nrow + SLC_BLOCK - 1,
                   jnp.where(lo + CMP_BLOCK - 1 >= SLC_BLOCK * nrow,
                             jnp.where(nrow >= 0, 1.0, 0.0), 0.0), 0.0).astype(BF16)
    imp = sum(_dot_nt(ov, pi) for pi in _split3(psum))[half:]
    nio = lax.broadcasted_iota(jnp.int32, (half, T), 0)
    tio = lax.broadcasted_iota(jnp.int32, (half, T), 1)
    tb = (qi * T + tio) // SLC_BLOCK
    forced = (nio == 0) | (nio == tb) | (nio == tb - 1)
    imp = jnp.where(forced, FORCE_SCORE, imp)
    imp = jnp.where(nio > tb, NEG_INF, imp)
    rank = jnp.zeros((half, T), F32)
    for mm in range(n_slc):
        rowv = imp[mm:mm + 1, :]
        rank = rank + jnp.where(nio > mm, jnp.where(rowv >= imp, 1.0, 0.0), jnp.where(rowv > imp, 1.0, 0.0))
    pen_hi = jnp.where(rank < top_k, jnp.where(nio <= tb, 0.0, SEL_PENALTY), SEL_PENALTY)
    pen = jnp.concatenate([jnp.zeros((half, T), F32), pen_hi], axis=0).T
    pen_b = pen.astype(BF16)
    lane_hi = lax.broadcasted_iota(jnp.int32, (T, LANES), 1) >= half
    q_aug = jnp.concatenate([jnp.where(lane_hi, pen_b, q[r * T:(r + 1) * T]) for r in range(R)], axis=0)

    tmk = (lax.broadcasted_iota(jnp.int32, (R * T, T), 0) % T
           - lax.broadcasted_iota(jnp.int32, (R * T, T), 1))
    krow = lax.broadcasted_iota(jnp.int32, (T, LANES), 0)
    klane = lax.broadcasted_iota(jnp.int32, (T, LANES), 1)

    def attend(k_ref, v_ref, first, windowed, tag_blocks):
        m_ref[...] = jnp.full(m_ref.shape, NEG_INF, F32)
        l_ref[...] = jnp.zeros(l_ref.shape, F32)
        acc_ref[...] = jnp.zeros(acc_ref.shape, F32)

        def body(kj, carry):
            ko = pl.multiple_of(kj * T, T)
            kt = k_ref[pl.ds(ko, T), :]
            if tag_blocks:
                blk = kj * (T // SLC_BLOCK) + krow // SLC_BLOCK
                kt = jnp.where(klane >= half, jnp.where(klane - half == blk, 1.0, 0.0).astype(BF16), kt)
            vt = v_ref[pl.ds(ko, T), :]
            rel = qi - kj
            s = _dot_nt(q_aug, kt) + tp_ref[jnp.minimum(rel, 2)]
            ok = tmk >= -rel * T
            if windowed:
                ok = ok & (tmk < WINDOW - rel * T)
            s = jnp.where(ok, s, NEG_INF)
            m_old = m_ref[...]
            m_new = jnp.maximum(m_old, jnp.max(s, axis=-1, keepdims=True))
            alpha = jnp.exp(m_old - m_new)
            pe = jnp.exp(s - m_new)
            l_ref[...] = alpha * l_ref[...] + jnp.sum(pe, axis=-1, keepdims=True)
            acc_ref[...] = alpha * acc_ref[...] + _dot(pe.astype(BF16), vt)
            m_ref[...] = m_new
            return carry

        lax.fori_loop(first, qi + 1, body, 0)
        return acc_ref[...] / l_ref[...]

    o_slc = attend(ksl_ref, vsl_ref, 0, False, True)
    o_win = attend(kw_ref, vw_ref, jnp.maximum(qi - WINDOW // T, 0), True, False)

    sg = _sigmoid(gl_ref[...])
    for r in range(R):
        rows = slice(r * T, (r + 1) * T)
        o = (sg[:, r:r + 1] * o_cmp[rows] + sg[:, R + r:R + r + 1] * o_slc[rows]
             + sg[:, 2 * R + r:2 * R + r + 1] * o_win[rows])
        o_ref[:, LANES * r:LANES * (r + 1)] = o.astype(o_ref.dtype)


def _t5_bucket_np(dist):
    n = np.maximum(dist, 0)
    max_exact = REL_BUCKETS // 2
    nf = np.maximum(n, 1).astype(np.float32)
    large = max_exact + (np.log(nf / np.float32(max_exact)) / np.float32(math.log(REL_MAX_DIST / max_exact))
                         * np.float32(REL_BUCKETS - max_exact)).astype(np.int32)
    large = np.minimum(large, REL_BUCKETS - 1)
    return np.where(n < max_exact, n, large).astype(np.int32)


def _nsa_bias_tables(rel_bias, nr):
    T, G, R = NSA_TQ, NSA_KV_HEADS, NSA_REP
    table = rel_bias.astype(F32)
    front = nr - T // CMP_STRIDE
    t = np.arange(T)[:, None]
    cend = CMP_STRIDE * (np.arange(nr)[None, :] - front) + CMP_BLOCK - 1
    dist = t - cend
    cb = table[_t5_bucket_np(dist)]
    cb = jnp.where((dist >= 0)[..., None], cb, NEG_INF)
    cb = jnp.transpose(cb, (2, 0, 1)).reshape(G, R * T, nr)
    k = np.arange(T)[None, :]
    assert 2 * T - (T - 1) >= REL_MAX_DIST, "far tiles must sit in the last bucket"
    tiles = []
    for rel in (0, 1):
        d = rel * T + t - k
        tiles.append(table[_t5_bucket_np(d)] - table[REL_BUCKETS - 1])
    tiles.append(jnp.zeros_like(tiles[0]))
    tp = jnp.stack(tiles, axis=0)
    tp = jnp.transpose(tp, (3, 0, 1, 2)).reshape(G, R, 3, T, T)
    tp = jnp.transpose(tp, (0, 2, 1, 3, 4)).reshape(G, 3, R * T, T)
    return cb, tp


def _nsa_attention(q_ext, kc, vc, ksl, vsl, kw, vw, cb, tp, gl):
    b, s, _ = q_ext.shape
    G, R, T = NSA_KV_HEADS, NSA_REP, NSA_TQ
    nr = cb.shape[-1]
    ncp = kc.shape[-2]
    n_slc = s // SLC_BLOCK
    assert n_slc <= LANES // 2 and s % T == 0
    kern = functools.partial(_nsa_kernel, n_slc=n_slc, top_k=min(SLC_TOPK, n_slc))
    kv_spec = pl.BlockSpec((None, s, LANES), lambda i, j, k: (i, 0, j))
    cmp_spec = pl.BlockSpec((None, None, ncp, LANES), lambda i, j, k: (i, j, 0, 0))
    return pl.pallas_call(
        kern, grid=(b, G, s // T),
        in_specs=[
            pl.BlockSpec((None, T, R * LANES), lambda i, j, k: (i, k, j)),
            cmp_spec, cmp_spec, kv_spec, kv_spec, kv_spec, kv_spec,
            pl.BlockSpec((None, R * T, nr), lambda i, j, k: (j, 0, 0)),
            pl.BlockSpec((None, 3, R * T, T), lambda i, j, k: (j, 0, 0, 0)),
            pl.BlockSpec((None, None, T, 16), lambda i, j, k: (i, j, k, 0)),
        ],
        out_specs=pl.BlockSpec((None, T, R * LANES), lambda i, j, k: (i, k, j)),
        out_shape=jax.ShapeDtypeStruct((b, s, G * R * LANES), BF16),
        scratch_shapes=[pltpu.VMEM((R * T, 1), F32), pltpu.VMEM((R * T, 1), F32),
                        pltpu.VMEM((R * T, LANES), F32)],
        compiler_params=_cparams(3), name="nsa_attention",
    )(q_ext, kc, vc, ksl, vsl, kw, vw, cb, tp, gl)


def _merge_kernel(x_ref, ys_ref, yn_ref, mg_ref, ws_ref, wn_ref, wo_ref, g_ref, o_ref):
    d = x_ref.shape[1]
    a = _dot(ys_ref[...], ws_ref[...])
    bb = _dot(yn_ref[...], wn_ref[...])
    mg = _sigmoid(mg_ref[...].astype(F32))
    mixed = mg[:, :d] * a + mg[:, d:] * bb
    o = _dot(mixed.astype(BF16), wo_ref[...])
    o_ref[...] = x_ref[...] + _rms(o, g_ref[...])


def _merge(x2d, y_ssd, y_nsa, mg, w_s, w_n, w_o, g, tm=512):
    m, d = x2d.shape
    tm = min(tm, m)
    row = lambda a: pl.BlockSpec((tm, a.shape[1]), lambda i: (i, 0))
    full = lambda a: pl.BlockSpec(a.shape, lambda i: (0, 0))
    g = g.reshape(1, d)
    return pl.pallas_call(
        _merge_kernel, grid=(m // tm,),
        in_specs=[row(x2d), row(y_ssd), row(y_nsa), row(mg), full(w_s), full(w_n), full(w_o), full(g)],
        out_specs=pl.BlockSpec((tm, d), lambda i: (i, 0)),
        out_shape=jax.ShapeDtypeStruct((m, d), F32),
        compiler_params=_cparams(1), name="merge",
    )(x2d, y_ssd, y_nsa, mg, w_s, w_n, w_o, g)


def _xattn_kernel(x_ref, kv_ref, gpre_ref, wq_ref, wo_ref, gpost_ref, o_ref):
    x = x_ref[...]
    h = _rms(x, gpre_ref[...]).astype(BF16)
    q = _dot(h, wq_ref[...])
    width = X_HEADS * X_HEAD_DIM
    scale = X_HEAD_DIM ** -0.5
    outs = []
    for hh in range(X_HEADS):
        sl = slice(hh * X_HEAD_DIM, (hh + 1) * X_HEAD_DIM)
        lg = _dot_nt(q[:, sl].astype(BF16), kv_ref[:, sl]) * scale
        e = jnp.exp(lg - jnp.max(lg, axis=-1, keepdims=True))
        p = e / jnp.sum(e, axis=-1, keepdims=True)
        outs.append(_dot(p.astype(BF16), kv_ref[:, width + hh * X_HEAD_DIM:width + (hh + 1) * X_HEAD_DIM]))
    o = _dot(jnp.concatenate(outs, axis=-1).astype(BF16), wo_ref[...])
    o_ref[...] = x + _rms(o, gpost_ref[...])


def _xattn(x3d, kv, g_pre, w_q, w_o, g_post, tm=512):
    b, s, d = x3d.shape
    tm = min(tm, s)
    ml = kv.shape[1]
    full = lambda a: pl.BlockSpec(a.shape, lambda i, j: (0, 0))
    g_pre, g_post = g_pre.reshape(1, d), g_post.reshape(1, d)
    return pl.pallas_call(
        _xattn_kernel, grid=(b, s // tm),
        in_specs=[pl.BlockSpec((None, tm, d), lambda i, j: (i, j, 0)),
                  pl.BlockSpec((None, ml, kv.shape[2]), lambda i, j: (i, 0, 0)),
                  full(g_pre), full(w_q), full(w_o), full(g_post)],
        out_specs=pl.BlockSpec((None, tm, d), lambda i, j: (i, j, 0)),
        out_shape=jax.ShapeDtypeStruct((b, s, d), F32),
        compiler_params=_cparams(2), name="xattn",
    )(x3d, kv, g_pre, w_q, w_o, g_post)


def _mlp_kernel(x_ref, gpre_ref, w1_ref, w2_ref, gpost_ref, o_ref, *, f_chunk):
    x = x_ref[...]
    h = _rms(x, gpre_ref[...]).astype(BF16)
    d_ff = w1_ref.shape[1]
    acc = jnp.zeros(x.shape, F32)
    for f0 in range(0, d_ff, f_chunk):
        u = jnp.maximum(_dot(h, w1_ref[:, f0:f0 + f_chunk]), 0.0)
        acc = acc + _dot((u * u).astype(BF16), w2_ref[f0:f0 + f_chunk, :])
    o_ref[...] = x + _rms(acc, gpost_ref[...])


def _mlp(x2d, g_pre, w1, w2, g_post, tm=512):
    m, d = x2d.shape
    tm = min(tm, m)
    full = lambda a: pl.BlockSpec(a.shape, lambda i: (0, 0))
    g_pre, g_post = g_pre.reshape(1, d), g_post.reshape(1, d)
    return pl.pallas_call(
        functools.partial(_mlp_kernel, f_chunk=1024), grid=(m // tm,),
        in_specs=[pl.BlockSpec((tm, d), lambda i: (i, 0)), full(g_pre), full(w1), full(w2), full(g_post)],
        out_specs=pl.BlockSpec((tm, d), lambda i: (i, 0)),
        out_shape=jax.ShapeDtypeStruct((m, d), F32),
        compiler_params=_cparams(1), name="mlp",
    )(x2d, g_pre, w1, w2, g_post)


def _pad_heads(w, n_heads, hd):
    k = w.shape[0]
    w = w.reshape(k, n_heads, hd)
    return jnp.pad(w, ((0, 0), (0, 0), (0, LANES - hd))).reshape(k, n_heads * LANES)


def _layer(x, mem, w_in, conv_w, conv_b, dt_bias, a_log, d_skip, ssd_norm, cmp_pos, cmp_w1, cmp_w2,
           rel_bias, w_br_ssd, w_br_nsa, w_out, w_xq, w_xkv, w_xo, w_ff1, w_ff2,
           n_mix_pre, n_mix_post, n_x_pre, n_x_post, n_mem, n_ffn_pre, n_ffn_post):
    b, s, d = x.shape
    G, R, HD = NSA_KV_HEADS, NSA_REP, NSA_HEAD_DIM
    d_inner = 2 * d
    n_ssd_heads = d_inner // SSD_HEAD_DIM
    conv_dim = d_inner + 2 * SSD_GROUPS * SSD_STATE
    nsa_w, kv_w = NSA_HEADS * HD, G * HD
    sizes = (d_inner, conv_dim, n_ssd_heads, nsa_w, 6 * kv_w, 3 * NSA_HEADS, 2 * d)
    offs = np.concatenate([[0], np.cumsum(sizes)])
    seg = lambda i: w_in[:, offs[i]:offs[i + 1]]
    w_z, w_xbc, w_dt, w_q, w_kv, w_gate, w_mg = (seg(i) for i in range(7))
    w_small = jnp.pad(jnp.concatenate([w_dt, w_gate], axis=1),
                      ((0, 0), (0, LANES - n_ssd_heads - 3 * NSA_HEADS)))
    w_q_ext = _pad_heads(w_q * (HD ** -0.5), NSA_HEADS, HD)
    w_kv6 = w_kv.reshape(d, 6, kv_w)
    w_cmp = jnp.concatenate([w_kv6[:, 0], w_kv6[:, 1]], axis=1)
    w_kv_ext = [_pad_heads(w_kv6[:, i], G, HD) for i in range(2, 6)]
    bf = lambda a: a.astype(BF16)

    x2d = x.reshape(b * s, d)
    z, xbc = _norm_matmul(x2d, n_mix_pre, [bf(w_z), bf(w_xbc)], [BF16, BF16])
    small, q_ext, kv_cmp = _norm_matmul(x2d, n_mix_pre, [bf(w_small), bf(w_q_ext), bf(w_cmp)],
                                        [F32, BF16, BF16])
    ksl, vsl, kw, vw, mg = _norm_matmul(x2d, n_mix_pre, [bf(w) for w in w_kv_ext] + [bf(w_mg)],
                                        [BF16] * 5)

    y_ssd = _ssd_mixer(z.reshape(b, s, -1), xbc.reshape(b, s, -1), small.reshape(b, s, -1),
                       conv_w, conv_b, dt_bias, a_log, d_skip, ssd_norm)

    nr = s // CMP_STRIDE
    kr = kv_cmp.reshape(b, nr, CMP_STRIDE, 2, G, HD)
    kr = jnp.transpose(kr, (3, 0, 4, 1, 2, 5)).reshape(2, b, G, nr, CMP_STRIDE * HD)
    pos = jnp.broadcast_to(bf(cmp_pos).reshape(2, 1, CMP_BLOCK * HD), (2, 8, CMP_BLOCK * HD))
    w2p = jnp.pad(bf(cmp_w2), ((0, 0), (0, 0), (0, LANES - HD)))
    kvc = _compress(kr, pos, bf(cmp_w1), w2p)
    cb, tp = _nsa_bias_tables(rel_bias, nr)
    gl = small.reshape(b, s, -1)[:, :, n_ssd_heads:n_ssd_heads + 3 * NSA_HEADS]
    gl = jnp.transpose(gl.reshape(b, s, 3, G, R), (0, 3, 1, 2, 4)).reshape(b, G, s, 3 * R)
    gl = jnp.pad(gl, ((0, 0), (0, 0), (0, 0), (0, 16 - 3 * R)))
    r3 = lambda a: a.reshape(b, s, -1)
    y_nsa = _nsa_attention(r3(q_ext), kvc[0], kvc[1], r3(ksl), r3(vsl), r3(kw), r3(vw), cb, tp, gl)

    w_n_ext = jnp.pad(w_br_nsa.reshape(NSA_HEADS, HD, d), ((0, 0), (0, LANES - HD), (0, 0)))
    x1 = _merge(x2d, y_ssd.reshape(b * s, -1), y_nsa.reshape(b * s, -1), mg,
                bf(w_br_ssd), bf(w_n_ext.reshape(NSA_HEADS * LANES, d)), bf(w_out), n_mix_post)

    (kv_mem,) = _norm_matmul(mem.reshape(-1, d), n_mem, [bf(w_xkv)], [BF16])
    x2 = _xattn(x1.reshape(b, s, d), kv_mem.reshape(b, mem.shape[1], -1), n_x_pre, bf(w_xq), bf(w_xo),
                n_x_post)

    x3 = _mlp(x2.reshape(b * s, d), n_ffn_pre, bf(w_ff1), bf(w_ff2), n_ffn_post)
    return x3.reshape(b, s, d)


def kernel(x, mem, w_in, ssd_conv_w, ssd_conv_b, ssd_dt_bias, ssd_a_log, ssd_d_skip, ssd_norm, cmp_pos,
           cmp_w1, cmp_w2, rel_bias, w_br_ssd, w_br_nsa, w_out, w_xq, w_xkv, w_xo, w_ff1, w_ff2,
           norm_mix_pre, norm_mix_post, norm_x_pre, norm_x_post, norm_mem, norm_ffn_pre, norm_ffn_post):
    for l in range(w_in.shape[0]):
        x = _layer(x, mem, w_in[l], ssd_conv_w[l], ssd_conv_b[l], ssd_dt_bias[l], ssd_a_log[l],
                   ssd_d_skip[l], ssd_norm[l], cmp_pos[l], cmp_w1[l], cmp_w2[l], rel_bias,
                   w_br_ssd[l], w_br_nsa[l], w_out[l], w_xq[l], w_xkv[l], w_xo[l], w_ff1[l], w_ff2[l],
                   norm_mix_pre[l], norm_mix_post[l], norm_x_pre[l], norm_x_post[l], norm_mem[l],
                   norm_ffn_pre[l], norm_ffn_post[l])
    return x
```

```python
import functools
import math

import numpy as np
import jax
import jax.numpy as jnp
from jax import lax
from jax.experimental import pallas as pl
from jax.experimental.pallas import tpu as pltpu

F32 = jnp.float32
BF16 = jnp.bfloat16

NORM_EPS = 1e-6
NEG_INF = -1e30
FORCE_SCORE = 1e9
SEL_PENALTY = -1e9

LANES = 128
SUBLANES = 8
VMEM_LIMIT = 56 * 1024 * 1024

SSD_HEAD_DIM = 64
SSD_GROUPS = 8
SSD_STATE = 128
SSD_CONV = 4
SSD_CHUNK = 128
NSA_HEADS = 16
NSA_KV_HEADS = 4
NSA_HEAD_DIM = 64
NSA_REP = NSA_HEADS // NSA_KV_HEADS
CMP_BLOCK = 32
CMP_STRIDE = 16
SLC_BLOCK = 64
SLC_TOPK = 16
WINDOW = 512
NSA_TQ = 128
REL_BUCKETS = 32
REL_MAX_DIST = 128
X_HEADS = 4
X_HEAD_DIM = 128

TILE_DIAG, TILE_NEAR, TILE_NONE, TILE_EDGE, TILE_ZERO = range(5)


def _cparams(n_grid):
    return pltpu.CompilerParams(dimension_semantics=("arbitrary",) * n_grid,
                                vmem_limit_bytes=VMEM_LIMIT)


def _sigmoid(x):
    return 1.0 / (1.0 + jnp.exp(-x))


def _rms(x, g):
    return x * lax.rsqrt(jnp.mean(x * x, axis=-1, keepdims=True) + NORM_EPS) * g


def _dot(a, b):
    return jnp.dot(a, b, preferred_element_type=F32)


def _dot_nt(a, b):
    return lax.dot_general(a, b, (((1,), (1,)), ((), ())), preferred_element_type=F32)


def _split3(x):
    x1 = x.astype(BF16)
    r1 = x - x1.astype(F32)
    x2 = r1.astype(BF16)
    x3 = (r1 - x2.astype(F32)).astype(BF16)
    return x1, x2, x3


def _norm_matmul_kernel(x_ref, g_ref, *refs, n_out, n_chunk):
    w_refs, o_refs = refs[:n_out], refs[n_out:]
    h = _rms(x_ref[...], g_ref[...]).astype(BF16)
    for w_ref, o_ref in zip(w_refs, o_refs):
        n = w_ref.shape[1]
        step = min(n, n_chunk)
        for n0 in range(0, n, step):
            o_ref[:, n0:n0 + step] = _dot(h, w_ref[:, n0:n0 + step]).astype(o_ref.dtype)


def _norm_matmul(x2d, g, ws, out_dtypes, tm=512):
    m, k = x2d.shape
    tm = min(tm, m)
    in_specs = [pl.BlockSpec((tm, k), lambda i: (i, 0)), pl.BlockSpec((1, k), lambda i: (0, 0))]
    in_specs += [pl.BlockSpec(w.shape, lambda i: (0, 0)) for w in ws]
    out_specs = [pl.BlockSpec((tm, w.shape[1]), lambda i: (i, 0)) for w in ws]
    out_shape = [jax.ShapeDtypeStruct((m, w.shape[1]), dt) for w, dt in zip(ws, out_dtypes)]
    return pl.pallas_call(
        functools.partial(_norm_matmul_kernel, n_out=len(ws), n_chunk=1024),
        grid=(m // tm,), in_specs=in_specs, out_specs=out_specs, out_shape=out_shape,
        compiler_params=_cparams(1), name="norm_proj",
    )(x2d, g.reshape(1, k), *ws)


def _softplus(x):
    return jnp.maximum(x, 0.0) + jnp.log(1.0 + jnp.exp(-jnp.abs(x)))


def _ssd_kernel(xbc_ref, prev_ref, z_ref, dt_ref, dtT_ref, cw_ref, cb_ref, dtb_ref, dtbT_ref,
                alog_ref, alogT_ref, dskip_ref, nw_ref, y_ref, state_ref, xc_ref, ybuf_ref,
                *, n_heads, d_inner):
    L, P, N, G = SSD_CHUNK, SSD_HEAD_DIM, SSD_STATE, SSD_GROUPS
    conv_dim = xc_ref.shape[1]
    c = pl.program_id(1)

    @pl.when(c == 0)
    def _():
        state_ref[...] = jnp.zeros_like(state_ref)

    cw = 512
    prev_rows = prev_ref.shape[0]
    row = lax.broadcasted_iota(jnp.int32, (prev_rows, cw), 0)
    for j in range(conv_dim // cw):
        sl = slice(j * cw, (j + 1) * cw)
        cur = xbc_ref[:, sl].astype(F32)
        prev = jnp.where(c == 0, 0.0, prev_ref[:, sl].astype(F32))
        acc = cb_ref[:, sl] + cw_ref[SSD_CONV - 1:SSD_CONV, sl] * cur
        for k in range(1, SSD_CONV):
            rc = pltpu.roll(cur, k, 0)
            rp = pltpu.roll(prev, k, 0)
            head = jnp.where(row < k, rp, rc[:prev_rows])
            shifted = jnp.concatenate([head, rc[prev_rows:]], axis=0)
            acc = acc + cw_ref[SSD_CONV - 1 - k:SSD_CONV - k, sl] * shifted
        xc_ref[:, sl] = acc * _sigmoid(acc)

    dt = _softplus(dt_ref[:, 0:n_heads] + dtb_ref[...])
    a = dt * (-jnp.exp(alog_ref[...]))
    dtT = _softplus(dtT_ref[...] + dtbT_ref[...])
    aT = dtT * (-jnp.exp(alogT_ref[...]))
    ri = lax.broadcasted_iota(jnp.int32, (L, L), 0)
    ci = lax.broadcasted_iota(jnp.int32, (L, L), 1)
    causal = ci <= ri
    tri = jnp.where(causal, 1.0, 0.0).astype(BF16)
    triu = jnp.where(ri <= ci, 1.0, 0.0).astype(BF16)
    a_cs = sum(_dot(tri, ai) for ai in _split3(a))
    a_csT = sum(_dot(ai, triu) for ai in _split3(aT))
    lane = lax.broadcasted_iota(jnp.int32, (L, 2 * P), 1)
    lo_half = lane < P
    lane1 = lax.broadcasted_iota(jnp.int32, (1, 2 * P), 1) < P

    heads_per_group = n_heads // G
    for g in range(G):
        b_g = xc_ref[:, d_inner + g * N:d_inner + (g + 1) * N]
        c_g = xc_ref[:, d_inner + G * N + g * N:d_inner + G * N + (g + 1) * N]
        c_gb = c_g.astype(BF16)
        cb = _dot_nt(c_gb, b_g.astype(BF16))
        b_gT = b_g.T.astype(BF16)
        for pp in range(heads_per_group // 2):
            pair = g * (heads_per_group // 2) + pp
            h0, h1 = 2 * pair, 2 * pair + 1
            xs = xc_ref[:, pair * 2 * P:(pair + 1) * 2 * P]
            dt_pair = jnp.where(lo_half, dt[:, h0:h0 + 1], dt[:, h1:h1 + 1])
            xd = xs * dt_pair
            xd_b = xd.astype(BF16)
            ys = []
            for h in (h0, h1):
                seg = jnp.where(causal, a_cs[:, h:h + 1] - a_csT[h:h + 1, :], NEG_INF)
                mat = (cb * jnp.exp(seg)).astype(BF16)
                ys.append(_dot(mat, xd_b))
            y = jnp.where(lo_half, ys[0], ys[1])
            cs_pair = jnp.where(lo_half, a_cs[:, h0:h0 + 1], a_cs[:, h1:h1 + 1])
            st = state_ref[pair]
            y = y + _dot(c_gb, st.astype(BF16)) * jnp.exp(cs_pair)
            tot = jnp.where(lane1, a_csT[h0:h0 + 1, L - 1:L], a_csT[h1:h1 + 1, L - 1:L])
            xdd = (xd * jnp.exp(tot - cs_pair)).astype(BF16)
            state_ref[pair] = st * jnp.exp(tot) + _dot(b_gT, xdd)
            ybuf_ref[:, pair * 2 * P:(pair + 1) * 2 * P] = y + xs * dskip_ref[:, pair * 2 * P:(pair + 1) * 2 * P]

    gw = d_inner // G
    for g in range(G):
        sl = slice(g * gw, (g + 1) * gw)
        zz = z_ref[:, sl].astype(F32)
        yg = ybuf_ref[:, sl] * (zz * _sigmoid(zz))
        y_ref[:, sl] = _rms(yg, nw_ref[:, sl]).astype(y_ref.dtype)


def _ssd_mixer(z, xbc, small, conv_w, conv_b, dt_bias, a_log, d_skip, norm_w):
    b, s, d_inner = z.shape
    conv_dim = xbc.shape[-1]
    n_heads = d_inner // SSD_HEAD_DIM
    L = SSD_CHUNK
    nc = s // L
    prev_rows = 16
    xbc_prev = xbc.reshape(b, s // prev_rows, prev_rows, conv_dim)
    dtT = jnp.swapaxes(small[:, :, :n_heads], 1, 2)
    blocks_per_chunk = L // prev_rows
    kern = functools.partial(_ssd_kernel, n_heads=n_heads, d_inner=d_inner)
    const = lambda shape: pl.BlockSpec(shape, lambda i, j: (0,) * len(shape))
    return pl.pallas_call(
        kern, grid=(b, nc),
        in_specs=[
            pl.BlockSpec((None, L, conv_dim), lambda i, j: (i, j, 0)),
            pl.BlockSpec((None, None, prev_rows, conv_dim),
                         lambda i, j: (i, jnp.maximum(j * blocks_per_chunk - 1, 0), 0, 0)),
            pl.BlockSpec((None, L, d_inner), lambda i, j: (i, j, 0)),
            pl.BlockSpec((None, L, small.shape[-1]), lambda i, j: (i, j, 0)),
            pl.BlockSpec((None, n_heads, L), lambda i, j: (i, 0, j)),
            const((SSD_CONV, conv_dim)), const((1, conv_dim)),
            const((1, n_heads)), const((n_heads, 1)),
            const((1, n_heads)), const((n_heads, 1)),
            const((1, d_inner)), const((1, d_inner)),
        ],
        out_specs=pl.BlockSpec((None, L, d_inner), lambda i, j: (i, j, 0)),
        out_shape=jax.ShapeDtypeStruct((b, s, d_inner), BF16),
        scratch_shapes=[pltpu.VMEM((n_heads // 2, SSD_STATE, 2 * SSD_HEAD_DIM), F32),
                        pltpu.VMEM((L, conv_dim), F32),
                        pltpu.VMEM((L, d_inner), F32)],
        compiler_params=_cparams(2), name="ssd",
    )(xbc, xbc_prev, z, small, dtT, conv_w.astype(F32), conv_b.reshape(1, -1).astype(F32),
      dt_bias.reshape(1, -1), dt_bias.reshape(-1, 1), a_log.reshape(1, -1), a_log.reshape(-1, 1),
      jnp.repeat(d_skip, SSD_HEAD_DIM).reshape(1, -1), norm_w.reshape(1, -1))


def _compress_kernel(kr_ref, pos_ref, w1_ref, w2_ref, o_ref):
    nr, half = kr_ref.shape
    kr = kr_ref[...]
    p1 = _dot(kr, w1_ref[0:half, :])
    p2 = _dot(kr, w1_ref[half:2 * half, :])
    pb = _dot(pos_ref[...], w1_ref[...])[0:1]
    hid = p1 + pltpu.roll(p2, nr - 1, 0) + pb
    hid = hid * _sigmoid(hid)
    out = _dot(hid.astype(BF16), w2_ref[...])
    rows = lax.broadcasted_iota(jnp.int32, out.shape, 0)
    o_ref[...] = jnp.where(rows < nr - 1, out, 0.0).astype(o_ref.dtype)


def _compress(kr, pos, w1, w2p):
    two, b, g, nr, half = kr.shape
    hidden = w1.shape[-1]
    return pl.pallas_call(
        _compress_kernel,
        grid=(two, b, g),
        in_specs=[
            pl.BlockSpec((None, None, None, nr, half), lambda t, i, j: (t, i, j, 0, 0)),
            pl.BlockSpec((None, SUBLANES, 2 * half), lambda t, i, j: (t, 0, 0)),
            pl.BlockSpec((None, 2 * half, hidden), lambda t, i, j: (t, 0, 0)),
            pl.BlockSpec((None, hidden, LANES), lambda t, i, j: (t, 0, 0)),
        ],
        out_specs=pl.BlockSpec((None, None, None, nr, LANES), lambda t, i, j: (t, i, j, 0, 0)),
        out_shape=jax.ShapeDtypeStruct((two, b, g, nr, LANES), BF16),
        compiler_params=_cparams(3), name="nsa_compress",
    )(kr, pos, w1, w2p)


def _nsa_kernel(qT_ref, kc_ref, vcT_ref, ov_ref, ksl_ref, vslT_ref, kw_ref, vwT_ref, cb_ref, tp_ref,
                gl_ref, o_ref, q0_ref, qa_ref, *, n_slc, top_k):
    T, R, HD = NSA_TQ, NSA_REP, NSA_HEAD_DIM
    nr = kc_ref.shape[0]
    per_tile = T // CMP_STRIDE
    front = nr - per_tile
    half = LANES // 2
    qi = pl.program_id(2)

    qT = jnp.concatenate([qT_ref[r] for r in range(R)], axis=1)
    q0_ref[0:HD, :] = qT
    q0_ref[HD:2 * HD, :] = jnp.zeros((HD, R * T), BF16)
    qa_ref[0:HD, :] = qT

    def colmax(a):
        return jnp.max(a, axis=0, keepdims=True)

    def colsum(a):
        return jnp.sum(a, axis=0, keepdims=True)

    n_win = WINDOW // T
    scores, offsets = [], []
    for rel in range(n_win, -1, -1):
        tile = TILE_EDGE if rel == n_win else (TILE_ZERO if rel >= 2 else (TILE_NEAR if rel == 1 else TILE_DIAG))
        if rel > 0:
            tile = jnp.where(qi >= rel, tile, TILE_NONE)
        ko = pl.multiple_of(jnp.maximum(qi - rel, 0) * T, T)
        scores.append(_dot(kw_ref[pl.ds(ko, T), :], q0_ref[...]) + tp_ref[tile])
        offsets.append(ko)
    m_w = functools.reduce(jnp.maximum, [colmax(s) for s in scores])
    l_w = jnp.zeros((1, R * T), F32)
    acc_w = jnp.zeros((HD, R * T), F32)
    for s, ko in zip(scores, offsets):
        pe = jnp.exp(s - m_w)
        l_w = l_w + colsum(pe)
        acc_w = acc_w + _dot(vwT_ref[:, pl.ds(ko, T)], pe.astype(BF16))
    o_win = acc_w / l_w

    off = pl.multiple_of(front - qi * per_tile, SUBLANES)
    lg = _dot(kc_ref[...], q0_ref[...]) + cb_ref[pl.ds(off, nr), :]
    valid = jnp.where(lg > 0.5 * NEG_INF, 1.0, 0.0)
    e = jnp.exp(lg - colmax(lg))
    p = e / colsum(e) * valid
    o_cmp = _dot(vcT_ref[...], p.astype(BF16))

    psum = p[:, 0:T]
    for r in range(1, R):
        psum = psum + p[:, r * T:(r + 1) * T]
    imp = sum(_dot(ov_ref[...], pi) for pi in _split3(psum))
    nio = lax.broadcasted_iota(jnp.int32, (half, T), 0)
    tio = lax.broadcasted_iota(jnp.int32, (half, T), 1)
    tb = (qi * T + tio) // SLC_BLOCK
    forced = (nio == 0) | (nio == tb) | (nio == tb - 1)
    imp = jnp.where(forced, FORCE_SCORE, imp)
    imp = jnp.where(nio > tb, NEG_INF, imp)
    n_grp = half // SUBLANES
    grp = [imp[SUBLANES * v:SUBLANES * (v + 1)] for v in range(n_grp)]
    nio8 = lax.broadcasted_iota(jnp.int32, (SUBLANES, T), 0)
    rank = [jnp.zeros((SUBLANES, T), F32) for _ in range(n_grp)]
    for mm in range(n_slc):
        rowv = imp[mm:mm + 1, :]
        for v in range(n_grp):
            ge = jnp.where(rowv >= grp[v], 1.0, 0.0)
            gt = jnp.where(rowv > grp[v], 1.0, 0.0)
            if SUBLANES * v > mm:
                hit = ge
            elif SUBLANES * v + SUBLANES - 1 < mm:
                hit = gt
            else:
                hit = jnp.where(nio8 > mm - SUBLANES * v, ge, gt)
            rank[v] = rank[v] + hit
    rank = jnp.concatenate(rank, axis=0)
    pen = jnp.where(rank < top_k, jnp.where(nio <= tb, 0.0, SEL_PENALTY), SEL_PENALTY).astype(BF16)
    qa_ref[HD:2 * HD, :] = jnp.concatenate([pen] * R, axis=1)

    krow = lax.broadcasted_iota(jnp.int32, (T, LANES), 0)
    klane = lax.broadcasted_iota(jnp.int32, (T, LANES), 1)

    def sel_scores(kj):
        kt = ksl_ref[pl.ds(pl.multiple_of(kj * T, T), T), :]
        blk = kj * (T // SLC_BLOCK) + krow // SLC_BLOCK
        kt = jnp.where(klane >= half, jnp.where(klane - half == blk, 1.0, 0.0).astype(BF16), kt)
        return _dot(kt, qa_ref[...])

    def sel_values(kj):
        return vslT_ref[:, pl.ds(pl.multiple_of(kj * T, T), T)]

    n_far = jnp.maximum(qi - 1, 0)

    def far_body(kj, carry):
        s_cur, pe_prev, alpha_prev, m_old, l_old, acc = carry
        acc = alpha_prev * acc + _dot(sel_values(jnp.maximum(kj - 1, 0)), pe_prev)
        s_next = sel_scores(kj + 1)
        m_new = jnp.maximum(m_old, colmax(s_cur))
        alpha = jnp.exp(m_old - m_new)
        pe = jnp.exp(s_cur - m_new)
        return s_next, pe.astype(BF16), alpha, m_new, alpha * l_old + colsum(pe), acc

    init = (sel_scores(0), jnp.zeros((T, R * T), BF16), jnp.ones((1, R * T), F32),
            jnp.full((1, R * T), NEG_INF, F32), jnp.zeros((1, R * T), F32), jnp.zeros((HD, R * T), F32))
    s_near, pe_prev, alpha_prev, m_old, l_old, acc = lax.fori_loop(0, n_far, far_body, init)
    acc = alpha_prev * acc + _dot(sel_values(jnp.maximum(n_far - 1, 0)), pe_prev)
    s_near = s_near + tp_ref[jnp.where(qi >= 1, TILE_NEAR, TILE_NONE)]
    s_diag = sel_scores(qi) + tp_ref[TILE_DIAG]
    m_new = jnp.maximum(m_old, jnp.maximum(colmax(s_near), colmax(s_diag)))
    alpha = jnp.exp(m_old - m_new)
    pe_n = jnp.exp(s_near - m_new)
    pe_d = jnp.exp(s_diag - m_new)
    l_s = alpha * l_old + colsum(pe_n) + colsum(pe_d)
    acc = alpha * acc + _dot(sel_values(n_far), pe_n.astype(BF16)) + _dot(sel_values(qi), pe_d.astype(BF16))
    o_slc = acc / l_s

    sg = _sigmoid(gl_ref[...])
    zpad = jnp.zeros((LANES - HD, T), F32)
    for r in range(R):
        cols = slice(r * T, (r + 1) * T)
        o = (sg[r:r + 1] * o_cmp[:, cols] + sg[R + r:R + r + 1] * o_slc[:, cols]
             + sg[2 * R + r:2 * R + r + 1] * o_win[:, cols])
        o_ref[:, LANES * r:LANES * (r + 1)] = jnp.concatenate([o, zpad], axis=0).T.astype(o_ref.dtype)


def _t5_bucket_np(dist):
    n = np.maximum(dist, 0)
    max_exact = REL_BUCKETS // 2
    nf = np.maximum(n, 1).astype(np.float32)
    large = max_exact + (np.log(nf / np.float32(max_exact)) / np.float32(math.log(REL_MAX_DIST / max_exact))
                         * np.float32(REL_BUCKETS - max_exact)).astype(np.int32)
    large = np.minimum(large, REL_BUCKETS - 1)
    return np.where(n < max_exact, n, large).astype(np.int32)


def _nsa_tables(rel_bias, nr):
    T, G, R = NSA_TQ, NSA_KV_HEADS, NSA_REP
    table = rel_bias.astype(F32)
    per_tile = T // CMP_STRIDE
    front = nr - per_tile
    t = np.arange(T)[None, :]

    def lay(a):
        keys = a.shape[0]
        return jnp.transpose(a, (2, 0, 1)).reshape(G, R, keys, T).transpose(0, 2, 1, 3).reshape(G, keys, R * T)

    cend = CMP_STRIDE * (np.arange(front + nr)[:, None] - front) + CMP_BLOCK - 1
    dist = t - cend
    cb = jnp.where((dist >= 0)[..., None], table[_t5_bucket_np(dist)], NEG_INF)
    k = np.arange(T)[:, None]
    assert 2 * T - (T - 1) >= REL_MAX_DIST and WINDOW % T == 0 and WINDOW // T >= 2
    last = table[REL_BUCKETS - 1]
    diag = jnp.where((t - k >= 0)[..., None], table[_t5_bucket_np(t - k)] - last, NEG_INF)
    near = table[_t5_bucket_np(T + t - k)] - last
    none = jnp.full_like(near, NEG_INF)
    edge = jnp.where((k > t)[..., None], jnp.zeros_like(near), NEG_INF)
    tiles = [None] * 5
    tiles[TILE_DIAG], tiles[TILE_NEAR], tiles[TILE_NONE], tiles[TILE_EDGE] = diag, near, none, edge
    tiles[TILE_ZERO] = jnp.zeros_like(near)
    tp = jnp.stack([lay(a) for a in tiles], axis=1)
    return lay(cb), tp


def _overlap_matrix(nr, n_slc):
    half = LANES // 2
    n = np.arange(half)[:, None]
    c = np.arange(nr)[None, :]
    ov = ((CMP_STRIDE * c <= SLC_BLOCK * n + SLC_BLOCK - 1) & (CMP_STRIDE * c + CMP_BLOCK - 1 >= SLC_BLOCK * n)
          & (n < n_slc) & (c < nr - 1))
    return jnp.asarray(ov.astype(np.float32), dtype=BF16)


def _nsa_attention(qT, kc, vcT, ksl, vslT, kw, vwT, cb, tp, glT):
    b, n_heads, hd, s = qT.shape
    G, R, T = NSA_KV_HEADS, NSA_REP, NSA_TQ
    nr = kc.shape[-2]
    n_slc = s // SLC_BLOCK
    assert n_slc <= LANES // 2 and s % T == 0 and nr % SUBLANES == 0
    ov = _overlap_matrix(nr, n_slc)
    kern = functools.partial(_nsa_kernel, n_slc=n_slc, top_k=min(SLC_TOPK, n_slc))
    k_spec = pl.BlockSpec((None, s, LANES), lambda i, j, k: (i, 0, j))
    vT_spec = pl.BlockSpec((None, None, hd, s), lambda i, j, k: (i, j, 0, 0))
    return pl.pallas_call(
        kern, grid=(b, G, s // T),
        in_specs=[
            pl.BlockSpec((None, R, hd, T), lambda i, j, k: (i, j, 0, k)),
            pl.BlockSpec((None, None, nr, LANES), lambda i, j, k: (i, j, 0, 0)),
            pl.BlockSpec((None, None, hd, nr), lambda i, j, k: (i, j, 0, 0)),
            pl.BlockSpec(ov.shape, lambda i, j, k: (0, 0)),
            k_spec, vT_spec, k_spec, vT_spec,
            pl.BlockSpec((None,) + cb.shape[1:], lambda i, j, k: (j, 0, 0)),
            pl.BlockSpec((None,) + tp.shape[1:], lambda i, j, k: (j, 0, 0, 0)),
            pl.BlockSpec((None, None, 16, T), lambda i, j, k: (i, j, 0, k)),
        ],
        out_specs=pl.BlockSpec((None, T, R * LANES), lambda i, j, k: (i, k, j)),
        out_shape=jax.ShapeDtypeStruct((b, s, G * R * LANES), BF16),
        scratch_shapes=[pltpu.VMEM((2 * hd, R * T), BF16), pltpu.VMEM((2 * hd, R * T), BF16)],
        compiler_params=_cparams(3), name="nsa_attention",
    )(qT, kc, vcT, ov, ksl, vslT, kw, vwT, cb, tp, glT)


def _merge_kernel(x_ref, ys_ref, yn_ref, mg_ref, ws_ref, wn_ref, wo_ref, g_ref, o_ref):
    d = x_ref.shape[1]
    a = _dot(ys_ref[...], ws_ref[...])
    bb = _dot(yn_ref[...], wn_ref[...])
    mg = _sigmoid(mg_ref[...].astype(F32))
    mixed = mg[:, :d] * a + mg[:, d:] * bb
    o = _dot(mixed.astype(BF16), wo_ref[...])
    o_ref[...] = x_ref[...] + _rms(o, g_ref[...])


def _merge(x2d, y_ssd, y_nsa, mg, w_s, w_n, w_o, g, tm=512):
    m, d = x2d.shape
    tm = min(tm, m)
    row = lambda a: pl.BlockSpec((tm, a.shape[1]), lambda i: (i, 0))
    full = lambda a: pl.BlockSpec(a.shape, lambda i: (0, 0))
    g = g.reshape(1, d)
    return pl.pallas_call(
        _merge_kernel, grid=(m // tm,),
        in_specs=[row(x2d), row(y_ssd), row(y_nsa), row(mg), full(w_s), full(w_n), full(w_o), full(g)],
        out_specs=pl.BlockSpec((tm, d), lambda i: (i, 0)),
        out_shape=jax.ShapeDtypeStruct((m, d), F32),
        compiler_params=_cparams(1), name="merge",
    )(x2d, y_ssd, y_nsa, mg, w_s, w_n, w_o, g)


def _xattn_kernel(x_ref, kv_ref, gpre_ref, wq_ref, wo_ref, gpost_ref, o_ref):
    x = x_ref[...]
    h = _rms(x, gpre_ref[...]).astype(BF16)
    q = _dot(h, wq_ref[...])
    width = X_HEADS * X_HEAD_DIM
    scale = X_HEAD_DIM ** -0.5
    outs = []
    for hh in range(X_HEADS):
        sl = slice(hh * X_HEAD_DIM, (hh + 1) * X_HEAD_DIM)
        lg = _dot_nt(q[:, sl].astype(BF16), kv_ref[:, sl]) * scale
        e = jnp.exp(lg - jnp.max(lg, axis=-1, keepdims=True))
        p = e / jnp.sum(e, axis=-1, keepdims=True)
        outs.append(_dot(p.astype(BF16), kv_ref[:, width + hh * X_HEAD_DIM:width + (hh + 1) * X_HEAD_DIM]))
    o = _dot(jnp.concatenate(outs, axis=-1).astype(BF16), wo_ref[...])
    o_ref[...] = x + _rms(o, gpost_ref[...])


def _xattn(x3d, kv, g_pre, w_q, w_o, g_post, tm=512):
    b, s, d = x3d.shape
    tm = min(tm, s)
    ml = kv.shape[1]
    full = lambda a: pl.BlockSpec(a.shape, lambda i, j: (0, 0))
    g_pre, g_post = g_pre.reshape(1, d), g_post.reshape(1, d)
    return pl.pallas_call(
        _xattn_kernel, grid=(b, s // tm),
        in_specs=[pl.BlockSpec((None, tm, d), lambda i, j: (i, j, 0)),
                  pl.BlockSpec((None, ml, kv.shape[2]), lambda i, j: (i, 0, 0)),
                  full(g_pre), full(w_q), full(w_o), full(g_post)],
        out_specs=pl.BlockSpec((None, tm, d), lambda i, j: (i, j, 0)),
        out_shape=jax.ShapeDtypeStruct((b, s, d), F32),
        compiler_params=_cparams(2), name="xattn",
    )(x3d, kv, g_pre, w_q, w_o, g_post)


def _mlp_kernel(x_ref, gpre_ref, w1_ref, w2_ref, gpost_ref, o_ref, *, f_chunk):
    x = x_ref[...]
    h = _rms(x, gpre_ref[...]).astype(BF16)
    d_ff = w1_ref.shape[1]
    acc = jnp.zeros(x.shape, F32)
    for f0 in range(0, d_ff, f_chunk):
        u = jnp.maximum(_dot(h, w1_ref[:, f0:f0 + f_chunk]), 0.0)
        acc = acc + _dot((u * u).astype(BF16), w2_ref[f0:f0 + f_chunk, :])
    o_ref[...] = x + _rms(acc, gpost_ref[...])


def _mlp(x2d, g_pre, w1, w2, g_post, tm=512):
    m, d = x2d.shape
    tm = min(tm, m)
    full = lambda a: pl.BlockSpec(a.shape, lambda i: (0, 0))
    g_pre, g_post = g_pre.reshape(1, d), g_post.reshape(1, d)
    return pl.pallas_call(
        functools.partial(_mlp_kernel, f_chunk=1024), grid=(m // tm,),
        in_specs=[pl.BlockSpec((tm, d), lambda i: (i, 0)), full(g_pre), full(w1), full(w2), full(g_post)],
        out_specs=pl.BlockSpec((tm, d), lambda i: (i, 0)),
        out_shape=jax.ShapeDtypeStruct((m, d), F32),
        compiler_params=_cparams(1), name="mlp",
    )(x2d, g_pre, w1, w2, g_post)


def _pad_heads(w, n_heads, hd):
    k = w.shape[0]
    w = w.reshape(k, n_heads, hd)
    return jnp.pad(w, ((0, 0), (0, 0), (0, LANES - hd))).reshape(k, n_heads * LANES)


def _heads_major(a, b, s, n_heads, hd):
    return jnp.transpose(a.reshape(b, s, n_heads, hd), (0, 2, 3, 1))


def _layer(x, mem, w_in, conv_w, conv_b, dt_bias, a_log, d_skip, ssd_norm, cmp_pos, cmp_w1, cmp_w2,
           rel_bias, w_br_ssd, w_br_nsa, w_out, w_xq, w_xkv, w_xo, w_ff1, w_ff2,
           n_mix_pre, n_mix_post, n_x_pre, n_x_post, n_mem, n_ffn_pre, n_ffn_post):
    b, s, d = x.shape
    G, R, HD = NSA_KV_HEADS, NSA_REP, NSA_HEAD_DIM
    d_inner = 2 * d
    n_ssd_heads = d_inner // SSD_HEAD_DIM
    conv_dim = d_inner + 2 * SSD_GROUPS * SSD_STATE
    nsa_w, kv_w = NSA_HEADS * HD, G * HD
    sizes = (d_inner, conv_dim, n_ssd_heads, nsa_w, 6 * kv_w, 3 * NSA_HEADS, 2 * d)
    offs = np.concatenate([[0], np.cumsum(sizes)])
    seg = lambda i: w_in[:, offs[i]:offs[i + 1]]
    w_z, w_xbc, w_dt, w_q, w_kv, w_gate, w_mg = (seg(i) for i in range(7))
    w_small = jnp.pad(jnp.concatenate([w_dt, w_gate], axis=1),
                      ((0, 0), (0, LANES - n_ssd_heads - 3 * NSA_HEADS)))
    w_kv6 = w_kv.reshape(d, 6, kv_w)
    w_cmp = jnp.concatenate([w_kv6[:, 0], w_kv6[:, 1]], axis=1)
    bf = lambda a: a.astype(BF16)

    x2d = x.reshape(b * s, d)
    z, xbc = _norm_matmul(x2d, n_mix_pre, [bf(w_z), bf(w_xbc)], [BF16, BF16])
    small, q, kv_cmp, vsl, vw = _norm_matmul(
        x2d, n_mix_pre, [bf(w_small), bf(w_q * (HD ** -0.5)), bf(w_cmp), bf(w_kv6[:, 3]), bf(w_kv6[:, 5])],
        [F32, BF16, BF16, BF16, BF16])
    ksl, kw, mg = _norm_matmul(
        x2d, n_mix_pre, [bf(_pad_heads(w_kv6[:, 2], G, HD)), bf(_pad_heads(w_kv6[:, 4], G, HD)), bf(w_mg)],
        [BF16] * 3)

    y_ssd = _ssd_mixer(z.reshape(b, s, -1), xbc.reshape(b, s, -1), small.reshape(b, s, -1),
                       conv_w, conv_b, dt_bias, a_log, d_skip, ssd_norm)

    nr = s // CMP_STRIDE
    kr = kv_cmp.reshape(b, nr, CMP_STRIDE, 2, G, HD)
    kr = jnp.transpose(kr, (3, 0, 4, 1, 2, 5)).reshape(2, b, G, nr, CMP_STRIDE * HD)
    pos = jnp.broadcast_to(bf(cmp_pos).reshape(2, 1, CMP_BLOCK * HD), (2, SUBLANES, CMP_BLOCK * HD))
    w2p = jnp.pad(bf(cmp_w2), ((0, 0), (0, 0), (0, LANES - HD)))
    kvc = _compress(kr, pos, bf(cmp_w1), w2p)
    vcT = jnp.swapaxes(kvc[1][..., :HD], -1, -2)
    cb, tp = _nsa_tables(rel_bias, nr)
    gl = small.reshape(b, s, -1)[:, :, n_ssd_heads:n_ssd_heads + 3 * NSA_HEADS]
    glT = jnp.transpose(gl.reshape(b, s, 3, G, R), (0, 3, 2, 4, 1)).reshape(b, G, 3 * R, s)
    glT = jnp.pad(glT, ((0, 0), (0, 0), (0, 16 - 3 * R), (0, 0)))
    y_nsa = _nsa_attention(_heads_major(q, b, s, NSA_HEADS, HD), kvc[0], vcT,
                           ksl.reshape(b, s, -1), _heads_major(vsl, b, s, G, HD),
                           kw.reshape(b, s, -1), _heads_major(vw, b, s, G, HD), cb, tp, glT)

    w_n_ext = jnp.pad(w_br_nsa.reshape(NSA_HEADS, HD, d), ((0, 0), (0, LANES - HD), (0, 0)))
    x1 = _merge(x2d, y_ssd.reshape(b * s, -1), y_nsa.reshape(b * s, -1), mg,
                bf(w_br_ssd), bf(w_n_ext.reshape(NSA_HEADS * LANES, d)), bf(w_out), n_mix_post)

    (kv_mem,) = _norm_matmul(mem.reshape(-1, d), n_mem, [bf(w_xkv)], [BF16])
    x2 = _xattn(x1.reshape(b, s, d), kv_mem.reshape(b, mem.shape[1], -1), n_x_pre, bf(w_xq), bf(w_xo),
                n_x_post)

    x3 = _mlp(x2.reshape(b * s, d), n_ffn_pre, bf(w_ff1), bf(w_ff2), n_ffn_post)
    return x3.reshape(b, s, d)


def kernel(x, mem, w_in, ssd_conv_w, ssd_conv_b, ssd_dt_bias, ssd_a_log, ssd_d_skip, ssd_norm, cmp_pos,
           cmp_w1, cmp_w2, rel_bias, w_br_ssd, w_br_nsa, w_out, w_xq, w_xkv, w_xo, w_ff1, w_ff2,
           norm_mix_pre, norm_mix_post, norm_x_pre, norm_x_post, norm_mem, norm_ffn_pre, norm_ffn_post):
    for l in range(w_in.shape[0]):
        x = _layer(x, mem, w_in[l], ssd_conv_w[l], ssd_conv_b[l], ssd_dt_bias[l], ssd_a_log[l],
                   ssd_d_skip[l], ssd_norm[l], cmp_pos[l], cmp_w1[l], cmp_w2[l], rel_bias,
                   w_br_ssd[l], w_br_nsa[l], w_out[l], w_xq[l], w_xkv[l], w_xo[l], w_ff1[l], w_ff2[l],
                   norm_mix_pre[l], norm_mix_post[l], norm_x_pre[l], norm_x_post[l], norm_mem[l],
                   norm_ffn_pre[l], norm_ffn_post[l])
    return x
```

```python
import functools
import math

import numpy as np
import jax
import jax.numpy as jnp
from jax import lax
from jax.experimental import pallas as pl
from jax.experimental.pallas import tpu as pltpu

F32 = jnp.float32
BF16 = jnp.bfloat16

NORM_EPS = 1e-6
NEG_INF = -1e30
FORCE_SCORE = 1e9
SEL_PENALTY = -1e9

LANES = 128
SUBLANES = 8
VMEM_LIMIT = 56 * 1024 * 1024

SSD_HEAD_DIM = 64
SSD_GROUPS = 8
SSD_STATE = 128
SSD_CONV = 4
SSD_CHUNK = 128
NSA_HEADS = 16
NSA_KV_HEADS = 4
NSA_HEAD_DIM = 64
NSA_REP = NSA_HEADS // NSA_KV_HEADS
CMP_BLOCK = 32
CMP_STRIDE = 16
SLC_BLOCK = 64
SLC_TOPK = 16
WINDOW = 512
NSA_TQ = 256
NSA_GROUPS_PER_STEP = 1
LOG2E = math.log2(math.e)
REL_BUCKETS = 32
REL_MAX_DIST = 128
X_HEADS = 4
X_HEAD_DIM = 128

TILE_DIAG, TILE_NEAR, TILE_NONE, TILE_EDGE, TILE_ZERO = range(5)


def _cparams(n_grid):
    return pltpu.CompilerParams(dimension_semantics=("arbitrary",) * n_grid,
                                vmem_limit_bytes=VMEM_LIMIT)


def _sigmoid(x):
    return 1.0 / (1.0 + jnp.exp(-x))


def _rms(x, g):
    return x * lax.rsqrt(jnp.mean(x * x, axis=-1, keepdims=True) + NORM_EPS) * g


def _dot(a, b):
    return jnp.dot(a, b, preferred_element_type=F32)


def _dot_nt(a, b):
    return lax.dot_general(a, b, (((1,), (1,)), ((), ())), preferred_element_type=F32)


def _split3(x):
    x1 = x.astype(BF16)
    r1 = x - x1.astype(F32)
    x2 = r1.astype(BF16)
    x3 = (r1 - x2.astype(F32)).astype(BF16)
    return x1, x2, x3


def _norm_matmul_kernel(x_ref, g_ref, *refs, n_out, n_chunk):
    w_refs, o_refs = refs[:n_out], refs[n_out:]
    h = _rms(x_ref[...], g_ref[...]).astype(BF16)
    for w_ref, o_ref in zip(w_refs, o_refs):
        n = w_ref.shape[1]
        step = min(n, n_chunk)
        for n0 in range(0, n, step):
            o_ref[:, n0:n0 + step] = _dot(h, w_ref[:, n0:n0 + step]).astype(o_ref.dtype)


def _norm_matmul(x2d, g, ws, out_dtypes, tm=512):
    m, k = x2d.shape
    tm = min(tm, m)
    in_specs = [pl.BlockSpec((tm, k), lambda i: (i, 0)), pl.BlockSpec((1, k), lambda i: (0, 0))]
    in_specs += [pl.BlockSpec(w.shape, lambda i: (0, 0)) for w in ws]
    out_specs = [pl.BlockSpec((tm, w.shape[1]), lambda i: (i, 0)) for w in ws]
    out_shape = [jax.ShapeDtypeStruct((m, w.shape[1]), dt) for w, dt in zip(ws, out_dtypes)]
    return pl.pallas_call(
        functools.partial(_norm_matmul_kernel, n_out=len(ws), n_chunk=1024),
        grid=(m // tm,), in_specs=in_specs, out_specs=out_specs, out_shape=out_shape,
        compiler_params=_cparams(1), name="norm_proj",
    )(x2d, g.reshape(1, k), *ws)


def _softplus(x):
    return jnp.maximum(x, 0.0) + jnp.log(1.0 + jnp.exp(-jnp.abs(x)))


def _ssd_kernel(xbc_ref, prev_ref, z_ref, dt_ref, dtT_ref, cw_ref, cb_ref, dtb_ref, dtbT_ref,
                alog_ref, alogT_ref, dskip_ref, nw_ref, y_ref, state_ref, xc_ref, ybuf_ref,
                *, n_heads, d_inner):
    L, P, N, G = SSD_CHUNK, SSD_HEAD_DIM, SSD_STATE, SSD_GROUPS
    conv_dim = xc_ref.shape[1]
    c = pl.program_id(1)

    @pl.when(c == 0)
    def _():
        state_ref[...] = jnp.zeros_like(state_ref)

    cw = 512
    prev_rows = prev_ref.shape[0]
    row = lax.broadcasted_iota(jnp.int32, (prev_rows, cw), 0)
    for j in range(conv_dim // cw):
        sl = slice(j * cw, (j + 1) * cw)
        cur = xbc_ref[:, sl].astype(F32)
        prev = jnp.where(c == 0, 0.0, prev_ref[:, sl].astype(F32))
        acc = cb_ref[:, sl] + cw_ref[SSD_CONV - 1:SSD_CONV, sl] * cur
        for k in range(1, SSD_CONV):
            rc = pltpu.roll(cur, k, 0)
            rp = pltpu.roll(prev, k, 0)
            head = jnp.where(row < k, rp, rc[:prev_rows])
            shifted = jnp.concatenate([head, rc[prev_rows:]], axis=0)
            acc = acc + cw_ref[SSD_CONV - 1 - k:SSD_CONV - k, sl] * shifted
        xc_ref[:, sl] = acc * _sigmoid(acc)

    dt = _softplus(dt_ref[:, 0:n_heads] + dtb_ref[...])
    a = dt * (-jnp.exp(alog_ref[...]))
    dtT = _softplus(dtT_ref[...] + dtbT_ref[...])
    aT = dtT * (-jnp.exp(alogT_ref[...]))
    ri = lax.broadcasted_iota(jnp.int32, (L, L), 0)
    ci = lax.broadcasted_iota(jnp.int32, (L, L), 1)
    causal = ci <= ri
    tri = jnp.where(causal, 1.0, 0.0).astype(BF16)
    triu = jnp.where(ri <= ci, 1.0, 0.0).astype(BF16)
    a_cs = sum(_dot(tri, ai) for ai in _split3(a))
    a_csT = sum(_dot(ai, triu) for ai in _split3(aT))
    lane = lax.broadcasted_iota(jnp.int32, (L, 2 * P), 1)
    lo_half = lane < P
    lane1 = lax.broadcasted_iota(jnp.int32, (1, 2 * P), 1) < P

    heads_per_group = n_heads // G
    for g in range(G):
        b_g = xc_ref[:, d_inner + g * N:d_inner + (g + 1) * N]
        c_g = xc_ref[:, d_inner + G * N + g * N:d_inner + G * N + (g + 1) * N]
        c_gb = c_g.astype(BF16)
        cb = _dot_nt(c_gb, b_g.astype(BF16))
        b_gT = b_g.T.astype(BF16)
        for pp in range(heads_per_group // 2):
            pair = g * (heads_per_group // 2) + pp
            h0, h1 = 2 * pair, 2 * pair + 1
            xs = xc_ref[:, pair * 2 * P:(pair + 1) * 2 * P]
            dt_pair = jnp.where(lo_half, dt[:, h0:h0 + 1], dt[:, h1:h1 + 1])
            xd = xs * dt_pair
            xd_b = xd.astype(BF16)
            ys = []
            for h in (h0, h1):
                seg = jnp.where(causal, a_cs[:, h:h + 1] - a_csT[h:h + 1, :], NEG_INF)
                mat = (cb * jnp.exp(seg)).astype(BF16)
                ys.append(_dot(mat, xd_b))
            y = jnp.where(lo_half, ys[0], ys[1])
            cs_pair = jnp.where(lo_half, a_cs[:, h0:h0 + 1], a_cs[:, h1:h1 + 1])
            st = state_ref[pair]
            y = y + _dot(c_gb, st.astype(BF16)) * jnp.exp(cs_pair)
            tot = jnp.where(lane1, a_csT[h0:h0 + 1, L - 1:L], a_csT[h1:h1 + 1, L - 1:L])
            xdd = (xd * jnp.exp(tot - cs_pair)).astype(BF16)
            state_ref[pair] = st * jnp.exp(tot) + _dot(b_gT, xdd)
            ybuf_ref[:, pair * 2 * P:(pair + 1) * 2 * P] = y + xs * dskip_ref[:, pair * 2 * P:(pair + 1) * 2 * P]

    gw = d_inner // G
    for g in range(G):
        sl = slice(g * gw, (g + 1) * gw)
        zz = z_ref[:, sl].astype(F32)
        yg = ybuf_ref[:, sl] * (zz * _sigmoid(zz))
        y_ref[:, sl] = _rms(yg, nw_ref[:, sl]).astype(y_ref.dtype)


def _ssd_mixer(z, xbc, small, conv_w, conv_b, dt_bias, a_log, d_skip, norm_w):
    b, s, d_inner = z.shape
    conv_dim = xbc.shape[-1]
    n_heads = d_inner // SSD_HEAD_DIM
    L = SSD_CHUNK
    nc = s // L
    prev_rows = 16
    xbc_prev = xbc.reshape(b, s // prev_rows, prev_rows, conv_dim)
    dtT = jnp.swapaxes(small[:, :, :n_heads], 1, 2)
    blocks_per_chunk = L // prev_rows
    kern = functools.partial(_ssd_kernel, n_heads=n_heads, d_inner=d_inner)
    const = lambda shape: pl.BlockSpec(shape, lambda i, j: (0,) * len(shape))
    return pl.pallas_call(
        kern, grid=(b, nc),
        in_specs=[
            pl.BlockSpec((None, L, conv_dim), lambda i, j: (i, j, 0)),
            pl.BlockSpec((None, None, prev_rows, conv_dim),
                         lambda i, j: (i, jnp.maximum(j * blocks_per_chunk - 1, 0), 0, 0)),
            pl.BlockSpec((None, L, d_inner), lambda i, j: (i, j, 0)),
            pl.BlockSpec((None, L, small.shape[-1]), lambda i, j: (i, j, 0)),
            pl.BlockSpec((None, n_heads, L), lambda i, j: (i, 0, j)),
            const((SSD_CONV, conv_dim)), const((1, conv_dim)),
            const((1, n_heads)), const((n_heads, 1)),
            const((1, n_heads)), const((n_heads, 1)),
            const((1, d_inner)), const((1, d_inner)),
        ],
        out_specs=pl.BlockSpec((None, L, d_inner), lambda i, j: (i, j, 0)),
        out_shape=jax.ShapeDtypeStruct((b, s, d_inner), BF16),
        scratch_shapes=[pltpu.VMEM((n_heads // 2, SSD_STATE, 2 * SSD_HEAD_DIM), F32),
                        pltpu.VMEM((L, conv_dim), F32),
                        pltpu.VMEM((L, d_inner), F32)],
        compiler_params=_cparams(2), name="ssd",
    )(xbc, xbc_prev, z, small, dtT, conv_w.astype(F32), conv_b.reshape(1, -1).astype(F32),
      dt_bias.reshape(1, -1), dt_bias.reshape(-1, 1), a_log.reshape(1, -1), a_log.reshape(-1, 1),
      jnp.repeat(d_skip, SSD_HEAD_DIM).reshape(1, -1), norm_w.reshape(1, -1))


def _compress_kernel(kr_ref, pos_ref, w1_ref, w2_ref, o_ref):
    nr, half = kr_ref.shape
    kr = kr_ref[...]
    p1 = _dot(kr, w1_ref[0:half, :])
    p2 = _dot(kr, w1_ref[half:2 * half, :])
    pb = _dot(pos_ref[...], w1_ref[...])[0:1]
    hid = p1 + pltpu.roll(p2, nr - 1, 0) + pb
    hid = hid * _sigmoid(hid)
    out = _dot(hid.astype(BF16), w2_ref[...])
    rows = lax.broadcasted_iota(jnp.int32, out.shape, 0)
    o_ref[...] = jnp.where(rows < nr - 1, out, 0.0).astype(o_ref.dtype)


def _compress(kr, pos, w1, w2p):
    two, b, g, nr, half = kr.shape
    hidden = w1.shape[-1]
    return pl.pallas_call(
        _compress_kernel,
        grid=(two, b, g),
        in_specs=[
            pl.BlockSpec((None, None, None, nr, half), lambda t, i, j: (t, i, j, 0, 0)),
            pl.BlockSpec((None, SUBLANES, 2 * half), lambda t, i, j: (t, 0, 0)),
            pl.BlockSpec((None, 2 * half, hidden), lambda t, i, j: (t, 0, 0)),
            pl.BlockSpec((None, hidden, LANES), lambda t, i, j: (t, 0, 0)),
        ],
        out_specs=pl.BlockSpec((None, None, None, nr, LANES), lambda t, i, j: (t, i, j, 0, 0)),
        out_shape=jax.ShapeDtypeStruct((two, b, g, nr, LANES), BF16),
        compiler_params=_cparams(3), name="nsa_compress",
    )(kr, pos, w1, w2p)


def _nsa_group(qi, qT_ref, kc_ref, vcT_ref, ov_ref, ksl_ref, vslT_ref, kw_ref, vwT_ref, cb_ref, tp_ref,
               gl_ref, o_ref, q0_ref, qa_ref, s_ref, pe_ref, acc_ref, n_slc, top_k):
    T, R, HD = NSA_TQ, NSA_REP, NSA_HEAD_DIM
    nr = kc_ref.shape[0]
    per_tile = T // CMP_STRIDE
    front = nr - per_tile
    half = LANES // 2

    qT = jnp.concatenate([qT_ref[r] for r in range(R)], axis=1)
    q0_ref[0:HD, :] = qT
    q0_ref[HD:2 * HD, :] = jnp.zeros((HD, R * T), BF16)
    qa_ref[0:HD, :] = qT

    def colmax(a):
        return jnp.max(a, axis=0, keepdims=True)

    def colsum(a):
        return jnp.sum(a, axis=0, keepdims=True)

    n_win = WINDOW // T
    scores, offsets = [], []
    for rel in range(n_win, -1, -1):
        tile = TILE_EDGE if rel == n_win else (TILE_ZERO if rel >= 2 else (TILE_NEAR if rel == 1 else TILE_DIAG))
        if rel > 0:
            tile = jnp.where(qi >= rel, tile, TILE_NONE)
        ko = pl.multiple_of(jnp.maximum(qi - rel, 0) * T, T)
        scores.append(_dot(kw_ref[pl.ds(ko, T), :], q0_ref[...]) + tp_ref[tile])
        offsets.append(ko)
    m_w = functools.reduce(jnp.maximum, [colmax(s) for s in scores])
    l_w = jnp.zeros((1, R * T), F32)
    acc_w = jnp.zeros((HD, R * T), F32)
    for s, ko in zip(scores, offsets):
        pe = jnp.exp2(s - m_w)
        l_w = l_w + colsum(pe)
        acc_w = acc_w + _dot(vwT_ref[:, pl.ds(ko, T)], pe.astype(BF16))
    o_win = acc_w * (1.0 / l_w)

    off = pl.multiple_of(front - qi * per_tile, SUBLANES)
    lg = _dot(kc_ref[...], q0_ref[...]) + cb_ref[pl.ds(off, nr), :]
    m_c = colmax(lg)
    e = jnp.exp2(lg - m_c)
    p = e * jnp.where(m_c > 0.5 * NEG_INF, 1.0 / colsum(e), 0.0)
    o_cmp = _dot(vcT_ref[...], p.astype(BF16))

    psum = p[:, 0:T]
    for r in range(1, R):
        psum = psum + p[:, r * T:(r + 1) * T]
    imp = sum(_dot(ov_ref[...], pi) for pi in _split3(psum))
    nio = lax.broadcasted_iota(jnp.int32, (half, T), 0)
    tio = lax.broadcasted_iota(jnp.int32, (half, T), 1)
    tb = (qi * T + tio) // SLC_BLOCK
    forced = (nio == 0) | (nio == tb) | (nio == tb - 1)
    imp = jnp.where(forced, FORCE_SCORE, imp)
    imp = jnp.where(nio > tb, NEG_INF, imp)
    n_grp = half // SUBLANES
    grp = [imp[SUBLANES * v:SUBLANES * (v + 1)] for v in range(n_grp)]
    nio8 = lax.broadcasted_iota(jnp.int32, (SUBLANES, T), 0)
    rank = [jnp.zeros((SUBLANES, T), F32) for _ in range(n_grp)]
    for mm in range(n_slc):
        rowv = imp[mm:mm + 1, :]
        for v in range(n_grp):
            if SUBLANES * v > mm:
                hit = jnp.where(rowv >= grp[v], 1.0, 0.0)
            elif SUBLANES * v + SUBLANES - 1 < mm:
                hit = jnp.where(rowv > grp[v], 1.0, 0.0)
            else:
                hit = jnp.where(nio8 > mm - SUBLANES * v, jnp.where(rowv >= grp[v], 1.0, 0.0),
                                jnp.where(rowv > grp[v], 1.0, 0.0))
            rank[v] = rank[v] + hit
    rank = jnp.concatenate(rank, axis=0)
    pen = jnp.where(rank < top_k, jnp.where(nio <= tb, 0.0, SEL_PENALTY), SEL_PENALTY).astype(BF16)
    qa_ref[HD:2 * HD, :] = jnp.concatenate([pen] * R, axis=1)

    krow = lax.broadcasted_iota(jnp.int32, (T, LANES), 0)
    klane = lax.broadcasted_iota(jnp.int32, (T, LANES), 1)

    def sel_keys(kj):
        kt = ksl_ref[pl.ds(pl.multiple_of(kj * T, T), T), :]
        blk = kj * (T // SLC_BLOCK) + krow // SLC_BLOCK
        return jnp.where(klane >= half, jnp.where(klane - half == blk, 1.0, 0.0).astype(BF16), kt)

    def sel_scores(kj):
        return _dot(sel_keys(kj), qa_ref[...])

    def sel_values(kj):
        return vslT_ref[:, pl.ds(pl.multiple_of(kj * T, T), T)]

    n_far = jnp.maximum(qi - 1, 0)

    def far_step(kj, slot, carry):
        cm_cur, alpha_prev, m_old, l_old = carry
        acc_ref[...] = alpha_prev * acc_ref[...] + _dot(sel_values(jnp.maximum(kj - 1, 0)), pe_ref[1 - slot])
        s_next = sel_scores(kj + 1)
        s_ref[1 - slot] = s_next
        m_new = jnp.maximum(m_old, cm_cur)
        alpha = jnp.exp2(m_old - m_new)
        pe = jnp.exp2(s_ref[slot] - m_new)
        pe_ref[slot] = pe.astype(BF16)
        return colmax(s_next), alpha, m_new, alpha * l_old + colsum(pe)

    def far_pair(j, carry):
        return far_step(2 * j + 1, 1, far_step(2 * j, 0, carry))

    s_first = sel_scores(0)
    s_ref[0] = s_first
    pe_ref[1] = jnp.zeros((T, R * T), BF16)
    acc_ref[...] = jnp.zeros((HD, R * T), F32)
    init = (colmax(s_first), jnp.ones((1, R * T), F32), jnp.full((1, R * T), NEG_INF, F32),
            jnp.zeros((1, R * T), F32))

    def finish(carry):
        carry = lax.cond(n_far % 2 == 1, lambda c: far_step(n_far - 1, 0, c), lambda c: c, carry)
        _, alpha_prev, m_old, l_old = carry
        acc = alpha_prev * acc_ref[...] + _dot(sel_values(jnp.maximum(n_far - 1, 0)), pe_ref[(n_far + 1) % 2])
        s_near = s_ref[n_far % 2] + tp_ref[jnp.where(qi >= 1, TILE_NEAR, TILE_NONE)]
        s_diag = sel_scores(qi) + tp_ref[TILE_DIAG]
        m_new = jnp.maximum(m_old, jnp.maximum(colmax(s_near), colmax(s_diag)))
        alpha = jnp.exp2(m_old - m_new)
        pe_n = jnp.exp2(s_near - m_new)
        pe_d = jnp.exp2(s_diag - m_new)
        l_s = alpha * l_old + colsum(pe_n) + colsum(pe_d)
        acc = (alpha * acc + _dot(sel_values(n_far), pe_n.astype(BF16))
               + _dot(sel_values(qi), pe_d.astype(BF16)))
        o_slc = acc * (1.0 / l_s)

        sg = _sigmoid(gl_ref[...])
        zpad = jnp.zeros((LANES - HD, T), F32)
        for r in range(R):
            cols = slice(r * T, (r + 1) * T)
            o = (sg[r:r + 1] * o_cmp[:, cols] + sg[R + r:R + r + 1] * o_slc[:, cols]
                 + sg[2 * R + r:2 * R + r + 1] * o_win[:, cols])
            o_ref[:, LANES * r:LANES * (r + 1)] = jnp.concatenate([o, zpad], axis=0).T.astype(o_ref.dtype)

    return init, far_pair, finish


def _nsa_kernel(qT_ref, kc_ref, vcT_ref, ov_ref, ksl_ref, vslT_ref, kw_ref, vwT_ref, cb_ref, tp_ref,
                gl_ref, o_ref, q0_ref, qa_ref, s_ref, pe_ref, acc_ref, *, gb, n_slc, top_k):
    R, HD = NSA_REP, NSA_HEAD_DIM
    qi = pl.program_id(2)
    parts = []
    for g in range(gb):
        lanes = slice(g * LANES, (g + 1) * LANES)
        parts.append(_nsa_group(
            qi, qT_ref.at[g * R:(g + 1) * R], kc_ref.at[g], vcT_ref.at[g], ov_ref, ksl_ref.at[:, lanes],
            vslT_ref.at[g], kw_ref.at[:, lanes], vwT_ref.at[g], cb_ref.at[g], tp_ref.at[g], gl_ref.at[g],
            o_ref.at[:, g * R * LANES:(g + 1) * R * LANES], q0_ref.at[g], qa_ref.at[g],
            s_ref.at[g], pe_ref.at[g], acc_ref.at[g], n_slc, top_k))
    carries = lax.fori_loop(
        0, jnp.maximum(qi - 1, 0) // 2,
        lambda j, cs: tuple(body(j, c) for (_, body, _), c in zip(parts, cs)),
        tuple(init for init, _, _ in parts))
    for (_, _, finish), c in zip(parts, carries):
        finish(c)


def _t5_bucket_np(dist):
    n = np.maximum(dist, 0)
    max_exact = REL_BUCKETS // 2
    nf = np.maximum(n, 1).astype(np.float32)
    large = max_exact + (np.log(nf / np.float32(max_exact)) / np.float32(math.log(REL_MAX_DIST / max_exact))
                         * np.float32(REL_BUCKETS - max_exact)).astype(np.int32)
    large = np.minimum(large, REL_BUCKETS - 1)
    return np.where(n < max_exact, n, large).astype(np.int32)


def _nsa_tables(rel_bias, nr):
    T, G, R = NSA_TQ, NSA_KV_HEADS, NSA_REP
    table = rel_bias.astype(F32) * LOG2E
    per_tile = T // CMP_STRIDE
    front = nr - per_tile
    t = np.arange(T)[None, :]

    def lay(a):
        keys = a.shape[0]
        return jnp.transpose(a, (2, 0, 1)).reshape(G, R, keys, T).transpose(0, 2, 1, 3).reshape(G, keys, R * T)

    cend = CMP_STRIDE * (np.arange(front + nr)[:, None] - front) + CMP_BLOCK - 1
    dist = t - cend
    cb = jnp.where((dist >= 0)[..., None], table[_t5_bucket_np(dist)], NEG_INF)
    k = np.arange(T)[:, None]
    assert 2 * T - (T - 1) >= REL_MAX_DIST and WINDOW % T == 0 and WINDOW // T >= 2
    last = table[REL_BUCKETS - 1]
    diag = jnp.where((t - k >= 0)[..., None], table[_t5_bucket_np(t - k)] - last, NEG_INF)
    near = table[_t5_bucket_np(T + t - k)] - last
    none = jnp.full_like(near, NEG_INF)
    edge = jnp.where((k > t)[..., None], jnp.zeros_like(near), NEG_INF)
    tiles = [None] * 5
    tiles[TILE_DIAG], tiles[TILE_NEAR], tiles[TILE_NONE], tiles[TILE_EDGE] = diag, near, none, edge
    tiles[TILE_ZERO] = jnp.zeros_like(near)
    tp = jnp.stack([lay(a) for a in tiles], axis=1)
    return lay(cb), tp


def _overlap_matrix(nr, n_slc):
    half = LANES // 2
    n = np.arange(half)[:, None]
    c = np.arange(nr)[None, :]
    ov = ((CMP_STRIDE * c <= SLC_BLOCK * n + SLC_BLOCK - 1) & (CMP_STRIDE * c + CMP_BLOCK - 1 >= SLC_BLOCK * n)
          & (n < n_slc) & (c < nr - 1))
    return jnp.asarray(ov.astype(np.float32), dtype=BF16)


def _nsa_attention(qT, kc, vcT, ksl, vslT, kw, vwT, cb, tp, glT):
    b, n_heads, hd, s = qT.shape
    G, R, T = NSA_KV_HEADS, NSA_REP, NSA_TQ
    nr = kc.shape[-2]
    n_slc = s // SLC_BLOCK
    assert n_slc <= LANES // 2 and s % T == 0 and nr % SUBLANES == 0
    ov = _overlap_matrix(nr, n_slc)
    gb = NSA_GROUPS_PER_STEP
    assert G % gb == 0
    kern = functools.partial(_nsa_kernel, gb=gb, n_slc=n_slc, top_k=min(SLC_TOPK, n_slc))
    k_spec = pl.BlockSpec((None, s, gb * LANES), lambda i, j, k: (i, 0, j))
    vT_spec = pl.BlockSpec((None, gb, hd, s), lambda i, j, k: (i, j, 0, 0))
    return pl.pallas_call(
        kern, grid=(b, G // gb, s // T),
        in_specs=[
            pl.BlockSpec((None, gb * R, hd, T), lambda i, j, k: (i, j, 0, k)),
            pl.BlockSpec((None, gb, nr, LANES), lambda i, j, k: (i, j, 0, 0)),
            pl.BlockSpec((None, gb, hd, nr), lambda i, j, k: (i, j, 0, 0)),
            pl.BlockSpec(ov.shape, lambda i, j, k: (0, 0)),
            k_spec, vT_spec, k_spec, vT_spec,
            pl.BlockSpec((gb,) + cb.shape[1:], lambda i, j, k: (j, 0, 0)),
            pl.BlockSpec((gb,) + tp.shape[1:], lambda i, j, k: (j, 0, 0, 0)),
            pl.BlockSpec((None, gb, 16, T), lambda i, j, k: (i, j, 0, k)),
        ],
        out_specs=pl.BlockSpec((None, T, gb * R * LANES), lambda i, j, k: (i, k, j)),
        out_shape=jax.ShapeDtypeStruct((b, s, G * R * LANES), BF16),
        scratch_shapes=[pltpu.VMEM((gb, 2 * hd, R * T), BF16), pltpu.VMEM((gb, 2 * hd, R * T), BF16),
                        pltpu.VMEM((gb, 2, T, R * T), F32), pltpu.VMEM((gb, 2, T, R * T), BF16),
                        pltpu.VMEM((gb, hd, R * T), F32)],
        compiler_params=_cparams(3), name="nsa_attention",
    )(qT, kc, vcT, ov, ksl, vslT, kw, vwT, cb, tp, glT)


def _merge_kernel(x_ref, ys_ref, yn_ref, mg_ref, ws_ref, wn_ref, wo_ref, g_ref, o_ref):
    d = x_ref.shape[1]
    a = _dot(ys_ref[...], ws_ref[...])
    bb = _dot(yn_ref[...], wn_ref[...])
    mg = _sigmoid(mg_ref[...].astype(F32))
    mixed = mg[:, :d] * a + mg[:, d:] * bb
    o = _dot(mixed.astype(BF16), wo_ref[...])
    o_ref[...] = x_ref[...] + _rms(o, g_ref[...])


def _merge(x2d, y_ssd, y_nsa, mg, w_s, w_n, w_o, g, tm=512):
    m, d = x2d.shape
    tm = min(tm, m)
    row = lambda a: pl.BlockSpec((tm, a.shape[1]), lambda i: (i, 0))
    full = lambda a: pl.BlockSpec(a.shape, lambda i: (0, 0))
    g = g.reshape(1, d)
    return pl.pallas_call(
        _merge_kernel, grid=(m // tm,),
        in_specs=[row(x2d), row(y_ssd), row(y_nsa), row(mg), full(w_s), full(w_n), full(w_o), full(g)],
        out_specs=pl.BlockSpec((tm, d), lambda i: (i, 0)),
        out_shape=jax.ShapeDtypeStruct((m, d), F32),
        compiler_params=_cparams(1), name="merge",
    )(x2d, y_ssd, y_nsa, mg, w_s, w_n, w_o, g)


def _xattn_kernel(x_ref, kv_ref, gpre_ref, wq_ref, wo_ref, gpost_ref, o_ref):
    x = x_ref[...]
    h = _rms(x, gpre_ref[...]).astype(BF16)
    q = _dot(h, wq_ref[...])
    width = X_HEADS * X_HEAD_DIM
    scale = X_HEAD_DIM ** -0.5
    outs = []
    for hh in range(X_HEADS):
        sl = slice(hh * X_HEAD_DIM, (hh + 1) * X_HEAD_DIM)
        lg = _dot_nt(q[:, sl].astype(BF16), kv_ref[:, sl]) * scale
        e = jnp.exp(lg - jnp.max(lg, axis=-1, keepdims=True))
        p = e / jnp.sum(e, axis=-1, keepdims=True)
        outs.append(_dot(p.astype(BF16), kv_ref[:, width + hh * X_HEAD_DIM:width + (hh + 1) * X_HEAD_DIM]))
    o = _dot(jnp.concatenate(outs, axis=-1).astype(BF16), wo_ref[...])
    o_ref[...] = x + _rms(o, gpost_ref[...])


def _xattn(x3d, kv, g_pre, w_q, w_o, g_post, tm=512):
    b, s, d = x3d.shape
    tm = min(tm, s)
    ml = kv.shape[1]
    full = lambda a: pl.BlockSpec(a.shape, lambda i, j: (0, 0))
    g_pre, g_post = g_pre.reshape(1, d), g_post.reshape(1, d)
    return pl.pallas_call(
        _xattn_kernel, grid=(b, s // tm),
        in_specs=[pl.BlockSpec((None, tm, d), lambda i, j: (i, j, 0)),
                  pl.BlockSpec((None, ml, kv.shape[2]), lambda i, j: (i, 0, 0)),
                  full(g_pre), full(w_q), full(w_o), full(g_post)],
        out_specs=pl.BlockSpec((None, tm, d), lambda i, j: (i, j, 0)),
        out_shape=jax.ShapeDtypeStruct((b, s, d), F32),
        compiler_params=_cparams(2), name="xattn",
    )(x3d, kv, g_pre, w_q, w_o, g_post)


def _mlp_kernel(x_ref, gpre_ref, w1_ref, w2_ref, gpost_ref, o_ref, *, f_chunk):
    x = x_ref[...]
    h = _rms(x, gpre_ref[...]).astype(BF16)
    d_ff = w1_ref.shape[1]
    acc = jnp.zeros(x.shape, F32)
    for f0 in range(0, d_ff, f_chunk):
        u = jnp.maximum(_dot(h, w1_ref[:, f0:f0 + f_chunk]), 0.0)
        acc = acc + _dot((u * u).astype(BF16), w2_ref[f0:f0 + f_chunk, :])
    o_ref[...] = x + _rms(acc, gpost_ref[...])


def _mlp(x2d, g_pre, w1, w2, g_post, tm=512):
    m, d = x2d.shape
    tm = min(tm, m)
    full = lambda a: pl.BlockSpec(a.shape, lambda i: (0, 0))
    g_pre, g_post = g_pre.reshape(1, d), g_post.reshape(1, d)
    return pl.pallas_call(
        functools.partial(_mlp_kernel, f_chunk=1024), grid=(m // tm,),
        in_specs=[pl.BlockSpec((tm, d), lambda i: (i, 0)), full(g_pre), full(w1), full(w2), full(g_post)],
        out_specs=pl.BlockSpec((tm, d), lambda i: (i, 0)),
        out_shape=jax.ShapeDtypeStruct((m, d), F32),
        compiler_params=_cparams(1), name="mlp",
    )(x2d, g_pre, w1, w2, g_post)


def _pad_heads(w, n_heads, hd):
    k = w.shape[0]
    w = w.reshape(k, n_heads, hd)
    return jnp.pad(w, ((0, 0), (0, 0), (0, LANES - hd))).reshape(k, n_heads * LANES)


def _heads_major(a, b, s, n_heads, hd):
    return jnp.transpose(a.reshape(b, s, n_heads, hd), (0, 2, 3, 1))


def _layer(x, mem, w_in, conv_w, conv_b, dt_bias, a_log, d_skip, ssd_norm, cmp_pos, cmp_w1, cmp_w2,
           rel_bias, w_br_ssd, w_br_nsa, w_out, w_xq, w_xkv, w_xo, w_ff1, w_ff2,
           n_mix_pre, n_mix_post, n_x_pre, n_x_post, n_mem, n_ffn_pre, n_ffn_post):
    b, s, d = x.shape
    G, R, HD = NSA_KV_HEADS, NSA_REP, NSA_HEAD_DIM
    d_inner = 2 * d
    n_ssd_heads = d_inner // SSD_HEAD_DIM
    conv_dim = d_inner + 2 * SSD_GROUPS * SSD_STATE
    nsa_w, kv_w = NSA_HEADS * HD, G * HD
    sizes = (d_inner, conv_dim, n_ssd_heads, nsa_w, 6 * kv_w, 3 * NSA_HEADS, 2 * d)
    offs = np.concatenate([[0], np.cumsum(sizes)])
    seg = lambda i: w_in[:, offs[i]:offs[i + 1]]
    w_z, w_xbc, w_dt, w_q, w_kv, w_gate, w_mg = (seg(i) for i in range(7))
    w_small = jnp.pad(jnp.concatenate([w_dt, w_gate], axis=1),
                      ((0, 0), (0, LANES - n_ssd_heads - 3 * NSA_HEADS)))
    w_kv6 = w_kv.reshape(d, 6, kv_w)
    w_cmp = jnp.concatenate([w_kv6[:, 0], w_kv6[:, 1]], axis=1)
    bf = lambda a: a.astype(BF16)

    x2d = x.reshape(b * s, d)
    z, xbc = _norm_matmul(x2d, n_mix_pre, [bf(w_z), bf(w_xbc)], [BF16, BF16])
    small, q, kv_cmp, vsl, vw = _norm_matmul(
        x2d, n_mix_pre, [bf(w_small), bf(w_q * (HD ** -0.5 * LOG2E)), bf(w_cmp), bf(w_kv6[:, 3]), bf(w_kv6[:, 5])],
        [F32, BF16, BF16, BF16, BF16])
    ksl, kw, mg = _norm_matmul(
        x2d, n_mix_pre, [bf(_pad_heads(w_kv6[:, 2], G, HD)), bf(_pad_heads(w_kv6[:, 4], G, HD)), bf(w_mg)],
        [BF16] * 3)

    y_ssd = _ssd_mixer(z.reshape(b, s, -1), xbc.reshape(b, s, -1), small.reshape(b, s, -1),
                       conv_w, conv_b, dt_bias, a_log, d_skip, ssd_norm)

    nr = s // CMP_STRIDE
    kr = kv_cmp.reshape(b, nr, CMP_STRIDE, 2, G, HD)
    kr = jnp.transpose(kr, (3, 0, 4, 1, 2, 5)).reshape(2, b, G, nr, CMP_STRIDE * HD)
    pos = jnp.broadcast_to(bf(cmp_pos).reshape(2, 1, CMP_BLOCK * HD), (2, SUBLANES, CMP_BLOCK * HD))
    w2p = jnp.pad(bf(cmp_w2), ((0, 0), (0, 0), (0, LANES - HD)))
    kvc = _compress(kr, pos, bf(cmp_w1), w2p)
    vcT = jnp.swapaxes(kvc[1][..., :HD], -1, -2)
    cb, tp = _nsa_tables(rel_bias, nr)
    gl = small.reshape(b, s, -1)[:, :, n_ssd_heads:n_ssd_heads + 3 * NSA_HEADS]
    glT = jnp.transpose(gl.reshape(b, s, 3, G, R), (0, 3, 2, 4, 1)).reshape(b, G, 3 * R, s)
    glT = jnp.pad(glT, ((0, 0), (0, 0), (0, 16 - 3 * R), (0, 0)))
    y_nsa = _nsa_attention(_heads_major(q, b, s, NSA_HEADS, HD), kvc[0], vcT,
                           ksl.reshape(b, s, -1), _heads_major(vsl, b, s, G, HD),
                           kw.reshape(b, s, -1), _heads_major(vw, b, s, G, HD), cb, tp, glT)

    w_n_ext = jnp.pad(w_br_nsa.reshape(NSA_HEADS, HD, d), ((0, 0), (0, LANES - HD), (0, 0)))
    x1 = _merge(x2d, y_ssd.reshape(b * s, -1), y_nsa.reshape(b * s, -1), mg,
                bf(w_br_ssd), bf(w_n_ext.reshape(NSA_HEADS * LANES, d)), bf(w_out), n_mix_post)

    (kv_mem,) = _norm_matmul(mem.reshape(-1, d), n_mem, [bf(w_xkv)], [BF16])
    x2 = _xattn(x1.reshape(b, s, d), kv_mem.reshape(b, mem.shape[1], -1), n_x_pre, bf(w_xq), bf(w_xo),
                n_x_post)

    x3 = _mlp(x2.reshape(b * s, d), n_ffn_pre, bf(w_ff1), bf(w_ff2), n_ffn_post)
    return x3.reshape(b, s, d)


def kernel(x, mem, w_in, ssd_conv_w, ssd_conv_b, ssd_dt_bias, ssd_a_log, ssd_d_skip, ssd_norm, cmp_pos,
           cmp_w1, cmp_w2, rel_bias, w_br_ssd, w_br_nsa, w_out, w_xq, w_xkv, w_xo, w_ff1, w_ff2,
           norm_mix_pre, norm_mix_post, norm_x_pre, norm_x_post, norm_mem, norm_ffn_pre, norm_ffn_post):
    for l in range(w_in.shape[0]):
        x = _layer(x, mem, w_in[l], ssd_conv_w[l], ssd_conv_b[l], ssd_dt_bias[l], ssd_a_log[l],
                   ssd_d_skip[l], ssd_norm[l], cmp_pos[l], cmp_w1[l], cmp_w2[l], rel_bias,
                   w_br_ssd[l], w_br_nsa[l], w_out[l], w_xq[l], w_xkv[l], w_xo[l], w_ff1[l], w_ff2[l],
                   norm_mix_pre[l], norm_mix_post[l], norm_x_pre[l], norm_x_post[l], norm_mem[l],
                   norm_ffn_pre[l], norm_ffn_post[l])
    return x
```

```python
import functools
import math

import numpy as np
import jax
import jax.numpy as jnp
from jax import lax
from jax.experimental import pallas as pl
from jax.experimental.pallas import tpu as pltpu

F32 = jnp.float32
BF16 = jnp.bfloat16

NORM_EPS = 1e-6
NEG_INF = -1e30
FORCE_SCORE = 1e9
SEL_PENALTY = -1e9

LANES = 128
SUBLANES = 8
VMEM_LIMIT = 56 * 1024 * 1024

SSD_HEAD_DIM = 64
SSD_GROUPS = 8
SSD_STATE = 128
SSD_CONV = 4
SSD_CHUNK = 128
NSA_HEADS = 16
NSA_KV_HEADS = 4
NSA_HEAD_DIM = 64
NSA_REP = NSA_HEADS // NSA_KV_HEADS
CMP_BLOCK = 32
CMP_STRIDE = 16
SLC_BLOCK = 64
SLC_TOPK = 16
WINDOW = 512
NSA_TQ = 256
NSA_GROUPS_PER_STEP = 1
LOG2E = math.log2(math.e)
REL_BUCKETS = 32
REL_MAX_DIST = 128
X_HEADS = 4
X_HEAD_DIM = 128

TILE_DIAG, TILE_NEAR, TILE_NONE, TILE_EDGE, TILE_ZERO = range(5)


def _cparams(n_grid):
    return pltpu.CompilerParams(dimension_semantics=("arbitrary",) * n_grid,
                                vmem_limit_bytes=VMEM_LIMIT)


def _sigmoid(x):
    return 1.0 / (1.0 + jnp.exp(-x))


def _rms(x, g):
    return x * lax.rsqrt(jnp.mean(x * x, axis=-1, keepdims=True) + NORM_EPS) * g


def _dot(a, b):
    return jnp.dot(a, b, preferred_element_type=F32)


def _dot_nt(a, b):
    return lax.dot_general(a, b, (((1,), (1,)), ((), ())), preferred_element_type=F32)


def _split3(x):
    x1 = x.astype(BF16)
    r1 = x - x1.astype(F32)
    x2 = r1.astype(BF16)
    x3 = (r1 - x2.astype(F32)).astype(BF16)
    return x1, x2, x3


def _norm_matmul_kernel(x_ref, g_ref, *refs, transposed, n_chunk):
    n_out = len(transposed)
    w_refs, o_refs = refs[:n_out], refs[n_out:]
    h = _rms(x_ref[...], g_ref[...]).astype(BF16)
    for w_ref, o_ref, tr in zip(w_refs, o_refs, transposed):
        if tr:
            o_ref[...] = _dot_nt(w_ref[...], h).astype(o_ref.dtype)
            continue
        n = w_ref.shape[1]
        step = min(n, n_chunk)
        for n0 in range(0, n, step):
            o_ref[:, n0:n0 + step] = _dot(h, w_ref[:, n0:n0 + step]).astype(o_ref.dtype)


def _norm_matmul(x2d, g, ws, out_dtypes, transposed=None, tm=512):
    m, k = x2d.shape
    tm = min(tm, m)
    transposed = tuple(transposed) if transposed is not None else (False,) * len(ws)
    in_specs = [pl.BlockSpec((tm, k), lambda i: (i, 0)), pl.BlockSpec((1, k), lambda i: (0, 0))]
    in_specs += [pl.BlockSpec(w.shape, lambda i: (0, 0)) for w in ws]
    out_specs, out_shape = [], []
    for w, dt, tr in zip(ws, out_dtypes, transposed):
        if tr:
            out_specs.append(pl.BlockSpec((w.shape[0], tm), lambda i: (0, i)))
            out_shape.append(jax.ShapeDtypeStruct((w.shape[0], m), dt))
        else:
            out_specs.append(pl.BlockSpec((tm, w.shape[1]), lambda i: (i, 0)))
            out_shape.append(jax.ShapeDtypeStruct((m, w.shape[1]), dt))
    return pl.pallas_call(
        functools.partial(_norm_matmul_kernel, transposed=transposed, n_chunk=1024),
        grid=(m // tm,), in_specs=in_specs, out_specs=out_specs, out_shape=out_shape,
        compiler_params=_cparams(1), name="norm_proj",
    )(x2d, g.reshape(1, k), *ws)


def _softplus(x):
    return jnp.maximum(x, 0.0) + jnp.log(1.0 + jnp.exp(-jnp.abs(x)))


def _ssd_kernel(xbc_ref, prev_ref, z_ref, dt_ref, dtT_ref, cw_ref, cb_ref, dtb_ref, dtbT_ref,
                alog_ref, alogT_ref, dskip_ref, nw_ref, y_ref, state_ref, xc_ref, ybuf_ref,
                *, n_heads, d_inner):
    L, P, N, G = SSD_CHUNK, SSD_HEAD_DIM, SSD_STATE, SSD_GROUPS
    conv_dim = xc_ref.shape[1]
    c = pl.program_id(1)

    @pl.when(c == 0)
    def _():
        state_ref[...] = jnp.zeros_like(state_ref)

    cw = 512
    prev_rows = prev_ref.shape[0]
    row = lax.broadcasted_iota(jnp.int32, (prev_rows, cw), 0)
    for j in range(conv_dim // cw):
        sl = slice(j * cw, (j + 1) * cw)
        cur = xbc_ref[:, sl].astype(F32)
        prev = jnp.where(c == 0, 0.0, prev_ref[:, sl].astype(F32))
        acc = cb_ref[:, sl] + cw_ref[SSD_CONV - 1:SSD_CONV, sl] * cur
        for k in range(1, SSD_CONV):
            rc = pltpu.roll(cur, k, 0)
            rp = pltpu.roll(prev, k, 0)
            head = jnp.where(row < k, rp, rc[:prev_rows])
            shifted = jnp.concatenate([head, rc[prev_rows:]], axis=0)
            acc = acc + cw_ref[SSD_CONV - 1 - k:SSD_CONV - k, sl] * shifted
        xc_ref[:, sl] = acc * _sigmoid(acc)

    dt = _softplus(dt_ref[:, 0:n_heads] + dtb_ref[...])
    a = dt * (-jnp.exp(alog_ref[...]))
    dtT = _softplus(dtT_ref[...] + dtbT_ref[...])
    aT = dtT * (-jnp.exp(alogT_ref[...]))
    ri = lax.broadcasted_iota(jnp.int32, (L, L), 0)
    ci = lax.broadcasted_iota(jnp.int32, (L, L), 1)
    causal = ci <= ri
    tri = jnp.where(causal, 1.0, 0.0).astype(BF16)
    triu = jnp.where(ri <= ci, 1.0, 0.0).astype(BF16)
    a_cs = sum(_dot(tri, ai) for ai in _split3(a))
    a_csT = sum(_dot(ai, triu) for ai in _split3(aT))
    lane = lax.broadcasted_iota(jnp.int32, (L, 2 * P), 1)
    lo_half = lane < P
    lane1 = lax.broadcasted_iota(jnp.int32, (1, 2 * P), 1) < P

    heads_per_group = n_heads // G
    for g in range(G):
        b_g = xc_ref[:, d_inner + g * N:d_inner + (g + 1) * N]
        c_g = xc_ref[:, d_inner + G * N + g * N:d_inner + G * N + (g + 1) * N]
        c_gb = c_g.astype(BF16)
        cb = _dot_nt(c_gb, b_g.astype(BF16))
        b_gT = b_g.T.astype(BF16)
        for pp in range(heads_per_group // 2):
            pair = g * (heads_per_group // 2) + pp
            h0, h1 = 2 * pair, 2 * pair + 1
            xs = xc_ref[:, pair * 2 * P:(pair + 1) * 2 * P]
            dt_pair = jnp.where(lo_half, dt[:, h0:h0 + 1], dt[:, h1:h1 + 1])
            xd = xs * dt_pair
            xd_b = xd.astype(BF16)
            ys = []
            for h in (h0, h1):
                seg = jnp.where(causal, a_cs[:, h:h + 1] - a_csT[h:h + 1, :], NEG_INF)
                mat = (cb * jnp.exp(seg)).astype(BF16)
                ys.append(_dot(mat, xd_b))
            y = jnp.where(lo_half, ys[0], ys[1])
            cs_pair = jnp.where(lo_half, a_cs[:, h0:h0 + 1], a_cs[:, h1:h1 + 1])
            st = state_ref[pair]
            y = y + _dot(c_gb, st.astype(BF16)) * jnp.exp(cs_pair)
            tot = jnp.where(lane1, a_csT[h0:h0 + 1, L - 1:L], a_csT[h1:h1 + 1, L - 1:L])
            xdd = (xd * jnp.exp(tot - cs_pair)).astype(BF16)
            state_ref[pair] = st * jnp.exp(tot) + _dot(b_gT, xdd)
            ybuf_ref[:, pair * 2 * P:(pair + 1) * 2 * P] = y + xs * dskip_ref[:, pair * 2 * P:(pair + 1) * 2 * P]

    gw = d_inner // G
    for g in range(G):
        sl = slice(g * gw, (g + 1) * gw)
        zz = z_ref[:, sl].astype(F32)
        yg = ybuf_ref[:, sl] * (zz * _sigmoid(zz))
        y_ref[:, sl] = _rms(yg, nw_ref[:, sl]).astype(y_ref.dtype)


def _ssd_mixer(z, xbc, small, dtT, conv_w, conv_b, dt_bias, a_log, d_skip, norm_w):
    b, s, d_inner = z.shape
    conv_dim = xbc.shape[-1]
    n_heads = d_inner // SSD_HEAD_DIM
    L = SSD_CHUNK
    nc = s // L
    prev_rows = 16
    xbc_prev = xbc.reshape(b, s // prev_rows, prev_rows, conv_dim)
    blocks_per_chunk = L // prev_rows
    kern = functools.partial(_ssd_kernel, n_heads=n_heads, d_inner=d_inner)
    const = lambda shape: pl.BlockSpec(shape, lambda i, j: (0,) * len(shape))
    return pl.pallas_call(
        kern, grid=(b, nc),
        in_specs=[
            pl.BlockSpec((None, L, conv_dim), lambda i, j: (i, j, 0)),
            pl.BlockSpec((None, None, prev_rows, conv_dim),
                         lambda i, j: (i, jnp.maximum(j * blocks_per_chunk - 1, 0), 0, 0)),
            pl.BlockSpec((None, L, d_inner), lambda i, j: (i, j, 0)),
            pl.BlockSpec((None, L, small.shape[-1]), lambda i, j: (i, j, 0)),
            pl.BlockSpec((n_heads, L), lambda i, j: (0, i * nc + j)),
            const((SSD_CONV, conv_dim)), const((1, conv_dim)),
            const((1, n_heads)), const((n_heads, 1)),
            const((1, n_heads)), const((n_heads, 1)),
            const((1, d_inner)), const((1, d_inner)),
        ],
        out_specs=pl.BlockSpec((None, L, d_inner), lambda i, j: (i, j, 0)),
        out_shape=jax.ShapeDtypeStruct((b, s, d_inner), BF16),
        scratch_shapes=[pltpu.VMEM((n_heads // 2, SSD_STATE, 2 * SSD_HEAD_DIM), F32),
                        pltpu.VMEM((L, conv_dim), F32),
                        pltpu.VMEM((L, d_inner), F32)],
        compiler_params=_cparams(2), name="ssd",
    )(xbc, xbc_prev, z, small, dtT, conv_w.astype(F32), conv_b.reshape(1, -1).astype(F32),
      dt_bias.reshape(1, -1), dt_bias.reshape(-1, 1), a_log.reshape(1, -1), a_log.reshape(-1, 1),
      jnp.repeat(d_skip, SSD_HEAD_DIM).reshape(1, -1), norm_w.reshape(1, -1))


def _compress_kernel(kr_ref, pos_ref, w1_ref, w2_ref, o_ref):
    nr, half = kr_ref.shape
    kr = kr_ref[...]
    p1 = _dot(kr, w1_ref[0:half, :])
    p2 = _dot(kr, w1_ref[half:2 * half, :])
    pb = _dot(pos_ref[...], w1_ref[...])[0:1]
    hid = p1 + pltpu.roll(p2, nr - 1, 0) + pb
    hid = hid * _sigmoid(hid)
    out = _dot(hid.astype(BF16), w2_ref[...])
    rows = lax.broadcasted_iota(jnp.int32, out.shape, 0)
    o_ref[...] = jnp.where(rows < nr - 1, out, 0.0).astype(o_ref.dtype)


def _compress(kr, pos, w1, w2p):
    two, b, g, nr, half = kr.shape
    hidden = w1.shape[-1]
    return pl.pallas_call(
        _compress_kernel,
        grid=(two, b, g),
        in_specs=[
            pl.BlockSpec((None, None, None, nr, half), lambda t, i, j: (t, i, j, 0, 0)),
            pl.BlockSpec((None, SUBLANES, 2 * half), lambda t, i, j: (t, 0, 0)),
            pl.BlockSpec((None, 2 * half, hidden), lambda t, i, j: (t, 0, 0)),
            pl.BlockSpec((None, hidden, LANES), lambda t, i, j: (t, 0, 0)),
        ],
        out_specs=pl.BlockSpec((None, None, None, nr, LANES), lambda t, i, j: (t, i, j, 0, 0)),
        out_shape=jax.ShapeDtypeStruct((two, b, g, nr, LANES), BF16),
        compiler_params=_cparams(3), name="nsa_compress",
    )(kr, pos, w1, w2p)


def _nsa_group(qi, qT_ref, kc_ref, vcT_ref, ov_ref, ksl_ref, vslT_ref, kw_ref, vwT_ref, cb_ref, tp_ref,
               gl_ref, o_ref, q0_ref, qa_ref, s_ref, pe_ref, acc_ref, n_slc, top_k):
    T, R, HD = NSA_TQ, NSA_REP, NSA_HEAD_DIM
    nr = kc_ref.shape[0]
    per_tile = T // CMP_STRIDE
    front = nr - per_tile
    half = LANES // 2

    qT = jnp.concatenate([qT_ref[r] for r in range(R)], axis=1)
    q0_ref[0:HD, :] = qT
    q0_ref[HD:2 * HD, :] = jnp.zeros((HD, R * T), BF16)
    qa_ref[0:HD, :] = qT

    def colmax(a):
        return jnp.max(a, axis=0, keepdims=True)

    def colsum(a):
        return jnp.sum(a, axis=0, keepdims=True)

    n_win = WINDOW // T
    scores, offsets = [], []
    for rel in range(n_win, -1, -1):
        tile = TILE_EDGE if rel == n_win else (TILE_ZERO if rel >= 2 else (TILE_NEAR if rel == 1 else TILE_DIAG))
        if rel > 0:
            tile = jnp.where(qi >= rel, tile, TILE_NONE)
        ko = pl.multiple_of(jnp.maximum(qi - rel, 0) * T, T)
        scores.append(_dot(kw_ref[pl.ds(ko, T), :], q0_ref[...]) + tp_ref[tile])
        offsets.append(ko)
    m_w = functools.reduce(jnp.maximum, [colmax(s) for s in scores])
    l_w = jnp.zeros((1, R * T), F32)
    acc_w = jnp.zeros((HD, R * T), F32)
    for s, ko in zip(scores, offsets):
        pe = jnp.exp2(s - m_w)
        l_w = l_w + colsum(pe)
        acc_w = acc_w + _dot(vwT_ref[:, pl.ds(ko, T)], pe.astype(BF16))
    o_win = acc_w * (1.0 / l_w)

    off = pl.multiple_of(front - qi * per_tile, SUBLANES)
    lg = _dot(kc_ref[...], q0_ref[...]) + cb_ref[pl.ds(off, nr), :]
    m_c = colmax(lg)
    e = jnp.exp2(lg - m_c)
    p = e * jnp.where(m_c > 0.5 * NEG_INF, 1.0 / colsum(e), 0.0)
    o_cmp = _dot(vcT_ref[...], p.astype(BF16))

    psum = p[:, 0:T]
    for r in range(1, R):
        psum = psum + p[:, r * T:(r + 1) * T]
    imp = sum(_dot(ov_ref[...], pi) for pi in _split3(psum))
    nio = lax.broadcasted_iota(jnp.int32, (half, T), 0)
    tio = lax.broadcasted_iota(jnp.int32, (half, T), 1)
    tb = (qi * T + tio) // SLC_BLOCK
    forced = (nio == 0) | (nio == tb) | (nio == tb - 1)
    imp = jnp.where(forced, FORCE_SCORE, imp)
    imp = jnp.where(nio > tb, NEG_INF, imp)
    n_grp = half // SUBLANES
    grp = [imp[SUBLANES * v:SUBLANES * (v + 1)] for v in range(n_grp)]
    nio8 = lax.broadcasted_iota(jnp.int32, (SUBLANES, T), 0)
    rank = [jnp.zeros((SUBLANES, T), F32) for _ in range(n_grp)]
    for mm in range(n_slc):
        rowv = imp[mm:mm + 1, :]
        for v in range(n_grp):
            if SUBLANES * v > mm:
                hit = jnp.where(rowv >= grp[v], 1.0, 0.0)
            elif SUBLANES * v + SUBLANES - 1 < mm:
                hit = jnp.where(rowv > grp[v], 1.0, 0.0)
            else:
                hit = jnp.where(nio8 > mm - SUBLANES * v, jnp.where(rowv >= grp[v], 1.0, 0.0),
                                jnp.where(rowv > grp[v], 1.0, 0.0))
            rank[v] = rank[v] + hit
    rank = jnp.concatenate(rank, axis=0)
    pen = jnp.where(rank < top_k, jnp.where(nio <= tb, 0.0, SEL_PENALTY), SEL_PENALTY).astype(BF16)
    qa_ref[HD:2 * HD, :] = jnp.concatenate([pen] * R, axis=1)

    krow = lax.broadcasted_iota(jnp.int32, (T, LANES), 0)
    klane = lax.broadcasted_iota(jnp.int32, (T, LANES), 1)

    def sel_keys(kj):
        kt = ksl_ref[pl.ds(pl.multiple_of(kj * T, T), T), :]
        blk = kj * (T // SLC_BLOCK) + krow // SLC_BLOCK
        return jnp.where(klane >= half, jnp.where(klane - half == blk, 1.0, 0.0).astype(BF16), kt)

    def sel_scores(kj):
        return _dot(sel_keys(kj), qa_ref[...])

    def sel_values(kj):
        return vslT_ref[:, pl.ds(pl.multiple_of(kj * T, T), T)]

    n_far = jnp.maximum(qi - 1, 0)

    def far_step(kj, slot, carry):
        cm_cur, alpha_prev, m_old, l_old = carry
        acc_ref[...] = alpha_prev * acc_ref[...] + _dot(sel_values(jnp.maximum(kj - 1, 0)), pe_ref[1 - slot])
        s_next = sel_scores(kj + 1)
        s_ref[1 - slot] = s_next
        m_new = jnp.maximum(m_old, cm_cur)
        alpha = jnp.exp2(m_old - m_new)
        pe = jnp.exp2(s_ref[slot] - m_new)
        pe_ref[slot] = pe.astype(BF16)
        return colmax(s_next), alpha, m_new, alpha * l_old + colsum(pe)

    def far_pair(j, carry):
        return far_step(2 * j + 1, 1, far_step(2 * j, 0, carry))

    s_first = sel_scores(0)
    s_ref[0] = s_first
    pe_ref[1] = jnp.zeros((T, R * T), BF16)
    acc_ref[...] = jnp.zeros((HD, R * T), F32)
    init = (colmax(s_first), jnp.ones((1, R * T), F32), jnp.full((1, R * T), NEG_INF, F32),
            jnp.zeros((1, R * T), F32))

    def finish(carry):
        carry = lax.cond(n_far % 2 == 1, lambda c: far_step(n_far - 1, 0, c), lambda c: c, carry)
        _, alpha_prev, m_old, l_old = carry
        acc = alpha_prev * acc_ref[...] + _dot(sel_values(jnp.maximum(n_far - 1, 0)), pe_ref[(n_far + 1) % 2])
        s_near = s_ref[n_far % 2] + tp_ref[jnp.where(qi >= 1, TILE_NEAR, TILE_NONE)]
        s_diag = sel_scores(qi) + tp_ref[TILE_DIAG]
        m_new = jnp.maximum(m_old, jnp.maximum(colmax(s_near), colmax(s_diag)))
        alpha = jnp.exp2(m_old - m_new)
        pe_n = jnp.exp2(s_near - m_new)
        pe_d = jnp.exp2(s_diag - m_new)
        l_s = alpha * l_old + colsum(pe_n) + colsum(pe_d)
        acc = (alpha * acc + _dot(sel_values(n_far), pe_n.astype(BF16))
               + _dot(sel_values(qi), pe_d.astype(BF16)))
        o_slc = acc * (1.0 / l_s)

        sg = _sigmoid(gl_ref[...])
        zpad = jnp.zeros((LANES - HD, T), F32)
        for r in range(R):
            cols = slice(r * T, (r + 1) * T)
            o = (sg[r:r + 1] * o_cmp[:, cols] + sg[R + r:R + r + 1] * o_slc[:, cols]
                 + sg[2 * R + r:2 * R + r + 1] * o_win[:, cols])
            o_ref[:, LANES * r:LANES * (r + 1)] = jnp.concatenate([o, zpad], axis=0).T.astype(o_ref.dtype)

    return init, far_pair, finish


def _nsa_kernel(qT_ref, kc_ref, vcT_ref, ov_ref, ksl_ref, vslT_ref, kw_ref, vwT_ref, cb_ref, tp_ref,
                gl_ref, o_ref, q0_ref, qa_ref, s_ref, pe_ref, acc_ref, *, gb, n_slc, top_k):
    R, HD = NSA_REP, NSA_HEAD_DIM
    qi = pl.program_id(2)
    parts = []
    for g in range(gb):
        lanes = slice(g * LANES, (g + 1) * LANES)
        parts.append(_nsa_group(
            qi, qT_ref.at[g * R:(g + 1) * R], kc_ref.at[g], vcT_ref.at[g], ov_ref, ksl_ref.at[:, lanes],
            vslT_ref.at[g], kw_ref.at[:, lanes], vwT_ref.at[g], cb_ref.at[g], tp_ref.at[g], gl_ref.at[g],
            o_ref.at[:, g * R * LANES:(g + 1) * R * LANES], q0_ref.at[g], qa_ref.at[g],
            s_ref.at[g], pe_ref.at[g], acc_ref.at[g], n_slc, top_k))
    carries = lax.fori_loop(
        0, jnp.maximum(qi - 1, 0) // 2,
        lambda j, cs: tuple(body(j, c) for (_, body, _), c in zip(parts, cs)),
        tuple(init for init, _, _ in parts))
    for (_, _, finish), c in zip(parts, carries):
        finish(c)


def _t5_bucket_np(dist):
    n = np.maximum(dist, 0)
    max_exact = REL_BUCKETS // 2
    nf = np.maximum(n, 1).astype(np.float32)
    large = max_exact + (np.log(nf / np.float32(max_exact)) / np.float32(math.log(REL_MAX_DIST / max_exact))
                         * np.float32(REL_BUCKETS - max_exact)).astype(np.int32)
    large = np.minimum(large, REL_BUCKETS - 1)
    return np.where(n < max_exact, n, large).astype(np.int32)


def _nsa_tables(rel_bias, nr):
    T, G, R = NSA_TQ, NSA_KV_HEADS, NSA_REP
    table = rel_bias.astype(F32) * LOG2E
    per_tile = T // CMP_STRIDE
    front = nr - per_tile
    t = np.arange(T)[None, :]

    def lay(a):
        keys = a.shape[0]
        return jnp.transpose(a, (2, 0, 1)).reshape(G, R, keys, T).transpose(0, 2, 1, 3).reshape(G, keys, R * T)

    cend = CMP_STRIDE * (np.arange(front + nr)[:, None] - front) + CMP_BLOCK - 1
    dist = t - cend
    def lookup(d):
        onehot = jax.nn.one_hot(_t5_bucket_np(d), REL_BUCKETS, dtype=F32)
        return jnp.einsum("ktb,bh->kth", onehot, table, precision=lax.Precision.HIGHEST)

    cb = jnp.where((dist >= 0)[..., None], lookup(dist), NEG_INF)
    k = np.arange(T)[:, None]
    assert 2 * T - (T - 1) >= REL_MAX_DIST and WINDOW % T == 0 and WINDOW // T >= 2
    last = table[REL_BUCKETS - 1]
    diag = jnp.where((t - k >= 0)[..., None], lookup(t - k) - last, NEG_INF)
    near = lookup(T + t - k) - last
    none = jnp.full_like(near, NEG_INF)
    edge = jnp.where((k > t)[..., None], jnp.zeros_like(near), NEG_INF)
    tiles = [None] * 5
    tiles[TILE_DIAG], tiles[TILE_NEAR], tiles[TILE_NONE], tiles[TILE_EDGE] = diag, near, none, edge
    tiles[TILE_ZERO] = jnp.zeros_like(near)
    tp = jnp.stack([lay(a) for a in tiles], axis=1)
    return lay(cb), tp


def _overlap_matrix(nr, n_slc):
    half = LANES // 2
    n = np.arange(half)[:, None]
    c = np.arange(nr)[None, :]
    ov = ((CMP_STRIDE * c <= SLC_BLOCK * n + SLC_BLOCK - 1) & (CMP_STRIDE * c + CMP_BLOCK - 1 >= SLC_BLOCK * n)
          & (n < n_slc) & (c < nr - 1))
    return jnp.asarray(ov.astype(np.float32), dtype=BF16)


def _nsa_attention(qT, kc, vcT, ksl, vslT, kw, vwT, cb, tp, glT):
    n_heads, hd, _ = qT.shape
    b, s, _ = ksl.shape
    G, R, T = NSA_KV_HEADS, NSA_REP, NSA_TQ
    nr = kc.shape[-2]
    nq = s // T
    n_slc = s // SLC_BLOCK
    assert n_slc <= LANES // 2 and s % T == 0 and nr % SUBLANES == 0
    ov = _overlap_matrix(nr, n_slc)
    gb = NSA_GROUPS_PER_STEP
    assert G % gb == 0
    kern = functools.partial(_nsa_kernel, gb=gb, n_slc=n_slc, top_k=min(SLC_TOPK, n_slc))
    k_spec = pl.BlockSpec((None, s, gb * LANES), lambda i, j, k: (i, 0, j))
    vT_spec = pl.BlockSpec((gb, hd, s), lambda i, j, k: (j, 0, i))
    return pl.pallas_call(
        kern, grid=(b, G // gb, nq),
        in_specs=[
            pl.BlockSpec((gb * R, hd, T), lambda i, j, k: (j, 0, i * nq + k)),
            pl.BlockSpec((None, gb, nr, LANES), lambda i, j, k: (i, j, 0, 0)),
            pl.BlockSpec((None, gb, hd, nr), lambda i, j, k: (i, j, 0, 0)),
            pl.BlockSpec(ov.shape, lambda i, j, k: (0, 0)),
            k_spec, vT_spec, k_spec, vT_spec,
            pl.BlockSpec((gb,) + cb.shape[1:], lambda i, j, k: (j, 0, 0)),
            pl.BlockSpec((gb,) + tp.shape[1:], lambda i, j, k: (j, 0, 0, 0)),
            pl.BlockSpec((gb, 16, T), lambda i, j, k: (j, 0, i * nq + k)),
        ],
        out_specs=pl.BlockSpec((None, T, gb * R * LANES), lambda i, j, k: (i, k, j)),
        out_shape=jax.ShapeDtypeStruct((b, s, G * R * LANES), BF16),
        scratch_shapes=[pltpu.VMEM((gb, 2 * hd, R * T), BF16), pltpu.VMEM((gb, 2 * hd, R * T), BF16),
                        pltpu.VMEM((gb, 2, T, R * T), F32), pltpu.VMEM((gb, 2, T, R * T), BF16),
                        pltpu.VMEM((gb, hd, R * T), F32)],
        compiler_params=_cparams(3), name="nsa_attention",
    )(qT, kc, vcT, ov, ksl, vslT, kw, vwT, cb, tp, glT)


def _merge_kernel(x_ref, ys_ref, yn_ref, mg_ref, ws_ref, wn_ref, wo_ref, g_ref, o_ref):
    d = x_ref.shape[1]
    a = _dot(ys_ref[...], ws_ref[...])
    bb = _dot(yn_ref[...], wn_ref[...])
    mg = _sigmoid(mg_ref[...].astype(F32))
    mixed = mg[:, :d] * a + mg[:, d:] * bb
    o = _dot(mixed.astype(BF16), wo_ref[...])
    o_ref[...] = x_ref[...] + _rms(o, g_ref[...])


def _merge(x2d, y_ssd, y_nsa, mg, w_s, w_n, w_o, g, tm=512):
    m, d = x2d.shape
    tm = min(tm, m)
    row = lambda a: pl.BlockSpec((tm, a.shape[1]), lambda i: (i, 0))
    full = lambda a: pl.BlockSpec(a.shape, lambda i: (0, 0))
    g = g.reshape(1, d)
    return pl.pallas_call(
        _merge_kernel, grid=(m // tm,),
        in_specs=[row(x2d), row(y_ssd), row(y_nsa), row(mg), full(w_s), full(w_n), full(w_o), full(g)],
        out_specs=pl.BlockSpec((tm, d), lambda i: (i, 0)),
        out_shape=jax.ShapeDtypeStruct((m, d), F32),
        compiler_params=_cparams(1), name="merge",
    )(x2d, y_ssd, y_nsa, mg, w_s, w_n, w_o, g)


def _xattn_kernel(x_ref, kv_ref, gpre_ref, wq_ref, wo_ref, gpost_ref, o_ref):
    x = x_ref[...]
    h = _rms(x, gpre_ref[...]).astype(BF16)
    q = _dot(h, wq_ref[...])
    width = X_HEADS * X_HEAD_DIM
    scale = X_HEAD_DIM ** -0.5
    outs = []
    for hh in range(X_HEADS):
        sl = slice(hh * X_HEAD_DIM, (hh + 1) * X_HEAD_DIM)
        lg = _dot_nt(q[:, sl].astype(BF16), kv_ref[:, sl]) * scale
        e = jnp.exp(lg - jnp.max(lg, axis=-1, keepdims=True))
        p = e / jnp.sum(e, axis=-1, keepdims=True)
        outs.append(_dot(p.astype(BF16), kv_ref[:, width + hh * X_HEAD_DIM:width + (hh + 1) * X_HEAD_DIM]))
    o = _dot(jnp.concatenate(outs, axis=-1).astype(BF16), wo_ref[...])
    o_ref[...] = x + _rms(o, gpost_ref[...])


def _xattn(x3d, kv, g_pre, w_q, w_o, g_post, tm=512):
    b, s, d = x3d.shape
    tm = min(tm, s)
    ml = kv.shape[1]
    full = lambda a: pl.BlockSpec(a.shape, lambda i, j: (0, 0))
    g_pre, g_post = g_pre.reshape(1, d), g_post.reshape(1, d)
    return pl.pallas_call(
        _xattn_kernel, grid=(b, s // tm),
        in_specs=[pl.BlockSpec((None, tm, d), lambda i, j: (i, j, 0)),
                  pl.BlockSpec((None, ml, kv.shape[2]), lambda i, j: (i, 0, 0)),
                  full(g_pre), full(w_q), full(w_o), full(g_post)],
        out_specs=pl.BlockSpec((None, tm, d), lambda i, j: (i, j, 0)),
        out_shape=jax.ShapeDtypeStruct((b, s, d), F32),
        compiler_params=_cparams(2), name="xattn",
    )(x3d, kv, g_pre, w_q, w_o, g_post)


def _mlp_kernel(x_ref, gpre_ref, w1_ref, w2_ref, gpost_ref, o_ref, *, f_chunk):
    x = x_ref[...]
    h = _rms(x, gpre_ref[...]).astype(BF16)
    d_ff = w1_ref.shape[1]
    acc = jnp.zeros(x.shape, F32)
    for f0 in range(0, d_ff, f_chunk):
        u = jnp.maximum(_dot(h, w1_ref[:, f0:f0 + f_chunk]), 0.0)
        acc = acc + _dot((u * u).astype(BF16), w2_ref[f0:f0 + f_chunk, :])
    o_ref[...] = x + _rms(acc, gpost_ref[...])


def _mlp(x2d, g_pre, w1, w2, g_post, tm=512):
    m, d = x2d.shape
    tm = min(tm, m)
    full = lambda a: pl.BlockSpec(a.shape, lambda i: (0, 0))
    g_pre, g_post = g_pre.reshape(1, d), g_post.reshape(1, d)
    return pl.pallas_call(
        functools.partial(_mlp_kernel, f_chunk=1024), grid=(m // tm,),
        in_specs=[pl.BlockSpec((tm, d), lambda i: (i, 0)), full(g_pre), full(w1), full(w2), full(g_post)],
        out_specs=pl.BlockSpec((tm, d), lambda i: (i, 0)),
        out_shape=jax.ShapeDtypeStruct((m, d), F32),
        compiler_params=_cparams(1), name="mlp",
    )(x2d, g_pre, w1, w2, g_post)


def _pad_heads(w, n_heads, hd):
    k = w.shape[0]
    w = w.reshape(k, n_heads, hd)
    return jnp.pad(w, ((0, 0), (0, 0), (0, LANES - hd))).reshape(k, n_heads * LANES)


def _layer(x, mem, w_in, conv_w, conv_b, dt_bias, a_log, d_skip, ssd_norm, cmp_pos, cmp_w1, cmp_w2,
           rel_bias, w_br_ssd, w_br_nsa, w_out, w_xq, w_xkv, w_xo, w_ff1, w_ff2,
           n_mix_pre, n_mix_post, n_x_pre, n_x_post, n_mem, n_ffn_pre, n_ffn_post):
    b, s, d = x.shape
    G, R, HD = NSA_KV_HEADS, NSA_REP, NSA_HEAD_DIM
    d_inner = 2 * d
    n_ssd_heads = d_inner // SSD_HEAD_DIM
    conv_dim = d_inner + 2 * SSD_GROUPS * SSD_STATE
    nsa_w, kv_w = NSA_HEADS * HD, G * HD
    sizes = (d_inner, conv_dim, n_ssd_heads, nsa_w, 6 * kv_w, 3 * NSA_HEADS, 2 * d)
    offs = np.concatenate([[0], np.cumsum(sizes)])
    seg = lambda i: w_in[:, offs[i]:offs[i + 1]]
    w_z, w_xbc, w_dt, w_q, w_kv, w_gate, w_mg = (seg(i) for i in range(7))
    w_small = jnp.pad(jnp.concatenate([w_dt, w_gate], axis=1),
                      ((0, 0), (0, LANES - n_ssd_heads - 3 * NSA_HEADS)))
    w_kv6 = w_kv.reshape(d, 6, kv_w)
    w_cmp = jnp.concatenate([w_kv6[:, 0], w_kv6[:, 1]], axis=1)
    bf = lambda a: a.astype(BF16)

    x2d = x.reshape(b * s, d)
    z, xbc = _norm_matmul(x2d, n_mix_pre, [bf(w_z), bf(w_xbc)], [BF16, BF16])
    w_gl = jnp.transpose(w_gate.reshape(d, 3, G, R), (2, 1, 3, 0)).reshape(G, 3 * R, d)
    w_gl = jnp.pad(w_gl, ((0, 0), (0, 16 - 3 * R), (0, 0))).reshape(G * 16, d)
    small, dtT, glT, qT, kv_cmp, vslT, vwT = _norm_matmul(
        x2d, n_mix_pre,
        [bf(w_small), bf(w_dt.T), bf(w_gl), bf(w_q.T * (HD ** -0.5 * LOG2E)), bf(w_cmp),
         bf(w_kv6[:, 3].T), bf(w_kv6[:, 5].T)],
        [F32, F32, F32, BF16, BF16, BF16, BF16],
        transposed=(False, True, True, True, False, True, True))
    ksl, kw, mg = _norm_matmul(
        x2d, n_mix_pre, [bf(_pad_heads(w_kv6[:, 2], G, HD)), bf(_pad_heads(w_kv6[:, 4], G, HD)), bf(w_mg)],
        [BF16] * 3)

    y_ssd = _ssd_mixer(z.reshape(b, s, -1), xbc.reshape(b, s, -1), small.reshape(b, s, -1), dtT,
                       conv_w, conv_b, dt_bias, a_log, d_skip, ssd_norm)

    nr = s // CMP_STRIDE
    kr = kv_cmp.reshape(b, nr, CMP_STRIDE, 2, G, HD)
    kr = jnp.transpose(kr, (3, 0, 4, 1, 2, 5)).reshape(2, b, G, nr, CMP_STRIDE * HD)
    pos = jnp.broadcast_to(bf(cmp_pos).reshape(2, 1, CMP_BLOCK * HD), (2, SUBLANES, CMP_BLOCK * HD))
    w2p = jnp.pad(bf(cmp_w2), ((0, 0), (0, 0), (0, LANES - HD)))
    kvc = _compress(kr, pos, bf(cmp_w1), w2p)
    vcT = jnp.swapaxes(kvc[1][..., :HD], -1, -2)
    cb, tp = _nsa_tables(rel_bias, nr)
    y_nsa = _nsa_attention(qT.reshape(NSA_HEADS, HD, b * s), kvc[0], vcT,
                           ksl.reshape(b, s, -1), vslT.reshape(G, HD, b * s),
                           kw.reshape(b, s, -1), vwT.reshape(G, HD, b * s), cb, tp,
                           glT.reshape(G, 16, b * s))

    w_n_ext = jnp.pad(w_br_nsa.reshape(NSA_HEADS, HD, d), ((0, 0), (0, LANES - HD), (0, 0)))
    x1 = _merge(x2d, y_ssd.reshape(b * s, -1), y_nsa.reshape(b * s, -1), mg,
                bf(w_br_ssd), bf(w_n_ext.reshape(NSA_HEADS * LANES, d)), bf(w_out), n_mix_post)

    (kv_mem,) = _norm_matmul(mem.reshape(-1, d), n_mem, [bf(w_xkv)], [BF16])
    x2 = _xattn(x1.reshape(b, s, d), kv_mem.reshape(b, mem.shape[1], -1), n_x_pre, bf(w_xq), bf(w_xo),
                n_x_post)

    x3 = _mlp(x2.reshape(b * s, d), n_ffn_pre, bf(w_ff1), bf(w_ff2), n_ffn_post)
    return x3.reshape(b, s, d)


def kernel(x, mem, w_in, ssd_conv_w, ssd_conv_b, ssd_dt_bias, ssd_a_log, ssd_d_skip, ssd_norm, cmp_pos,
           cmp_w1, cmp_w2, rel_bias, w_br_ssd, w_br_nsa, w_out, w_xq, w_xkv, w_xo, w_ff1, w_ff2,
           norm_mix_pre, norm_mix_post, norm_x_pre, norm_x_post, norm_mem, norm_ffn_pre, norm_ffn_post):
    for l in range(w_in.shape[0]):
        x = _layer(x, mem, w_in[l], ssd_conv_w[l], ssd_conv_b[l], ssd_dt_bias[l], ssd_a_log[l],
                   ssd_d_skip[l], ssd_norm[l], cmp_pos[l], cmp_w1[l], cmp_w2[l], rel_bias,
                   w_br_ssd[l], w_br_nsa[l], w_out[l], w_xq[l], w_xkv[l], w_xo[l], w_ff1[l], w_ff2[l],
                   norm_mix_pre[l], norm_mix_post[l], norm_x_pre[l], norm_x_post[l], norm_mem[l],
                   norm_ffn_pre[l], norm_ffn_post[l])
    return x
```

```python
import functools
import math

import numpy as np
import jax
import jax.numpy as jnp
from jax import lax
from jax.experimental import pallas as pl
from jax.experimental.pallas import tpu as pltpu

F32 = jnp.float32
BF16 = jnp.bfloat16

NORM_EPS = 1e-6
NEG_INF = -1e30
FORCE_SCORE = 1e9
SEL_PENALTY = -1e9

LANES = 128
SUBLANES = 8
VMEM_LIMIT = 56 * 1024 * 1024

SSD_HEAD_DIM = 64
SSD_GROUPS = 8
SSD_STATE = 128
SSD_CONV = 4
SSD_CHUNK = 128
NSA_HEADS = 16
NSA_KV_HEADS = 4
NSA_HEAD_DIM = 64
NSA_REP = NSA_HEADS // NSA_KV_HEADS
CMP_BLOCK = 32
CMP_STRIDE = 16
SLC_BLOCK = 64
SLC_TOPK = 16
WINDOW = 512
NSA_TQ = 256
LOG2E = math.log2(math.e)
REL_BUCKETS = 32
REL_MAX_DIST = 128
X_HEADS = 4
X_HEAD_DIM = 128

TILE_DIAG, TILE_NEAR, TILE_NONE, TILE_EDGE, TILE_ZERO = range(5)


def _cparams(n_grid):
    return pltpu.CompilerParams(dimension_semantics=("arbitrary",) * n_grid,
                                vmem_limit_bytes=VMEM_LIMIT)


def _sigmoid(x):
    return 1.0 / (1.0 + jnp.exp(-x))


def _rms(x, g):
    return x * lax.rsqrt(jnp.mean(x * x, axis=-1, keepdims=True) + NORM_EPS) * g


def _dot(a, b):
    return jnp.dot(a, b, preferred_element_type=F32)


def _dot_nt(a, b):
    return lax.dot_general(a, b, (((1,), (1,)), ((), ())), preferred_element_type=F32)


def _split3(x):
    x1 = x.astype(BF16)
    r1 = x - x1.astype(F32)
    x2 = r1.astype(BF16)
    x3 = (r1 - x2.astype(F32)).astype(BF16)
    return x1, x2, x3


def _norm_matmul_kernel(x_ref, g_ref, *refs, transposed, n_chunk):
    n_out = len(transposed)
    w_refs, o_refs = refs[:n_out], refs[n_out:]
    h = _rms(x_ref[...], g_ref[...]).astype(BF16)
    for w_ref, o_ref, tr in zip(w_refs, o_refs, transposed):
        if tr:
            o_ref[...] = _dot_nt(w_ref[...], h).astype(o_ref.dtype)
            continue
        n = w_ref.shape[1]
        step = min(n, n_chunk)
        for n0 in range(0, n, step):
            o_ref[:, n0:n0 + step] = _dot(h, w_ref[:, n0:n0 + step]).astype(o_ref.dtype)


def _norm_matmul(x2d, g, ws, out_dtypes, transposed=None, tm=512):
    m, k = x2d.shape
    tm = min(tm, m)
    transposed = tuple(transposed) if transposed is not None else (False,) * len(ws)
    in_specs = [pl.BlockSpec((tm, k), lambda i: (i, 0)), pl.BlockSpec((1, k), lambda i: (0, 0))]
    in_specs += [pl.BlockSpec(w.shape, lambda i: (0, 0)) for w in ws]
    out_specs, out_shape = [], []
    for w, dt, tr in zip(ws, out_dtypes, transposed):
        if tr:
            out_specs.append(pl.BlockSpec((w.shape[0], tm), lambda i: (0, i)))
            out_shape.append(jax.ShapeDtypeStruct((w.shape[0], m), dt))
        else:
            out_specs.append(pl.BlockSpec((tm, w.shape[1]), lambda i: (i, 0)))
            out_shape.append(jax.ShapeDtypeStruct((m, w.shape[1]), dt))
    return pl.pallas_call(
        functools.partial(_norm_matmul_kernel, transposed=transposed, n_chunk=1024),
        grid=(m // tm,), in_specs=in_specs, out_specs=out_specs, out_shape=out_shape,
        compiler_params=_cparams(1), name="norm_proj",
    )(x2d, g.reshape(1, k), *ws)


def _softplus(x):
    return jnp.maximum(x, 0.0) + jnp.log(1.0 + jnp.exp(-jnp.abs(x)))


def _ssd_kernel(xbc_ref, prev_ref, z_ref, dt_ref, dtT_ref, cw_ref, cb_ref, dtb_ref, dtbT_ref,
                alog_ref, alogT_ref, dskip_ref, nw_ref, y_ref, state_ref, xc_ref, ybuf_ref,
                *, n_heads, d_inner):
    L, P, N, G = SSD_CHUNK, SSD_HEAD_DIM, SSD_STATE, SSD_GROUPS
    conv_dim = xc_ref.shape[1]
    c = pl.program_id(1)

    @pl.when(c == 0)
    def _():
        state_ref[...] = jnp.zeros_like(state_ref)

    cw = 512
    prev_rows = prev_ref.shape[0]
    row = lax.broadcasted_iota(jnp.int32, (prev_rows, cw), 0)
    for j in range(conv_dim // cw):
        sl = slice(j * cw, (j + 1) * cw)
        cur = xbc_ref[:, sl].astype(F32)
        prev = jnp.where(c == 0, 0.0, prev_ref[:, sl].astype(F32))
        acc = cb_ref[:, sl] + cw_ref[SSD_CONV - 1:SSD_CONV, sl] * cur
        for k in range(1, SSD_CONV):
            rc = pltpu.roll(cur, k, 0)
            rp = pltpu.roll(prev, k, 0)
            head = jnp.where(row < k, rp, rc[:prev_rows])
            shifted = jnp.concatenate([head, rc[prev_rows:]], axis=0)
            acc = acc + cw_ref[SSD_CONV - 1 - k:SSD_CONV - k, sl] * shifted
        xc_ref[:, sl] = acc * _sigmoid(acc)

    dt = _softplus(dt_ref[:, 0:n_heads] + dtb_ref[...])
    a = dt * (-jnp.exp(alog_ref[...]))
    dtT = _softplus(dtT_ref[...] + dtbT_ref[...])
    aT = dtT * (-jnp.exp(alogT_ref[...]))
    ri = lax.broadcasted_iota(jnp.int32, (L, L), 0)
    ci = lax.broadcasted_iota(jnp.int32, (L, L), 1)
    causal = ci <= ri
    tri = jnp.where(causal, 1.0, 0.0).astype(BF16)
    triu = jnp.where(ri <= ci, 1.0, 0.0).astype(BF16)
    a_cs = sum(_dot(tri, ai) for ai in _split3(a))
    a_csT = sum(_dot(ai, triu) for ai in _split3(aT))
    lane = lax.broadcasted_iota(jnp.int32, (L, 2 * P), 1)
    lo_half = lane < P
    lane1 = lax.broadcasted_iota(jnp.int32, (1, 2 * P), 1) < P

    heads_per_group = n_heads // G
    for g in range(G):
        b_g = xc_ref[:, d_inner + g * N:d_inner + (g + 1) * N]
        c_g = xc_ref[:, d_inner + G * N + g * N:d_inner + G * N + (g + 1) * N]
        c_gb = c_g.astype(BF16)
        cb = _dot_nt(c_gb, b_g.astype(BF16))
        b_gT = b_g.T.astype(BF16)
        for pp in range(heads_per_group // 2):
            pair = g * (heads_per_group // 2) + pp
            h0, h1 = 2 * pair, 2 * pair + 1
            xs = xc_ref[:, pair * 2 * P:(pair + 1) * 2 * P]
            dt_pair = jnp.where(lo_half, dt[:, h0:h0 + 1], dt[:, h1:h1 + 1])
            xd = xs * dt_pair
            xd_b = xd.astype(BF16)
            ys = []
            for h in (h0, h1):
                seg = jnp.where(causal, a_cs[:, h:h + 1] - a_csT[h:h + 1, :], NEG_INF)
                mat = (cb * jnp.exp(seg)).astype(BF16)
                ys.append(_dot(mat, xd_b))
            y = jnp.where(lo_half, ys[0], ys[1])
            cs_pair = jnp.where(lo_half, a_cs[:, h0:h0 + 1], a_cs[:, h1:h1 + 1])
            st = state_ref[pair]
            y = y + _dot(c_gb, st.astype(BF16)) * jnp.exp(cs_pair)
            tot = jnp.where(lane1, a_csT[h0:h0 + 1, L - 1:L], a_csT[h1:h1 + 1, L - 1:L])
            xdd = (xd * jnp.exp(tot - cs_pair)).astype(BF16)
            state_ref[pair] = st * jnp.exp(tot) + _dot(b_gT, xdd)
            ybuf_ref[:, pair * 2 * P:(pair + 1) * 2 * P] = y + xs * dskip_ref[:, pair * 2 * P:(pair + 1) * 2 * P]

    gw = d_inner // G
    for g in range(G):
        sl = slice(g * gw, (g + 1) * gw)
        zz = z_ref[:, sl].astype(F32)
        yg = ybuf_ref[:, sl] * (zz * _sigmoid(zz))
        y_ref[:, sl] = _rms(yg, nw_ref[:, sl]).astype(y_ref.dtype)


def _ssd_mixer(z, xbc, small, dtT, conv_w, conv_b, dt_bias, a_log, d_skip, norm_w):
    b, s, d_inner = z.shape
    conv_dim = xbc.shape[-1]
    n_heads = d_inner // SSD_HEAD_DIM
    L = SSD_CHUNK
    nc = s // L
    prev_rows = 16
    xbc_prev = xbc.reshape(b, s // prev_rows, prev_rows, conv_dim)
    blocks_per_chunk = L // prev_rows
    kern = functools.partial(_ssd_kernel, n_heads=n_heads, d_inner=d_inner)
    const = lambda shape: pl.BlockSpec(shape, lambda i, j: (0,) * len(shape))
    return pl.pallas_call(
        kern, grid=(b, nc),
        in_specs=[
            pl.BlockSpec((None, L, conv_dim), lambda i, j: (i, j, 0)),
            pl.BlockSpec((None, None, prev_rows, conv_dim),
                         lambda i, j: (i, jnp.maximum(j * blocks_per_chunk - 1, 0), 0, 0)),
            pl.BlockSpec((None, L, d_inner), lambda i, j: (i, j, 0)),
            pl.BlockSpec((None, L, small.shape[-1]), lambda i, j: (i, j, 0)),
            pl.BlockSpec((n_heads, L), lambda i, j: (0, i * nc + j)),
            const((SSD_CONV, conv_dim)), const((1, conv_dim)),
            const((1, n_heads)), const((n_heads, 1)),
            const((1, n_heads)), const((n_heads, 1)),
            const((1, d_inner)), const((1, d_inner)),
        ],
        out_specs=pl.BlockSpec((None, L, d_inner), lambda i, j: (i, j, 0)),
        out_shape=jax.ShapeDtypeStruct((b, s, d_inner), BF16),
        scratch_shapes=[pltpu.VMEM((n_heads // 2, SSD_STATE, 2 * SSD_HEAD_DIM), F32),
                        pltpu.VMEM((L, conv_dim), F32),
                        pltpu.VMEM((L, d_inner), F32)],
        compiler_params=_cparams(2), name="ssd",
    )(xbc, xbc_prev, z, small, dtT, conv_w.astype(F32), conv_b.reshape(1, -1).astype(F32),
      dt_bias.reshape(1, -1), dt_bias.reshape(-1, 1), a_log.reshape(1, -1), a_log.reshape(-1, 1),
      jnp.repeat(d_skip, SSD_HEAD_DIM).reshape(1, -1), norm_w.reshape(1, -1))


def _compress_kernel(kr_ref, pos_ref, w1_ref, w2_ref, o_ref):
    nr, half = kr_ref.shape
    kr = kr_ref[...]
    p1 = _dot(kr, w1_ref[0:half, :])
    p2 = _dot(kr, w1_ref[half:2 * half, :])
    pb = _dot(pos_ref[...], w1_ref[...])[0:1]
    hid = p1 + pltpu.roll(p2, nr - 1, 0) + pb
    hid = hid * _sigmoid(hid)
    out = _dot(hid.astype(BF16), w2_ref[...])
    rows = lax.broadcasted_iota(jnp.int32, out.shape, 0)
    o_ref[...] = jnp.where(rows < nr - 1, out, 0.0).astype(o_ref.dtype)


def _compress(kr, pos, w1, w2p):
    two, b, g, nr, half = kr.shape
    hidden = w1.shape[-1]
    return pl.pallas_call(
        _compress_kernel,
        grid=(two, b, g),
        in_specs=[
            pl.BlockSpec((None, None, None, nr, half), lambda t, i, j: (t, i, j, 0, 0)),
            pl.BlockSpec((None, SUBLANES, 2 * half), lambda t, i, j: (t, 0, 0)),
            pl.BlockSpec((None, 2 * half, hidden), lambda t, i, j: (t, 0, 0)),
            pl.BlockSpec((None, hidden, LANES), lambda t, i, j: (t, 0, 0)),
        ],
        out_specs=pl.BlockSpec((None, None, None, nr, LANES), lambda t, i, j: (t, i, j, 0, 0)),
        out_shape=jax.ShapeDtypeStruct((two, b, g, nr, LANES), BF16),
        compiler_params=_cparams(3), name="nsa_compress",
    )(kr, pos, w1, w2p)


def _colmax(a):
    return jnp.max(a, axis=0, keepdims=True)


def _colsum(a):
    return jnp.sum(a, axis=0, keepdims=True)


class _SoftmaxPipe:
    def __init__(self, s_ref, pe_ref, acc_ref, hd):
        self.s, self.pe, self.acc, self.hd = s_ref, pe_ref, acc_ref, hd

    def start(self, first_scores, slot):
        cols = first_scores.shape[1]
        self.s[slot] = first_scores
        self.pe[1 - slot] = jnp.zeros(self.pe.shape[1:], self.pe.dtype)
        self.acc[...] = jnp.zeros(self.acc.shape, F32)
        return _colmax(first_scores), jnp.ones((1, cols), F32), jnp.full((1, cols), NEG_INF, F32)

    def step(self, slot, carry, v_prev, next_scores):
        cm_cur, alpha_prev, m_old = carry
        self.acc[...] = alpha_prev * self.acc[...] + _dot(v_prev, self.pe[1 - slot])
        cm_next = cm_cur
        if next_scores is not None:
            s_next = next_scores()
            self.s[1 - slot] = s_next
            cm_next = _colmax(s_next)
        m_new = jnp.maximum(m_old, cm_cur)
        self.pe[slot] = jnp.exp2(self.s[slot] - m_new).astype(self.pe.dtype)
        return cm_next, jnp.exp2(m_old - m_new), m_new

    def finish(self, slot_last, carry, v_last):
        _, alpha_prev, _ = carry
        acc = alpha_prev * self.acc[...] + _dot(v_last, self.pe[slot_last])
        return acc[0:self.hd] * (1.0 / acc[self.hd:self.hd + 1])


def _nsa_kernel(qT_ref, kc_ref, vcT_ref, ov_ref, ksl_ref, vslT_ref, kw_ref, vwT_ref, cb_ref, tp_ref,
                gl_ref, o_ref, q0_ref, qa_ref, vsa_ref, vwa_ref, s_ref, pe_ref, acc_ref, ws_ref, wpe_ref,
                wacc_ref, *, n_slc, top_k):
    T, R, HD = NSA_TQ, NSA_REP, NSA_HEAD_DIM
    nr = kc_ref.shape[0]
    per_tile = T // CMP_STRIDE
    front = nr - per_tile
    half = LANES // 2
    qi = pl.program_id(2)
    colmax, colsum = _colmax, _colsum

    @pl.when(qi == 0)
    def _():
        for src, dst in ((vslT_ref, vsa_ref), (vwT_ref, vwa_ref)):
            dst[0:HD, :] = src[...]
            dst[HD:, :] = jnp.ones((dst.shape[0] - HD, dst.shape[1]), dst.dtype)

    qT = jnp.concatenate([qT_ref[r] for r in range(R)], axis=1)
    q0_ref[0:HD, :] = qT
    q0_ref[HD:2 * HD, :] = jnp.zeros((HD, R * T), BF16)
    qa_ref[0:HD, :] = qT

    n_win = WINDOW // T

    def win_offset(rel):
        return pl.multiple_of(jnp.maximum(qi - rel, 0) * T, T)

    def win_scores(rel):
        tile = TILE_EDGE if rel == n_win else (TILE_ZERO if rel >= 2 else (TILE_NEAR if rel == 1 else TILE_DIAG))
        if rel > 0:
            tile = jnp.where(qi >= rel, tile, TILE_NONE)
        return _dot(kw_ref[pl.ds(win_offset(rel), T), :], q0_ref[...]) + tp_ref[tile]

    def win_values(rel):
        return vwa_ref[:, pl.ds(win_offset(rel), T)]

    win = _SoftmaxPipe(ws_ref, wpe_ref, wacc_ref, HD)
    rels = list(range(n_win, -1, -1))
    carry = win.start(win_scores(rels[0]), 0)
    for i, rel in enumerate(rels):
        nxt = functools.partial(win_scores, rels[i + 1]) if i + 1 < len(rels) else None
        carry = win.step(i % 2, carry, win_values(rels[max(i - 1, 0)]), nxt)
    o_win = win.finish((len(rels) - 1) % 2, carry, win_values(0))

    off = pl.multiple_of(front - qi * per_tile, SUBLANES)
    lg = _dot(kc_ref[...], q0_ref[...]) + cb_ref[pl.ds(off, nr), :]
    m_c = colmax(lg)
    e = jnp.exp2(lg - m_c)
    p = e * jnp.where(m_c > 0.5 * NEG_INF, 1.0 / colsum(e), 0.0)
    o_cmp = _dot(vcT_ref[...], p.astype(BF16))

    psum = p[:, 0:T]
    for r in range(1, R):
        psum = psum + p[:, r * T:(r + 1) * T]
    imp = sum(_dot(ov_ref[...], pi) for pi in _split3(psum))
    nio = lax.broadcasted_iota(jnp.int32, (half, T), 0)
    tio = lax.broadcasted_iota(jnp.int32, (half, T), 1)
    tb = (qi * T + tio) // SLC_BLOCK
    forced = (nio == 0) | (nio == tb) | (nio == tb - 1)
    imp = jnp.where(forced, FORCE_SCORE, imp)
    imp = jnp.where(nio > tb, NEG_INF, imp)
    n_grp = half // SUBLANES
    grp = [imp[SUBLANES * v:SUBLANES * (v + 1)] for v in range(n_grp)]
    nio8 = lax.broadcasted_iota(jnp.int32, (SUBLANES, T), 0)
    rank = [jnp.zeros((SUBLANES, T), F32) for _ in range(n_grp)]
    for mm in range(n_slc):
        rowv = imp[mm:mm + 1, :]
        for v in range(n_grp):
            if SUBLANES * v > mm:
                hit = jnp.where(rowv >= grp[v], 1.0, 0.0)
            elif SUBLANES * v + SUBLANES - 1 < mm:
                hit = jnp.where(rowv > grp[v], 1.0, 0.0)
            else:
                hit = jnp.where(nio8 > mm - SUBLANES * v, jnp.where(rowv >= grp[v], 1.0, 0.0),
                                jnp.where(rowv > grp[v], 1.0, 0.0))
            rank[v] = rank[v] + hit
    rank = jnp.concatenate(rank, axis=0)
    pen = jnp.where(rank < top_k, jnp.where(nio <= tb, 0.0, SEL_PENALTY), SEL_PENALTY).astype(BF16)
    qa_ref[HD:2 * HD, :] = jnp.concatenate([pen] * R, axis=1)

    krow = lax.broadcasted_iota(jnp.int32, (T, LANES), 0)
    klane = lax.broadcasted_iota(jnp.int32, (T, LANES), 1)

    def sel_keys(kj):
        kt = ksl_ref[pl.ds(pl.multiple_of(kj * T, T), T), :]
        blk = kj * (T // SLC_BLOCK) + krow // SLC_BLOCK
        return jnp.where(klane >= half, jnp.where(klane - half == blk, 1.0, 0.0).astype(BF16), kt)

    def sel_scores(kj):
        return _dot(sel_keys(kj), qa_ref[...])

    def sel_values(kj):
        return vsa_ref[:, pl.ds(pl.multiple_of(kj * T, T), T)]

    n_far = jnp.maximum(qi - 1, 0)
    odd = n_far % 2
    sel = _SoftmaxPipe(s_ref, pe_ref, acc_ref, HD)

    def far_step(kj, slot, carry):
        return sel.step(slot, carry, sel_values(jnp.maximum(kj - 1, 0)), functools.partial(sel_scores, kj + 1))

    carry = sel.start(sel_scores(0), odd)
    carry = lax.cond(odd == 1, lambda c: far_step(0, 1, c), lambda c: c, carry)
    carry = lax.fori_loop(
        0, n_far // 2, lambda j, c: far_step(odd + 2 * j + 1, 1, far_step(odd + 2 * j, 0, c)), carry)
    s_near = s_ref[0] + tp_ref[jnp.where(qi >= 1, TILE_NEAR, TILE_NONE)]
    s_ref[0] = s_near
    carry = (colmax(s_near),) + tuple(carry[1:])
    carry = sel.step(0, carry, sel_values(jnp.maximum(n_far - 1, 0)),
                     lambda: sel_scores(qi) + tp_ref[TILE_DIAG])
    carry = sel.step(1, carry, sel_values(n_far), None)
    o_slc = sel.finish(1, carry, sel_values(qi))

    sg = _sigmoid(gl_ref[...])
    zpad = jnp.zeros((LANES - HD, T), F32)
    for r in range(R):
        cols = slice(r * T, (r + 1) * T)
        o = (sg[r:r + 1] * o_cmp[:, cols] + sg[R + r:R + r + 1] * o_slc[:, cols]
             + sg[2 * R + r:2 * R + r + 1] * o_win[:, cols])
        o_ref[:, LANES * r:LANES * (r + 1)] = jnp.concatenate([o, zpad], axis=0).T.astype(o_ref.dtype)


def _t5_bucket_np(dist):
    n = np.maximum(dist, 0)
    max_exact = REL_BUCKETS // 2
    nf = np.maximum(n, 1).astype(np.float32)
    large = max_exact + (np.log(nf / np.float32(max_exact)) / np.float32(math.log(REL_MAX_DIST / max_exact))
                         * np.float32(REL_BUCKETS - max_exact)).astype(np.int32)
    large = np.minimum(large, REL_BUCKETS - 1)
    return np.where(n < max_exact, n, large).astype(np.int32)


def _nsa_tables(rel_bias, nr):
    T, G, R = NSA_TQ, NSA_KV_HEADS, NSA_REP
    table = rel_bias.astype(F32) * LOG2E
    per_tile = T // CMP_STRIDE
    front = nr - per_tile
    t = np.arange(T)[None, :]

    def lay(a):
        keys = a.shape[0]
        return jnp.transpose(a, (2, 0, 1)).reshape(G, R, keys, T).transpose(0, 2, 1, 3).reshape(G, keys, R * T)

    cend = CMP_STRIDE * (np.arange(front + nr)[:, None] - front) + CMP_BLOCK - 1
    dist = t - cend
    def lookup(d):
        onehot = jax.nn.one_hot(_t5_bucket_np(d), REL_BUCKETS, dtype=F32)
        return jnp.einsum("ktb,bh->kth", onehot, table, precision=lax.Precision.HIGHEST)

    cb = jnp.where((dist >= 0)[..., None], lookup(dist), NEG_INF)
    k = np.arange(T)[:, None]
    assert 2 * T - (T - 1) >= REL_MAX_DIST and WINDOW % T == 0 and WINDOW // T >= 2
    last = table[REL_BUCKETS - 1]
    diag = jnp.where((t - k >= 0)[..., None], lookup(t - k) - last, NEG_INF)
    near = lookup(T + t - k) - last
    none = jnp.full_like(near, NEG_INF)
    edge = jnp.where((k > t)[..., None], jnp.zeros_like(near), NEG_INF)
    tiles = [None] * 5
    tiles[TILE_DIAG], tiles[TILE_NEAR], tiles[TILE_NONE], tiles[TILE_EDGE] = diag, near, none, edge
    tiles[TILE_ZERO] = jnp.zeros_like(near)
    tp = jnp.stack([lay(a) for a in tiles], axis=1)
    return lay(cb), tp


def _overlap_matrix(nr, n_slc):
    half = LANES // 2
    n = np.arange(half)[:, None]
    c = np.arange(nr)[None, :]
    ov = ((CMP_STRIDE * c <= SLC_BLOCK * n + SLC_BLOCK - 1) & (CMP_STRIDE * c + CMP_BLOCK - 1 >= SLC_BLOCK * n)
          & (n < n_slc) & (c < nr - 1))
    return jnp.asarray(ov.astype(np.float32), dtype=BF16)


def _nsa_attention(qT, kc, vcT, ksl, vslT, kw, vwT, cb, tp, glT):
    n_heads, hd, _ = qT.shape
    b, s, _ = ksl.shape
    G, R, T = NSA_KV_HEADS, NSA_REP, NSA_TQ
    nr = kc.shape[-2]
    nq = s // T
    n_slc = s // SLC_BLOCK
    assert n_slc <= LANES // 2 and s % T == 0 and nr % SUBLANES == 0
    ov = _overlap_matrix(nr, n_slc)
    kern = functools.partial(_nsa_kernel, n_slc=n_slc, top_k=min(SLC_TOPK, n_slc))
    k_spec = pl.BlockSpec((None, s, LANES), lambda i, j, k: (i, 0, j))
    vT_spec = pl.BlockSpec((None, hd, s), lambda i, j, k: (j, 0, i))
    ones_rows = 16
    pipe_scratch = [pltpu.VMEM((2, T, R * T), F32), pltpu.VMEM((2, T, R * T), BF16),
                    pltpu.VMEM((hd + ones_rows, R * T), F32)]
    value_scratch = [pltpu.VMEM((hd + ones_rows, s), BF16)] * 2
    return pl.pallas_call(
        kern, grid=(b, G, nq),
        in_specs=[
            pl.BlockSpec((R, hd, T), lambda i, j, k: (j, 0, i * nq + k)),
            pl.BlockSpec((None, None, nr, LANES), lambda i, j, k: (i, j, 0, 0)),
            pl.BlockSpec((None, None, hd, nr), lambda i, j, k: (i, j, 0, 0)),
            pl.BlockSpec(ov.shape, lambda i, j, k: (0, 0)),
            k_spec, vT_spec, k_spec, vT_spec,
            pl.BlockSpec((None,) + cb.shape[1:], lambda i, j, k: (j, 0, 0)),
            pl.BlockSpec((None,) + tp.shape[1:], lambda i, j, k: (j, 0, 0, 0)),
            pl.BlockSpec((None, 16, T), lambda i, j, k: (j, 0, i * nq + k)),
        ],
        out_specs=pl.BlockSpec((None, T, R * LANES), lambda i, j, k: (i, k, j)),
        out_shape=jax.ShapeDtypeStruct((b, s, G * R * LANES), BF16),
        scratch_shapes=[pltpu.VMEM((2 * hd, R * T), BF16), pltpu.VMEM((2 * hd, R * T), BF16)]
        + value_scratch + pipe_scratch + pipe_scratch,
        compiler_params=_cparams(3), name="nsa_attention",
    )(qT, kc, vcT, ov, ksl, vslT, kw, vwT, cb, tp, glT)


def _merge_kernel(x_ref, ys_ref, yn_ref, mg_ref, ws_ref, wn_ref, wo_ref, g_ref, o_ref):
    d = x_ref.shape[1]
    a = _dot(ys_ref[...], ws_ref[...])
    bb = _dot(yn_ref[...], wn_ref[...])
    mg = _sigmoid(mg_ref[...].astype(F32))
    mixed = mg[:, :d] * a + mg[:, d:] * bb
    o = _dot(mixed.astype(BF16), wo_ref[...])
    o_ref[...] = x_ref[...] + _rms(o, g_ref[...])


def _merge(x2d, y_ssd, y_nsa, mg, w_s, w_n, w_o, g, tm=512):
    m, d = x2d.shape
    tm = min(tm, m)
    row = lambda a: pl.BlockSpec((tm, a.shape[1]), lambda i: (i, 0))
    full = lambda a: pl.BlockSpec(a.shape, lambda i: (0, 0))
    g = g.reshape(1, d)
    return pl.pallas_call(
        _merge_kernel, grid=(m // tm,),
        in_specs=[row(x2d), row(y_ssd), row(y_nsa), row(mg), full(w_s), full(w_n), full(w_o), full(g)],
        out_specs=pl.BlockSpec((tm, d), lambda i: (i, 0)),
        out_shape=jax.ShapeDtypeStruct((m, d), F32),
        compiler_params=_cparams(1), name="merge",
    )(x2d, y_ssd, y_nsa, mg, w_s, w_n, w_o, g)


def _xattn_kernel(x_ref, kv_ref, gpre_ref, wq_ref, wo_ref, gpost_ref, o_ref):
    x = x_ref[...]
    h = _rms(x, gpre_ref[...]).astype(BF16)
    q = _dot(h, wq_ref[...])
    width = X_HEADS * X_HEAD_DIM
    scale = X_HEAD_DIM ** -0.5
    outs = []
    for hh in range(X_HEADS):
        sl = slice(hh * X_HEAD_DIM, (hh + 1) * X_HEAD_DIM)
        lg = _dot_nt(q[:, sl].astype(BF16), kv_ref[:, sl]) * scale
        e = jnp.exp(lg - jnp.max(lg, axis=-1, keepdims=True))
        p = e / jnp.sum(e, axis=-1, keepdims=True)
        outs.append(_dot(p.astype(BF16), kv_ref[:, width + hh * X_HEAD_DIM:width + (hh + 1) * X_HEAD_DIM]))
    o = _dot(jnp.concatenate(outs, axis=-1).astype(BF16), wo_ref[...])
    o_ref[...] = x + _rms(o, gpost_ref[...])


def _xattn(x3d, kv, g_pre, w_q, w_o, g_post, tm=512):
    b, s, d = x3d.shape
    tm = min(tm, s)
    ml = kv.shape[1]
    full = lambda a: pl.BlockSpec(a.shape, lambda i, j: (0, 0))
    g_pre, g_post = g_pre.reshape(1, d), g_post.reshape(1, d)
    return pl.pallas_call(
        _xattn_kernel, grid=(b, s // tm),
        in_specs=[pl.BlockSpec((None, tm, d), lambda i, j: (i, j, 0)),
                  pl.BlockSpec((None, ml, kv.shape[2]), lambda i, j: (i, 0, 0)),
                  full(g_pre), full(w_q), full(w_o), full(g_post)],
        out_specs=pl.BlockSpec((None, tm, d), lambda i, j: (i, j, 0)),
        out_shape=jax.ShapeDtypeStruct((b, s, d), F32),
        compiler_params=_cparams(2), name="xattn",
    )(x3d, kv, g_pre, w_q, w_o, g_post)


def _mlp_kernel(x_ref, gpre_ref, w1_ref, w2_ref, gpost_ref, o_ref, *, f_chunk):
    x = x_ref[...]
    h = _rms(x, gpre_ref[...]).astype(BF16)
    d_ff = w1_ref.shape[1]
    acc = jnp.zeros(x.shape, F32)
    for f0 in range(0, d_ff, f_chunk):
        u = jnp.maximum(_dot(h, w1_ref[:, f0:f0 + f_chunk]), 0.0)
        acc = acc + _dot((u * u).astype(BF16), w2_ref[f0:f0 + f_chunk, :])
    o_ref[...] = x + _rms(acc, gpost_ref[...])


def _mlp(x2d, g_pre, w1, w2, g_post, tm=512):
    m, d = x2d.shape
    tm = min(tm, m)
    full = lambda a: pl.BlockSpec(a.shape, lambda i: (0, 0))
    g_pre, g_post = g_pre.reshape(1, d), g_post.reshape(1, d)
    return pl.pallas_call(
        functools.partial(_mlp_kernel, f_chunk=1024), grid=(m // tm,),
        in_specs=[pl.BlockSpec((tm, d), lambda i: (i, 0)), full(g_pre), full(w1), full(w2), full(g_post)],
        out_specs=pl.BlockSpec((tm, d), lambda i: (i, 0)),
        out_shape=jax.ShapeDtypeStruct((m, d), F32),
        compiler_params=_cparams(1), name="mlp",
    )(x2d, g_pre, w1, w2, g_post)


def _pad_heads(w, n_heads, hd):
    k = w.shape[0]
    w = w.reshape(k, n_heads, hd)
    return jnp.pad(w, ((0, 0), (0, 0), (0, LANES - hd))).reshape(k, n_heads * LANES)


def _layer(x, mem, w_in, conv_w, conv_b, dt_bias, a_log, d_skip, ssd_norm, cmp_pos, cmp_w1, cmp_w2,
           rel_bias, w_br_ssd, w_br_nsa, w_out, w_xq, w_xkv, w_xo, w_ff1, w_ff2,
           n_mix_pre, n_mix_post, n_x_pre, n_x_post, n_mem, n_ffn_pre, n_ffn_post):
    b, s, d = x.shape
    G, R, HD = NSA_KV_HEADS, NSA_REP, NSA_HEAD_DIM
    d_inner = 2 * d
    n_ssd_heads = d_inner // SSD_HEAD_DIM
    conv_dim = d_inner + 2 * SSD_GROUPS * SSD_STATE
    nsa_w, kv_w = NSA_HEADS * HD, G * HD
    sizes = (d_inner, conv_dim, n_ssd_heads, nsa_w, 6 * kv_w, 3 * NSA_HEADS, 2 * d)
    offs = np.concatenate([[0], np.cumsum(sizes)])
    seg = lambda i: w_in[:, offs[i]:offs[i + 1]]
    w_z, w_xbc, w_dt, w_q, w_kv, w_gate, w_mg = (seg(i) for i in range(7))
    w_small = jnp.pad(jnp.concatenate([w_dt, w_gate], axis=1),
                      ((0, 0), (0, LANES - n_ssd_heads - 3 * NSA_HEADS)))
    w_kv6 = w_kv.reshape(d, 6, kv_w)
    w_cmp = jnp.concatenate([w_kv6[:, 0], w_kv6[:, 1]], axis=1)
    bf = lambda a: a.astype(BF16)

    x2d = x.reshape(b * s, d)
    z, xbc = _norm_matmul(x2d, n_mix_pre, [bf(w_z), bf(w_xbc)], [BF16, BF16])
    w_gl = jnp.transpose(w_gate.reshape(d, 3, G, R), (2, 1, 3, 0)).reshape(G, 3 * R, d)
    w_gl = jnp.pad(w_gl, ((0, 0), (0, 16 - 3 * R), (0, 0))).reshape(G * 16, d)
    small, dtT, glT, qT, kv_cmp, vslT, vwT = _norm_matmul(
        x2d, n_mix_pre,
        [bf(w_small), bf(w_dt.T), bf(w_gl), bf(w_q.T * (HD ** -0.5 * LOG2E)), bf(w_cmp),
         bf(w_kv6[:, 3].T), bf(w_kv6[:, 5].T)],
        [F32, F32, F32, BF16, BF16, BF16, BF16],
        transposed=(False, True, True, True, False, True, True))
    ksl, kw, mg = _norm_matmul(
        x2d, n_mix_pre, [bf(_pad_heads(w_kv6[:, 2], G, HD)), bf(_pad_heads(w_kv6[:, 4], G, HD)), bf(w_mg)],
        [BF16] * 3)

    y_ssd = _ssd_mixer(z.reshape(b, s, -1), xbc.reshape(b, s, -1), small.reshape(b, s, -1), dtT,
                       conv_w, conv_b, dt_bias, a_log, d_skip, ssd_norm)

    nr = s // CMP_STRIDE
    kr = kv_cmp.reshape(b, nr, CMP_STRIDE, 2, G, HD)
    kr = jnp.transpose(kr, (3, 0, 4, 1, 2, 5)).reshape(2, b, G, nr, CMP_STRIDE * HD)
    pos = jnp.broadcast_to(bf(cmp_pos).reshape(2, 1, CMP_BLOCK * HD), (2, SUBLANES, CMP_BLOCK * HD))
    w2p = jnp.pad(bf(cmp_w2), ((0, 0), (0, 0), (0, LANES - HD)))
    kvc = _compress(kr, pos, bf(cmp_w1), w2p)
    vcT = jnp.swapaxes(kvc[1][..., :HD], -1, -2)
    cb, tp = _nsa_tables(rel_bias, nr)
    y_nsa = _nsa_attention(qT.reshape(NSA_HEADS, HD, b * s), kvc[0], vcT,
                           ksl.reshape(b, s, -1), vslT.reshape(G, HD, b * s),
                           kw.reshape(b, s, -1), vwT.reshape(G, HD, b * s), cb, tp,
                           glT.reshape(G, 16, b * s))

    w_n_ext = jnp.pad(w_br_nsa.reshape(NSA_HEADS, HD, d), ((0, 0), (0, LANES - HD), (0, 0)))
    x1 = _merge(x2d, y_ssd.reshape(b * s, -1), y_nsa.reshape(b * s, -1), mg,
                bf(w_br_ssd), bf(w_n_ext.reshape(NSA_HEADS * LANES, d)), bf(w_out), n_mix_post)

    (kv_mem,) = _norm_matmul(mem.reshape(-1, d), n_mem, [bf(w_xkv)], [BF16])
    x2 = _xattn(x1.reshape(b, s, d), kv_mem.reshape(b, mem.shape[1], -1), n_x_pre, bf(w_xq), bf(w_xo),
                n_x_post)

    x3 = _mlp(x2.reshape(b * s, d), n_ffn_pre, bf(w_ff1), bf(w_ff2), n_ffn_post)
    return x3.reshape(b, s, d)


def kernel(x, mem, w_in, ssd_conv_w, ssd_conv_b, ssd_dt_bias, ssd_a_log, ssd_d_skip, ssd_norm, cmp_pos,
           cmp_w1, cmp_w2, rel_bias, w_br_ssd, w_br_nsa, w_out, w_xq, w_xkv, w_xo, w_ff1, w_ff2,
           norm_mix_pre, norm_mix_post, norm_x_pre, norm_x_post, norm_mem, norm_ffn_pre, norm_ffn_post):
    for l in range(w_in.shape[0]):
        x = _layer(x, mem, w_in[l], ssd_conv_w[l], ssd_conv_b[l], ssd_dt_bias[l], ssd_a_log[l],
                   ssd_d_skip[l], ssd_norm[l], cmp_pos[l], cmp_w1[l], cmp_w2[l], rel_bias,
                   w_br_ssd[l], w_br_nsa[l], w_out[l], w_xq[l], w_xkv[l], w_xo[l], w_ff1[l], w_ff2[l],
                   norm_mix_pre[l], norm_mix_post[l], norm_x_pre[l], norm_x_post[l], norm_mem[l],
                   norm_ffn_pre[l], norm_ffn_post[l])
    return x
```

```python
import functools
import math

import numpy as np
import jax
import jax.numpy as jnp
from jax import lax
from jax.experimental import pallas as pl
from jax.experimental.pallas import tpu as pltpu

F32 = jnp.float32
BF16 = jnp.bfloat16

NORM_EPS = 1e-6
NEG_INF = -1e30
FORCE_SCORE = 1e9
SEL_PENALTY = -1e9

LANES = 128
SUBLANES = 8
VMEM_LIMIT = 56 * 1024 * 1024

SSD_HEAD_DIM = 64
SSD_GROUPS = 8
SSD_STATE = 128
SSD_CONV = 4
SSD_CHUNK = 128
NSA_HEADS = 16
NSA_KV_HEADS = 4
NSA_HEAD_DIM = 64
NSA_REP = NSA_HEADS // NSA_KV_HEADS
CMP_BLOCK = 32
CMP_STRIDE = 16
SLC_BLOCK = 64
SLC_TOPK = 16
WINDOW = 512
NSA_TQ = 256
NSA_GROUPS_PER_STEP = 1
LOG2E = math.log2(math.e)
REL_BUCKETS = 32
REL_MAX_DIST = 128
X_HEADS = 4
X_HEAD_DIM = 128

TILE_DIAG, TILE_NEAR, TILE_NONE, TILE_EDGE, TILE_ZERO = range(5)


def _cparams(n_grid):
    return pltpu.CompilerParams(dimension_semantics=("arbitrary",) * n_grid,
                                vmem_limit_bytes=VMEM_LIMIT)


def _sigmoid(x):
    return 1.0 / (1.0 + jnp.exp(-x))


def _rms(x, g):
    return x * lax.rsqrt(jnp.mean(x * x, axis=-1, keepdims=True) + NORM_EPS) * g


def _dot(a, b):
    return jnp.dot(a, b, preferred_element_type=F32)


def _dot_nt(a, b):
    return lax.dot_general(a, b, (((1,), (1,)), ((), ())), preferred_element_type=F32)


def _split3(x):
    x1 = x.astype(BF16)
    r1 = x - x1.astype(F32)
    x2 = r1.astype(BF16)
    x3 = (r1 - x2.astype(F32)).astype(BF16)
    return x1, x2, x3


def _norm_matmul_kernel(x_ref, g_ref, *refs, transposed, n_chunk):
    n_out = len(transposed)
    w_refs, o_refs = refs[:n_out], refs[n_out:]
    h = _rms(x_ref[...], g_ref[...]).astype(BF16)
    for w_ref, o_ref, tr in zip(w_refs, o_refs, transposed):
        if tr:
            o_ref[...] = _dot_nt(w_ref[...], h).astype(o_ref.dtype)
            continue
        n = w_ref.shape[1]
        step = min(n, n_chunk)
        for n0 in range(0, n, step):
            o_ref[:, n0:n0 + step] = _dot(h, w_ref[:, n0:n0 + step]).astype(o_ref.dtype)


def _norm_matmul(x2d, g, ws, out_dtypes, transposed=None, tm=512):
    m, k = x2d.shape
    tm = min(tm, m)
    transposed = tuple(transposed) if transposed is not None else (False,) * len(ws)
    in_specs = [pl.BlockSpec((tm, k), lambda i: (i, 0)), pl.BlockSpec((1, k), lambda i: (0, 0))]
    in_specs += [pl.BlockSpec(w.shape, lambda i: (0, 0)) for w in ws]
    out_specs, out_shape = [], []
    for w, dt, tr in zip(ws, out_dtypes, transposed):
        if tr:
            out_specs.append(pl.BlockSpec((w.shape[0], tm), lambda i: (0, i)))
            out_shape.append(jax.ShapeDtypeStruct((w.shape[0], m), dt))
        else:
            out_specs.append(pl.BlockSpec((tm, w.shape[1]), lambda i: (i, 0)))
            out_shape.append(jax.ShapeDtypeStruct((m, w.shape[1]), dt))
    return pl.pallas_call(
        functools.partial(_norm_matmul_kernel, transposed=transposed, n_chunk=1024),
        grid=(m // tm,), in_specs=in_specs, out_specs=out_specs, out_shape=out_shape,
        compiler_params=_cparams(1), name="norm_proj",
    )(x2d, g.reshape(1, k), *ws)


def _softplus(x):
    return jnp.maximum(x, 0.0) + jnp.log(1.0 + jnp.exp(-jnp.abs(x)))


def _ssd_kernel(xbc_ref, prev_ref, z_ref, dt_ref, dtT_ref, cw_ref, cb_ref, dtb_ref, dtbT_ref,
                alog_ref, alogT_ref, dskip_ref, nw_ref, y_ref, state_ref, xc_ref, ybuf_ref,
                *, n_heads, d_inner):
    L, P, N, G = SSD_CHUNK, SSD_HEAD_DIM, SSD_STATE, SSD_GROUPS
    conv_dim = xc_ref.shape[1]
    c = pl.program_id(1)

    @pl.when(c == 0)
    def _():
        state_ref[...] = jnp.zeros_like(state_ref)

    cw = 256
    prev_rows = prev_ref.shape[0]
    si = lax.broadcasted_iota(jnp.int32, (L, L + prev_rows), 0)
    sj = lax.broadcasted_iota(jnp.int32, (L, L + prev_rows), 1)
    shift = [jnp.where((sj == si - k) | (sj == si - k + L + prev_rows), 1.0, 0.0).astype(BF16)
             for k in range(SSD_CONV)]
    keep_prev = jnp.where(c == 0, 0.0, 1.0).astype(BF16)
    for j in range(conv_dim // cw):
        sl = slice(j * cw, (j + 1) * cw)
        cur = xbc_ref[:, sl]
        both = jnp.concatenate([cur, prev_ref[:, sl] * keep_prev], axis=0)
        acc = cb_ref[:, sl] + cw_ref[SSD_CONV - 1:SSD_CONV, sl] * cur.astype(F32)
        for k in range(1, SSD_CONV):
            acc = acc + cw_ref[SSD_CONV - 1 - k:SSD_CONV - k, sl] * _dot(shift[k], both)
        xc_ref[:, sl] = acc * _sigmoid(acc)

    dt = _softplus(dt_ref[:, 0:n_heads] + dtb_ref[...])
    a = dt * (-jnp.exp(alog_ref[...]))
    dtT = _softplus(dtT_ref[...] + dtbT_ref[...])
    aT = dtT * (-jnp.exp(alogT_ref[...]))
    ri = lax.broadcasted_iota(jnp.int32, (L, L), 0)
    ci = lax.broadcasted_iota(jnp.int32, (L, L), 1)
    causal = ci <= ri
    tri = jnp.where(causal, 1.0, 0.0).astype(BF16)
    triu = jnp.where(ri <= ci, 1.0, 0.0).astype(BF16)
    a_cs = sum(_dot(tri, ai) for ai in _split3(a))
    a_csT = sum(_dot(ai, triu) for ai in _split3(aT))
    lane = lax.broadcasted_iota(jnp.int32, (L, 2 * P), 1)
    lo_half = lane < P
    lane1 = lax.broadcasted_iota(jnp.int32, (1, 2 * P), 1) < P

    heads_per_group = n_heads // G
    for g in range(G):
        b_g = xc_ref[:, d_inner + g * N:d_inner + (g + 1) * N]
        c_g = xc_ref[:, d_inner + G * N + g * N:d_inner + G * N + (g + 1) * N]
        c_gb = c_g.astype(BF16)
        cb = _dot_nt(c_gb, b_g.astype(BF16))
        b_gT = b_g.T.astype(BF16)
        for pp in range(heads_per_group // 2):
            pair = g * (heads_per_group // 2) + pp
            h0, h1 = 2 * pair, 2 * pair + 1
            xs = xc_ref[:, pair * 2 * P:(pair + 1) * 2 * P]
            dt_pair = jnp.where(lo_half, dt[:, h0:h0 + 1], dt[:, h1:h1 + 1])
            xd = xs * dt_pair
            xd_b = xd.astype(BF16)
            ys = []
            for h in (h0, h1):
                seg = jnp.where(causal, a_cs[:, h:h + 1] - a_csT[h:h + 1, :], NEG_INF)
                mat = (cb * jnp.exp(seg)).astype(BF16)
                ys.append(_dot(mat, xd_b))
            y = jnp.where(lo_half, ys[0], ys[1])
            cs_pair = jnp.where(lo_half, a_cs[:, h0:h0 + 1], a_cs[:, h1:h1 + 1])
            st = state_ref[pair]
            y = y + _dot(c_gb, st.astype(BF16)) * jnp.exp(cs_pair)
            tot = jnp.where(lane1, a_csT[h0:h0 + 1, L - 1:L], a_csT[h1:h1 + 1, L - 1:L])
            xdd = (xd * jnp.exp(tot - cs_pair)).astype(BF16)
            state_ref[pair] = st * jnp.exp(tot) + _dot(b_gT, xdd)
            ybuf_ref[:, pair * 2 * P:(pair + 1) * 2 * P] = y + xs * dskip_ref[:, pair * 2 * P:(pair + 1) * 2 * P]

    gw = d_inner // G
    for g in range(G):
        sl = slice(g * gw, (g + 1) * gw)
        zz = z_ref[:, sl].astype(F32)
        yg = ybuf_ref[:, sl] * (zz * _sigmoid(zz))
        y_ref[:, sl] = _rms(yg, nw_ref[:, sl]).astype(y_ref.dtype)


def _ssd_mixer(z, xbc, small, dtT, conv_w, conv_b, dt_bias, a_log, d_skip, norm_w):
    b, s, d_inner = z.shape
    conv_dim = xbc.shape[-1]
    n_heads = d_inner // SSD_HEAD_DIM
    L = SSD_CHUNK
    nc = s // L
    prev_rows = 16
    xbc_prev = xbc.reshape(b, s // prev_rows, prev_rows, conv_dim)
    blocks_per_chunk = L // prev_rows
    kern = functools.partial(_ssd_kernel, n_heads=n_heads, d_inner=d_inner)
    const = lambda shape: pl.BlockSpec(shape, lambda i, j: (0,) * len(shape))
    return pl.pallas_call(
        kern, grid=(b, nc),
        in_specs=[
            pl.BlockSpec((None, L, conv_dim), lambda i, j: (i, j, 0)),
            pl.BlockSpec((None, None, prev_rows, conv_dim),
                         lambda i, j: (i, jnp.maximum(j * blocks_per_chunk - 1, 0), 0, 0)),
            pl.BlockSpec((None, L, d_inner), lambda i, j: (i, j, 0)),
            pl.BlockSpec((None, L, small.shape[-1]), lambda i, j: (i, j, 0)),
            pl.BlockSpec((n_heads, L), lambda i, j: (0, i * nc + j)),
            const((SSD_CONV, conv_dim)), const((1, conv_dim)),
            const((1, n_heads)), const((n_heads, 1)),
            const((1, n_heads)), const((n_heads, 1)),
            const((1, d_inner)), const((1, d_inner)),
        ],
        out_specs=pl.BlockSpec((None, L, d_inner), lambda i, j: (i, j, 0)),
        out_shape=jax.ShapeDtypeStruct((b, s, d_inner), BF16),
        scratch_shapes=[pltpu.VMEM((n_heads // 2, SSD_STATE, 2 * SSD_HEAD_DIM), F32),
                        pltpu.VMEM((L, conv_dim), F32),
                        pltpu.VMEM((L, d_inner), F32)],
        compiler_params=_cparams(2), name="ssd",
    )(xbc, xbc_prev, z, small, dtT, conv_w.astype(F32), conv_b.reshape(1, -1).astype(F32),
      dt_bias.reshape(1, -1), dt_bias.reshape(-1, 1), a_log.reshape(1, -1), a_log.reshape(-1, 1),
      jnp.repeat(d_skip, SSD_HEAD_DIM).reshape(1, -1), norm_w.reshape(1, -1))


def _compress_kernel(kr_ref, pos_ref, w1_ref, w2_ref, o_ref):
    nr, half = kr_ref.shape
    kr = kr_ref[...]
    p1 = _dot(kr, w1_ref[0:half, :])
    p2 = _dot(kr, w1_ref[half:2 * half, :])
    pb = _dot(pos_ref[...], w1_ref[...])[0:1]
    hid = p1 + pltpu.roll(p2, nr - 1, 0) + pb
    hid = hid * _sigmoid(hid)
    out = _dot(hid.astype(BF16), w2_ref[...])
    rows = lax.broadcasted_iota(jnp.int32, out.shape, 0)
    o_ref[...] = jnp.where(rows < nr - 1, out, 0.0).astype(o_ref.dtype)


def _compress(kr, pos, w1, w2p):
    two, b, g, nr, half = kr.shape
    hidden = w1.shape[-1]
    return pl.pallas_call(
        _compress_kernel,
        grid=(two, b, g),
        in_specs=[
            pl.BlockSpec((None, None, None, nr, half), lambda t, i, j: (t, i, j, 0, 0)),
            pl.BlockSpec((None, SUBLANES, 2 * half), lambda t, i, j: (t, 0, 0)),
            pl.BlockSpec((None, 2 * half, hidden), lambda t, i, j: (t, 0, 0)),
            pl.BlockSpec((None, hidden, LANES), lambda t, i, j: (t, 0, 0)),
        ],
        out_specs=pl.BlockSpec((None, None, None, nr, LANES), lambda t, i, j: (t, i, j, 0, 0)),
        out_shape=jax.ShapeDtypeStruct((two, b, g, nr, LANES), BF16),
        compiler_params=_cparams(3), name="nsa_compress",
    )(kr, pos, w1, w2p)


def _colmax(a):
    return jnp.max(a, axis=0, keepdims=True)


def _colsum(a):
    return jnp.sum(a, axis=0, keepdims=True)


def _interleave(generators):
    results = [None] * len(generators)
    live = list(enumerate(generators))
    while live:
        still = []
        for i, gen in live:
            try:
                next(gen)
                still.append((i, gen))
            except StopIteration as stop:
                results[i] = stop.value
        live = still
    return results


class _SoftmaxPipe:
    def __init__(self, s_ref, pe_ref, acc_ref, hd):
        self.s, self.pe, self.acc, self.hd = s_ref, pe_ref, acc_ref, hd

    def start(self, first_scores, slot):
        cols = first_scores.shape[1]
        self.s[slot] = first_scores
        self.pe[1 - slot] = jnp.zeros(self.pe.shape[1:], self.pe.dtype)
        self.acc[...] = jnp.zeros(self.acc.shape, F32)
        return _colmax(first_scores), jnp.ones((1, cols), F32), jnp.full((1, cols), NEG_INF, F32)

    def step_phases(self, slot, carry, v_prev, next_scores):
        cm_cur, alpha_prev, m_old = carry
        m_new = jnp.maximum(m_old, cm_cur)
        self.pe[slot] = jnp.exp2(self.s[slot] - m_new).astype(self.pe.dtype)
        yield
        self.acc[...] = alpha_prev * self.acc[...] + _dot(v_prev, self.pe[1 - slot])
        yield
        cm_next = cm_cur
        if next_scores is not None:
            s_next = next_scores()
            self.s[1 - slot] = s_next
            cm_next = _colmax(s_next)
        return cm_next, jnp.exp2(m_old - m_new), m_new

    def step(self, slot, carry, v_prev, next_scores):
        return _interleave([self.step_phases(slot, carry, v_prev, next_scores)])[0]

    def finish(self, slot_last, carry, v_last):
        _, alpha_prev, _ = carry
        acc = alpha_prev * self.acc[...] + _dot(v_last, self.pe[slot_last])
        return acc[0:self.hd] * (1.0 / acc[self.hd:self.hd + 1])


def _nsa_group(qi, qT_ref, kc_ref, vcT_ref, ov_ref, ksl_ref, vslT_ref, kw_ref, vwT_ref, cb_ref, tp_ref,
               gl_ref, o_ref, q0_ref, qa_ref, rank_ref, vsa_ref, vwa_ref, s_ref, pe_ref, acc_ref, ws_ref,
               wpe_ref, wacc_ref, n_slc, top_k):
    T, R, HD = NSA_TQ, NSA_REP, NSA_HEAD_DIM
    nr = kc_ref.shape[0]
    per_tile = T // CMP_STRIDE
    front = nr - per_tile
    half = LANES // 2
    colmax, colsum = _colmax, _colsum

    @pl.when(qi == 0)
    def _():
        for src, dst in ((vslT_ref, vsa_ref), (vwT_ref, vwa_ref)):
            dst[0:HD, :] = src[...]
            dst[HD:, :] = jnp.ones((dst.shape[0] - HD, dst.shape[1]), dst.dtype)

    qT = jnp.concatenate([qT_ref[r] for r in range(R)], axis=1)
    q0_ref[0:HD, :] = qT
    q0_ref[HD:2 * HD, :] = jnp.zeros((HD, R * T), BF16)
    qa_ref[0:HD, :] = qT

    n_win = WINDOW // T

    def win_offset(rel):
        return pl.multiple_of(jnp.maximum(qi - rel, 0) * T, T)

    def win_scores(rel):
        tile = TILE_EDGE if rel == n_win else (TILE_ZERO if rel >= 2 else (TILE_NEAR if rel == 1 else TILE_DIAG))
        if rel > 0:
            tile = jnp.where(qi >= rel, tile, TILE_NONE)
        return _dot(kw_ref[pl.ds(win_offset(rel), T), :], q0_ref[...]) + tp_ref[tile]

    def win_values(rel):
        return vwa_ref[:, pl.ds(win_offset(rel), T)]

    win = _SoftmaxPipe(ws_ref, wpe_ref, wacc_ref, HD)
    rels = list(range(n_win, -1, -1))
    carry = win.start(win_scores(rels[0]), 0)
    for i, rel in enumerate(rels):
        nxt = functools.partial(win_scores, rels[i + 1]) if i + 1 < len(rels) else None
        carry = win.step(i % 2, carry, win_values(rels[max(i - 1, 0)]), nxt)
    o_win = win.finish((len(rels) - 1) % 2, carry, win_values(0))

    off = pl.multiple_of(front - qi * per_tile, SUBLANES)
    lg = _dot(kc_ref[...], q0_ref[...]) + cb_ref[pl.ds(off, nr), :]
    m_c = colmax(lg)
    e = jnp.exp2(lg - m_c)
    p = e * jnp.where(m_c > 0.5 * NEG_INF, 1.0 / colsum(e), 0.0)
    o_cmp = _dot(vcT_ref[...], p.astype(BF16))

    psum = p[:, 0:T]
    for r in range(1, R):
        psum = psum + p[:, r * T:(r + 1) * T]
    imp = sum(_dot(ov_ref[...], pi) for pi in _split3(psum))
    nio = lax.broadcasted_iota(jnp.int32, (half, T), 0)
    tio = lax.broadcasted_iota(jnp.int32, (half, T), 1)
    tb = (qi * T + tio) // SLC_BLOCK
    forced = (nio == 0) | (nio == tb) | (nio == tb - 1)
    imp = jnp.where(forced, FORCE_SCORE, imp)
    imp = jnp.where(nio > tb, NEG_INF, imp)
    n_grp = half // SUBLANES
    grp = [imp[SUBLANES * v:SUBLANES * (v + 1)] for v in range(n_grp)]
    nio8 = lax.broadcasted_iota(jnp.int32, (SUBLANES, T), 0)
    rank_ref[...] = jnp.zeros(rank_ref.shape, F32)
    last_block = (qi * T + T - 1) // SLC_BLOCK
    for mg in range(pl.cdiv(n_slc, SUBLANES)):

        @pl.when(mg * SUBLANES <= last_block)
        def _():
            hits = [jnp.zeros((SUBLANES, T), F32) for _ in range(n_grp)]
            for mm in range(mg * SUBLANES, min((mg + 1) * SUBLANES, n_slc)):
                rowv = imp[mm:mm + 1, :]
                for v in range(n_grp):
                    if SUBLANES * v > mm:
                        hit = jnp.where(rowv >= grp[v], 1.0, 0.0)
                    elif SUBLANES * v + SUBLANES - 1 < mm:
                        hit = jnp.where(rowv > grp[v], 1.0, 0.0)
                    else:
                        hit = jnp.where(nio8 > mm - SUBLANES * v, jnp.where(rowv >= grp[v], 1.0, 0.0),
                                        jnp.where(rowv > grp[v], 1.0, 0.0))
                    hits[v] = hits[v] + hit
            rank_ref[...] += jnp.concatenate(hits, axis=0)

    rank = rank_ref[...]
    pen = jnp.where(rank < top_k, jnp.where(nio <= tb, 0.0, SEL_PENALTY), SEL_PENALTY).astype(BF16)
    qa_ref[HD:2 * HD, :] = jnp.concatenate([pen] * R, axis=1)

    krow = lax.broadcasted_iota(jnp.int32, (T, LANES), 0)
    klane = lax.broadcasted_iota(jnp.int32, (T, LANES), 1)

    def sel_keys(kj):
        kt = ksl_ref[pl.ds(pl.multiple_of(kj * T, T), T), :]
        blk = kj * (T // SLC_BLOCK) + krow // SLC_BLOCK
        return jnp.where(klane >= half, jnp.where(klane - half == blk, 1.0, 0.0).astype(BF16), kt)

    def sel_scores(kj):
        return _dot(sel_keys(kj), qa_ref[...])

    def sel_values(kj):
        return vsa_ref[:, pl.ds(pl.multiple_of(kj * T, T), T)]

    n_far = jnp.maximum(qi - 1, 0)
    odd = n_far % 2
    sel = _SoftmaxPipe(s_ref, pe_ref, acc_ref, HD)

    def far_step(kj, slot, carry):
        return sel.step_phases(slot, carry, sel_values(jnp.maximum(kj - 1, 0)),
                               functools.partial(sel_scores, kj + 1))

    def finish(carry):
        s_near = s_ref[0] + tp_ref[jnp.where(qi >= 1, TILE_NEAR, TILE_NONE)]
        s_ref[0] = s_near
        carry = (colmax(s_near),) + tuple(carry[1:])
        carry = sel.step(0, carry, sel_values(jnp.maximum(n_far - 1, 0)),
                         lambda: sel_scores(qi) + tp_ref[TILE_DIAG])
        carry = sel.step(1, carry, sel_values(n_far), None)
        o_slc = sel.finish(1, carry, sel_values(qi))

        sg = _sigmoid(gl_ref[...])
        for r in range(R):
            cols = slice(r * T, (r + 1) * T)
            o = (sg[r:r + 1] * o_cmp[:, cols] + sg[R + r:R + r + 1] * o_slc[:, cols]
                 + sg[2 * R + r:2 * R + r + 1] * o_win[:, cols])
            o_ref[r * HD:(r + 1) * HD, :] = o.astype(o_ref.dtype)

    return sel.start(sel_scores(0), odd), far_step, finish


def _nsa_kernel(qT_ref, kc_ref, vcT_ref, ov_ref, ksl_ref, vslT_ref, kw_ref, vwT_ref, cb_ref, tp_ref,
                gl_ref, o_ref, *scratch, gb, n_slc, top_k):
    R, HD = NSA_REP, NSA_HEAD_DIM
    qi = pl.program_id(2)
    parts = []
    for g in range(gb):
        lanes = slice(g * LANES, (g + 1) * LANES)
        parts.append(_nsa_group(
            qi, qT_ref.at[g * R:(g + 1) * R], kc_ref.at[g], vcT_ref.at[g], ov_ref, ksl_ref.at[:, lanes],
            vslT_ref.at[g], kw_ref.at[:, lanes], vwT_ref.at[g], cb_ref.at[g], tp_ref.at[g], gl_ref.at[g],
            o_ref.at[g * R * HD:(g + 1) * R * HD], *[ref.at[g] for ref in scratch], n_slc, top_k))
    steps = [step for _, step, _ in parts]
    carries = tuple(c for c, _, _ in parts)
    n_far = jnp.maximum(qi - 1, 0)
    odd = n_far % 2

    def all_groups(kj, slot, cs):
        return tuple(_interleave([step(kj, slot, c) for step, c in zip(steps, cs)]))

    carries = lax.cond(odd == 1, lambda cs: all_groups(0, 1, cs), lambda cs: cs, carries)
    carries = lax.fori_loop(
        0, n_far // 2, lambda j, cs: all_groups(odd + 2 * j + 1, 1, all_groups(odd + 2 * j, 0, cs)), carries)
    for (_, _, finish), c in zip(parts, carries):
        finish(c)


def _t5_bucket_np(dist):
    n = np.maximum(dist, 0)
    max_exact = REL_BUCKETS // 2
    nf = np.maximum(n, 1).astype(np.float32)
    large = max_exact + (np.log(nf / np.float32(max_exact)) / np.float32(math.log(REL_MAX_DIST / max_exact))
                         * np.float32(REL_BUCKETS - max_exact)).astype(np.int32)
    large = np.minimum(large, REL_BUCKETS - 1)
    return np.where(n < max_exact, n, large).astype(np.int32)


def _nsa_tables(rel_bias, nr):
    T, G, R = NSA_TQ, NSA_KV_HEADS, NSA_REP
    table = rel_bias.astype(F32) * LOG2E
    per_tile = T // CMP_STRIDE
    front = nr - per_tile
    t = np.arange(T)[None, :]

    def lay(a):
        keys = a.shape[0]
        return jnp.transpose(a, (2, 0, 1)).reshape(G, R, keys, T).transpose(0, 2, 1, 3).reshape(G, keys, R * T)

    cend = CMP_STRIDE * (np.arange(front + nr)[:, None] - front) + CMP_BLOCK - 1
    dist = t - cend
    def lookup(d):
        onehot = jax.nn.one_hot(_t5_bucket_np(d), REL_BUCKETS, dtype=F32)
        return jnp.einsum("ktb,bh->kth", onehot, table, precision=lax.Precision.HIGHEST)

    cb = jnp.where((dist >= 0)[..., None], lookup(dist), NEG_INF)
    k = np.arange(T)[:, None]
    assert 2 * T - (T - 1) >= REL_MAX_DIST and WINDOW % T == 0 and WINDOW // T >= 2
    last = table[REL_BUCKETS - 1]
    diag = jnp.where((t - k >= 0)[..., None], lookup(t - k) - last, NEG_INF)
    near = lookup(T + t - k) - last
    none = jnp.full_like(near, NEG_INF)
    edge = jnp.where((k > t)[..., None], jnp.zeros_like(near), NEG_INF)
    tiles = [None] * 5
    tiles[TILE_DIAG], tiles[TILE_NEAR], tiles[TILE_NONE], tiles[TILE_EDGE] = diag, near, none, edge
    tiles[TILE_ZERO] = jnp.zeros_like(near)
    tp = jnp.stack([lay(a) for a in tiles], axis=1)
    return lay(cb), tp


def _overlap_matrix(nr, n_slc):
    half = LANES // 2
    n = np.arange(half)[:, None]
    c = np.arange(nr)[None, :]
    ov = ((CMP_STRIDE * c <= SLC_BLOCK * n + SLC_BLOCK - 1) & (CMP_STRIDE * c + CMP_BLOCK - 1 >= SLC_BLOCK * n)
          & (n < n_slc) & (c < nr - 1))
    return jnp.asarray(ov.astype(np.float32), dtype=BF16)


def _nsa_attention(qT, kc, vcT, ksl, vslT, kw, vwT, cb, tp, glT):
    n_heads, hd, _ = qT.shape
    b, s, _ = ksl.shape
    G, R, T = NSA_KV_HEADS, NSA_REP, NSA_TQ
    nr = kc.shape[-2]
    nq = s // T
    n_slc = s // SLC_BLOCK
    assert n_slc <= LANES // 2 and s % T == 0 and nr % SUBLANES == 0
    ov = _overlap_matrix(nr, n_slc)
    gb = NSA_GROUPS_PER_STEP
    assert G % gb == 0
    kern = functools.partial(_nsa_kernel, gb=gb, n_slc=n_slc, top_k=min(SLC_TOPK, n_slc))
    k_spec = pl.BlockSpec((None, s, gb * LANES), lambda i, j, k: (i, 0, j))
    vT_spec = pl.BlockSpec((gb, hd, s), lambda i, j, k: (j, 0, i))
    once = pl.Buffered(1)
    ones_rows = 16
    pipe_scratch = [pltpu.VMEM((gb, 2, T, R * T), F32), pltpu.VMEM((gb, 2, T, R * T), BF16),
                    pltpu.VMEM((gb, hd + ones_rows, R * T), F32)]
    value_scratch = [pltpu.VMEM((gb, hd + ones_rows, s), BF16)] * 2
    return pl.pallas_call(
        kern, grid=(b, G // gb, nq),
        in_specs=[
            pl.BlockSpec((gb * R, hd, T), lambda i, j, k: (j, 0, i * nq + k)),
            pl.BlockSpec((None, gb, nr, LANES), lambda i, j, k: (i, j, 0, 0)),
            pl.BlockSpec((None, gb, hd, nr), lambda i, j, k: (i, j, 0, 0)),
            pl.BlockSpec(ov.shape, lambda i, j, k: (0, 0)),
            k_spec, vT_spec, k_spec, vT_spec,
            pl.BlockSpec((gb,) + cb.shape[1:], lambda i, j, k: (j, 0, 0), pipeline_mode=once),
            pl.BlockSpec((gb,) + tp.shape[1:], lambda i, j, k: (j, 0, 0, 0), pipeline_mode=once),
            pl.BlockSpec((gb, 16, T), lambda i, j, k: (j, 0, i * nq + k)),
        ],
        out_specs=pl.BlockSpec((gb * R * hd, T), lambda i, j, k: (j, i * nq + k)),
        out_shape=jax.ShapeDtypeStruct((n_heads * hd, b * s), BF16),
        scratch_shapes=[pltpu.VMEM((gb, 2 * hd, R * T), BF16), pltpu.VMEM((gb, 2 * hd, R * T), BF16),
                        pltpu.VMEM((gb, LANES // 2, T), F32)]
        + value_scratch + pipe_scratch + pipe_scratch,
        compiler_params=_cparams(3), name="nsa_attention",
    )(qT, kc, vcT, ov, ksl, vslT, kw, vwT, cb, tp, glT)


def _merge_kernel(x_ref, ys_ref, ynT_ref, mg_ref, ws_ref, wn_ref, wo_ref, g_ref, o_ref):
    d = x_ref.shape[1]
    a = _dot(ys_ref[...], ws_ref[...])
    bb = lax.dot_general(ynT_ref[...], wn_ref[...], (((0,), (0,)), ((), ())), preferred_element_type=F32)
    mg = _sigmoid(mg_ref[...].astype(F32))
    mixed = mg[:, :d] * a + mg[:, d:] * bb
    o = _dot(mixed.astype(BF16), wo_ref[...])
    o_ref[...] = x_ref[...] + _rms(o, g_ref[...])


def _merge(x2d, y_ssd, y_nsaT, mg, w_s, w_n, w_o, g, tm=512):
    m, d = x2d.shape
    tm = min(tm, m)
    row = lambda a: pl.BlockSpec((tm, a.shape[1]), lambda i: (i, 0))
    full = lambda a: pl.BlockSpec(a.shape, lambda i: (0, 0))
    g = g.reshape(1, d)
    return pl.pallas_call(
        _merge_kernel, grid=(m // tm,),
        in_specs=[row(x2d), row(y_ssd), pl.BlockSpec((y_nsaT.shape[0], tm), lambda i: (0, i)), row(mg),
                  full(w_s), full(w_n), full(w_o), full(g)],
        out_specs=pl.BlockSpec((tm, d), lambda i: (i, 0)),
        out_shape=jax.ShapeDtypeStruct((m, d), F32),
        compiler_params=_cparams(1), name="merge",
    )(x2d, y_ssd, y_nsaT, mg, w_s, w_n, w_o, g)


def _xattn_kernel(x_ref, kv_ref, gpre_ref, wq_ref, wo_ref, gpost_ref, o_ref):
    x = x_ref[...]
    h = _rms(x, gpre_ref[...]).astype(BF16)
    q = _dot(h, wq_ref[...])
    width = X_HEADS * X_HEAD_DIM
    scale = X_HEAD_DIM ** -0.5
    outs = []
    for hh in range(X_HEADS):
        sl = slice(hh * X_HEAD_DIM, (hh + 1) * X_HEAD_DIM)
        lg = _dot_nt(q[:, sl].astype(BF16), kv_ref[:, sl]) * scale
        e = jnp.exp(lg - jnp.max(lg, axis=-1, keepdims=True))
        p = e / jnp.sum(e, axis=-1, keepdims=True)
        outs.append(_dot(p.astype(BF16), kv_ref[:, width + hh * X_HEAD_DIM:width + (hh + 1) * X_HEAD_DIM]))
    o = _dot(jnp.concatenate(outs, axis=-1).astype(BF16), wo_ref[...])
    o_ref[...] = x + _rms(o, gpost_ref[...])


def _xattn(x3d, kv, g_pre, w_q, w_o, g_post, tm=512):
    b, s, d = x3d.shape
    tm = min(tm, s)
    ml = kv.shape[1]
    full = lambda a: pl.BlockSpec(a.shape, lambda i, j: (0, 0))
    g_pre, g_post = g_pre.reshape(1, d), g_post.reshape(1, d)
    return pl.pallas_call(
        _xattn_kernel, grid=(b, s // tm),
        in_specs=[pl.BlockSpec((None, tm, d), lambda i, j: (i, j, 0)),
                  pl.BlockSpec((None, ml, kv.shape[2]), lambda i, j: (i, 0, 0)),
                  full(g_pre), full(w_q), full(w_o), full(g_post)],
        out_specs=pl.BlockSpec((None, tm, d), lambda i, j: (i, j, 0)),
        out_shape=jax.ShapeDtypeStruct((b, s, d), F32),
        compiler_params=_cparams(2), name="xattn",
    )(x3d, kv, g_pre, w_q, w_o, g_post)


def _mlp_kernel(x_ref, gpre_ref, w1_ref, w2_ref, gpost_ref, o_ref, *, f_chunk):
    x = x_ref[...]
    h = _rms(x, gpre_ref[...]).astype(BF16)
    d_ff = w1_ref.shape[1]
    acc = jnp.zeros(x.shape, F32)
    for f0 in range(0, d_ff, f_chunk):
        u = jnp.maximum(_dot(h, w1_ref[:, f0:f0 + f_chunk]), 0.0)
        acc = acc + _dot((u * u).astype(BF16), w2_ref[f0:f0 + f_chunk, :])
    o_ref[...] = x + _rms(acc, gpost_ref[...])


def _mlp(x2d, g_pre, w1, w2, g_post, tm=512):
    m, d = x2d.shape
    tm = min(tm, m)
    full = lambda a: pl.BlockSpec(a.shape, lambda i: (0, 0))
    g_pre, g_post = g_pre.reshape(1, d), g_post.reshape(1, d)
    return pl.pallas_call(
        functools.partial(_mlp_kernel, f_chunk=1024), grid=(m // tm,),
        in_specs=[pl.BlockSpec((tm, d), lambda i: (i, 0)), full(g_pre), full(w1), full(w2), full(g_post)],
        out_specs=pl.BlockSpec((tm, d), lambda i: (i, 0)),
        out_shape=jax.ShapeDtypeStruct((m, d), F32),
        compiler_params=_cparams(1), name="mlp",
    )(x2d, g_pre, w1, w2, g_post)


def _pad_heads(w, n_heads, hd):
    k = w.shape[0]
    w = w.reshape(k, n_heads, hd)
    return jnp.pad(w, ((0, 0), (0, 0), (0, LANES - hd))).reshape(k, n_heads * LANES)


def _layer(x, mem, w_in, conv_w, conv_b, dt_bias, a_log, d_skip, ssd_norm, cmp_pos, cmp_w1, cmp_w2,
           rel_bias, w_br_ssd, w_br_nsa, w_out, w_xq, w_xkv, w_xo, w_ff1, w_ff2,
           n_mix_pre, n_mix_post, n_x_pre, n_x_post, n_mem, n_ffn_pre, n_ffn_post):
    b, s, d = x.shape
    G, R, HD = NSA_KV_HEADS, NSA_REP, NSA_HEAD_DIM
    d_inner = 2 * d
    n_ssd_heads = d_inner // SSD_HEAD_DIM
    conv_dim = d_inner + 2 * SSD_GROUPS * SSD_STATE
    nsa_w, kv_w = NSA_HEADS * HD, G * HD
    sizes = (d_inner, conv_dim, n_ssd_heads, nsa_w, 6 * kv_w, 3 * NSA_HEADS, 2 * d)
    offs = np.concatenate([[0], np.cumsum(sizes)])
    seg = lambda i: w_in[:, offs[i]:offs[i + 1]]
    w_z, w_xbc, w_dt, w_q, w_kv, w_gate, w_mg = (seg(i) for i in range(7))
    w_small = jnp.pad(jnp.concatenate([w_dt, w_gate], axis=1),
                      ((0, 0), (0, LANES - n_ssd_heads - 3 * NSA_HEADS)))
    w_kv6 = w_kv.reshape(d, 6, kv_w)
    w_cmp = jnp.concatenate([w_kv6[:, 0], w_kv6[:, 1]], axis=1)
    bf = lambda a: a.astype(BF16)

    x2d = x.reshape(b * s, d)
    z, xbc = _norm_matmul(x2d, n_mix_pre, [bf(w_z), bf(w_xbc)], [BF16, BF16])
    w_gl = jnp.transpose(w_gate.reshape(d, 3, G, R), (2, 1, 3, 0)).reshape(G, 3 * R, d)
    w_gl = jnp.pad(w_gl, ((0, 0), (0, 16 - 3 * R), (0, 0))).reshape(G * 16, d)
    small, dtT, glT, qT, kv_cmp, vslT, vwT = _norm_matmul(
        x2d, n_mix_pre,
        [bf(w_small), bf(w_dt.T), bf(w_gl), bf(w_q.T * (HD ** -0.5 * LOG2E)), bf(w_cmp),
         bf(w_kv6[:, 3].T), bf(w_kv6[:, 5].T)],
        [F32, F32, F32, BF16, BF16, BF16, BF16],
        transposed=(False, True, True, True, False, True, True))
    ksl, kw, mg = _norm_matmul(
        x2d, n_mix_pre, [bf(_pad_heads(w_kv6[:, 2], G, HD)), bf(_pad_heads(w_kv6[:, 4], G, HD)), bf(w_mg)],
        [BF16] * 3)

    y_ssd = _ssd_mixer(z.reshape(b, s, -1), xbc.reshape(b, s, -1), small.reshape(b, s, -1), dtT,
                       conv_w, conv_b, dt_bias, a_log, d_skip, ssd_norm)

    nr = s // CMP_STRIDE
    kr = kv_cmp.reshape(b, nr, CMP_STRIDE, 2, G, HD)
    kr = jnp.transpose(kr, (3, 0, 4, 1, 2, 5)).reshape(2, b, G, nr, CMP_STRIDE * HD)
    pos = jnp.broadcast_to(bf(cmp_pos).reshape(2, 1, CMP_BLOCK * HD), (2, SUBLANES, CMP_BLOCK * HD))
    w2p = jnp.pad(bf(cmp_w2), ((0, 0), (0, 0), (0, LANES - HD)))
    kvc = _compress(kr, pos, bf(cmp_w1), w2p)
    vcT = jnp.swapaxes(kvc[1][..., :HD], -1, -2)
    cb, tp = _nsa_tables(rel_bias, nr)
    y_nsa = _nsa_attention(qT.reshape(NSA_HEADS, HD, b * s), kvc[0], vcT,
                           ksl.reshape(b, s, -1), vslT.reshape(G, HD, b * s),
                           kw.reshape(b, s, -1), vwT.reshape(G, HD, b * s), cb, tp,
                           glT.reshape(G, 16, b * s))

    x1 = _merge(x2d, y_ssd.reshape(b * s, -1), y_nsa, mg, bf(w_br_ssd), bf(w_br_nsa), bf(w_out), n_mix_post)

    (kv_mem,) = _norm_matmul(mem.reshape(-1, d), n_mem, [bf(w_xkv)], [BF16])
    x2 = _xattn(x1.reshape(b, s, d), kv_mem.reshape(b, mem.shape[1], -1), n_x_pre, bf(w_xq), bf(w_xo),
                n_x_post)

    x3 = _mlp(x2.reshape(b * s, d), n_ffn_pre, bf(w_ff1), bf(w_ff2), n_ffn_post)
    return x3.reshape(b, s, d)


def kernel(x, mem, w_in, ssd_conv_w, ssd_conv_b, ssd_dt_bias, ssd_a_log, ssd_d_skip, ssd_norm, cmp_pos,
           cmp_w1, cmp_w2, rel_bias, w_br_ssd, w_br_nsa, w_out, w_xq, w_xkv, w_xo, w_ff1, w_ff2,
           norm_mix_pre, norm_mix_post, norm_x_pre, norm_x_post, norm_mem, norm_ffn_pre, norm_ffn_post):
    for l in range(w_in.shape[0]):
        x = _layer(x, mem, w_in[l], ssd_conv_w[l], ssd_conv_b[l], ssd_dt_bias[l], ssd_a_log[l],
                   ssd_d_skip[l], ssd_norm[l], cmp_pos[l], cmp_w1[l], cmp_w2[l], rel_bias,
                   w_br_ssd[l], w_br_nsa[l], w_out[l], w_xq[l], w_xkv[l], w_xo[l], w_ff1[l], w_ff2[l],
                   norm_mix_pre[l], norm_mix_post[l], norm_x_pre[l], norm_x_post[l], norm_mem[l],
                   norm_ffn_pre[l], norm_ffn_post[l])
    return x
```

```python
import functools
import math

import numpy as np
import jax
import jax.numpy as jnp
from jax import lax
from jax.experimental import pallas as pl
from jax.experimental.pallas import tpu as pltpu

F32 = jnp.float32
BF16 = jnp.bfloat16

NORM_EPS = 1e-6
NEG_INF = -1e30
FORCE_SCORE = 1e9
SEL_PENALTY = -1e9

LANES = 128
SUBLANES = 8
VMEM_LIMIT = 56 * 1024 * 1024

SSD_HEAD_DIM = 64
SSD_GROUPS = 8
SSD_STATE = 128
SSD_CONV = 4
SSD_CHUNK = 128
NSA_HEADS = 16
NSA_KV_HEADS = 4
NSA_HEAD_DIM = 64
NSA_REP = NSA_HEADS // NSA_KV_HEADS
CMP_BLOCK = 32
CMP_STRIDE = 16
SLC_BLOCK = 64
SLC_TOPK = 16
WINDOW = 512
NSA_TQ = 256
LOG2E = math.log2(math.e)
REL_BUCKETS = 32
REL_MAX_DIST = 128
X_HEADS = 4
X_HEAD_DIM = 128

TILE_DIAG, TILE_NEAR, TILE_NONE, TILE_EDGE, TILE_ZERO = range(5)


def _cparams(n_grid):
    return pltpu.CompilerParams(dimension_semantics=("arbitrary",) * n_grid,
                                vmem_limit_bytes=VMEM_LIMIT)


def _sigmoid(x):
    return 1.0 / (1.0 + jnp.exp(-x))


def _rms(x, g):
    return x * lax.rsqrt(jnp.mean(x * x, axis=-1, keepdims=True) + NORM_EPS) * g


def _dot(a, b):
    return jnp.dot(a, b, preferred_element_type=F32)


def _dot_nt(a, b):
    return lax.dot_general(a, b, (((1,), (1,)), ((), ())), preferred_element_type=F32)


def _split3(x):
    x1 = x.astype(BF16)
    r1 = x - x1.astype(F32)
    x2 = r1.astype(BF16)
    x3 = (r1 - x2.astype(F32)).astype(BF16)
    return x1, x2, x3


def _norm_matmul_kernel(x_ref, g_ref, *refs, transposed, n_chunk):
    n_out = len(transposed)
    w_refs, o_refs = refs[:n_out], refs[n_out:]
    h = _rms(x_ref[...], g_ref[...]).astype(BF16)
    for w_ref, o_ref, tr in zip(w_refs, o_refs, transposed):
        if tr:
            o_ref[...] = _dot_nt(w_ref[...], h).astype(o_ref.dtype)
            continue
        n = w_ref.shape[1]
        step = min(n, n_chunk)
        for n0 in range(0, n, step):
            o_ref[:, n0:n0 + step] = _dot(h, w_ref[:, n0:n0 + step]).astype(o_ref.dtype)


def _norm_matmul(x2d, g, ws, out_dtypes, transposed=None, tm=512):
    m, k = x2d.shape
    tm = min(tm, m)
    transposed = tuple(transposed) if transposed is not None else (False,) * len(ws)
    in_specs = [pl.BlockSpec((tm, k), lambda i: (i, 0)), pl.BlockSpec((1, k), lambda i: (0, 0))]
    in_specs += [pl.BlockSpec(w.shape, lambda i: (0, 0)) for w in ws]
    out_specs, out_shape = [], []
    for w, dt, tr in zip(ws, out_dtypes, transposed):
        if tr:
            out_specs.append(pl.BlockSpec((w.shape[0], tm), lambda i: (0, i)))
            out_shape.append(jax.ShapeDtypeStruct((w.shape[0], m), dt))
        else:
            out_specs.append(pl.BlockSpec((tm, w.shape[1]), lambda i: (i, 0)))
            out_shape.append(jax.ShapeDtypeStruct((m, w.shape[1]), dt))
    return pl.pallas_call(
        functools.partial(_norm_matmul_kernel, transposed=transposed, n_chunk=1024),
        grid=(m // tm,), in_specs=in_specs, out_specs=out_specs, out_shape=out_shape,
        compiler_params=_cparams(1), name="norm_proj",
    )(x2d, g.reshape(1, k), *ws)


def _softplus(x):
    return jnp.maximum(x, 0.0) + jnp.log(1.0 + jnp.exp(-jnp.abs(x)))


def _ssd_kernel(xbc_ref, prev_ref, z_ref, dt_ref, dtT_ref, cw_ref, cb_ref, dtb_ref, dtbT_ref,
                alog_ref, alogT_ref, dskip_ref, nw_ref, y_ref, state_ref, xc_ref, ybuf_ref,
                *, n_heads, d_inner):
    L, P, N, G = SSD_CHUNK, SSD_HEAD_DIM, SSD_STATE, SSD_GROUPS
    conv_dim = xc_ref.shape[1]
    c = pl.program_id(1)

    @pl.when(c == 0)
    def _():
        state_ref[...] = jnp.zeros_like(state_ref)

    cw = 512
    prev_rows = prev_ref.shape[0]
    row = lax.broadcasted_iota(jnp.int32, (prev_rows, cw), 0)
    for j in range(conv_dim // cw):
        sl = slice(j * cw, (j + 1) * cw)
        cur = xbc_ref[:, sl].astype(F32)
        prev = jnp.where(c == 0, 0.0, prev_ref[:, sl].astype(F32))
        acc = cb_ref[:, sl] + cw_ref[SSD_CONV - 1:SSD_CONV, sl] * cur
        for k in range(1, SSD_CONV):
            rc = pltpu.roll(cur, k, 0)
            rp = pltpu.roll(prev, k, 0)
            head = jnp.where(row < k, rp, rc[:prev_rows])
            shifted = jnp.concatenate([head, rc[prev_rows:]], axis=0)
            acc = acc + cw_ref[SSD_CONV - 1 - k:SSD_CONV - k, sl] * shifted
        xc_ref[:, sl] = acc * _sigmoid(acc)

    dt = _softplus(dt_ref[:, 0:n_heads] + dtb_ref[...])
    a = dt * (-jnp.exp(alog_ref[...]))
    dtT = _softplus(dtT_ref[...] + dtbT_ref[...])
    aT = dtT * (-jnp.exp(alogT_ref[...]))
    ri = lax.broadcasted_iota(jnp.int32, (L, L), 0)
    ci = lax.broadcasted_iota(jnp.int32, (L, L), 1)
    causal = ci <= ri
    tri = jnp.where(causal, 1.0, 0.0).astype(BF16)
    triu = jnp.where(ri <= ci, 1.0, 0.0).astype(BF16)
    a_cs = sum(_dot(tri, ai) for ai in _split3(a))
    a_csT = sum(_dot(ai, triu) for ai in _split3(aT))
    lane = lax.broadcasted_iota(jnp.int32, (L, 2 * P), 1)
    lo_half = lane < P
    lane1 = lax.broadcasted_iota(jnp.int32, (1, 2 * P), 1) < P

    heads_per_group = n_heads // G
    for g in range(G):
        b_g = xc_ref[:, d_inner + g * N:d_inner + (g + 1) * N]
        c_g = xc_ref[:, d_inner + G * N + g * N:d_inner + G * N + (g + 1) * N]
        c_gb = c_g.astype(BF16)
        cb = _dot_nt(c_gb, b_g.astype(BF16))
        b_gT = b_g.T.astype(BF16)
        for pp in range(heads_per_group // 2):
            pair = g * (heads_per_group // 2) + pp
            h0, h1 = 2 * pair, 2 * pair + 1
            xs = xc_ref[:, pair * 2 * P:(pair + 1) * 2 * P]
            dt_pair = jnp.where(lo_half, dt[:, h0:h0 + 1], dt[:, h1:h1 + 1])
            xd = xs * dt_pair
            xd_b = xd.astype(BF16)
            ys = []
            for h in (h0, h1):
                seg = jnp.where(causal, a_cs[:, h:h + 1] - a_csT[h:h + 1, :], NEG_INF)
                mat = (cb * jnp.exp(seg)).astype(BF16)
                ys.append(_dot(mat, xd_b))
            y = jnp.where(lo_half, ys[0], ys[1])
            cs_pair = jnp.where(lo_half, a_cs[:, h0:h0 + 1], a_cs[:, h1:h1 + 1])
            st = state_ref[pair]
            y = y + _dot(c_gb, st.astype(BF16)) * jnp.exp(cs_pair)
            tot = jnp.where(lane1, a_csT[h0:h0 + 1, L - 1:L], a_csT[h1:h1 + 1, L - 1:L])
            xdd = (xd * jnp.exp(tot - cs_pair)).astype(BF16)
            state_ref[pair] = st * jnp.exp(tot) + _dot(b_gT, xdd)
            ybuf_ref[:, pair * 2 * P:(pair + 1) * 2 * P] = y + xs * dskip_ref[:, pair * 2 * P:(pair + 1) * 2 * P]

    gw = d_inner // G
    for g in range(G):
        sl = slice(g * gw, (g + 1) * gw)
        zz = z_ref[:, sl].astype(F32)
        yg = ybuf_ref[:, sl] * (zz * _sigmoid(zz))
        y_ref[:, sl] = _rms(yg, nw_ref[:, sl]).astype(y_ref.dtype)


def _ssd_mixer(z, xbc, small, dtT, conv_w, conv_b, dt_bias, a_log, d_skip, norm_w):
    b, s, d_inner = z.shape
    conv_dim = xbc.shape[-1]
    n_heads = d_inner // SSD_HEAD_DIM
    L = SSD_CHUNK
    nc = s // L
    prev_rows = 16
    xbc_prev = xbc.reshape(b, s // prev_rows, prev_rows, conv_dim)
    blocks_per_chunk = L // prev_rows
    kern = functools.partial(_ssd_kernel, n_heads=n_heads, d_inner=d_inner)
    const = lambda shape: pl.BlockSpec(shape, lambda i, j: (0,) * len(shape))
    return pl.pallas_call(
        kern, grid=(b, nc),
        in_specs=[
            pl.BlockSpec((None, L, conv_dim), lambda i, j: (i, j, 0)),
            pl.BlockSpec((None, None, prev_rows, conv_dim),
                         lambda i, j: (i, jnp.maximum(j * blocks_per_chunk - 1, 0), 0, 0)),
            pl.BlockSpec((None, L, d_inner), lambda i, j: (i, j, 0)),
            pl.BlockSpec((None, L, small.shape[-1]), lambda i, j: (i, j, 0)),
            pl.BlockSpec((n_heads, L), lambda i, j: (0, i * nc + j)),
            const((SSD_CONV, conv_dim)), const((1, conv_dim)),
            const((1, n_heads)), const((n_heads, 1)),
            const((1, n_heads)), const((n_heads, 1)),
            const((1, d_inner)), const((1, d_inner)),
        ],
        out_specs=pl.BlockSpec((None, L, d_inner), lambda i, j: (i, j, 0)),
        out_shape=jax.ShapeDtypeStruct((b, s, d_inner), BF16),
        scratch_shapes=[pltpu.VMEM((n_heads // 2, SSD_STATE, 2 * SSD_HEAD_DIM), F32),
                        pltpu.VMEM((L, conv_dim), F32),
                        pltpu.VMEM((L, d_inner), F32)],
        compiler_params=_cparams(2), name="ssd",
    )(xbc, xbc_prev, z, small, dtT, conv_w.astype(F32), conv_b.reshape(1, -1).astype(F32),
      dt_bias.reshape(1, -1), dt_bias.reshape(-1, 1), a_log.reshape(1, -1), a_log.reshape(-1, 1),
      jnp.repeat(d_skip, SSD_HEAD_DIM).reshape(1, -1), norm_w.reshape(1, -1))


def _compress_kernel(kr_ref, pos_ref, w1_ref, w2_ref, o_ref):
    nr, half = kr_ref.shape
    kr = kr_ref[...]
    p1 = _dot(kr, w1_ref[0:half, :])
    p2 = _dot(kr, w1_ref[half:2 * half, :])
    pb = _dot(pos_ref[...], w1_ref[...])[0:1]
    hid = p1 + pltpu.roll(p2, nr - 1, 0) + pb
    hid = hid * _sigmoid(hid)
    out = _dot(hid.astype(BF16), w2_ref[...])
    rows = lax.broadcasted_iota(jnp.int32, out.shape, 0)
    o_ref[...] = jnp.where(rows < nr - 1, out, 0.0).astype(o_ref.dtype)


def _compress(kr, pos, w1, w2p):
    two, b, g, nr, half = kr.shape
    hidden = w1.shape[-1]
    return pl.pallas_call(
        _compress_kernel,
        grid=(two, b, g),
        in_specs=[
            pl.BlockSpec((None, None, None, nr, half), lambda t, i, j: (t, i, j, 0, 0)),
            pl.BlockSpec((None, SUBLANES, 2 * half), lambda t, i, j: (t, 0, 0)),
            pl.BlockSpec((None, 2 * half, hidden), lambda t, i, j: (t, 0, 0)),
            pl.BlockSpec((None, hidden, LANES), lambda t, i, j: (t, 0, 0)),
        ],
        out_specs=pl.BlockSpec((None, None, None, nr, LANES), lambda t, i, j: (t, i, j, 0, 0)),
        out_shape=jax.ShapeDtypeStruct((two, b, g, nr, LANES), BF16),
        compiler_params=_cparams(3), name="nsa_compress",
    )(kr, pos, w1, w2p)


def _colmax(a):
    return jnp.max(a, axis=0, keepdims=True)


def _colsum(a):
    return jnp.sum(a, axis=0, keepdims=True)


class _SoftmaxPipe:
    def __init__(self, s_ref, pe_ref, acc_ref, hd):
        self.s, self.pe, self.acc, self.hd = s_ref, pe_ref, acc_ref, hd

    def start(self, first_scores, slot):
        cols = first_scores.shape[1]
        self.s[slot] = first_scores
        self.pe[1 - slot] = jnp.zeros(self.pe.shape[1:], self.pe.dtype)
        self.acc[...] = jnp.zeros(self.acc.shape, F32)
        return _colmax(first_scores), jnp.ones((1, cols), F32), jnp.full((1, cols), NEG_INF, F32)

    def step(self, slot, carry, v_prev, next_scores):
        cm_cur, alpha_prev, m_old = carry
        self.acc[...] = alpha_prev * self.acc[...] + _dot(v_prev, self.pe[1 - slot])
        cm_next = cm_cur
        if next_scores is not None:
            s_next = next_scores()
            self.s[1 - slot] = s_next
            cm_next = _colmax(s_next)
        m_new = jnp.maximum(m_old, cm_cur)
        self.pe[slot] = jnp.exp2(self.s[slot] - m_new).astype(self.pe.dtype)
        return cm_next, jnp.exp2(m_old - m_new), m_new

    def finish(self, slot_last, carry, v_last):
        _, alpha_prev, _ = carry
        acc = alpha_prev * self.acc[...] + _dot(v_last, self.pe[slot_last])
        return acc[0:self.hd] * (1.0 / acc[self.hd:self.hd + 1])


def _nsa_kernel(qT_ref, kc_ref, vcT_ref, ov_ref, ksl_ref, vslT_ref, kw_ref, vwT_ref, cb_ref, tp_ref,
                gl_ref, o_ref, q0_ref, qa_ref, rank_ref, vsa_ref, vwa_ref, s_ref, pe_ref, acc_ref, ws_ref,
                wpe_ref, wacc_ref, *, n_slc, top_k):
    T, R, HD = NSA_TQ, NSA_REP, NSA_HEAD_DIM
    nr = kc_ref.shape[0]
    per_tile = T // CMP_STRIDE
    front = nr - per_tile
    half = LANES // 2
    qi = pl.program_id(2)
    colmax, colsum = _colmax, _colsum

    @pl.when(qi == 0)
    def _():
        for src, dst in ((vslT_ref, vsa_ref), (vwT_ref, vwa_ref)):
            dst[0:HD, :] = src[...]
            dst[HD:, :] = jnp.ones((dst.shape[0] - HD, dst.shape[1]), dst.dtype)

    qT = jnp.concatenate([qT_ref[r] for r in range(R)], axis=1)
    q0_ref[0:HD, :] = qT
    q0_ref[HD:2 * HD, :] = jnp.zeros((HD, R * T), BF16)
    qa_ref[0:HD, :] = qT

    n_win = WINDOW // T

    def win_offset(rel):
        return pl.multiple_of(jnp.maximum(qi - rel, 0) * T, T)

    def win_scores(rel):
        tile = TILE_EDGE if rel == n_win else (TILE_ZERO if rel >= 2 else (TILE_NEAR if rel == 1 else TILE_DIAG))
        if rel > 0:
            tile = jnp.where(qi >= rel, tile, TILE_NONE)
        return _dot(kw_ref[pl.ds(win_offset(rel), T), :], q0_ref[...]) + tp_ref[tile]

    def win_values(rel):
        return vwa_ref[:, pl.ds(win_offset(rel), T)]

    win = _SoftmaxPipe(ws_ref, wpe_ref, wacc_ref, HD)
    rels = list(range(n_win, -1, -1))
    win_carry = win.start(win_scores(rels[0]), 0)

    def window_steps(carry, after_step):
        for i, rel in enumerate(rels):
            nxt = functools.partial(win_scores, rels[i + 1]) if i + 1 < len(rels) else None
            carry = win.step(i % 2, carry, win_values(rels[max(i - 1, 0)]), nxt)
            after_step(i)
        return win.finish((len(rels) - 1) % 2, carry, win_values(0))

    off = pl.multiple_of(front - qi * per_tile, SUBLANES)
    lg = _dot(kc_ref[...], q0_ref[...]) + cb_ref[pl.ds(off, nr), :]
    m_c = colmax(lg)
    e = jnp.exp2(lg - m_c)
    p = e * jnp.where(m_c > 0.5 * NEG_INF, 1.0 / colsum(e), 0.0)
    o_cmp = _dot(vcT_ref[...], p.astype(BF16))

    psum = p[:, 0:T]
    for r in range(1, R):
        psum = psum + p[:, r * T:(r + 1) * T]
    imp = sum(_dot(ov_ref[...], pi) for pi in _split3(psum))
    nio = lax.broadcasted_iota(jnp.int32, (half, T), 0)
    tio = lax.broadcasted_iota(jnp.int32, (half, T), 1)
    tb = (qi * T + tio) // SLC_BLOCK
    forced = (nio == 0) | (nio == tb) | (nio == tb - 1)
    imp = jnp.where(forced, FORCE_SCORE, imp)
    imp = jnp.where(nio > tb, NEG_INF, imp)
    n_grp = half // SUBLANES
    grp = [imp[SUBLANES * v:SUBLANES * (v + 1)] for v in range(n_grp)]
    nio8 = lax.broadcasted_iota(jnp.int32, (SUBLANES, T), 0)
    hits = [jnp.zeros((SUBLANES, T), F32) for _ in range(n_grp)]

    def count_outranking(m_groups):
        for mm in range(m_groups[0] * SUBLANES, min((m_groups[-1] + 1) * SUBLANES, n_slc)):
            rowv = imp[mm:mm + 1, :]
            for v in range(n_grp):
                if SUBLANES * v > mm:
                    hit = jnp.where(rowv >= grp[v], 1.0, 0.0)
                elif SUBLANES * v + SUBLANES - 1 < mm:
                    hit = jnp.where(rowv > grp[v], 1.0, 0.0)
                else:
                    hit = jnp.where(nio8 > mm - SUBLANES * v, jnp.where(rowv >= grp[v], 1.0, 0.0),
                                    jnp.where(rowv > grp[v], 1.0, 0.0))
                hits[v] = hits[v] + hit

    n_mg = pl.cdiv(n_slc, SUBLANES)
    shares = np.array_split(np.arange(n_mg), n_win + 1)
    o_win = window_steps(win_carry, lambda i: count_outranking(shares[i]) if len(shares[i]) else None)

    rank = jnp.concatenate(hits, axis=0)
    pen = jnp.where(rank < top_k, jnp.where(nio <= tb, 0.0, SEL_PENALTY), SEL_PENALTY).astype(BF16)
    qa_ref[HD:2 * HD, :] = jnp.concatenate([pen] * R, axis=1)

    krow = lax.broadcasted_iota(jnp.int32, (T, LANES), 0)
    klane = lax.broadcasted_iota(jnp.int32, (T, LANES), 1)

    def sel_keys(kj):
        kt = ksl_ref[pl.ds(pl.multiple_of(kj * T, T), T), :]
        blk = kj * (T // SLC_BLOCK) + krow // SLC_BLOCK
        return jnp.where(klane >= half, jnp.where(klane - half == blk, 1.0, 0.0).astype(BF16), kt)

    def sel_scores(kj):
        return _dot(sel_keys(kj), qa_ref[...])

    def sel_values(kj):
        return vsa_ref[:, pl.ds(pl.multiple_of(kj * T, T), T)]

    n_far = jnp.maximum(qi - 1, 0)
    odd = n_far % 2
    sel = _SoftmaxPipe(s_ref, pe_ref, acc_ref, HD)

    def far_step(kj, slot, carry):
        return sel.step(slot, carry, sel_values(jnp.maximum(kj - 1, 0)), functools.partial(sel_scores, kj + 1))

    carry = sel.start(sel_scores(0), odd)
    carry = lax.cond(odd == 1, lambda c: far_step(0, 1, c), lambda c: c, carry)
    carry = lax.fori_loop(
        0, n_far // 2, lambda j, c: far_step(odd + 2 * j + 1, 1, far_step(odd + 2 * j, 0, c)), carry)
    s_near = s_ref[0] + tp_ref[jnp.where(qi >= 1, TILE_NEAR, TILE_NONE)]
    s_ref[0] = s_near
    carry = (colmax(s_near),) + tuple(carry[1:])
    carry = sel.step(0, carry, sel_values(jnp.maximum(n_far - 1, 0)),
                     lambda: sel_scores(qi) + tp_ref[TILE_DIAG])
    carry = sel.step(1, carry, sel_values(n_far), None)
    o_slc = sel.finish(1, carry, sel_values(qi))

    sg = _sigmoid(gl_ref[...])
    for r in range(R):
        cols = slice(r * T, (r + 1) * T)
        o = (sg[r:r + 1] * o_cmp[:, cols] + sg[R + r:R + r + 1] * o_slc[:, cols]
             + sg[2 * R + r:2 * R + r + 1] * o_win[:, cols])
        o_ref[r * HD:(r + 1) * HD, :] = o.astype(o_ref.dtype)


def _t5_bucket_np(dist):
    n = np.maximum(dist, 0)
    max_exact = REL_BUCKETS // 2
    nf = np.maximum(n, 1).astype(np.float32)
    large = max_exact + (np.log(nf / np.float32(max_exact)) / np.float32(math.log(REL_MAX_DIST / max_exact))
                         * np.float32(REL_BUCKETS - max_exact)).astype(np.int32)
    large = np.minimum(large, REL_BUCKETS - 1)
    return np.where(n < max_exact, n, large).astype(np.int32)


def _nsa_tables(rel_bias, nr):
    T, G, R = NSA_TQ, NSA_KV_HEADS, NSA_REP
    table = rel_bias.astype(F32) * LOG2E
    per_tile = T // CMP_STRIDE
    front = nr - per_tile
    t = np.arange(T)[None, :]

    def lay(a):
        keys = a.shape[0]
        return jnp.transpose(a, (2, 0, 1)).reshape(G, R, keys, T).transpose(0, 2, 1, 3).reshape(G, keys, R * T)

    cend = CMP_STRIDE * (np.arange(front + nr)[:, None] - front) + CMP_BLOCK - 1
    dist = t - cend
    def lookup(d):
        onehot = jax.nn.one_hot(_t5_bucket_np(d), REL_BUCKETS, dtype=F32)
        return jnp.einsum("ktb,bh->kth", onehot, table, precision=lax.Precision.HIGHEST)

    cb = jnp.where((dist >= 0)[..., None], lookup(dist), NEG_INF)
    k = np.arange(T)[:, None]
    assert 2 * T - (T - 1) >= REL_MAX_DIST and WINDOW % T == 0 and WINDOW // T >= 2
    last = table[REL_BUCKETS - 1]
    diag = jnp.where((t - k >= 0)[..., None], lookup(t - k) - last, NEG_INF)
    near = lookup(T + t - k) - last
    none = jnp.full_like(near, NEG_INF)
    edge = jnp.where((k > t)[..., None], jnp.zeros_like(near), NEG_INF)
    tiles = [None] * 5
    tiles[TILE_DIAG], tiles[TILE_NEAR], tiles[TILE_NONE], tiles[TILE_EDGE] = diag, near, none, edge
    tiles[TILE_ZERO] = jnp.zeros_like(near)
    tp = jnp.stack([lay(a) for a in tiles], axis=1)
    return lay(cb), tp


def _overlap_matrix(nr, n_slc):
    half = LANES // 2
    n = np.arange(half)[:, None]
    c = np.arange(nr)[None, :]
    ov = ((CMP_STRIDE * c <= SLC_BLOCK * n + SLC_BLOCK - 1) & (CMP_STRIDE * c + CMP_BLOCK - 1 >= SLC_BLOCK * n)
          & (n < n_slc) & (c < nr - 1))
    return jnp.asarray(ov.astype(np.float32), dtype=BF16)


def _nsa_attention(qT, kc, vcT, ksl, vslT, kw, vwT, cb, tp, glT):
    n_heads, hd, _ = qT.shape
    b, s, _ = ksl.shape
    G, R, T = NSA_KV_HEADS, NSA_REP, NSA_TQ
    nr = kc.shape[-2]
    nq = s // T
    n_slc = s // SLC_BLOCK
    assert n_slc <= LANES // 2 and s % T == 0 and nr % SUBLANES == 0
    ov = _overlap_matrix(nr, n_slc)
    kern = functools.partial(_nsa_kernel, n_slc=n_slc, top_k=min(SLC_TOPK, n_slc))
    k_spec = pl.BlockSpec((None, s, LANES), lambda i, j, k: (i, 0, j))
    vT_spec = pl.BlockSpec((None, hd, s), lambda i, j, k: (j, 0, i))
    ones_rows = 16
    pipe_scratch = [pltpu.VMEM((2, T, R * T), F32), pltpu.VMEM((2, T, R * T), BF16),
                    pltpu.VMEM((hd + ones_rows, R * T), F32)]
    value_scratch = [pltpu.VMEM((hd + ones_rows, s), BF16)] * 2
    return pl.pallas_call(
        kern, grid=(b, G, nq),
        in_specs=[
            pl.BlockSpec((R, hd, T), lambda i, j, k: (j, 0, i * nq + k)),
            pl.BlockSpec((None, None, nr, LANES), lambda i, j, k: (i, j, 0, 0)),
            pl.BlockSpec((None, None, hd, nr), lambda i, j, k: (i, j, 0, 0)),
            pl.BlockSpec(ov.shape, lambda i, j, k: (0, 0)),
            k_spec, vT_spec, k_spec, vT_spec,
            pl.BlockSpec((None,) + cb.shape[1:], lambda i, j, k: (j, 0, 0)),
            pl.BlockSpec((None,) + tp.shape[1:], lambda i, j, k: (j, 0, 0, 0)),
            pl.BlockSpec((None, 16, T), lambda i, j, k: (j, 0, i * nq + k)),
        ],
        out_specs=pl.BlockSpec((R * hd, T), lambda i, j, k: (j, i * nq + k)),
        out_shape=jax.ShapeDtypeStruct((n_heads * hd, b * s), BF16),
        scratch_shapes=[pltpu.VMEM((2 * hd, R * T), BF16), pltpu.VMEM((2 * hd, R * T), BF16),
                        pltpu.VMEM((LANES // 2, T), F32)]
        + value_scratch + pipe_scratch + pipe_scratch,
        compiler_params=_cparams(3), name="nsa_attention",
    )(qT, kc, vcT, ov, ksl, vslT, kw, vwT, cb, tp, glT)


def _merge_kernel(x_ref, ys_ref, ynT_ref, mg_ref, ws_ref, wn_ref, wo_ref, g_ref, o_ref):
    d = x_ref.shape[1]
    a = _dot(ys_ref[...], ws_ref[...])
    bb = lax.dot_general(ynT_ref[...], wn_ref[...], (((0,), (0,)), ((), ())), preferred_element_type=F32)
    mg = _sigmoid(mg_ref[...].astype(F32))
    mixed = mg[:, :d] * a + mg[:, d:] * bb
    o = _dot(mixed.astype(BF16), wo_ref[...])
    o_ref[...] = x_ref[...] + _rms(o, g_ref[...])


def _merge(x2d, y_ssd, y_nsaT, mg, w_s, w_n, w_o, g, tm=512):
    m, d = x2d.shape
    tm = min(tm, m)
    row = lambda a: pl.BlockSpec((tm, a.shape[1]), lambda i: (i, 0))
    full = lambda a: pl.BlockSpec(a.shape, lambda i: (0, 0))
    g = g.reshape(1, d)
    return pl.pallas_call(
        _merge_kernel, grid=(m // tm,),
        in_specs=[row(x2d), row(y_ssd), pl.BlockSpec((y_nsaT.shape[0], tm), lambda i: (0, i)), row(mg),
                  full(w_s), full(w_n), full(w_o), full(g)],
        out_specs=pl.BlockSpec((tm, d), lambda i: (i, 0)),
        out_shape=jax.ShapeDtypeStruct((m, d), F32),
        compiler_params=_cparams(1), name="merge",
    )(x2d, y_ssd, y_nsaT, mg, w_s, w_n, w_o, g)


def _xattn_kernel(x_ref, kv_ref, gpre_ref, wq_ref, wo_ref, gpost_ref, o_ref):
    x = x_ref[...]
    h = _rms(x, gpre_ref[...]).astype(BF16)
    q = _dot(h, wq_ref[...])
    width = X_HEADS * X_HEAD_DIM
    scale = X_HEAD_DIM ** -0.5
    outs = []
    for hh in range(X_HEADS):
        sl = slice(hh * X_HEAD_DIM, (hh + 1) * X_HEAD_DIM)
        lg = _dot_nt(q[:, sl].astype(BF16), kv_ref[:, sl]) * scale
        e = jnp.exp(lg - jnp.max(lg, axis=-1, keepdims=True))
        p = e / jnp.sum(e, axis=-1, keepdims=True)
        outs.append(_dot(p.astype(BF16), kv_ref[:, width + hh * X_HEAD_DIM:width + (hh + 1) * X_HEAD_DIM]))
    o = _dot(jnp.concatenate(outs, axis=-1).astype(BF16), wo_ref[...])
    o_ref[...] = x + _rms(o, gpost_ref[...])


def _xattn(x3d, kv, g_pre, w_q, w_o, g_post, tm=512):
    b, s, d = x3d.shape
    tm = min(tm, s)
    ml = kv.shape[1]
    full = lambda a: pl.BlockSpec(a.shape, lambda i, j: (0, 0))
    g_pre, g_post = g_pre.reshape(1, d), g_post.reshape(1, d)
    return pl.pallas_call(
        _xattn_kernel, grid=(b, s // tm),
        in_specs=[pl.BlockSpec((None, tm, d), lambda i, j: (i, j, 0)),
                  pl.BlockSpec((None, ml, kv.shape[2]), lambda i, j: (i, 0, 0)),
                  full(g_pre), full(w_q), full(w_o), full(g_post)],
        out_specs=pl.BlockSpec((None, tm, d), lambda i, j: (i, j, 0)),
        out_shape=jax.ShapeDtypeStruct((b, s, d), F32),
        compiler_params=_cparams(2), name="xattn",
    )(x3d, kv, g_pre, w_q, w_o, g_post)


def _mlp_kernel(x_ref, gpre_ref, w1_ref, w2_ref, gpost_ref, o_ref, *, f_chunk):
    x = x_ref[...]
    h = _rms(x, gpre_ref[...]).astype(BF16)
    d_ff = w1_ref.shape[1]
    acc = jnp.zeros(x.shape, F32)
    for f0 in range(0, d_ff, f_chunk):
        u = jnp.maximum(_dot(h, w1_ref[:, f0:f0 + f_chunk]), 0.0)
        acc = acc + _dot((u * u).astype(BF16), w2_ref[f0:f0 + f_chunk, :])
    o_ref[...] = x + _rms(acc, gpost_ref[...])


def _mlp(x2d, g_pre, w1, w2, g_post, tm=512):
    m, d = x2d.shape
    tm = min(tm, m)
    full = lambda a: pl.BlockSpec(a.shape, lambda i: (0, 0))
    g_pre, g_post = g_pre.reshape(1, d), g_post.reshape(1, d)
    return pl.pallas_call(
        functools.partial(_mlp_kernel, f_chunk=1024), grid=(m // tm,),
        in_specs=[pl.BlockSpec((tm, d), lambda i: (i, 0)), full(g_pre), full(w1), full(w2), full(g_post)],
        out_specs=pl.BlockSpec((tm, d), lambda i: (i, 0)),
        out_shape=jax.ShapeDtypeStruct((m, d), F32),
        compiler_params=_cparams(1), name="mlp",
    )(x2d, g_pre, w1, w2, g_post)


def _pad_heads(w, n_heads, hd):
    k = w.shape[0]
    w = w.reshape(k, n_heads, hd)
    return jnp.pad(w, ((0, 0), (0, 0), (0, LANES - hd))).reshape(k, n_heads * LANES)


def _layer(x, mem, w_in, conv_w, conv_b, dt_bias, a_log, d_skip, ssd_norm, cmp_pos, cmp_w1, cmp_w2,
           rel_bias, w_br_ssd, w_br_nsa, w_out, w_xq, w_xkv, w_xo, w_ff1, w_ff2,
           n_mix_pre, n_mix_post, n_x_pre, n_x_post, n_mem, n_ffn_pre, n_ffn_post):
    b, s, d = x.shape
    G, R, HD = NSA_KV_HEADS, NSA_REP, NSA_HEAD_DIM
    d_inner = 2 * d
    n_ssd_heads = d_inner // SSD_HEAD_DIM
    conv_dim = d_inner + 2 * SSD_GROUPS * SSD_STATE
    nsa_w, kv_w = NSA_HEADS * HD, G * HD
    sizes = (d_inner, conv_dim, n_ssd_heads, nsa_w, 6 * kv_w, 3 * NSA_HEADS, 2 * d)
    offs = np.concatenate([[0], np.cumsum(sizes)])
    seg = lambda i: w_in[:, offs[i]:offs[i + 1]]
    w_z, w_xbc, w_dt, w_q, w_kv, w_gate, w_mg = (seg(i) for i in range(7))
    w_small = jnp.pad(jnp.concatenate([w_dt, w_gate], axis=1),
                      ((0, 0), (0, LANES - n_ssd_heads - 3 * NSA_HEADS)))
    w_kv6 = w_kv.reshape(d, 6, kv_w)
    w_cmp = jnp.concatenate([w_kv6[:, 0], w_kv6[:, 1]], axis=1)
    bf = lambda a: a.astype(BF16)

    x2d = x.reshape(b * s, d)
    z, xbc = _norm_matmul(x2d, n_mix_pre, [bf(w_z), bf(w_xbc)], [BF16, BF16])
    w_gl = jnp.transpose(w_gate.reshape(d, 3, G, R), (2, 1, 3, 0)).reshape(G, 3 * R, d)
    w_gl = jnp.pad(w_gl, ((0, 0), (0, 16 - 3 * R), (0, 0))).reshape(G * 16, d)
    small, dtT, glT, qT, kv_cmp, vslT, vwT = _norm_matmul(
        x2d, n_mix_pre,
        [bf(w_small), bf(w_dt.T), bf(w_gl), bf(w_q.T * (HD ** -0.5 * LOG2E)), bf(w_cmp),
         bf(w_kv6[:, 3].T), bf(w_kv6[:, 5].T)],
        [F32, F32, F32, BF16, BF16, BF16, BF16],
        transposed=(False, True, True, True, False, True, True))
    ksl, kw, mg = _norm_matmul(
        x2d, n_mix_pre, [bf(_pad_heads(w_kv6[:, 2], G, HD)), bf(_pad_heads(w_kv6[:, 4], G, HD)), bf(w_mg)],
        [BF16] * 3)

    y_ssd = _ssd_mixer(z.reshape(b, s, -1), xbc.reshape(b, s, -1), small.reshape(b, s, -1), dtT,
                       conv_w, conv_b, dt_bias, a_log, d_skip, ssd_norm)

    nr = s // CMP_STRIDE
    kr = kv_cmp.reshape(b, nr, CMP_STRIDE, 2, G, HD)
    kr = jnp.transpose(kr, (3, 0, 4, 1, 2, 5)).reshape(2, b, G, nr, CMP_STRIDE * HD)
    pos = jnp.broadcast_to(bf(cmp_pos).reshape(2, 1, CMP_BLOCK * HD), (2, SUBLANES, CMP_BLOCK * HD))
    w2p = jnp.pad(bf(cmp_w2), ((0, 0), (0, 0), (0, LANES - HD)))
    kvc = _compress(kr, pos, bf(cmp_w1), w2p)
    vcT = jnp.swapaxes(kvc[1][..., :HD], -1, -2)
    cb, tp = _nsa_tables(rel_bias, nr)
    y_nsa = _nsa_attention(qT.reshape(NSA_HEADS, HD, b * s), kvc[0], vcT,
                           ksl.reshape(b, s, -1), vslT.reshape(G, HD, b * s),
                           kw.reshape(b, s, -1), vwT.reshape(G, HD, b * s), cb, tp,
                           glT.reshape(G, 16, b * s))

    x1 = _merge(x2d, y_ssd.reshape(b * s, -1), y_nsa, mg, bf(w_br_ssd), bf(w_br_nsa), bf(w_out), n_mix_post)

    (kv_mem,) = _norm_matmul(mem.reshape(-1, d), n_mem, [bf(w_xkv)], [BF16])
    x2 = _xattn(x1.reshape(b, s, d), kv_mem.reshape(b, mem.shape[1], -1), n_x_pre, bf(w_xq), bf(w_xo),
                n_x_post)

    x3 = _mlp(x2.reshape(b * s, d), n_ffn_pre, bf(w_ff1), bf(w_ff2), n_ffn_post)
    return x3.reshape(b, s, d)


def kernel(x, mem, w_in, ssd_conv_w, ssd_conv_b, ssd_dt_bias, ssd_a_log, ssd_d_skip, ssd_norm, cmp_pos,
           cmp_w1, cmp_w2, rel_bias, w_br_ssd, w_br_nsa, w_out, w_xq, w_xkv, w_xo, w_ff1, w_ff2,
           norm_mix_pre, norm_mix_post, norm_x_pre, norm_x_post, norm_mem, norm_ffn_pre, norm_ffn_post):
    for l in range(w_in.shape[0]):
        x = _layer(x, mem, w_in[l], ssd_conv_w[l], ssd_conv_b[l], ssd_dt_bias[l], ssd_a_log[l],
                   ssd_d_skip[l], ssd_norm[l], cmp_pos[l], cmp_w1[l], cmp_w2[l], rel_bias,
                   w_br_ssd[l], w_br_nsa[l], w_out[l], w_xq[l], w_xkv[l], w_xo[l], w_ff1[l], w_ff2[l],
                   norm_mix_pre[l], norm_mix_post[l], norm_x_pre[l], norm_x_post[l], norm_mem[l],
                   norm_ffn_pre[l], norm_ffn_post[l])
    return x
```

```python
import functools
import math

import numpy as np
import jax
import jax.numpy as jnp
from jax import lax
from jax.experimental import pallas as pl
from jax.experimental.pallas import tpu as pltpu

F32 = jnp.float32
BF16 = jnp.bfloat16

NORM_EPS = 1e-6
NEG_INF = -1e30
FORCE_SCORE = 1e9
SEL_PENALTY = NEG_INF

LANES = 128
SUBLANES = 8
VMEM_LIMIT = 56 * 1024 * 1024

SSD_HEAD_DIM = 64
SSD_GROUPS = 8
SSD_STATE = 128
SSD_CONV = 4
SSD_CHUNK = 128
NSA_HEADS = 16
NSA_KV_HEADS = 4
NSA_HEAD_DIM = 64
NSA_REP = NSA_HEADS // NSA_KV_HEADS
CMP_BLOCK = 32
CMP_STRIDE = 16
SLC_BLOCK = 64
SLC_TOPK = 16
WINDOW = 512
NSA_TQ = 256
LOG2E = math.log2(math.e)
REL_BUCKETS = 32
REL_MAX_DIST = 128
X_HEADS = 4
X_HEAD_DIM = 128

TILE_DIAG, TILE_NEAR, TILE_NONE, TILE_EDGE, TILE_ZERO = range(5)


def _cparams(n_grid):
    return pltpu.CompilerParams(dimension_semantics=("arbitrary",) * n_grid,
                                vmem_limit_bytes=VMEM_LIMIT)


def _sigmoid(x):
    return 1.0 / (1.0 + jnp.exp(-x))


def _rms(x, g):
    return x * lax.rsqrt(jnp.mean(x * x, axis=-1, keepdims=True) + NORM_EPS) * g


def _dot(a, b):
    return jnp.dot(a, b, preferred_element_type=F32)


def _dot_nt(a, b):
    return lax.dot_general(a, b, (((1,), (1,)), ((), ())), preferred_element_type=F32)


def _split3(x):
    x1 = x.astype(BF16)
    r1 = x - x1.astype(F32)
    x2 = r1.astype(BF16)
    x3 = (r1 - x2.astype(F32)).astype(BF16)
    return x1, x2, x3


def _norm_matmul_kernel(x_ref, g_ref, *refs, transposed, n_chunk):
    n_out = len(transposed)
    w_refs, o_refs = refs[:n_out], refs[n_out:]
    h = _rms(x_ref[...], g_ref[...]).astype(BF16)
    for w_ref, o_ref, tr in zip(w_refs, o_refs, transposed):
        if tr:
            o_ref[...] = _dot_nt(w_ref[...], h).astype(o_ref.dtype)
            continue
        n = w_ref.shape[1]
        step = min(n, n_chunk)
        for n0 in range(0, n, step):
            o_ref[:, n0:n0 + step] = _dot(h, w_ref[:, n0:n0 + step]).astype(o_ref.dtype)


def _norm_matmul(x2d, g, ws, out_dtypes, transposed=None, tm=512):
    m, k = x2d.shape
    tm = min(tm, m)
    transposed = tuple(transposed) if transposed is not None else (False,) * len(ws)
    in_specs = [pl.BlockSpec((tm, k), lambda i: (i, 0)), pl.BlockSpec((1, k), lambda i: (0, 0))]
    in_specs += [pl.BlockSpec(w.shape, lambda i: (0, 0)) for w in ws]
    out_specs, out_shape = [], []
    for w, dt, tr in zip(ws, out_dtypes, transposed):
        if tr:
            out_specs.append(pl.BlockSpec((w.shape[0], tm), lambda i: (0, i)))
            out_shape.append(jax.ShapeDtypeStruct((w.shape[0], m), dt))
        else:
            out_specs.append(pl.BlockSpec((tm, w.shape[1]), lambda i: (i, 0)))
            out_shape.append(jax.ShapeDtypeStruct((m, w.shape[1]), dt))
    return pl.pallas_call(
        functools.partial(_norm_matmul_kernel, transposed=transposed, n_chunk=1024),
        grid=(m // tm,), in_specs=in_specs, out_specs=out_specs, out_shape=out_shape,
        compiler_params=_cparams(1), name="norm_proj",
    )(x2d, g.reshape(1, k), *ws)


def _proj_conv_kernel(x_ref, g_ref, wz_ref, wx_ref, cw_ref, cb_ref, z_ref, xc_ref, tail_ref,
                      *, tiles_per_seq, n_chunk):
    tm = x_ref.shape[0]
    h = _rms(x_ref[...], g_ref[...]).astype(BF16)
    first = pl.program_id(0) % tiles_per_seq == 0

    @pl.when(pl.program_id(0) == 0)
    def _():
        tail_ref[...] = jnp.zeros(tail_ref.shape, F32)

    row = lax.broadcasted_iota(jnp.int32, (SUBLANES, n_chunk), 0)

    def conv_silu(u, sl):
        prev = jnp.where(first, 0.0, tail_ref[:, sl])
        tail_ref[:, sl] = u[tm - SUBLANES:]
        acc = cb_ref[:, sl] + cw_ref[SSD_CONV - 1:SSD_CONV, sl] * u
        for k in range(1, SSD_CONV):
            ru = pltpu.roll(u, k, 0)
            head = jnp.where(row < k, pltpu.roll(prev, k, 0), ru[:SUBLANES])
            shifted = jnp.concatenate([head, ru[SUBLANES:]], axis=0)
            acc = acc + cw_ref[SSD_CONV - 1 - k:SSD_CONV - k, sl] * shifted
        xc_ref[:, sl] = (acc * _sigmoid(acc)).astype(xc_ref.dtype)

    x_chunks = [slice(n0, n0 + n_chunk) for n0 in range(0, wx_ref.shape[1], n_chunk)]
    z_chunks = [slice(n0, n0 + n_chunk) for n0 in range(0, wz_ref.shape[1], n_chunk)]
    u = _dot(h, wx_ref[:, x_chunks[0]])
    for j, sl in enumerate(x_chunks):
        u_next = _dot(h, wx_ref[:, x_chunks[j + 1]]) if j + 1 < len(x_chunks) else None
        if j < len(z_chunks):
            z_ref[:, z_chunks[j]] = _dot(h, wz_ref[:, z_chunks[j]]).astype(z_ref.dtype)
        conv_silu(u, sl)
        u = u_next
    for zs in z_chunks[len(x_chunks):]:
        z_ref[:, zs] = _dot(h, wz_ref[:, zs]).astype(z_ref.dtype)


def _proj_conv(x2d, g, w_z, w_xbc, conv_w, conv_b, seq_len, tm=512):
    m, k = x2d.shape
    tm = min(tm, seq_len)
    conv_dim = w_xbc.shape[1]
    full = lambda a: pl.BlockSpec(a.shape, lambda i: (0, 0))
    row = lambda n: pl.BlockSpec((tm, n), lambda i: (i, 0))
    conv_w, conv_b, g = conv_w.astype(F32), conv_b.reshape(1, -1).astype(F32), g.reshape(1, k)
    return pl.pallas_call(
        functools.partial(_proj_conv_kernel, tiles_per_seq=seq_len // tm, n_chunk=256),
        grid=(m // tm,),
        in_specs=[row(k), full(g), full(w_z), full(w_xbc), full(conv_w), full(conv_b)],
        out_specs=[row(w_z.shape[1]), row(conv_dim)],
        out_shape=[jax.ShapeDtypeStruct((m, w_z.shape[1]), BF16), jax.ShapeDtypeStruct((m, conv_dim), BF16)],
        scratch_shapes=[pltpu.VMEM((SUBLANES, conv_dim), F32)],
        compiler_params=_cparams(1), name="proj_conv",
    )(x2d, g, w_z, w_xbc, conv_w, conv_b)


def _softplus(x):
    return jnp.maximum(x, 0.0) + jnp.log(1.0 + jnp.exp(-jnp.abs(x)))


def _ssd_kernel(xc_ref, z_ref, dt_ref, dtT_ref, dtb_ref, dtbT_ref, alog_ref, alogT_ref, dskip_ref, nw_ref,
                y_ref, state_ref, ybuf_ref, *, n_heads, d_inner):
    L, P, N, G = SSD_CHUNK, SSD_HEAD_DIM, SSD_STATE, SSD_GROUPS
    c = pl.program_id(1)

    @pl.when(c == 0)
    def _():
        state_ref[...] = jnp.zeros_like(state_ref)

    dt = _softplus(dt_ref[:, 0:n_heads] + dtb_ref[...])
    a = dt * (-jnp.exp(alog_ref[...]))
    dtT = _softplus(dtT_ref[...] + dtbT_ref[...])
    aT = dtT * (-jnp.exp(alogT_ref[...]))
    ri = lax.broadcasted_iota(jnp.int32, (L, L), 0)
    ci = lax.broadcasted_iota(jnp.int32, (L, L), 1)
    causal = ci <= ri
    tri = jnp.where(causal, 1.0, 0.0).astype(BF16)
    triu = jnp.where(ri <= ci, 1.0, 0.0).astype(BF16)
    a_cs = sum(_dot(tri, ai) for ai in _split3(a))
    a_csT = sum(_dot(ai, triu) for ai in _split3(aT))
    lane = lax.broadcasted_iota(jnp.int32, (L, 2 * P), 1)
    lo_half = lane < P
    lane1 = lax.broadcasted_iota(jnp.int32, (1, 2 * P), 1) < P

    heads_per_group = n_heads // G
    for g in range(G):
        b_g = xc_ref[:, d_inner + g * N:d_inner + (g + 1) * N]
        c_gb = xc_ref[:, d_inner + G * N + g * N:d_inner + G * N + (g + 1) * N]
        cb = _dot_nt(c_gb, b_g)
        b_gT = b_g.astype(F32).T.astype(BF16)
        for pp in range(heads_per_group // 2):
            pair = g * (heads_per_group // 2) + pp
            h0, h1 = 2 * pair, 2 * pair + 1
            xs = xc_ref[:, pair * 2 * P:(pair + 1) * 2 * P].astype(F32)
            dt_pair = jnp.where(lo_half, dt[:, h0:h0 + 1], dt[:, h1:h1 + 1])
            xd = xs * dt_pair
            xd_b = xd.astype(BF16)
            ys = []
            for h in (h0, h1):
                seg = jnp.where(causal, a_cs[:, h:h + 1] - a_csT[h:h + 1, :], NEG_INF)
                mat = (cb * jnp.exp(seg)).astype(BF16)
                ys.append(_dot(mat, xd_b))
            y = jnp.where(lo_half, ys[0], ys[1])
            cs_pair = jnp.where(lo_half, a_cs[:, h0:h0 + 1], a_cs[:, h1:h1 + 1])
            st = state_ref[pair]
            y = y + _dot(c_gb, st.astype(BF16)) * jnp.exp(cs_pair)
            tot = jnp.where(lane1, a_csT[h0:h0 + 1, L - 1:L], a_csT[h1:h1 + 1, L - 1:L])
            xdd = (xd * jnp.exp(tot - cs_pair)).astype(BF16)
            state_ref[pair] = st * jnp.exp(tot) + _dot(b_gT, xdd)
            ybuf_ref[:, pair * 2 * P:(pair + 1) * 2 * P] = y + xs * dskip_ref[:, pair * 2 * P:(pair + 1) * 2 * P]

    gw = d_inner // G
    for g in range(G):
        sl = slice(g * gw, (g + 1) * gw)
        zz = z_ref[:, sl].astype(F32)
        yg = ybuf_ref[:, sl] * (zz * _sigmoid(zz))
        y_ref[:, sl] = _rms(yg, nw_ref[:, sl]).astype(y_ref.dtype)


def _ssd_mixer(z, xc, small, dtT, dt_bias, a_log, d_skip, norm_w):
    b, s, d_inner = z.shape
    conv_dim = xc.shape[-1]
    n_heads = d_inner // SSD_HEAD_DIM
    L = SSD_CHUNK
    nc = s // L
    kern = functools.partial(_ssd_kernel, n_heads=n_heads, d_inner=d_inner)
    const = lambda shape: pl.BlockSpec(shape, lambda i, j: (0,) * len(shape))
    return pl.pallas_call(
        kern, grid=(b, nc),
        in_specs=[
            pl.BlockSpec((None, L, conv_dim), lambda i, j: (i, j, 0)),
            pl.BlockSpec((None, L, d_inner), lambda i, j: (i, j, 0)),
            pl.BlockSpec((None, L, small.shape[-1]), lambda i, j: (i, j, 0)),
            pl.BlockSpec((n_heads, L), lambda i, j: (0, i * nc + j)),
            const((1, n_heads)), const((n_heads, 1)),
            const((1, n_heads)), const((n_heads, 1)),
            const((1, d_inner)), const((1, d_inner)),
        ],
        out_specs=pl.BlockSpec((None, L, d_inner), lambda i, j: (i, j, 0)),
        out_shape=jax.ShapeDtypeStruct((b, s, d_inner), BF16),
        scratch_shapes=[pltpu.VMEM((n_heads // 2, SSD_STATE, 2 * SSD_HEAD_DIM), F32),
                        pltpu.VMEM((L, d_inner), F32)],
        compiler_params=_cparams(2), name="ssd",
    )(xc, z, small, dtT,
      dt_bias.reshape(1, -1), dt_bias.reshape(-1, 1), a_log.reshape(1, -1), a_log.reshape(-1, 1),
      jnp.repeat(d_skip, SSD_HEAD_DIM).reshape(1, -1), norm_w.reshape(1, -1))


def _compress_kernel(kr_ref, pos_ref, w1_ref, w2_ref, o_ref):
    nr, half = kr_ref.shape
    kr = kr_ref[...]
    p1 = _dot(kr, w1_ref[0:half, :])
    p2 = _dot(kr, w1_ref[half:2 * half, :])
    pb = _dot(pos_ref[...], w1_ref[...])[0:1]
    hid = p1 + pltpu.roll(p2, nr - 1, 0) + pb
    hid = hid * _sigmoid(hid)
    out = _dot(hid.astype(BF16), w2_ref[...])
    rows = lax.broadcasted_iota(jnp.int32, out.shape, 0)
    o_ref[...] = jnp.where(rows < nr - 1, out, 0.0).astype(o_ref.dtype)


def _compress(kr, pos, w1, w2p):
    two, b, g, nr, half = kr.shape
    hidden = w1.shape[-1]
    return pl.pallas_call(
        _compress_kernel,
        grid=(two, b, g),
        in_specs=[
            pl.BlockSpec((None, None, None, nr, half), lambda t, i, j: (t, i, j, 0, 0)),
            pl.BlockSpec((None, SUBLANES, 2 * half), lambda t, i, j: (t, 0, 0)),
            pl.BlockSpec((None, 2 * half, hidden), lambda t, i, j: (t, 0, 0)),
            pl.BlockSpec((None, hidden, LANES), lambda t, i, j: (t, 0, 0)),
        ],
        out_specs=pl.BlockSpec((None, None, None, nr, LANES), lambda t, i, j: (t, i, j, 0, 0)),
        out_shape=jax.ShapeDtypeStruct((two, b, g, nr, LANES), BF16),
        compiler_params=_cparams(3), name="nsa_compress",
    )(kr, pos, w1, w2p)


def _colmax(a):
    return jnp.max(a, axis=0, keepdims=True)


def _colsum(a):
    return jnp.sum(a, axis=0, keepdims=True)


class _SoftmaxPipe:
    def __init__(self, s_ref, pe_ref, acc_ref, hd):
        self.s, self.pe, self.acc, self.hd = s_ref, pe_ref, acc_ref, hd

    def start(self, first_scores, slot):
        cols = first_scores.shape[1]
        self.s[slot] = first_scores
        self.pe[1 - slot] = jnp.zeros(self.pe.shape[1:], self.pe.dtype)
        self.acc[...] = jnp.zeros(self.acc.shape, F32)
        return _colmax(first_scores), jnp.ones((1, cols), F32), jnp.full((1, cols), NEG_INF, F32)

    def step(self, slot, carry, v_prev, next_scores):
        cm_cur, alpha_prev, m_old = carry
        self.acc[...] = alpha_prev * self.acc[...] + _dot(v_prev, self.pe[1 - slot])
        cm_next = cm_cur
        if next_scores is not None:
            s_next = next_scores()
            self.s[1 - slot] = s_next
            cm_next = _colmax(s_next)
        m_new = jnp.maximum(m_old, cm_cur)
        self.pe[slot] = jnp.exp2(self.s[slot] - m_new).astype(self.pe.dtype)
        return cm_next, jnp.exp2(m_old - m_new), m_new

    def finish(self, slot_last, carry, v_last):
        _, alpha_prev, _ = carry
        acc = alpha_prev * self.acc[...] + _dot(v_last, self.pe[slot_last])
        return acc[0:self.hd] * (1.0 / acc[self.hd:self.hd + 1])


def _nsa_kernel(qT_ref, kc_ref, vcT_ref, ov_ref, ksl_ref, vslT_ref, kw_ref, vwT_ref, cb_ref, tp_ref,
                gl_ref, o_ref, q0_ref, qa_ref, rank_ref, vsa_ref, vwa_ref, s_ref, pe_ref, acc_ref, ws_ref,
                wpe_ref, wacc_ref, *, n_slc, top_k):
    T, R, HD = NSA_TQ, NSA_REP, NSA_HEAD_DIM
    nr = kc_ref.shape[0]
    per_tile = T // CMP_STRIDE
    front = nr - per_tile
    half = LANES // 2
    qi = pl.program_id(2)
    colmax, colsum = _colmax, _colsum

    @pl.when(qi == 0)
    def _():
        for src, dst in ((vslT_ref, vsa_ref), (vwT_ref, vwa_ref)):
            dst[0:HD, :] = src[...]
            dst[HD:, :] = jnp.ones((dst.shape[0] - HD, dst.shape[1]), dst.dtype)

    qT = jnp.concatenate([qT_ref[r] for r in range(R)], axis=1)
    q0_ref[0:HD, :] = qT
    q0_ref[HD:2 * HD, :] = jnp.zeros((HD, R * T), BF16)
    qa_ref[0:HD, :] = qT

    n_win = WINDOW // T

    def win_offset(rel):
        return pl.multiple_of(jnp.maximum(qi - rel, 0) * T, T)

    def win_scores(rel):
        tile = TILE_EDGE if rel == n_win else (TILE_ZERO if rel >= 2 else (TILE_NEAR if rel == 1 else TILE_DIAG))
        if rel > 0:
            tile = jnp.where(qi >= rel, tile, TILE_NONE)
        return _dot(kw_ref[pl.ds(win_offset(rel), T), :], q0_ref[...]) + tp_ref[tile]

    def win_values(rel):
        return vwa_ref[:, pl.ds(win_offset(rel), T)]

    win = _SoftmaxPipe(ws_ref, wpe_ref, wacc_ref, HD)
    rels = list(range(n_win, -1, -1))
    carry = win.start(win_scores(rels[0]), 0)
    for i, rel in enumerate(rels):
        nxt = functools.partial(win_scores, rels[i + 1]) if i + 1 < len(rels) else None
        carry = win.step(i % 2, carry, win_values(rels[max(i - 1, 0)]), nxt)
    o_win = win.finish((len(rels) - 1) % 2, carry, win_values(0))

    off = pl.multiple_of(front - qi * per_tile, SUBLANES)
    lg = _dot(kc_ref[...], q0_ref[...]) + cb_ref[pl.ds(off, nr), :]
    m_c = colmax(lg)
    e = jnp.exp2(lg - m_c)
    p = e * jnp.where(m_c > 0.5 * NEG_INF, 1.0 / colsum(e), 0.0)
    o_cmp = _dot(vcT_ref[...], p.astype(BF16))

    psum = p[:, 0:T]
    for r in range(1, R):
        psum = psum + p[:, r * T:(r + 1) * T]
    imp = sum(_dot(ov_ref[...], pi) for pi in _split3(psum))
    nio = lax.broadcasted_iota(jnp.int32, (half, T), 0)
    tio = lax.broadcasted_iota(jnp.int32, (half, T), 1)
    tb = (qi * T + tio) // SLC_BLOCK
    forced = (nio == 0) | (nio == tb) | (nio == tb - 1)
    imp = jnp.where(forced, FORCE_SCORE, imp)
    imp = jnp.where(nio > tb, NEG_INF, imp)
    n_grp = half // SUBLANES
    grp = [imp[SUBLANES * v:SUBLANES * (v + 1)] for v in range(n_grp)]
    nio8 = lax.broadcasted_iota(jnp.int32, (SUBLANES, T), 0)
    rank_ref[...] = jnp.zeros(rank_ref.shape, F32)
    last_block = (qi * T + T - 1) // SLC_BLOCK
    for mg in range(pl.cdiv(n_slc, SUBLANES)):

        @pl.when(mg * SUBLANES <= last_block)
        def _():
            hits = [jnp.zeros((SUBLANES, T), F32) for _ in range(n_grp)]
            for mm in range(mg * SUBLANES, min((mg + 1) * SUBLANES, n_slc)):
                rowv = imp[mm:mm + 1, :]
                for v in range(n_grp):
                    if SUBLANES * v > mm:
                        hit = jnp.where(rowv >= grp[v], 1.0, 0.0)
                    elif SUBLANES * v + SUBLANES - 1 < mm:
                        hit = jnp.where(rowv > grp[v], 1.0, 0.0)
                    else:
                        hit = jnp.where(nio8 > mm - SUBLANES * v, jnp.where(rowv >= grp[v], 1.0, 0.0),
                                        jnp.where(rowv > grp[v], 1.0, 0.0))
                    hits[v] = hits[v] + hit
            rank_ref[...] += jnp.concatenate(hits, axis=0)

    rank = rank_ref[...]
    pen = jnp.where(rank < top_k, jnp.where(nio <= tb, 0.0, SEL_PENALTY), SEL_PENALTY).astype(BF16)
    qa_ref[HD:2 * HD, :] = jnp.concatenate([pen] * R, axis=1)

    krow = lax.broadcasted_iota(jnp.int32, (T, LANES), 0)
    klane = lax.broadcasted_iota(jnp.int32, (T, LANES), 1)

    def sel_keys(kj):
        kt = ksl_ref[pl.ds(pl.multiple_of(kj * T, T), T), :]
        blk = kj * (T // SLC_BLOCK) + krow // SLC_BLOCK
        return jnp.where(klane >= half, jnp.where(klane - half == blk, 1.0, 0.0).astype(BF16), kt)

    def sel_scores(kj):
        return _dot(sel_keys(kj), qa_ref[...])

    def sel_values(kj):
        return vsa_ref[:, pl.ds(pl.multiple_of(kj * T, T), T)]

    n_far = jnp.maximum(qi - 1, 0)
    odd = n_far % 2
    sel = _SoftmaxPipe(s_ref, pe_ref, acc_ref, HD)

    def far_step(kj, slot, carry):
        return sel.step(slot, carry, sel_values(jnp.maximum(kj - 1, 0)), functools.partial(sel_scores, kj + 1))

    carry = sel.start(sel_scores(0), odd)
    carry = lax.cond(odd == 1, lambda c: far_step(0, 1, c), lambda c: c, carry)
    carry = lax.fori_loop(
        0, n_far // 2, lambda j, c: far_step(odd + 2 * j + 1, 1, far_step(odd + 2 * j, 0, c)), carry)
    s_near = s_ref[0] + tp_ref[jnp.where(qi >= 1, TILE_NEAR, TILE_NONE)]
    s_ref[0] = s_near
    carry = (colmax(s_near),) + tuple(carry[1:])
    carry = sel.step(0, carry, sel_values(jnp.maximum(n_far - 1, 0)),
                     lambda: sel_scores(qi) + tp_ref[TILE_DIAG])
    carry = sel.step(1, carry, sel_values(n_far), None)
    o_slc = sel.finish(1, carry, sel_values(qi))

    sg = _sigmoid(gl_ref[...])
    for r in range(R):
        cols = slice(r * T, (r + 1) * T)
        o = (sg[r:r + 1] * o_cmp[:, cols] + sg[R + r:R + r + 1] * o_slc[:, cols]
             + sg[2 * R + r:2 * R + r + 1] * o_win[:, cols])
        o_ref[r * HD:(r + 1) * HD, :] = o.astype(o_ref.dtype)


def _t5_bucket_np(dist):
    n = np.maximum(dist, 0)
    max_exact = REL_BUCKETS // 2
    nf = np.maximum(n, 1).astype(np.float32)
    large = max_exact + (np.log(nf / np.float32(max_exact)) / np.float32(math.log(REL_MAX_DIST / max_exact))
                         * np.float32(REL_BUCKETS - max_exact)).astype(np.int32)
    large = np.minimum(large, REL_BUCKETS - 1)
    return np.where(n < max_exact, n, large).astype(np.int32)


def _nsa_tables(rel_bias, nr):
    T, G, R = NSA_TQ, NSA_KV_HEADS, NSA_REP
    table = rel_bias.astype(F32) * LOG2E
    per_tile = T // CMP_STRIDE
    front = nr - per_tile
    t = np.arange(T)[None, :]

    def lay(a):
        keys = a.shape[0]
        return jnp.transpose(a, (2, 0, 1)).reshape(G, R, keys, T).transpose(0, 2, 1, 3).reshape(G, keys, R * T)

    cend = CMP_STRIDE * (np.arange(front + nr)[:, None] - front) + CMP_BLOCK - 1
    dist = t - cend
    def lookup(d):
        onehot = jax.nn.one_hot(_t5_bucket_np(d), REL_BUCKETS, dtype=F32)
        return jnp.einsum("ktb,bh->kth", onehot, table, precision=lax.Precision.HIGHEST)

    cb = jnp.where((dist >= 0)[..., None], lookup(dist), NEG_INF)
    k = np.arange(T)[:, None]
    assert 2 * T - (T - 1) >= REL_MAX_DIST and WINDOW % T == 0 and WINDOW // T >= 2
    last = table[REL_BUCKETS - 1]
    diag = jnp.where((t - k >= 0)[..., None], lookup(t - k) - last, NEG_INF)
    near = lookup(T + t - k) - last
    none = jnp.full_like(near, NEG_INF)
    edge = jnp.where((k > t)[..., None], jnp.zeros_like(near), NEG_INF)
    tiles = [None] * 5
    tiles[TILE_DIAG], tiles[TILE_NEAR], tiles[TILE_NONE], tiles[TILE_EDGE] = diag, near, none, edge
    tiles[TILE_ZERO] = jnp.zeros_like(near)
    tp = jnp.stack([lay(a) for a in tiles], axis=1)
    return lay(cb), tp


def _overlap_matrix(nr, n_slc):
    half = LANES // 2
    n = np.arange(half)[:, None]
    c = np.arange(nr)[None, :]
    ov = ((CMP_STRIDE * c <= SLC_BLOCK * n + SLC_BLOCK - 1) & (CMP_STRIDE * c + CMP_BLOCK - 1 >= SLC_BLOCK * n)
          & (n < n_slc) & (c < nr - 1))
    return jnp.asarray(ov.astype(np.float32), dtype=BF16)


def _nsa_attention(qT, kc, vcT, ksl, vslT, kw, vwT, cb, tp, glT):
    n_heads, hd, _ = qT.shape
    b, s, _ = ksl.shape
    G, R, T = NSA_KV_HEADS, NSA_REP, NSA_TQ
    nr = kc.shape[-2]
    nq = s // T
    n_slc = s // SLC_BLOCK
    assert n_slc <= LANES // 2 and s % T == 0 and nr % SUBLANES == 0
    ov = _overlap_matrix(nr, n_slc)
    kern = functools.partial(_nsa_kernel, n_slc=n_slc, top_k=min(SLC_TOPK, n_slc))
    k_spec = pl.BlockSpec((None, s, LANES), lambda i, j, k: (i, 0, j))
    vT_spec = pl.BlockSpec((None, hd, s), lambda i, j, k: (j, 0, i))
    ones_rows = 16
    pipe_scratch = [pltpu.VMEM((2, T, R * T), F32), pltpu.VMEM((2, T, R * T), BF16),
                    pltpu.VMEM((hd + ones_rows, R * T), F32)]
    value_scratch = [pltpu.VMEM((hd + ones_rows, s), BF16)] * 2
    return pl.pallas_call(
        kern, grid=(b, G, nq),
        in_specs=[
            pl.BlockSpec((R, hd, T), lambda i, j, k: (j, 0, i * nq + k)),
            pl.BlockSpec((None, None, nr, LANES), lambda i, j, k: (i, j, 0, 0)),
            pl.BlockSpec((None, None, hd, nr), lambda i, j, k: (i, j, 0, 0)),
            pl.BlockSpec(ov.shape, lambda i, j, k: (0, 0)),
            k_spec, vT_spec, k_spec, vT_spec,
            pl.BlockSpec((None,) + cb.shape[1:], lambda i, j, k: (j, 0, 0)),
            pl.BlockSpec((None,) + tp.shape[1:], lambda i, j, k: (j, 0, 0, 0)),
            pl.BlockSpec((None, 16, T), lambda i, j, k: (j, 0, i * nq + k)),
        ],
        out_specs=pl.BlockSpec((R * hd, T), lambda i, j, k: (j, i * nq + k)),
        out_shape=jax.ShapeDtypeStruct((n_heads * hd, b * s), BF16),
        scratch_shapes=[pltpu.VMEM((2 * hd, R * T), BF16), pltpu.VMEM((2 * hd, R * T), BF16),
                        pltpu.VMEM((LANES // 2, T), F32)]
        + value_scratch + pipe_scratch + pipe_scratch,
        compiler_params=_cparams(3), name="nsa_attention",
    )(qT, kc, vcT, ov, ksl, vslT, kw, vwT, cb, tp, glT)


def _merge_kernel(x_ref, ys_ref, ynT_ref, mg_ref, ws_ref, wn_ref, wo_ref, g_ref, o_ref):
    d = x_ref.shape[1]
    a = _dot(ys_ref[...], ws_ref[...])
    bb = lax.dot_general(ynT_ref[...], wn_ref[...], (((0,), (0,)), ((), ())), preferred_element_type=F32)
    mg = _sigmoid(mg_ref[...].astype(F32))
    mixed = mg[:, :d] * a + mg[:, d:] * bb
    o = _dot(mixed.astype(BF16), wo_ref[...])
    o_ref[...] = x_ref[...] + _rms(o, g_ref[...])


def _merge(x2d, y_ssd, y_nsaT, mg, w_s, w_n, w_o, g, tm=512):
    m, d = x2d.shape
    tm = min(tm, m)
    row = lambda a: pl.BlockSpec((tm, a.shape[1]), lambda i: (i, 0))
    full = lambda a: pl.BlockSpec(a.shape, lambda i: (0, 0))
    g = g.reshape(1, d)
    return pl.pallas_call(
        _merge_kernel, grid=(m // tm,),
        in_specs=[row(x2d), row(y_ssd), pl.BlockSpec((y_nsaT.shape[0], tm), lambda i: (0, i)), row(mg),
                  full(w_s), full(w_n), full(w_o), full(g)],
        out_specs=pl.BlockSpec((tm, d), lambda i: (i, 0)),
        out_shape=jax.ShapeDtypeStruct((m, d), F32),
        compiler_params=_cparams(1), name="merge",
    )(x2d, y_ssd, y_nsaT, mg, w_s, w_n, w_o, g)


def _xattn_kernel(x_ref, kv_ref, gpre_ref, wq_ref, wo_ref, gpost_ref, o_ref):
    x = x_ref[...]
    h = _rms(x, gpre_ref[...]).astype(BF16)
    q = _dot(h, wq_ref[...])
    width = X_HEADS * X_HEAD_DIM
    scale = X_HEAD_DIM ** -0.5
    outs = []
    for hh in range(X_HEADS):
        sl = slice(hh * X_HEAD_DIM, (hh + 1) * X_HEAD_DIM)
        lg = _dot_nt(q[:, sl].astype(BF16), kv_ref[:, sl]) * scale
        e = jnp.exp(lg - jnp.max(lg, axis=-1, keepdims=True))
        p = e / jnp.sum(e, axis=-1, keepdims=True)
        outs.append(_dot(p.astype(BF16), kv_ref[:, width + hh * X_HEAD_DIM:width + (hh + 1) * X_HEAD_DIM]))
    o = _dot(jnp.concatenate(outs, axis=-1).astype(BF16), wo_ref[...])
    o_ref[...] = x + _rms(o, gpost_ref[...])


def _xattn(x3d, kv, g_pre, w_q, w_o, g_post, tm=512):
    b, s, d = x3d.shape
    tm = min(tm, s)
    ml = kv.shape[1]
    full = lambda a: pl.BlockSpec(a.shape, lambda i, j: (0, 0))
    g_pre, g_post = g_pre.reshape(1, d), g_post.reshape(1, d)
    return pl.pallas_call(
        _xattn_kernel, grid=(b, s // tm),
        in_specs=[pl.BlockSpec((None, tm, d), lambda i, j: (i, j, 0)),
                  pl.BlockSpec((None, ml, kv.shape[2]), lambda i, j: (i, 0, 0)),
                  full(g_pre), full(w_q), full(w_o), full(g_post)],
        out_specs=pl.BlockSpec((None, tm, d), lambda i, j: (i, j, 0)),
        out_shape=jax.ShapeDtypeStruct((b, s, d), F32),
        compiler_params=_cparams(2), name="xattn",
    )(x3d, kv, g_pre, w_q, w_o, g_post)


def _mlp_kernel(x_ref, gpre_ref, w1_ref, w2_ref, gpost_ref, o_ref, *, f_chunk):
    x = x_ref[...]
    h = _rms(x, gpre_ref[...]).astype(BF16)
    d_ff = w1_ref.shape[1]
    acc = jnp.zeros(x.shape, F32)
    for f0 in range(0, d_ff, f_chunk):
        u = jnp.maximum(_dot(h, w1_ref[:, f0:f0 + f_chunk]), 0.0)
        acc = acc + _dot((u * u).astype(BF16), w2_ref[f0:f0 + f_chunk, :])
    o_ref[...] = x + _rms(acc, gpost_ref[...])


def _mlp(x2d, g_pre, w1, w2, g_post, tm=512):
    m, d = x2d.shape
    tm = min(tm, m)
    full = lambda a: pl.BlockSpec(a.shape, lambda i: (0, 0))
    g_pre, g_post = g_pre.reshape(1, d), g_post.reshape(1, d)
    return pl.pallas_call(
        functools.partial(_mlp_kernel, f_chunk=1024), grid=(m // tm,),
        in_specs=[pl.BlockSpec((tm, d), lambda i: (i, 0)), full(g_pre), full(w1), full(w2), full(g_post)],
        out_specs=pl.BlockSpec((tm, d), lambda i: (i, 0)),
        out_shape=jax.ShapeDtypeStruct((m, d), F32),
        compiler_params=_cparams(1), name="mlp",
    )(x2d, g_pre, w1, w2, g_post)


def _pad_heads(w, n_heads, hd):
    k = w.shape[0]
    w = w.reshape(k, n_heads, hd)
    return jnp.pad(w, ((0, 0), (0, 0), (0, LANES - hd))).reshape(k, n_heads * LANES)


def _layer(x, mem, w_in, conv_w, conv_b, dt_bias, a_log, d_skip, ssd_norm, cmp_pos, cmp_w1, cmp_w2,
           rel_bias, w_br_ssd, w_br_nsa, w_out, w_xq, w_xkv, w_xo, w_ff1, w_ff2,
           n_mix_pre, n_mix_post, n_x_pre, n_x_post, n_mem, n_ffn_pre, n_ffn_post):
    b, s, d = x.shape
    G, R, HD = NSA_KV_HEADS, NSA_REP, NSA_HEAD_DIM
    d_inner = 2 * d
    n_ssd_heads = d_inner // SSD_HEAD_DIM
    conv_dim = d_inner + 2 * SSD_GROUPS * SSD_STATE
    nsa_w, kv_w = NSA_HEADS * HD, G * HD
    sizes = (d_inner, conv_dim, n_ssd_heads, nsa_w, 6 * kv_w, 3 * NSA_HEADS, 2 * d)
    offs = np.concatenate([[0], np.cumsum(sizes)])
    seg = lambda i: w_in[:, offs[i]:offs[i + 1]]
    w_z, w_xbc, w_dt, w_q, w_kv, w_gate, w_mg = (seg(i) for i in range(7))
    w_small = jnp.pad(jnp.concatenate([w_dt, w_gate], axis=1),
                      ((0, 0), (0, LANES - n_ssd_heads - 3 * NSA_HEADS)))
    w_kv6 = w_kv.reshape(d, 6, kv_w)
    w_cmp = jnp.concatenate([w_kv6[:, 0], w_kv6[:, 1]], axis=1)
    bf = lambda a: a.astype(BF16)

    x2d = x.reshape(b * s, d)
    z, xc = _proj_conv(x2d, n_mix_pre, bf(w_z), bf(w_xbc), conv_w, conv_b, s)
    w_gl = jnp.transpose(w_gate.reshape(d, 3, G, R), (2, 1, 3, 0)).reshape(G, 3 * R, d)
    w_gl = jnp.pad(w_gl, ((0, 0), (0, 16 - 3 * R), (0, 0))).reshape(G * 16, d)
    small, dtT, glT, qT, kv_cmp, vslT, vwT = _norm_matmul(
        x2d, n_mix_pre,
        [bf(w_small), bf(w_dt.T), bf(w_gl), bf(w_q.T * (HD ** -0.5 * LOG2E)), bf(w_cmp),
         bf(w_kv6[:, 3].T), bf(w_kv6[:, 5].T)],
        [F32, F32, F32, BF16, BF16, BF16, BF16],
        transposed=(False, True, True, True, False, True, True))
    ksl, kw, mg = _norm_matmul(
        x2d, n_mix_pre, [bf(_pad_heads(w_kv6[:, 2], G, HD)), bf(_pad_heads(w_kv6[:, 4], G, HD)), bf(w_mg)],
        [BF16] * 3)

    y_ssd = _ssd_mixer(z.reshape(b, s, -1), xc.reshape(b, s, -1), small.reshape(b, s, -1), dtT,
                       dt_bias, a_log, d_skip, ssd_norm)

    nr = s // CMP_STRIDE
    kr = kv_cmp.reshape(b, nr, CMP_STRIDE, 2, G, HD)
    kr = jnp.transpose(kr, (3, 0, 4, 1, 2, 5)).reshape(2, b, G, nr, CMP_STRIDE * HD)
    pos = jnp.broadcast_to(bf(cmp_pos).reshape(2, 1, CMP_BLOCK * HD), (2, SUBLANES, CMP_BLOCK * HD))
    w2p = jnp.pad(bf(cmp_w2), ((0, 0), (0, 0), (0, LANES - HD)))
    kvc = _compress(kr, pos, bf(cmp_w1), w2p)
    vcT = jnp.swapaxes(kvc[1][..., :HD], -1, -2)
    cb, tp = _nsa_tables(rel_bias, nr)
    y_nsa = _nsa_attention(qT.reshape(NSA_HEADS, HD, b * s), kvc[0], vcT,
                           ksl.reshape(b, s, -1), vslT.reshape(G, HD, b * s),
                           kw.reshape(b, s, -1), vwT.reshape(G, HD, b * s), cb, tp,
                           glT.reshape(G, 16, b * s))

    x1 = _merge(x2d, y_ssd.reshape(b * s, -1), y_nsa, mg, bf(w_br_ssd), bf(w_br_nsa), bf(w_out), n_mix_post)

    (kv_mem,) = _norm_matmul(mem.reshape(-1, d), n_mem, [bf(w_xkv)], [BF16])
    x2 = _xattn(x1.reshape(b, s, d), kv_mem.reshape(b, mem.shape[1], -1), n_x_pre, bf(w_xq), bf(w_xo),
                n_x_post)

    x3 = _mlp(x2.reshape(b * s, d), n_ffn_pre, bf(w_ff1), bf(w_ff2), n_ffn_post)
    return x3.reshape(b, s, d)


def kernel(x, mem, w_in, ssd_conv_w, ssd_conv_b, ssd_dt_bias, ssd_a_log, ssd_d_skip, ssd_norm, cmp_pos,
           cmp_w1, cmp_w2, rel_bias, w_br_ssd, w_br_nsa, w_out, w_xq, w_xkv, w_xo, w_ff1, w_ff2,
           norm_mix_pre, norm_mix_post, norm_x_pre, norm_x_post, norm_mem, norm_ffn_pre, norm_ffn_post):
    for l in range(w_in.shape[0]):
        x = _layer(x, mem, w_in[l], ssd_conv_w[l], ssd_conv_b[l], ssd_dt_bias[l], ssd_a_log[l],
                   ssd_d_skip[l], ssd_norm[l], cmp_pos[l], cmp_w1[l], cmp_w2[l], rel_bias,
                   w_br_ssd[l], w_br_nsa[l], w_out[l], w_xq[l], w_xkv[l], w_xo[l], w_ff1[l], w_ff2[l],
                   norm_mix_pre[l], norm_mix_post[l], norm_x_pre[l], norm_x_post[l], norm_mem[l],
                   norm_ffn_pre[l], norm_ffn_post[l])
    return x
```

```python
import functools
import math

import numpy as np
import jax
import jax.numpy as jnp
from jax import lax
from jax.experimental import pallas as pl
from jax.experimental.pallas import tpu as pltpu

F32 = jnp.float32
BF16 = jnp.bfloat16

NORM_EPS = 1e-6
NEG_INF = -1e30
FORCE_SCORE = 1e9
SEL_PENALTY = NEG_INF

LANES = 128
SUBLANES = 8
VMEM_LIMIT = 56 * 1024 * 1024

SSD_HEAD_DIM = 64
SSD_GROUPS = 8
SSD_STATE = 128
SSD_CONV = 4
SSD_CHUNK = 128
NSA_HEADS = 16
NSA_KV_HEADS = 4
NSA_HEAD_DIM = 64
NSA_REP = NSA_HEADS // NSA_KV_HEADS
CMP_BLOCK = 32
CMP_STRIDE = 16
SLC_BLOCK = 64
SLC_TOPK = 16
WINDOW = 512
NSA_TQ = 256
LOG2E = math.log2(math.e)
REL_BUCKETS = 32
REL_MAX_DIST = 128
X_HEADS = 4
X_HEAD_DIM = 128

TILE_DIAG, TILE_NEAR, TILE_NONE, TILE_EDGE, TILE_ZERO = range(5)


def _cparams(n_grid):
    return pltpu.CompilerParams(dimension_semantics=("arbitrary",) * n_grid,
                                vmem_limit_bytes=VMEM_LIMIT)


def _sigmoid(x):
    return jax.nn.sigmoid(x)


def _rms(x, g):
    return x * lax.rsqrt(jnp.mean(x * x, axis=-1, keepdims=True) + NORM_EPS) * g


def _dot(a, b):
    return jnp.dot(a, b, preferred_element_type=F32)


def _dot_nt(a, b):
    return lax.dot_general(a, b, (((1,), (1,)), ((), ())), preferred_element_type=F32)


def _split3(x):
    x1 = x.astype(BF16)
    r1 = x - x1.astype(F32)
    x2 = r1.astype(BF16)
    x3 = (r1 - x2.astype(F32)).astype(BF16)
    return x1, x2, x3


def _norm_matmul_kernel(x_ref, g_ref, *refs, transposed, n_chunk):
    n_out = len(transposed)
    w_refs, o_refs = refs[:n_out], refs[n_out:]
    h = _rms(x_ref[...], g_ref[...]).astype(BF16)
    for w_ref, o_ref, tr in zip(w_refs, o_refs, transposed):
        if tr:
            o_ref[...] = _dot_nt(w_ref[...], h).astype(o_ref.dtype)
            continue
        n = w_ref.shape[1]
        step = min(n, n_chunk)
        for n0 in range(0, n, step):
            o_ref[:, n0:n0 + step] = _dot(h, w_ref[:, n0:n0 + step]).astype(o_ref.dtype)


def _norm_matmul(x2d, g, ws, out_dtypes, transposed=None, tm=512):
    m, k = x2d.shape
    tm = min(tm, m)
    transposed = tuple(transposed) if transposed is not None else (False,) * len(ws)
    in_specs = [pl.BlockSpec((tm, k), lambda i: (i, 0)), pl.BlockSpec((1, k), lambda i: (0, 0))]
    in_specs += [pl.BlockSpec(w.shape, lambda i: (0, 0)) for w in ws]
    out_specs, out_shape = [], []
    for w, dt, tr in zip(ws, out_dtypes, transposed):
        if tr:
            out_specs.append(pl.BlockSpec((w.shape[0], tm), lambda i: (0, i)))
            out_shape.append(jax.ShapeDtypeStruct((w.shape[0], m), dt))
        else:
            out_specs.append(pl.BlockSpec((tm, w.shape[1]), lambda i: (i, 0)))
            out_shape.append(jax.ShapeDtypeStruct((m, w.shape[1]), dt))
    return pl.pallas_call(
        functools.partial(_norm_matmul_kernel, transposed=transposed, n_chunk=1024),
        grid=(m // tm,), in_specs=in_specs, out_specs=out_specs, out_shape=out_shape,
        compiler_params=_cparams(1), name="norm_proj",
    )(x2d, g.reshape(1, k), *ws)


def _softplus(x):
    return jnp.maximum(x, 0.0) + jnp.log(1.0 + jnp.exp(-jnp.abs(x)))


def _ssd_kernel(xbc_ref, prev_ref, z_ref, dt_ref, dtT_ref, cw_ref, cb_ref, dtb_ref, dtbT_ref,
                alog_ref, alogT_ref, dskip_ref, nw_ref, y_ref, state_ref, xc_ref, ybuf_ref,
                *, n_heads, d_inner):
    L, P, N, G = SSD_CHUNK, SSD_HEAD_DIM, SSD_STATE, SSD_GROUPS
    conv_dim = xc_ref.shape[1]
    c = pl.program_id(1)

    @pl.when(c == 0)
    def _():
        state_ref[...] = jnp.zeros_like(state_ref)

    cw = 512
    prev_rows = prev_ref.shape[0]
    row = lax.broadcasted_iota(jnp.int32, (prev_rows, cw), 0)
    for j in range(conv_dim // cw):
        sl = slice(j * cw, (j + 1) * cw)
        cur = xbc_ref[:, sl].astype(F32)
        prev = jnp.where(c == 0, 0.0, prev_ref[:, sl].astype(F32))
        acc = cb_ref[:, sl] + cw_ref[SSD_CONV - 1:SSD_CONV, sl] * cur
        for k in range(1, SSD_CONV):
            rc = pltpu.roll(cur, k, 0)
            rp = pltpu.roll(prev, k, 0)
            head = jnp.where(row < k, rp, rc[:prev_rows])
            shifted = jnp.concatenate([head, rc[prev_rows:]], axis=0)
            acc = acc + cw_ref[SSD_CONV - 1 - k:SSD_CONV - k, sl] * shifted
        xc_ref[:, sl] = acc * _sigmoid(acc)

    dt = _softplus(dt_ref[:, 0:n_heads] + dtb_ref[...])
    a = dt * (-LOG2E * jnp.exp(alog_ref[...]))
    dtT = _softplus(dtT_ref[...] + dtbT_ref[...])
    aT = dtT * (-LOG2E * jnp.exp(alogT_ref[...]))
    ri = lax.broadcasted_iota(jnp.int32, (L, L), 0)
    ci = lax.broadcasted_iota(jnp.int32, (L, L), 1)
    causal = ci <= ri
    tri = jnp.where(causal, 1.0, 0.0).astype(BF16)
    triu = jnp.where(ri <= ci, 1.0, 0.0).astype(BF16)
    a_cs = sum(_dot(tri, ai) for ai in _split3(a))
    a_csT = sum(_dot(ai, triu) for ai in _split3(aT))
    lane = lax.broadcasted_iota(jnp.int32, (L, 2 * P), 1)
    lo_half = lane < P
    lane1 = lax.broadcasted_iota(jnp.int32, (1, 2 * P), 1) < P

    heads_per_group = n_heads // G
    for g in range(G):
        b_g = xc_ref[:, d_inner + g * N:d_inner + (g + 1) * N]
        c_g = xc_ref[:, d_inner + G * N + g * N:d_inner + G * N + (g + 1) * N]
        c_gb = c_g.astype(BF16)
        cb = _dot_nt(c_gb, b_g.astype(BF16))
        b_gT = b_g.T.astype(BF16)
        for pp in range(heads_per_group // 2):
            pair = g * (heads_per_group // 2) + pp
            h0, h1 = 2 * pair, 2 * pair + 1
            xs = xc_ref[:, pair * 2 * P:(pair + 1) * 2 * P]
            dt_pair = jnp.where(lo_half, dt[:, h0:h0 + 1], dt[:, h1:h1 + 1])
            xd = xs * dt_pair
            xd_b = xd.astype(BF16)
            ys = []
            for h in (h0, h1):
                seg = jnp.where(causal, a_cs[:, h:h + 1] - a_csT[h:h + 1, :], NEG_INF)
                mat = (cb * jnp.exp2(seg)).astype(BF16)
                ys.append(_dot(mat, xd_b))
            y = jnp.where(lo_half, ys[0], ys[1])
            cs_pair = jnp.where(lo_half, a_cs[:, h0:h0 + 1], a_cs[:, h1:h1 + 1])
            st = state_ref[pair]
            y = y + _dot(c_gb, st.astype(BF16)) * jnp.exp2(cs_pair)
            tot = jnp.where(lane1, a_csT[h0:h0 + 1, L - 1:L], a_csT[h1:h1 + 1, L - 1:L])
            xdd = (xd * jnp.exp2(tot - cs_pair)).astype(BF16)
            state_ref[pair] = st * jnp.exp2(tot) + _dot(b_gT, xdd)
            ybuf_ref[:, pair * 2 * P:(pair + 1) * 2 * P] = y + xs * dskip_ref[:, pair * 2 * P:(pair + 1) * 2 * P]

    gw = d_inner // G
    for g in range(G):
        sl = slice(g * gw, (g + 1) * gw)
        zz = z_ref[:, sl].astype(F32)
        yg = ybuf_ref[:, sl] * (zz * _sigmoid(zz))
        y_ref[:, sl] = _rms(yg, nw_ref[:, sl]).astype(y_ref.dtype)


def _ssd_mixer(z, xbc, small, dtT, conv_w, conv_b, dt_bias, a_log, d_skip, norm_w):
    b, s, d_inner = z.shape
    conv_dim = xbc.shape[-1]
    n_heads = d_inner // SSD_HEAD_DIM
    L = SSD_CHUNK
    nc = s // L
    prev_rows = 16
    xbc_prev = xbc.reshape(b, s // prev_rows, prev_rows, conv_dim)
    blocks_per_chunk = L // prev_rows
    kern = functools.partial(_ssd_kernel, n_heads=n_heads, d_inner=d_inner)
    const = lambda shape: pl.BlockSpec(shape, lambda i, j: (0,) * len(shape))
    return pl.pallas_call(
        kern, grid=(b, nc),
        in_specs=[
            pl.BlockSpec((None, L, conv_dim), lambda i, j: (i, j, 0)),
            pl.BlockSpec((None, None, prev_rows, conv_dim),
                         lambda i, j: (i, jnp.maximum(j * blocks_per_chunk - 1, 0), 0, 0)),
            pl.BlockSpec((None, L, d_inner), lambda i, j: (i, j, 0)),
            pl.BlockSpec((None, L, small.shape[-1]), lambda i, j: (i, j, 0)),
            pl.BlockSpec((n_heads, L), lambda i, j: (0, i * nc + j)),
            const((SSD_CONV, conv_dim)), const((1, conv_dim)),
            const((1, n_heads)), const((n_heads, 1)),
            const((1, n_heads)), const((n_heads, 1)),
            const((1, d_inner)), const((1, d_inner)),
        ],
        out_specs=pl.BlockSpec((None, L, d_inner), lambda i, j: (i, j, 0)),
        out_shape=jax.ShapeDtypeStruct((b, s, d_inner), BF16),
        scratch_shapes=[pltpu.VMEM((n_heads // 2, SSD_STATE, 2 * SSD_HEAD_DIM), F32),
                        pltpu.VMEM((L, conv_dim), F32),
                        pltpu.VMEM((L, d_inner), F32)],
        compiler_params=_cparams(2), name="ssd",
    )(xbc, xbc_prev, z, small, dtT, conv_w.astype(F32), conv_b.reshape(1, -1).astype(F32),
      dt_bias.reshape(1, -1), dt_bias.reshape(-1, 1), a_log.reshape(1, -1), a_log.reshape(-1, 1),
      jnp.repeat(d_skip, SSD_HEAD_DIM).reshape(1, -1), norm_w.reshape(1, -1))


def _compress_kernel(kr_ref, pos_ref, w1_ref, w2_ref, o_ref):
    nr, half = kr_ref.shape
    kr = kr_ref[...]
    p1 = _dot(kr, w1_ref[0:half, :])
    p2 = _dot(kr, w1_ref[half:2 * half, :])
    pb = _dot(pos_ref[...], w1_ref[...])[0:1]
    hid = p1 + pltpu.roll(p2, nr - 1, 0) + pb
    hid = hid * _sigmoid(hid)
    out = _dot(hid.astype(BF16), w2_ref[...])
    rows = lax.broadcasted_iota(jnp.int32, out.shape, 0)
    o_ref[...] = jnp.where(rows < nr - 1, out, 0.0).astype(o_ref.dtype)


def _compress(kr, pos, w1, w2p):
    two, b, g, nr, half = kr.shape
    hidden = w1.shape[-1]
    return pl.pallas_call(
        _compress_kernel,
        grid=(two, b, g),
        in_specs=[
            pl.BlockSpec((None, None, None, nr, half), lambda t, i, j: (t, i, j, 0, 0)),
            pl.BlockSpec((None, SUBLANES, 2 * half), lambda t, i, j: (t, 0, 0)),
            pl.BlockSpec((None, 2 * half, hidden), lambda t, i, j: (t, 0, 0)),
            pl.BlockSpec((None, hidden, LANES), lambda t, i, j: (t, 0, 0)),
        ],
        out_specs=pl.BlockSpec((None, None, None, nr, LANES), lambda t, i, j: (t, i, j, 0, 0)),
        out_shape=jax.ShapeDtypeStruct((two, b, g, nr, LANES), BF16),
        compiler_params=_cparams(3), name="nsa_compress",
    )(kr, pos, w1, w2p)


def _colmax(a):
    return jnp.max(a, axis=0, keepdims=True)


def _colsum(a):
    return jnp.sum(a, axis=0, keepdims=True)


class _SoftmaxPipe:
    def __init__(self, s_ref, pe_ref, acc_ref, hd):
        self.s, self.pe, self.acc, self.hd = s_ref, pe_ref, acc_ref, hd

    def start(self, first_scores, slot):
        cols = first_scores.shape[1]
        self.s[slot] = first_scores
        self.pe[1 - slot] = jnp.zeros(self.pe.shape[1:], self.pe.dtype)
        self.acc[...] = jnp.zeros(self.acc.shape, F32)
        return _colmax(first_scores), jnp.ones((1, cols), F32), jnp.full((1, cols), NEG_INF, F32)

    def step(self, slot, carry, v_prev, next_scores):
        cm_cur, alpha_prev, m_old = carry
        self.acc[...] = alpha_prev * self.acc[...] + _dot(v_prev, self.pe[1 - slot])
        cm_next = cm_cur
        if next_scores is not None:
            s_next = next_scores()
            self.s[1 - slot] = s_next
            cm_next = _colmax(s_next)
        m_new = jnp.maximum(m_old, cm_cur)
        self.pe[slot] = jnp.exp2(self.s[slot] - m_new).astype(self.pe.dtype)
        return cm_next, jnp.exp2(m_old - m_new), m_new

    def finish(self, slot_last, carry, v_last):
        _, alpha_prev, _ = carry
        acc = alpha_prev * self.acc[...] + _dot(v_last, self.pe[slot_last])
        return acc[0:self.hd] * (1.0 / acc[self.hd:self.hd + 1])


def _nsa_kernel(qT_ref, kc_ref, vcT_ref, ov_ref, ksl_ref, vslT_ref, kw_ref, vwT_ref, cb_ref, tp_ref,
                gl_ref, o_ref, q0_ref, qa_ref, rank_ref, vsa_ref, vwa_ref, s_ref, pe_ref, acc_ref, ws_ref,
                wpe_ref, wacc_ref, *, n_slc, top_k):
    T, R, HD = NSA_TQ, NSA_REP, NSA_HEAD_DIM
    nr = kc_ref.shape[0]
    per_tile = T // CMP_STRIDE
    front = nr - per_tile
    half = LANES // 2
    qi = pl.program_id(2)
    colmax, colsum = _colmax, _colsum

    @pl.when(qi == 0)
    def _():
        for src, dst in ((vslT_ref, vsa_ref), (vwT_ref, vwa_ref)):
            dst[0:HD, :] = src[...]
            dst[HD:, :] = jnp.ones((dst.shape[0] - HD, dst.shape[1]), dst.dtype)

    qT = jnp.concatenate([qT_ref[r] for r in range(R)], axis=1)
    q0_ref[0:HD, :] = qT
    q0_ref[HD:2 * HD, :] = jnp.zeros((HD, R * T), BF16)
    qa_ref[0:HD, :] = qT

    n_win = WINDOW // T

    def win_offset(rel):
        return pl.multiple_of(jnp.maximum(qi - rel, 0) * T, T)

    def win_scores(rel):
        tile = TILE_EDGE if rel == n_win else (TILE_ZERO if rel >= 2 else (TILE_NEAR if rel == 1 else TILE_DIAG))
        if rel > 0:
            tile = jnp.where(qi >= rel, tile, TILE_NONE)
        return _dot(kw_ref[pl.ds(win_offset(rel), T), :], q0_ref[...]) + tp_ref[tile]

    def win_values(rel):
        return vwa_ref[:, pl.ds(win_offset(rel), T)]

    win = _SoftmaxPipe(ws_ref, wpe_ref, wacc_ref, HD)
    rels = list(range(n_win, -1, -1))
    carry = win.start(win_scores(rels[0]), 0)
    for i, rel in enumerate(rels):
        nxt = functools.partial(win_scores, rels[i + 1]) if i + 1 < len(rels) else None
        carry = win.step(i % 2, carry, win_values(rels[max(i - 1, 0)]), nxt)
    o_win = win.finish((len(rels) - 1) % 2, carry, win_values(0))

    off = pl.multiple_of(front - qi * per_tile, SUBLANES)
    lg = _dot(kc_ref[...], q0_ref[...]) + cb_ref[pl.ds(off, nr), :]
    m_c = colmax(lg)
    e = jnp.exp2(lg - m_c)
    p = e * jnp.where(m_c > 0.5 * NEG_INF, 1.0 / colsum(e), 0.0)
    o_cmp = _dot(vcT_ref[...], p.astype(BF16))

    psum = p[:, 0:T]
    for r in range(1, R):
        psum = psum + p[:, r * T:(r + 1) * T]
    imp = sum(_dot(ov_ref[...], pi) for pi in _split3(psum))
    nio = lax.broadcasted_iota(jnp.int32, (half, T), 0)
    tio = lax.broadcasted_iota(jnp.int32, (half, T), 1)
    tb = (qi * T + tio) // SLC_BLOCK
    forced = (nio == 0) | (nio == tb) | (nio == tb - 1)
    imp = jnp.where(forced, FORCE_SCORE, imp)
    imp = jnp.where(nio > tb, NEG_INF, imp)
    n_grp = half // SUBLANES
    grp = [imp[SUBLANES * v:SUBLANES * (v + 1)] for v in range(n_grp)]
    nio8 = lax.broadcasted_iota(jnp.int32, (SUBLANES, T), 0)
    rank_ref[...] = jnp.zeros(rank_ref.shape, F32)
    last_block = (qi * T + T - 1) // SLC_BLOCK
    for mg in range(pl.cdiv(n_slc, SUBLANES)):

        @pl.when(mg * SUBLANES <= last_block)
        def _():
            hits = [jnp.zeros((SUBLANES, T), F32) for _ in range(n_grp)]
            for mm in range(mg * SUBLANES, min((mg + 1) * SUBLANES, n_slc)):
                rowv = imp[mm:mm + 1, :]
                for v in range(n_grp):
                    if SUBLANES * v > mm:
                        hit = jnp.where(rowv >= grp[v], 1.0, 0.0)
                    elif SUBLANES * v + SUBLANES - 1 < mm:
                        hit = jnp.where(rowv > grp[v], 1.0, 0.0)
                    else:
                        hit = jnp.where(nio8 > mm - SUBLANES * v, jnp.where(rowv >= grp[v], 1.0, 0.0),
                                        jnp.where(rowv > grp[v], 1.0, 0.0))
                    hits[v] = hits[v] + hit
            rank_ref[...] += jnp.concatenate(hits, axis=0)

    rank = rank_ref[...]
    pen = jnp.where(rank < top_k, jnp.where(nio <= tb, 0.0, SEL_PENALTY), SEL_PENALTY).astype(BF16)
    qa_ref[HD:2 * HD, :] = jnp.concatenate([pen] * R, axis=1)

    krow = lax.broadcasted_iota(jnp.int32, (T, LANES), 0)
    klane = lax.broadcasted_iota(jnp.int32, (T, LANES), 1)

    def sel_keys(kj):
        kt = ksl_ref[pl.ds(pl.multiple_of(kj * T, T), T), :]
        blk = kj * (T // SLC_BLOCK) + krow // SLC_BLOCK
        return jnp.where(klane >= half, jnp.where(klane - half == blk, 1.0, 0.0).astype(BF16), kt)

    def sel_scores(kj):
        return _dot(sel_keys(kj), qa_ref[...])

    def sel_values(kj):
        return vsa_ref[:, pl.ds(pl.multiple_of(kj * T, T), T)]

    n_far = jnp.maximum(qi - 1, 0)
    odd = n_far % 2
    sel = _SoftmaxPipe(s_ref, pe_ref, acc_ref, HD)

    def far_step(kj, slot, carry):
        return sel.step(slot, carry, sel_values(jnp.maximum(kj - 1, 0)), functools.partial(sel_scores, kj + 1))

    carry = sel.start(sel_scores(0), odd)
    carry = lax.cond(odd == 1, lambda c: far_step(0, 1, c), lambda c: c, carry)
    carry = lax.fori_loop(
        0, n_far // 2, lambda j, c: far_step(odd + 2 * j + 1, 1, far_step(odd + 2 * j, 0, c)), carry)
    s_near = s_ref[0] + tp_ref[jnp.where(qi >= 1, TILE_NEAR, TILE_NONE)]
    s_ref[0] = s_near
    carry = (colmax(s_near),) + tuple(carry[1:])
    carry = sel.step(0, carry, sel_values(jnp.maximum(n_far - 1, 0)),
                     lambda: sel_scores(qi) + tp_ref[TILE_DIAG])
    carry = sel.step(1, carry, sel_values(n_far), None)
    o_slc = sel.finish(1, carry, sel_values(qi))

    sg = _sigmoid(gl_ref[...])
    for r in range(R):
        cols = slice(r * T, (r + 1) * T)
        o = (sg[r:r + 1] * o_cmp[:, cols] + sg[R + r:R + r + 1] * o_slc[:, cols]
             + sg[2 * R + r:2 * R + r + 1] * o_win[:, cols])
        o_ref[r * HD:(r + 1) * HD, :] = o.astype(o_ref.dtype)


def _t5_bucket_np(dist):
    n = np.maximum(dist, 0)
    max_exact = REL_BUCKETS // 2
    nf = np.maximum(n, 1).astype(np.float32)
    large = max_exact + (np.log(nf / np.float32(max_exact)) / np.float32(math.log(REL_MAX_DIST / max_exact))
                         * np.float32(REL_BUCKETS - max_exact)).astype(np.int32)
    large = np.minimum(large, REL_BUCKETS - 1)
    return np.where(n < max_exact, n, large).astype(np.int32)


def _nsa_tables(rel_bias, nr):
    T, G, R = NSA_TQ, NSA_KV_HEADS, NSA_REP
    table = rel_bias.astype(F32) * LOG2E
    per_tile = T // CMP_STRIDE
    front = nr - per_tile
    t = np.arange(T)[None, :]

    def lay(a):
        keys = a.shape[0]
        return jnp.transpose(a, (2, 0, 1)).reshape(G, R, keys, T).transpose(0, 2, 1, 3).reshape(G, keys, R * T)

    cend = CMP_STRIDE * (np.arange(front + nr)[:, None] - front) + CMP_BLOCK - 1
    dist = t - cend
    def lookup(d):
        onehot = jax.nn.one_hot(_t5_bucket_np(d), REL_BUCKETS, dtype=F32)
        return jnp.einsum("ktb,bh->kth", onehot, table, precision=lax.Precision.HIGHEST)

    cb = jnp.where((dist >= 0)[..., None], lookup(dist), NEG_INF)
    k = np.arange(T)[:, None]
    assert 2 * T - (T - 1) >= REL_MAX_DIST and WINDOW % T == 0 and WINDOW // T >= 2
    last = table[REL_BUCKETS - 1]
    diag = jnp.where((t - k >= 0)[..., None], lookup(t - k) - last, NEG_INF)
    near = lookup(T + t - k) - last
    none = jnp.full_like(near, NEG_INF)
    edge = jnp.where((k > t)[..., None], jnp.zeros_like(near), NEG_INF)
    tiles = [None] * 5
    tiles[TILE_DIAG], tiles[TILE_NEAR], tiles[TILE_NONE], tiles[TILE_EDGE] = diag, near, none, edge
    tiles[TILE_ZERO] = jnp.zeros_like(near)
    tp = jnp.stack([lay(a) for a in tiles], axis=1)
    return lay(cb), tp


def _overlap_matrix(nr, n_slc):
    half = LANES // 2
    n = np.arange(half)[:, None]
    c = np.arange(nr)[None, :]
    ov = ((CMP_STRIDE * c <= SLC_BLOCK * n + SLC_BLOCK - 1) & (CMP_STRIDE * c + CMP_BLOCK - 1 >= SLC_BLOCK * n)
          & (n < n_slc) & (c < nr - 1))
    return jnp.asarray(ov.astype(np.float32), dtype=BF16)


def _nsa_attention(qT, kc, vcT, ksl, vslT, kw, vwT, cb, tp, glT):
    n_heads, hd, _ = qT.shape
    b, s, _ = ksl.shape
    G, R, T = NSA_KV_HEADS, NSA_REP, NSA_TQ
    nr = kc.shape[-2]
    nq = s // T
    n_slc = s // SLC_BLOCK
    assert n_slc <= LANES // 2 and s % T == 0 and nr % SUBLANES == 0
    ov = _overlap_matrix(nr, n_slc)
    kern = functools.partial(_nsa_kernel, n_slc=n_slc, top_k=min(SLC_TOPK, n_slc))
    k_spec = pl.BlockSpec((None, s, LANES), lambda i, j, k: (i, 0, j))
    vT_spec = pl.BlockSpec((None, hd, s), lambda i, j, k: (j, 0, i))
    ones_rows = 16
    pipe_scratch = [pltpu.VMEM((2, T, R * T), F32), pltpu.VMEM((2, T, R * T), BF16),
                    pltpu.VMEM((hd + ones_rows, R * T), F32)]
    value_scratch = [pltpu.VMEM((hd + ones_rows, s), BF16)] * 2
    return pl.pallas_call(
        kern, grid=(b, G, nq),
        in_specs=[
            pl.BlockSpec((R, hd, T), lambda i, j, k: (j, 0, i * nq + k)),
            pl.BlockSpec((None, None, nr, LANES), lambda i, j, k: (i, j, 0, 0)),
            pl.BlockSpec((None, None, hd, nr), lambda i, j, k: (i, j, 0, 0)),
            pl.BlockSpec(ov.shape, lambda i, j, k: (0, 0)),
            k_spec, vT_spec, k_spec, vT_spec,
            pl.BlockSpec((None,) + cb.shape[1:], lambda i, j, k: (j, 0, 0)),
            pl.BlockSpec((None,) + tp.shape[1:], lambda i, j, k: (j, 0, 0, 0)),
            pl.BlockSpec((None, 16, T), lambda i, j, k: (j, 0, i * nq + k)),
        ],
        out_specs=pl.BlockSpec((R * hd, T), lambda i, j, k: (j, i * nq + k)),
        out_shape=jax.ShapeDtypeStruct((n_heads * hd, b * s), BF16),
        scratch_shapes=[pltpu.VMEM((2 * hd, R * T), BF16), pltpu.VMEM((2 * hd, R * T), BF16),
                        pltpu.VMEM((LANES // 2, T), F32)]
        + value_scratch + pipe_scratch + pipe_scratch,
        compiler_params=_cparams(3), name="nsa_attention",
    )(qT, kc, vcT, ov, ksl, vslT, kw, vwT, cb, tp, glT)


def _merge_kernel(x_ref, ys_ref, ynT_ref, mg_ref, ws_ref, wn_ref, wo_ref, g_ref, o_ref):
    d = x_ref.shape[1]
    a = _dot(ys_ref[...], ws_ref[...])
    bb = lax.dot_general(ynT_ref[...], wn_ref[...], (((0,), (0,)), ((), ())), preferred_element_type=F32)
    mg = _sigmoid(mg_ref[...].astype(F32))
    mixed = mg[:, :d] * a + mg[:, d:] * bb
    o = _dot(mixed.astype(BF16), wo_ref[...])
    o_ref[...] = x_ref[...] + _rms(o, g_ref[...])


def _merge(x2d, y_ssd, y_nsaT, mg, w_s, w_n, w_o, g, tm=512):
    m, d = x2d.shape
    tm = min(tm, m)
    row = lambda a: pl.BlockSpec((tm, a.shape[1]), lambda i: (i, 0))
    full = lambda a: pl.BlockSpec(a.shape, lambda i: (0, 0))
    g = g.reshape(1, d)
    return pl.pallas_call(
        _merge_kernel, grid=(m // tm,),
        in_specs=[row(x2d), row(y_ssd), pl.BlockSpec((y_nsaT.shape[0], tm), lambda i: (0, i)), row(mg),
                  full(w_s), full(w_n), full(w_o), full(g)],
        out_specs=pl.BlockSpec((tm, d), lambda i: (i, 0)),
        out_shape=jax.ShapeDtypeStruct((m, d), F32),
        compiler_params=_cparams(1), name="merge",
    )(x2d, y_ssd, y_nsaT, mg, w_s, w_n, w_o, g)


def _xattn_kernel(x_ref, kv_ref, gpre_ref, wq_ref, wo_ref, gpost_ref, o_ref):
    x = x_ref[...]
    h = _rms(x, gpre_ref[...]).astype(BF16)
    q = _dot(h, wq_ref[...])
    width = X_HEADS * X_HEAD_DIM
    scale = X_HEAD_DIM ** -0.5
    outs = []
    for hh in range(X_HEADS):
        sl = slice(hh * X_HEAD_DIM, (hh + 1) * X_HEAD_DIM)
        lg = _dot_nt(q[:, sl].astype(BF16), kv_ref[:, sl]) * scale
        e = jnp.exp(lg - jnp.max(lg, axis=-1, keepdims=True))
        p = e / jnp.sum(e, axis=-1, keepdims=True)
        outs.append(_dot(p.astype(BF16), kv_ref[:, width + hh * X_HEAD_DIM:width + (hh + 1) * X_HEAD_DIM]))
    o = _dot(jnp.concatenate(outs, axis=-1).astype(BF16), wo_ref[...])
    o_ref[...] = x + _rms(o, gpost_ref[...])


def _xattn(x3d, kv, g_pre, w_q, w_o, g_post, tm=512):
    b, s, d = x3d.shape
    tm = min(tm, s)
    ml = kv.shape[1]
    full = lambda a: pl.BlockSpec(a.shape, lambda i, j: (0, 0))
    g_pre, g_post = g_pre.reshape(1, d), g_post.reshape(1, d)
    return pl.pallas_call(
        _xattn_kernel, grid=(b, s // tm),
        in_specs=[pl.BlockSpec((None, tm, d), lambda i, j: (i, j, 0)),
                  pl.BlockSpec((None, ml, kv.shape[2]), lambda i, j: (i, 0, 0)),
                  full(g_pre), full(w_q), full(w_o), full(g_post)],
        out_specs=pl.BlockSpec((None, tm, d), lambda i, j: (i, j, 0)),
        out_shape=jax.ShapeDtypeStruct((b, s, d), F32),
        compiler_params=_cparams(2), name="xattn",
    )(x3d, kv, g_pre, w_q, w_o, g_post)


def _mlp_kernel(x_ref, gpre_ref, w1_ref, w2_ref, gpost_ref, o_ref, *, f_chunk):
    x = x_ref[...]
    h = _rms(x, gpre_ref[...]).astype(BF16)
    d_ff = w1_ref.shape[1]
    acc = jnp.zeros(x.shape, F32)
    for f0 in range(0, d_ff, f_chunk):
        u = jnp.maximum(_dot(h, w1_ref[:, f0:f0 + f_chunk]), 0.0)
        acc = acc + _dot((u * u).astype(BF16), w2_ref[f0:f0 + f_chunk, :])
    o_ref[...] = x + _rms(acc, gpost_ref[...])


def _mlp(x2d, g_pre, w1, w2, g_post, tm=512):
    m, d = x2d.shape
    tm = min(tm, m)
    full = lambda a: pl.BlockSpec(a.shape, lambda i: (0, 0))
    g_pre, g_post = g_pre.reshape(1, d), g_post.reshape(1, d)
    return pl.pallas_call(
        functools.partial(_mlp_kernel, f_chunk=1024), grid=(m // tm,),
        in_specs=[pl.BlockSpec((tm, d), lambda i: (i, 0)), full(g_pre), full(w1), full(w2), full(g_post)],
        out_specs=pl.BlockSpec((tm, d), lambda i: (i, 0)),
        out_shape=jax.ShapeDtypeStruct((m, d), F32),
        compiler_params=_cparams(1), name="mlp",
    )(x2d, g_pre, w1, w2, g_post)


def _pad_heads(w, n_heads, hd):
    k = w.shape[0]
    w = w.reshape(k, n_heads, hd)
    return jnp.pad(w, ((0, 0), (0, 0), (0, LANES - hd))).reshape(k, n_heads * LANES)


def _layer(x, mem, w_in, conv_w, conv_b, dt_bias, a_log, d_skip, ssd_norm, cmp_pos, cmp_w1, cmp_w2,
           rel_bias, w_br_ssd, w_br_nsa, w_out, w_xq, w_xkv, w_xo, w_ff1, w_ff2,
           n_mix_pre, n_mix_post, n_x_pre, n_x_post, n_mem, n_ffn_pre, n_ffn_post):
    b, s, d = x.shape
    G, R, HD = NSA_KV_HEADS, NSA_REP, NSA_HEAD_DIM
    d_inner = 2 * d
    n_ssd_heads = d_inner // SSD_HEAD_DIM
    conv_dim = d_inner + 2 * SSD_GROUPS * SSD_STATE
    nsa_w, kv_w = NSA_HEADS * HD, G * HD
    sizes = (d_inner, conv_dim, n_ssd_heads, nsa_w, 6 * kv_w, 3 * NSA_HEADS, 2 * d)
    offs = np.concatenate([[0], np.cumsum(sizes)])
    seg = lambda i: w_in[:, offs[i]:offs[i + 1]]
    w_z, w_xbc, w_dt, w_q, w_kv, w_gate, w_mg = (seg(i) for i in range(7))
    w_small = jnp.pad(jnp.concatenate([w_dt, w_gate], axis=1),
                      ((0, 0), (0, LANES - n_ssd_heads - 3 * NSA_HEADS)))
    w_kv6 = w_kv.reshape(d, 6, kv_w)
    w_cmp = jnp.concatenate([w_kv6[:, 0], w_kv6[:, 1]], axis=1)
    bf = lambda a: a.astype(BF16)

    x2d = x.reshape(b * s, d)
    z, xbc = _norm_matmul(x2d, n_mix_pre, [bf(w_z), bf(w_xbc)], [BF16, BF16])
    w_gl = jnp.transpose(w_gate.reshape(d, 3, G, R), (2, 1, 3, 0)).reshape(G, 3 * R, d)
    w_gl = jnp.pad(w_gl, ((0, 0), (0, 16 - 3 * R), (0, 0))).reshape(G * 16, d)
    small, dtT, glT, qT, kv_cmp, vslT, vwT = _norm_matmul(
        x2d, n_mix_pre,
        [bf(w_small), bf(w_dt.T), bf(w_gl), bf(w_q.T * (HD ** -0.5 * LOG2E)), bf(w_cmp),
         bf(w_kv6[:, 3].T), bf(w_kv6[:, 5].T)],
        [F32, F32, F32, BF16, BF16, BF16, BF16],
        transposed=(False, True, True, True, False, True, True))
    ksl, kw, mg = _norm_matmul(
        x2d, n_mix_pre, [bf(_pad_heads(w_kv6[:, 2], G, HD)), bf(_pad_heads(w_kv6[:, 4], G, HD)), bf(w_mg)],
        [BF16] * 3)

    y_ssd = _ssd_mixer(z.reshape(b, s, -1), xbc.reshape(b, s, -1), small.reshape(b, s, -1), dtT,
                       conv_w, conv_b, dt_bias, a_log, d_skip, ssd_norm)

    nr = s // CMP_STRIDE
    kr = kv_cmp.reshape(b, nr, CMP_STRIDE, 2, G, HD)
    kr = jnp.transpose(kr, (3, 0, 4, 1, 2, 5)).reshape(2, b, G, nr, CMP_STRIDE * HD)
    pos = jnp.broadcast_to(bf(cmp_pos).reshape(2, 1, CMP_BLOCK * HD), (2, SUBLANES, CMP_BLOCK * HD))
    w2p = jnp.pad(bf(cmp_w2), ((0, 0), (0, 0), (0, LANES - HD)))
    kvc = _compress(kr, pos, bf(cmp_w1), w2p)
    vcT = jnp.swapaxes(kvc[1][..., :HD], -1, -2)
    cb, tp = _nsa_tables(rel_bias, nr)
    y_nsa = _nsa_attention(qT.reshape(NSA_HEADS, HD, b * s), kvc[0], vcT,
                           ksl.reshape(b, s, -1), vslT.reshape(G, HD, b * s),
                           kw.reshape(b, s, -1), vwT.reshape(G, HD, b * s), cb, tp,
                           glT.reshape(G, 16, b * s))

    x1 = _merge(x2d, y_ssd.reshape(b * s, -1), y_nsa, mg, bf(w_br_ssd), bf(w_br_nsa), bf(w_out), n_mix_post)

    (kv_mem,) = _norm_matmul(mem.reshape(-1, d), n_mem, [bf(w_xkv)], [BF16])
    x2 = _xattn(x1.reshape(b, s, d), kv_mem.reshape(b, mem.shape[1], -1), n_x_pre, bf(w_xq), bf(w_xo),
                n_x_post)

    x3 = _mlp(x2.reshape(b * s, d), n_ffn_pre, bf(w_ff1), bf(w_ff2), n_ffn_post)
    return x3.reshape(b, s, d)


def kernel(x, mem, w_in, ssd_conv_w, ssd_conv_b, ssd_dt_bias, ssd_a_log, ssd_d_skip, ssd_norm, cmp_pos,
           cmp_w1, cmp_w2, rel_bias, w_br_ssd, w_br_nsa, w_out, w_xq, w_xkv, w_xo, w_ff1, w_ff2,
           norm_mix_pre, norm_mix_post, norm_x_pre, norm_x_post, norm_mem, norm_ffn_pre, norm_ffn_post):
    for l in range(w_in.shape[0]):
        x = _layer(x, mem, w_in[l], ssd_conv_w[l], ssd_conv_b[l], ssd_dt_bias[l], ssd_a_log[l],
                   ssd_d_skip[l], ssd_norm[l], cmp_pos[l], cmp_w1[l], cmp_w2[l], rel_bias,
                   w_br_ssd[l], w_br_nsa[l], w_out[l], w_xq[l], w_xkv[l], w_xo[l], w_ff1[l], w_ff2[l],
                   norm_mix_pre[l], norm_mix_post[l], norm_x_pre[l], norm_x_post[l], norm_mem[l],
                   norm_ffn_pre[l], norm_ffn_post[l])
    return x
```

```python
import functools
import math

import numpy as np
import jax
import jax.numpy as jnp
from jax import lax
from jax.experimental import pallas as pl
from jax.experimental.pallas import tpu as pltpu

F32 = jnp.float32
BF16 = jnp.bfloat16

NORM_EPS = 1e-6
NEG_INF = -1e30
FORCE_SCORE = 1e9
SEL_PENALTY = NEG_INF

LANES = 128
SUBLANES = 8
VMEM_LIMIT = 56 * 1024 * 1024

SSD_HEAD_DIM = 64
SSD_GROUPS = 8
SSD_STATE = 128
SSD_CONV = 4
SSD_CHUNK = 128
NSA_HEADS = 16
NSA_KV_HEADS = 4
NSA_HEAD_DIM = 64
NSA_REP = NSA_HEADS // NSA_KV_HEADS
CMP_BLOCK = 32
CMP_STRIDE = 16
SLC_BLOCK = 64
SLC_TOPK = 16
WINDOW = 512
NSA_TQ = 256
LOG2E = math.log2(math.e)
REL_BUCKETS = 32
REL_MAX_DIST = 128
X_HEADS = 4
X_HEAD_DIM = 128

TILE_DIAG, TILE_NEAR, TILE_NONE, TILE_EDGE, TILE_ZERO = range(5)


def _cparams(n_grid):
    return pltpu.CompilerParams(dimension_semantics=("arbitrary",) * n_grid,
                                vmem_limit_bytes=VMEM_LIMIT)


def _sigmoid(x):
    return jax.nn.sigmoid(x)


def _rms(x, g):
    return x * lax.rsqrt(jnp.mean(x * x, axis=-1, keepdims=True) + NORM_EPS) * g


def _dot(a, b):
    return jnp.dot(a, b, preferred_element_type=F32)


def _dot_nt(a, b):
    return lax.dot_general(a, b, (((1,), (1,)), ((), ())), preferred_element_type=F32)


def _split3(x):
    x1 = x.astype(BF16)
    r1 = x - x1.astype(F32)
    x2 = r1.astype(BF16)
    x3 = (r1 - x2.astype(F32)).astype(BF16)
    return x1, x2, x3


def _norm_matmul_kernel(x_ref, g_ref, *refs, transposed, n_chunk):
    n_out = len(transposed)
    w_refs, o_refs = refs[:n_out], refs[n_out:]
    h = _rms(x_ref[...], g_ref[...]).astype(BF16)
    for w_ref, o_ref, tr in zip(w_refs, o_refs, transposed):
        if tr:
            o_ref[...] = _dot_nt(w_ref[...], h).astype(o_ref.dtype)
            continue
        n = w_ref.shape[1]
        step = min(n, n_chunk)
        for n0 in range(0, n, step):
            o_ref[:, n0:n0 + step] = _dot(h, w_ref[:, n0:n0 + step]).astype(o_ref.dtype)


def _norm_matmul(x2d, g, ws, out_dtypes, transposed=None, tm=512):
    m, k = x2d.shape
    tm = min(tm, m)
    transposed = tuple(transposed) if transposed is not None else (False,) * len(ws)
    in_specs = [pl.BlockSpec((tm, k), lambda i: (i, 0)), pl.BlockSpec((1, k), lambda i: (0, 0))]
    in_specs += [pl.BlockSpec(w.shape, lambda i: (0, 0)) for w in ws]
    out_specs, out_shape = [], []
    for w, dt, tr in zip(ws, out_dtypes, transposed):
        if tr:
            out_specs.append(pl.BlockSpec((w.shape[0], tm), lambda i: (0, i)))
            out_shape.append(jax.ShapeDtypeStruct((w.shape[0], m), dt))
        else:
            out_specs.append(pl.BlockSpec((tm, w.shape[1]), lambda i: (i, 0)))
            out_shape.append(jax.ShapeDtypeStruct((m, w.shape[1]), dt))
    return pl.pallas_call(
        functools.partial(_norm_matmul_kernel, transposed=transposed, n_chunk=1024),
        grid=(m // tm,), in_specs=in_specs, out_specs=out_specs, out_shape=out_shape,
        compiler_params=_cparams(1), name="norm_proj",
    )(x2d, g.reshape(1, k), *ws)


def _softplus(x):
    return jnp.maximum(x, 0.0) + jnp.log(1.0 + jnp.exp(-jnp.abs(x)))


def _ssd_kernel(xbc_ref, prev_ref, z_ref, dt_ref, dtT_ref, cw_ref, cb_ref, dtb_ref, dtbT_ref,
                alog_ref, alogT_ref, dskip_ref, nw_ref, y_ref, state_ref, xc_ref, ybuf_ref,
                *, n_heads, d_inner):
    L, P, N, G = SSD_CHUNK, SSD_HEAD_DIM, SSD_STATE, SSD_GROUPS
    conv_dim = xc_ref.shape[1]
    c = pl.program_id(1)

    @pl.when(c == 0)
    def _():
        state_ref[...] = jnp.zeros_like(state_ref)

    cw = 512
    prev_rows = prev_ref.shape[0]
    row = lax.broadcasted_iota(jnp.int32, (prev_rows, cw), 0)
    for j in range(conv_dim // cw):
        sl = slice(j * cw, (j + 1) * cw)
        cur = xbc_ref[:, sl].astype(F32)
        prev = jnp.where(c == 0, 0.0, prev_ref[:, sl].astype(F32))
        acc = cb_ref[:, sl] + cw_ref[SSD_CONV - 1:SSD_CONV, sl] * cur
        for k in range(1, SSD_CONV):
            rc = pltpu.roll(cur, k, 0)
            rp = pltpu.roll(prev, k, 0)
            head = jnp.where(row < k, rp, rc[:prev_rows])
            shifted = jnp.concatenate([head, rc[prev_rows:]], axis=0)
            acc = acc + cw_ref[SSD_CONV - 1 - k:SSD_CONV - k, sl] * shifted
        xc_ref[:, sl] = acc * _sigmoid(acc)

    dt = _softplus(dt_ref[:, 0:n_heads] + dtb_ref[...])
    a = dt * (-LOG2E * jnp.exp(alog_ref[...]))
    dtT = _softplus(dtT_ref[...] + dtbT_ref[...])
    aT = dtT * (-LOG2E * jnp.exp(alogT_ref[...]))
    ri = lax.broadcasted_iota(jnp.int32, (L, L), 0)
    ci = lax.broadcasted_iota(jnp.int32, (L, L), 1)
    causal = ci <= ri
    tri = jnp.where(causal, 1.0, 0.0).astype(BF16)
    triu = jnp.where(ri <= ci, 1.0, 0.0).astype(BF16)
    a_cs = sum(_dot(tri, ai) for ai in _split3(a))
    a_csT = sum(_dot(ai, triu) for ai in _split3(aT))
    lane = lax.broadcasted_iota(jnp.int32, (L, 2 * P), 1)
    lo_half = lane < P
    lane1 = lax.broadcasted_iota(jnp.int32, (1, 2 * P), 1) < P

    heads_per_group = n_heads // G
    for g in range(G):
        b_g = xc_ref[:, d_inner + g * N:d_inner + (g + 1) * N]
        c_g = xc_ref[:, d_inner + G * N + g * N:d_inner + G * N + (g + 1) * N]
        c_gb = c_g.astype(BF16)
        cb = _dot_nt(c_gb, b_g.astype(BF16))
        b_gT = b_g.T.astype(BF16)
        for pp in range(heads_per_group // 2):
            pair = g * (heads_per_group // 2) + pp
            h0, h1 = 2 * pair, 2 * pair + 1
            xs = xc_ref[:, pair * 2 * P:(pair + 1) * 2 * P]
            dt_pair = jnp.where(lo_half, dt[:, h0:h0 + 1], dt[:, h1:h1 + 1])
            xd = xs * dt_pair
            xd_b = xd.astype(BF16)
            ys = []
            for h in (h0, h1):
                seg = jnp.where(causal, a_cs[:, h:h + 1] - a_csT[h:h + 1, :], NEG_INF)
                mat = (cb * jnp.exp2(seg)).astype(BF16)
                ys.append(_dot(mat, xd_b))
            y = jnp.where(lo_half, ys[0], ys[1])
            cs_pair = jnp.where(lo_half, a_cs[:, h0:h0 + 1], a_cs[:, h1:h1 + 1])
            st = state_ref[pair]
            y = y + _dot(c_gb, st.astype(BF16)) * jnp.exp2(cs_pair)
            tot = jnp.where(lane1, a_csT[h0:h0 + 1, L - 1:L], a_csT[h1:h1 + 1, L - 1:L])
            xdd = (xd * jnp.exp2(tot - cs_pair)).astype(BF16)
            state_ref[pair] = st * jnp.exp2(tot) + _dot(b_gT, xdd)
            ybuf_ref[:, pair * 2 * P:(pair + 1) * 2 * P] = y + xs * dskip_ref[:, pair * 2 * P:(pair + 1) * 2 * P]

    gw = d_inner // G
    for g in range(G):
        sl = slice(g * gw, (g + 1) * gw)
        zz = z_ref[:, sl].astype(F32)
        yg = ybuf_ref[:, sl] * (zz * _sigmoid(zz))
        y_ref[:, sl] = _rms(yg, nw_ref[:, sl]).astype(y_ref.dtype)


def _ssd_mixer(z, xbc, small, dtT, conv_w, conv_b, dt_bias, a_log, d_skip, norm_w):
    b, s, d_inner = z.shape
    conv_dim = xbc.shape[-1]
    n_heads = d_inner // SSD_HEAD_DIM
    L = SSD_CHUNK
    nc = s // L
    prev_rows = 16
    xbc_prev = xbc.reshape(b, s // prev_rows, prev_rows, conv_dim)
    blocks_per_chunk = L // prev_rows
    kern = functools.partial(_ssd_kernel, n_heads=n_heads, d_inner=d_inner)
    const = lambda shape: pl.BlockSpec(shape, lambda i, j: (0,) * len(shape))
    return pl.pallas_call(
        kern, grid=(b, nc),
        in_specs=[
            pl.BlockSpec((None, L, conv_dim), lambda i, j: (i, j, 0)),
            pl.BlockSpec((None, None, prev_rows, conv_dim),
                         lambda i, j: (i, jnp.maximum(j * blocks_per_chunk - 1, 0), 0, 0)),
            pl.BlockSpec((None, L, d_inner), lambda i, j: (i, j, 0)),
            pl.BlockSpec((None, L, small.shape[-1]), lambda i, j: (i, j, 0)),
            pl.BlockSpec((n_heads, L), lambda i, j: (0, i * nc + j)),
            const((SSD_CONV, conv_dim)), const((1, conv_dim)),
            const((1, n_heads)), const((n_heads, 1)),
            const((1, n_heads)), const((n_heads, 1)),
            const((1, d_inner)), const((1, d_inner)),
        ],
        out_specs=pl.BlockSpec((None, L, d_inner), lambda i, j: (i, j, 0)),
        out_shape=jax.ShapeDtypeStruct((b, s, d_inner), BF16),
        scratch_shapes=[pltpu.VMEM((n_heads // 2, SSD_STATE, 2 * SSD_HEAD_DIM), F32),
                        pltpu.VMEM((L, conv_dim), F32),
                        pltpu.VMEM((L, d_inner), F32)],
        compiler_params=_cparams(2), name="ssd",
    )(xbc, xbc_prev, z, small, dtT, conv_w.astype(F32), conv_b.reshape(1, -1).astype(F32),
      dt_bias.reshape(1, -1), dt_bias.reshape(-1, 1), a_log.reshape(1, -1), a_log.reshape(-1, 1),
      jnp.repeat(d_skip, SSD_HEAD_DIM).reshape(1, -1), norm_w.reshape(1, -1))


def _compress_kernel(kr_ref, pos_ref, w1_ref, w2_ref, o_ref):
    nr, half = kr_ref.shape
    kr = kr_ref[...]
    p1 = _dot(kr, w1_ref[0:half, :])
    p2 = _dot(kr, w1_ref[half:2 * half, :])
    pb = _dot(pos_ref[...], w1_ref[...])[0:1]
    hid = p1 + pltpu.roll(p2, nr - 1, 0) + pb
    hid = hid * _sigmoid(hid)
    out = _dot(hid.astype(BF16), w2_ref[...])
    rows = lax.broadcasted_iota(jnp.int32, out.shape, 0)
    o_ref[...] = jnp.where(rows < nr - 1, out, 0.0).astype(o_ref.dtype)


def _compress(kr, pos, w1, w2p):
    two, b, g, nr, half = kr.shape
    hidden = w1.shape[-1]
    return pl.pallas_call(
        _compress_kernel,
        grid=(two, b, g),
        in_specs=[
            pl.BlockSpec((None, None, None, nr, half), lambda t, i, j: (t, i, j, 0, 0)),
            pl.BlockSpec((None, SUBLANES, 2 * half), lambda t, i, j: (t, 0, 0)),
            pl.BlockSpec((None, 2 * half, hidden), lambda t, i, j: (t, 0, 0)),
            pl.BlockSpec((None, hidden, LANES), lambda t, i, j: (t, 0, 0)),
        ],
        out_specs=pl.BlockSpec((None, None, None, nr, LANES), lambda t, i, j: (t, i, j, 0, 0)),
        out_shape=jax.ShapeDtypeStruct((two, b, g, nr, LANES), BF16),
        compiler_params=_cparams(3), name="nsa_compress",
    )(kr, pos, w1, w2p)


def _colmax(a):
    return jnp.max(a, axis=0, keepdims=True)


def _colsum(a):
    return jnp.sum(a, axis=0, keepdims=True)


class _SoftmaxPipe:
    def __init__(self, s_ref, pe_ref, acc_ref, hd):
        self.s, self.pe, self.acc, self.hd = s_ref, pe_ref, acc_ref, hd

    def start(self, first_scores, slot):
        cols = first_scores.shape[1]
        self.s[slot] = first_scores
        self.pe[1 - slot] = jnp.zeros(self.pe.shape[1:], self.pe.dtype)
        self.acc[...] = jnp.zeros(self.acc.shape, F32)
        return _colmax(first_scores), jnp.ones((1, cols), F32), jnp.full((1, cols), NEG_INF, F32)

    def step(self, slot, carry, v_prev, next_scores):
        cm_cur, alpha_prev, m_old = carry
        self.acc[...] = alpha_prev * self.acc[...] + _dot(v_prev, self.pe[1 - slot])
        cm_next = cm_cur
        if next_scores is not None:
            s_next = next_scores()
            self.s[1 - slot] = s_next
            cm_next = _colmax(s_next)
        m_new = jnp.maximum(m_old, cm_cur)
        self.pe[slot] = jnp.exp2(self.s[slot] - m_new).astype(self.pe.dtype)
        return cm_next, jnp.exp2(m_old - m_new), m_new

    def finish(self, slot_last, carry, v_last):
        _, alpha_prev, _ = carry
        acc = alpha_prev * self.acc[...] + _dot(v_last, self.pe[slot_last])
        return acc[0:self.hd] * (1.0 / acc[self.hd:self.hd + 1])


def _nsa_kernel(qT_ref, kc_ref, vcT_ref, ov_ref, ksl_ref, vslT_ref, kw_ref, vwT_ref, cb_ref, tp_ref,
                gl_ref, o_ref, q0_ref, qa_ref, rank_ref, kst_ref, vsa_ref, vwa_ref, s_ref, pe_ref, acc_ref,
                ws_ref, wpe_ref, wacc_ref, *, n_slc, top_k):
    T, R, HD = NSA_TQ, NSA_REP, NSA_HEAD_DIM
    nr = kc_ref.shape[0]
    per_tile = T // CMP_STRIDE
    front = nr - per_tile
    half = LANES // 2
    qi = pl.program_id(2)
    colmax, colsum = _colmax, _colsum

    @pl.when(qi == 0)
    def _():
        for src, dst in ((vslT_ref, vsa_ref), (vwT_ref, vwa_ref)):
            dst[0:HD, :] = src[...]
            dst[HD:, :] = jnp.ones((dst.shape[0] - HD, dst.shape[1]), dst.dtype)
        krow = lax.broadcasted_iota(jnp.int32, (T, LANES), 0)
        klane = lax.broadcasted_iota(jnp.int32, (T, LANES), 1)
        for t0 in range(0, ksl_ref.shape[0], T):
            blk = t0 // SLC_BLOCK + krow // SLC_BLOCK
            tag = jnp.where(klane - half == blk, 1.0, 0.0).astype(BF16)
            kst_ref[t0:t0 + T, :] = jnp.where(klane >= half, tag, ksl_ref[t0:t0 + T, :])

    qT = jnp.concatenate([qT_ref[r] for r in range(R)], axis=1)
    q0_ref[0:HD, :] = qT
    q0_ref[HD:2 * HD, :] = jnp.zeros((HD, R * T), BF16)
    qa_ref[0:HD, :] = qT

    n_win = WINDOW // T

    def win_offset(rel):
        return pl.multiple_of(jnp.maximum(qi - rel, 0) * T, T)

    def win_scores(rel):
        tile = TILE_EDGE if rel == n_win else (TILE_ZERO if rel >= 2 else (TILE_NEAR if rel == 1 else TILE_DIAG))
        if rel > 0:
            tile = jnp.where(qi >= rel, tile, TILE_NONE)
        return _dot(kw_ref[pl.ds(win_offset(rel), T), :], q0_ref[...]) + tp_ref[tile]

    def win_values(rel):
        return vwa_ref[:, pl.ds(win_offset(rel), T)]

    win = _SoftmaxPipe(ws_ref, wpe_ref, wacc_ref, HD)
    rels = list(range(n_win, -1, -1))
    carry = win.start(win_scores(rels[0]), 0)
    for i, rel in enumerate(rels):
        nxt = functools.partial(win_scores, rels[i + 1]) if i + 1 < len(rels) else None
        carry = win.step(i % 2, carry, win_values(rels[max(i - 1, 0)]), nxt)
    o_win = win.finish((len(rels) - 1) % 2, carry, win_values(0))

    off = pl.multiple_of(front - qi * per_tile, SUBLANES)
    lg = _dot(kc_ref[...], q0_ref[...]) + cb_ref[pl.ds(off, nr), :]
    m_c = colmax(lg)
    e = jnp.exp2(lg - m_c)
    p = e * jnp.where(m_c > 0.5 * NEG_INF, 1.0 / colsum(e), 0.0)
    o_cmp = _dot(vcT_ref[...], p.astype(BF16))

    psum = p[:, 0:T]
    for r in range(1, R):
        psum = psum + p[:, r * T:(r + 1) * T]
    imp = sum(_dot(ov_ref[...], pi) for pi in _split3(psum))
    nio = lax.broadcasted_iota(jnp.int32, (half, T), 0)
    tio = lax.broadcasted_iota(jnp.int32, (half, T), 1)
    tb = (qi * T + tio) // SLC_BLOCK
    forced = (nio == 0) | (nio == tb) | (nio == tb - 1)
    imp = jnp.where(forced, FORCE_SCORE, imp)
    imp = jnp.where(nio > tb, NEG_INF, imp)
    n_grp = half // SUBLANES
    grp = [imp[SUBLANES * v:SUBLANES * (v + 1)] for v in range(n_grp)]
    nio8 = lax.broadcasted_iota(jnp.int32, (SUBLANES, T), 0)
    rank_ref[...] = jnp.zeros(rank_ref.shape, F32)
    last_block = (qi * T + T - 1) // SLC_BLOCK
    for mg in range(pl.cdiv(n_slc, SUBLANES)):

        @pl.when(mg * SUBLANES <= last_block)
        def _():
            hits = [jnp.zeros((SUBLANES, T), F32) for _ in range(n_grp)]
            for mm in range(mg * SUBLANES, min((mg + 1) * SUBLANES, n_slc)):
                rowv = imp[mm:mm + 1, :]
                for v in range(n_grp):
                    if SUBLANES * v > mm:
                        hit = jnp.where(rowv >= grp[v], 1.0, 0.0)
                    elif SUBLANES * v + SUBLANES - 1 < mm:
                        hit = jnp.where(rowv > grp[v], 1.0, 0.0)
                    else:
                        hit = jnp.where(nio8 > mm - SUBLANES * v, jnp.where(rowv >= grp[v], 1.0, 0.0),
                                        jnp.where(rowv > grp[v], 1.0, 0.0))
                    hits[v] = hits[v] + hit
            rank_ref[...] += jnp.concatenate(hits, axis=0)

    rank = rank_ref[...]
    pen = jnp.where(rank < top_k, jnp.where(nio <= tb, 0.0, SEL_PENALTY), SEL_PENALTY).astype(BF16)
    qa_ref[HD:2 * HD, :] = jnp.concatenate([pen] * R, axis=1)

    def sel_scores(kj):
        return _dot(kst_ref[pl.ds(pl.multiple_of(kj * T, T), T), :], qa_ref[...])

    def sel_values(kj):
        return vsa_ref[:, pl.ds(pl.multiple_of(kj * T, T), T)]

    n_far = jnp.maximum(qi - 1, 0)
    odd = n_far % 2
    sel = _SoftmaxPipe(s_ref, pe_ref, acc_ref, HD)

    def far_step(kj, slot, carry):
        return sel.step(slot, carry, sel_values(jnp.maximum(kj - 1, 0)), functools.partial(sel_scores, kj + 1))

    carry = sel.start(sel_scores(0), odd)
    carry = lax.cond(odd == 1, lambda c: far_step(0, 1, c), lambda c: c, carry)
    carry = lax.fori_loop(
        0, n_far // 2, lambda j, c: far_step(odd + 2 * j + 1, 1, far_step(odd + 2 * j, 0, c)), carry)
    s_near = s_ref[0] + tp_ref[jnp.where(qi >= 1, TILE_NEAR, TILE_NONE)]
    s_ref[0] = s_near
    carry = (colmax(s_near),) + tuple(carry[1:])
    carry = sel.step(0, carry, sel_values(jnp.maximum(n_far - 1, 0)),
                     lambda: sel_scores(qi) + tp_ref[TILE_DIAG])
    carry = sel.step(1, carry, sel_values(n_far), None)
    o_slc = sel.finish(1, carry, sel_values(qi))

    sg = _sigmoid(gl_ref[...])
    for r in range(R):
        cols = slice(r * T, (r + 1) * T)
        o = (sg[r:r + 1] * o_cmp[:, cols] + sg[R + r:R + r + 1] * o_slc[:, cols]
             + sg[2 * R + r:2 * R + r + 1] * o_win[:, cols])
        o_ref[r * HD:(r + 1) * HD, :] = o.astype(o_ref.dtype)


def _t5_bucket_np(dist):
    n = np.maximum(dist, 0)
    max_exact = REL_BUCKETS // 2
    nf = np.maximum(n, 1).astype(np.float32)
    large = max_exact + (np.log(nf / np.float32(max_exact)) / np.float32(math.log(REL_MAX_DIST / max_exact))
                         * np.float32(REL_BUCKETS - max_exact)).astype(np.int32)
    large = np.minimum(large, REL_BUCKETS - 1)
    return np.where(n < max_exact, n, large).astype(np.int32)


def _nsa_tables(rel_bias, nr):
    T, G, R = NSA_TQ, NSA_KV_HEADS, NSA_REP
    table = rel_bias.astype(F32) * LOG2E
    per_tile = T // CMP_STRIDE
    front = nr - per_tile
    t = np.arange(T)[None, :]

    def lay(a):
        keys = a.shape[0]
        return jnp.transpose(a, (2, 0, 1)).reshape(G, R, keys, T).transpose(0, 2, 1, 3).reshape(G, keys, R * T)

    cend = CMP_STRIDE * (np.arange(front + nr)[:, None] - front) + CMP_BLOCK - 1
    dist = t - cend
    def lookup(d):
        onehot = jax.nn.one_hot(_t5_bucket_np(d), REL_BUCKETS, dtype=F32)
        return jnp.einsum("ktb,bh->kth", onehot, table, precision=lax.Precision.HIGHEST)

    cb = jnp.where((dist >= 0)[..., None], lookup(dist), NEG_INF)
    k = np.arange(T)[:, None]
    assert 2 * T - (T - 1) >= REL_MAX_DIST and WINDOW % T == 0 and WINDOW // T >= 2
    last = table[REL_BUCKETS - 1]
    diag = jnp.where((t - k >= 0)[..., None], lookup(t - k) - last, NEG_INF)
    near = lookup(T + t - k) - last
    none = jnp.full_like(near, NEG_INF)
    edge = jnp.where((k > t)[..., None], jnp.zeros_like(near), NEG_INF)
    tiles = [None] * 5
    tiles[TILE_DIAG], tiles[TILE_NEAR], tiles[TILE_NONE], tiles[TILE_EDGE] = diag, near, none, edge
    tiles[TILE_ZERO] = jnp.zeros_like(near)
    tp = jnp.stack([lay(a) for a in tiles], axis=1)
    return lay(cb), tp


def _overlap_matrix(nr, n_slc):
    half = LANES // 2
    n = np.arange(half)[:, None]
    c = np.arange(nr)[None, :]
    ov = ((CMP_STRIDE * c <= SLC_BLOCK * n + SLC_BLOCK - 1) & (CMP_STRIDE * c + CMP_BLOCK - 1 >= SLC_BLOCK * n)
          & (n < n_slc) & (c < nr - 1))
    return jnp.asarray(ov.astype(np.float32), dtype=BF16)


def _nsa_attention(qT, kc, vcT, ksl, vslT, kw, vwT, cb, tp, glT):
    n_heads, hd, _ = qT.shape
    b, s, _ = ksl.shape
    G, R, T = NSA_KV_HEADS, NSA_REP, NSA_TQ
    nr = kc.shape[-2]
    nq = s // T
    n_slc = s // SLC_BLOCK
    assert n_slc <= LANES // 2 and s % T == 0 and nr % SUBLANES == 0
    ov = _overlap_matrix(nr, n_slc)
    kern = functools.partial(_nsa_kernel, n_slc=n_slc, top_k=min(SLC_TOPK, n_slc))
    k_spec = pl.BlockSpec((None, s, LANES), lambda i, j, k: (i, 0, j))
    vT_spec = pl.BlockSpec((None, hd, s), lambda i, j, k: (j, 0, i))
    ones_rows = 16
    pipe_scratch = [pltpu.VMEM((2, T, R * T), F32), pltpu.VMEM((2, T, R * T), BF16),
                    pltpu.VMEM((hd + ones_rows, R * T), F32)]
    value_scratch = [pltpu.VMEM((hd + ones_rows, s), BF16)] * 2
    return pl.pallas_call(
        kern, grid=(b, G, nq),
        in_specs=[
            pl.BlockSpec((R, hd, T), lambda i, j, k: (j, 0, i * nq + k)),
            pl.BlockSpec((None, None, nr, LANES), lambda i, j, k: (i, j, 0, 0)),
            pl.BlockSpec((None, None, hd, nr), lambda i, j, k: (i, j, 0, 0)),
            pl.BlockSpec(ov.shape, lambda i, j, k: (0, 0)),
            k_spec, vT_spec, k_spec, vT_spec,
            pl.BlockSpec((None,) + cb.shape[1:], lambda i, j, k: (j, 0, 0)),
            pl.BlockSpec((None,) + tp.shape[1:], lambda i, j, k: (j, 0, 0, 0)),
            pl.BlockSpec((None, 16, T), lambda i, j, k: (j, 0, i * nq + k)),
        ],
        out_specs=pl.BlockSpec((R * hd, T), lambda i, j, k: (j, i * nq + k)),
        out_shape=jax.ShapeDtypeStruct((n_heads * hd, b * s), BF16),
        scratch_shapes=[pltpu.VMEM((2 * hd, R * T), BF16), pltpu.VMEM((2 * hd, R * T), BF16),
                        pltpu.VMEM((LANES // 2, T), F32), pltpu.VMEM((s, LANES), BF16)]
        + value_scratch + pipe_scratch + pipe_scratch,
        compiler_params=_cparams(3), name="nsa_attention",
    )(qT, kc, vcT, ov, ksl, vslT, kw, vwT, cb, tp, glT)


def _merge_kernel(x_ref, ys_ref, ynT_ref, mg_ref, ws_ref, wn_ref, wo_ref, g_ref, o_ref):
    d = x_ref.shape[1]
    a = _dot(ys_ref[...], ws_ref[...])
    bb = lax.dot_general(ynT_ref[...], wn_ref[...], (((0,), (0,)), ((), ())), preferred_element_type=F32)
    mg = _sigmoid(mg_ref[...].astype(F32))
    mixed = mg[:, :d] * a + mg[:, d:] * bb
    o = _dot(mixed.astype(BF16), wo_ref[...])
    o_ref[...] = x_ref[...] + _rms(o, g_ref[...])


def _merge(x2d, y_ssd, y_nsaT, mg, w_s, w_n, w_o, g, tm=512):
    m, d = x2d.shape
    tm = min(tm, m)
    row = lambda a: pl.BlockSpec((tm, a.shape[1]), lambda i: (i, 0))
    full = lambda a: pl.BlockSpec(a.shape, lambda i: (0, 0))
    g = g.reshape(1, d)
    return pl.pallas_call(
        _merge_kernel, grid=(m // tm,),
        in_specs=[row(x2d), row(y_ssd), pl.BlockSpec((y_nsaT.shape[0], tm), lambda i: (0, i)), row(mg),
                  full(w_s), full(w_n), full(w_o), full(g)],
        out_specs=pl.BlockSpec((tm, d), lambda i: (i, 0)),
        out_shape=jax.ShapeDtypeStruct((m, d), F32),
        compiler_params=_cparams(1), name="merge",
    )(x2d, y_ssd, y_nsaT, mg, w_s, w_n, w_o, g)


def _xattn_kernel(x_ref, kv_ref, gpre_ref, wq_ref, wo_ref, gpost_ref, o_ref):
    x = x_ref[...]
    h = _rms(x, gpre_ref[...]).astype(BF16)
    q = _dot(h, wq_ref[...])
    width = X_HEADS * X_HEAD_DIM
    scale = X_HEAD_DIM ** -0.5
    outs = []
    for hh in range(X_HEADS):
        sl = slice(hh * X_HEAD_DIM, (hh + 1) * X_HEAD_DIM)
        lg = _dot_nt(q[:, sl].astype(BF16), kv_ref[:, sl]) * scale
        e = jnp.exp(lg - jnp.max(lg, axis=-1, keepdims=True))
        p = e / jnp.sum(e, axis=-1, keepdims=True)
        outs.append(_dot(p.astype(BF16), kv_ref[:, width + hh * X_HEAD_DIM:width + (hh + 1) * X_HEAD_DIM]))
    o = _dot(jnp.concatenate(outs, axis=-1).astype(BF16), wo_ref[...])
    o_ref[...] = x + _rms(o, gpost_ref[...])


def _xattn(x3d, kv, g_pre, w_q, w_o, g_post, tm=512):
    b, s, d = x3d.shape
    tm = min(tm, s)
    ml = kv.shape[1]
    full = lambda a: pl.BlockSpec(a.shape, lambda i, j: (0, 0))
    g_pre, g_post = g_pre.reshape(1, d), g_post.reshape(1, d)
    return pl.pallas_call(
        _xattn_kernel, grid=(b, s // tm),
        in_specs=[pl.BlockSpec((None, tm, d), lambda i, j: (i, j, 0)),
                  pl.BlockSpec((None, ml, kv.shape[2]), lambda i, j: (i, 0, 0)),
                  full(g_pre), full(w_q), full(w_o), full(g_post)],
        out_specs=pl.BlockSpec((None, tm, d), lambda i, j: (i, j, 0)),
        out_shape=jax.ShapeDtypeStruct((b, s, d), F32),
        compiler_params=_cparams(2), name="xattn",
    )(x3d, kv, g_pre, w_q, w_o, g_post)


def _mlp_kernel(x_ref, gpre_ref, w1_ref, w2_ref, gpost_ref, o_ref, *, f_chunk):
    x = x_ref[...]
    h = _rms(x, gpre_ref[...]).astype(BF16)
    d_ff = w1_ref.shape[1]
    acc = jnp.zeros(x.shape, F32)
    for f0 in range(0, d_ff, f_chunk):
        u = jnp.maximum(_dot(h, w1_ref[:, f0:f0 + f_chunk]), 0.0)
        acc = acc + _dot((u * u).astype(BF16), w2_ref[f0:f0 + f_chunk, :])
    o_ref[...] = x + _rms(acc, gpost_ref[...])


def _mlp(x2d, g_pre, w1, w2, g_post, tm=512):
    m, d = x2d.shape
    tm = min(tm, m)
    full = lambda a: pl.BlockSpec(a.shape, lambda i: (0, 0))
    g_pre, g_post = g_pre.reshape(1, d), g_post.reshape(1, d)
    return pl.pallas_call(
        functools.partial(_mlp_kernel, f_chunk=1024), grid=(m // tm,),
        in_specs=[pl.BlockSpec((tm, d), lambda i: (i, 0)), full(g_pre), full(w1), full(w2), full(g_post)],
        out_specs=pl.BlockSpec((tm, d), lambda i: (i, 0)),
        out_shape=jax.ShapeDtypeStruct((m, d), F32),
        compiler_params=_cparams(1), name="mlp",
    )(x2d, g_pre, w1, w2, g_post)


def _pad_heads(w, n_heads, hd):
    k = w.shape[0]
    w = w.reshape(k, n_heads, hd)
    return jnp.pad(w, ((0, 0), (0, 0), (0, LANES - hd))).reshape(k, n_heads * LANES)


def _layer(x, mem, w_in, conv_w, conv_b, dt_bias, a_log, d_skip, ssd_norm, cmp_pos, cmp_w1, cmp_w2,
           rel_bias, w_br_ssd, w_br_nsa, w_out, w_xq, w_xkv, w_xo, w_ff1, w_ff2,
           n_mix_pre, n_mix_post, n_x_pre, n_x_post, n_mem, n_ffn_pre, n_ffn_post):
    b, s, d = x.shape
    G, R, HD = NSA_KV_HEADS, NSA_REP, NSA_HEAD_DIM
    d_inner = 2 * d
    n_ssd_heads = d_inner // SSD_HEAD_DIM
    conv_dim = d_inner + 2 * SSD_GROUPS * SSD_STATE
    nsa_w, kv_w = NSA_HEADS * HD, G * HD
    sizes = (d_inner, conv_dim, n_ssd_heads, nsa_w, 6 * kv_w, 3 * NSA_HEADS, 2 * d)
    offs = np.concatenate([[0], np.cumsum(sizes)])
    seg = lambda i: w_in[:, offs[i]:offs[i + 1]]
    w_z, w_xbc, w_dt, w_q, w_kv, w_gate, w_mg = (seg(i) for i in range(7))
    w_small = jnp.pad(jnp.concatenate([w_dt, w_gate], axis=1),
                      ((0, 0), (0, LANES - n_ssd_heads - 3 * NSA_HEADS)))
    w_kv6 = w_kv.reshape(d, 6, kv_w)
    w_cmp = jnp.concatenate([w_kv6[:, 0], w_kv6[:, 1]], axis=1)
    bf = lambda a: a.astype(BF16)

    x2d = x.reshape(b * s, d)
    z, xbc = _norm_matmul(x2d, n_mix_pre, [bf(w_z), bf(w_xbc)], [BF16, BF16])
    w_gl = jnp.transpose(w_gate.reshape(d, 3, G, R), (2, 1, 3, 0)).reshape(G, 3 * R, d)
    w_gl = jnp.pad(w_gl, ((0, 0), (0, 16 - 3 * R), (0, 0))).reshape(G * 16, d)
    small, dtT, glT, qT, kv_cmp, vslT, vwT = _norm_matmul(
        x2d, n_mix_pre,
        [bf(w_small), bf(w_dt.T), bf(w_gl), bf(w_q.T * (HD ** -0.5 * LOG2E)), bf(w_cmp),
         bf(w_kv6[:, 3].T), bf(w_kv6[:, 5].T)],
        [F32, F32, F32, BF16, BF16, BF16, BF16],
        transposed=(False, True, True, True, False, True, True))
    ksl, kw, mg = _norm_matmul(
        x2d, n_mix_pre, [bf(_pad_heads(w_kv6[:, 2], G, HD)), bf(_pad_heads(w_kv6[:, 4], G, HD)), bf(w_mg)],
        [BF16] * 3)

    y_ssd = _ssd_mixer(z.reshape(b, s, -1), xbc.reshape(b, s, -1), small.reshape(b, s, -1), dtT,
                       conv_w, conv_b, dt_bias, a_log, d_skip, ssd_norm)

    nr = s // CMP_STRIDE
    kr = kv_cmp.reshape(b, nr, CMP_STRIDE, 2, G, HD)
    kr = jnp.transpose(kr, (3, 0, 4, 1, 2, 5)).reshape(2, b, G, nr, CMP_STRIDE * HD)
    pos = jnp.broadcast_to(bf(cmp_pos).reshape(2, 1, CMP_BLOCK * HD), (2, SUBLANES, CMP_BLOCK * HD))
    w2p = jnp.pad(bf(cmp_w2), ((0, 0), (0, 0), (0, LANES - HD)))
    kvc = _compress(kr, pos, bf(cmp_w1), w2p)
    vcT = jnp.swapaxes(kvc[1][..., :HD], -1, -2)
    cb, tp = _nsa_tables(rel_bias, nr)
    y_nsa = _nsa_attention(qT.reshape(NSA_HEADS, HD, b * s), kvc[0], vcT,
                           ksl.reshape(b, s, -1), vslT.reshape(G, HD, b * s),
                           kw.reshape(b, s, -1), vwT.reshape(G, HD, b * s), cb, tp,
                           glT.reshape(G, 16, b * s))

    x1 = _merge(x2d, y_ssd.reshape(b * s, -1), y_nsa, mg, bf(w_br_ssd), bf(w_br_nsa), bf(w_out), n_mix_post)

    (kv_mem,) = _norm_matmul(mem.reshape(-1, d), n_mem, [bf(w_xkv)], [BF16])
    x2 = _xattn(x1.reshape(b, s, d), kv_mem.reshape(b, mem.shape[1], -1), n_x_pre, bf(w_xq), bf(w_xo),
                n_x_post)

    x3 = _mlp(x2.reshape(b * s, d), n_ffn_pre, bf(w_ff1), bf(w_ff2), n_ffn_post)
    return x3.reshape(b, s, d)


def kernel(x, mem, w_in, ssd_conv_w, ssd_conv_b, ssd_dt_bias, ssd_a_log, ssd_d_skip, ssd_norm, cmp_pos,
           cmp_w1, cmp_w2, rel_bias, w_br_ssd, w_br_nsa, w_out, w_xq, w_xkv, w_xo, w_ff1, w_ff2,
           norm_mix_pre, norm_mix_post, norm_x_pre, norm_x_post, norm_mem, norm_ffn_pre, norm_ffn_post):
    for l in range(w_in.shape[0]):
        x = _layer(x, mem, w_in[l], ssd_conv_w[l], ssd_conv_b[l], ssd_dt_bias[l], ssd_a_log[l],
                   ssd_d_skip[l], ssd_norm[l], cmp_pos[l], cmp_w1[l], cmp_w2[l], rel_bias,
                   w_br_ssd[l], w_br_nsa[l], w_out[l], w_xq[l], w_xkv[l], w_xo[l], w_ff1[l], w_ff2[l],
                   norm_mix_pre[l], norm_mix_post[l], norm_x_pre[l], norm_x_post[l], norm_mem[l],
                   norm_ffn_pre[l], norm_ffn_post[l])
    return x
```

```python
import functools
import math

import numpy as np
import jax
import jax.numpy as jnp
from jax import lax
from jax.experimental import pallas as pl
from jax.experimental.pallas import tpu as pltpu

F32 = jnp.float32
BF16 = jnp.bfloat16

NORM_EPS = 1e-6
NEG_INF = -1e30
FORCE_SCORE = 1e9
SEL_PENALTY = NEG_INF

LANES = 128
SUBLANES = 8
VMEM_LIMIT = 56 * 1024 * 1024

SSD_HEAD_DIM = 64
SSD_GROUPS = 8
SSD_STATE = 128
SSD_CONV = 4
SSD_CHUNK = 128
NSA_HEADS = 16
NSA_KV_HEADS = 4
NSA_HEAD_DIM = 64
NSA_REP = NSA_HEADS // NSA_KV_HEADS
CMP_BLOCK = 32
CMP_STRIDE = 16
SLC_BLOCK = 64
SLC_TOPK = 16
WINDOW = 512
NSA_TQ = 256
LOG2E = math.log2(math.e)
REL_BUCKETS = 32
REL_MAX_DIST = 128
X_HEADS = 4
X_HEAD_DIM = 128

TILE_DIAG, TILE_NEAR, TILE_NONE, TILE_EDGE, TILE_ZERO = range(5)


def _cparams(n_grid):
    return pltpu.CompilerParams(dimension_semantics=("arbitrary",) * n_grid,
                                vmem_limit_bytes=VMEM_LIMIT)


def _sigmoid(x):
    return jax.nn.sigmoid(x)


def _rms(x, g):
    return x * lax.rsqrt(jnp.mean(x * x, axis=-1, keepdims=True) + NORM_EPS) * g


def _dot(a, b):
    return jnp.dot(a, b, preferred_element_type=F32)


def _dot_nt(a, b):
    return lax.dot_general(a, b, (((1,), (1,)), ((), ())), preferred_element_type=F32)


def _split3(x):
    x1 = x.astype(BF16)
    r1 = x - x1.astype(F32)
    x2 = r1.astype(BF16)
    x3 = (r1 - x2.astype(F32)).astype(BF16)
    return x1, x2, x3


def _norm_matmul_kernel(x_ref, g_ref, *refs, transposed, n_chunk):
    n_out = len(transposed)
    w_refs, o_refs = refs[:n_out], refs[n_out:]
    h = _rms(x_ref[...], g_ref[...]).astype(BF16)
    for w_ref, o_ref, tr in zip(w_refs, o_refs, transposed):
        if tr:
            o_ref[...] = _dot_nt(w_ref[...], h).astype(o_ref.dtype)
            continue
        n = w_ref.shape[1]
        step = min(n, n_chunk)
        for n0 in range(0, n, step):
            o_ref[:, n0:n0 + step] = _dot(h, w_ref[:, n0:n0 + step]).astype(o_ref.dtype)


def _norm_matmul(x2d, g, ws, out_dtypes, transposed=None, tm=512):
    m, k = x2d.shape
    tm = min(tm, m)
    transposed = tuple(transposed) if transposed is not None else (False,) * len(ws)
    in_specs = [pl.BlockSpec((tm, k), lambda i: (i, 0)), pl.BlockSpec((1, k), lambda i: (0, 0))]
    in_specs += [pl.BlockSpec(w.shape, lambda i: (0, 0)) for w in ws]
    out_specs, out_shape = [], []
    for w, dt, tr in zip(ws, out_dtypes, transposed):
        if tr:
            out_specs.append(pl.BlockSpec((w.shape[0], tm), lambda i: (0, i)))
            out_shape.append(jax.ShapeDtypeStruct((w.shape[0], m), dt))
        else:
            out_specs.append(pl.BlockSpec((tm, w.shape[1]), lambda i: (i, 0)))
            out_shape.append(jax.ShapeDtypeStruct((m, w.shape[1]), dt))
    return pl.pallas_call(
        functools.partial(_norm_matmul_kernel, transposed=transposed, n_chunk=1024),
        grid=(m // tm,), in_specs=in_specs, out_specs=out_specs, out_shape=out_shape,
        compiler_params=_cparams(1), name="norm_proj",
    )(x2d, g.reshape(1, k), *ws)


def _softplus(x):
    return jnp.maximum(x, 0.0) + jnp.log(1.0 + jnp.exp(-jnp.abs(x)))


def _ssd_kernel(xbc_ref, prev_ref, z_ref, dt_ref, dtT_ref, cw_ref, cb_ref, dtb_ref, dtbT_ref,
                alog_ref, alogT_ref, dskip_ref, nw_ref, y_ref, state_ref, xc_ref, ybuf_ref,
                *, n_heads, d_inner):
    L, P, N, G = SSD_CHUNK, SSD_HEAD_DIM, SSD_STATE, SSD_GROUPS
    conv_dim = xc_ref.shape[1]
    c = pl.program_id(1)

    @pl.when(c == 0)
    def _():
        state_ref[...] = jnp.zeros_like(state_ref)

    cw = 512
    prev_rows = prev_ref.shape[0]
    row = lax.broadcasted_iota(jnp.int32, (prev_rows, cw), 0)
    for j in range(conv_dim // cw):
        sl = slice(j * cw, (j + 1) * cw)
        cur = xbc_ref[:, sl].astype(F32)
        prev = jnp.where(c == 0, 0.0, prev_ref[:, sl].astype(F32))
        acc = cb_ref[:, sl] + cw_ref[SSD_CONV - 1:SSD_CONV, sl] * cur
        for k in range(1, SSD_CONV):
            rc = pltpu.roll(cur, k, 0)
            rp = pltpu.roll(prev, k, 0)
            head = jnp.where(row < k, rp, rc[:prev_rows])
            shifted = jnp.concatenate([head, rc[prev_rows:]], axis=0)
            acc = acc + cw_ref[SSD_CONV - 1 - k:SSD_CONV - k, sl] * shifted
        xc_ref[:, sl] = acc * _sigmoid(acc)

    dt = _softplus(dt_ref[:, 0:n_heads] + dtb_ref[...])
    a = dt * (-LOG2E * jnp.exp(alog_ref[...]))
    dtT = _softplus(dtT_ref[...] + dtbT_ref[...])
    aT = dtT * (-LOG2E * jnp.exp(alogT_ref[...]))
    ri = lax.broadcasted_iota(jnp.int32, (L, L), 0)
    ci = lax.broadcasted_iota(jnp.int32, (L, L), 1)
    causal = ci <= ri
    tri = jnp.where(causal, 1.0, 0.0).astype(BF16)
    triu = jnp.where(ri <= ci, 1.0, 0.0).astype(BF16)
    a_cs = sum(_dot(tri, ai) for ai in _split3(a))
    a_csT = sum(_dot(ai, triu) for ai in _split3(aT))
    lane = lax.broadcasted_iota(jnp.int32, (L, 2 * P), 1)
    lo_half = lane < P
    lane1 = lax.broadcasted_iota(jnp.int32, (1, 2 * P), 1) < P

    heads_per_group = n_heads // G
    for g in range(G):
        b_g = xc_ref[:, d_inner + g * N:d_inner + (g + 1) * N]
        c_g = xc_ref[:, d_inner + G * N + g * N:d_inner + G * N + (g + 1) * N]
        c_gb = c_g.astype(BF16)
        cb = _dot_nt(c_gb, b_g.astype(BF16))
        b_gT = b_g.T.astype(BF16)
        for pp in range(heads_per_group // 2):
            pair = g * (heads_per_group // 2) + pp
            h0, h1 = 2 * pair, 2 * pair + 1
            xs = xc_ref[:, pair * 2 * P:(pair + 1) * 2 * P]
            dt_pair = jnp.where(lo_half, dt[:, h0:h0 + 1], dt[:, h1:h1 + 1])
            xd = xs * dt_pair
            xd_b = xd.astype(BF16)
            ys = []
            for h in (h0, h1):
                seg = jnp.where(causal, a_cs[:, h:h + 1] - a_csT[h:h + 1, :], NEG_INF)
                mat = (cb * jnp.exp2(seg)).astype(BF16)
                ys.append(_dot(mat, xd_b))
            y = jnp.where(lo_half, ys[0], ys[1])
            cs_pair = jnp.where(lo_half, a_cs[:, h0:h0 + 1], a_cs[:, h1:h1 + 1])
            st = state_ref[pair]
            y = y + _dot(c_gb, st.astype(BF16)) * jnp.exp2(cs_pair)
            tot = jnp.where(lane1, a_csT[h0:h0 + 1, L - 1:L], a_csT[h1:h1 + 1, L - 1:L])
            xdd = (xd * jnp.exp2(tot - cs_pair)).astype(BF16)
            state_ref[pair] = st * jnp.exp2(tot) + _dot(b_gT, xdd)
            ybuf_ref[:, pair * 2 * P:(pair + 1) * 2 * P] = y + xs * dskip_ref[:, pair * 2 * P:(pair + 1) * 2 * P]

    gw = d_inner // G
    for g in range(G):
        sl = slice(g * gw, (g + 1) * gw)
        zz = z_ref[:, sl].astype(F32)
        yg = ybuf_ref[:, sl] * (zz * _sigmoid(zz))
        y_ref[:, sl] = _rms(yg, nw_ref[:, sl]).astype(y_ref.dtype)


def _ssd_mixer(z, xbc, small, dtT, conv_w, conv_b, dt_bias, a_log, d_skip, norm_w):
    b, s, d_inner = z.shape
    conv_dim = xbc.shape[-1]
    n_heads = d_inner // SSD_HEAD_DIM
    L = SSD_CHUNK
    nc = s // L
    prev_rows = 16
    xbc_prev = xbc.reshape(b, s // prev_rows, prev_rows, conv_dim)
    blocks_per_chunk = L // prev_rows
    kern = functools.partial(_ssd_kernel, n_heads=n_heads, d_inner=d_inner)
    const = lambda shape: pl.BlockSpec(shape, lambda i, j: (0,) * len(shape))
    return pl.pallas_call(
        kern, grid=(b, nc),
        in_specs=[
            pl.BlockSpec((None, L, conv_dim), lambda i, j: (i, j, 0)),
            pl.BlockSpec((None, None, prev_rows, conv_dim),
                         lambda i, j: (i, jnp.maximum(j * blocks_per_chunk - 1, 0), 0, 0)),
            pl.BlockSpec((None, L, d_inner), lambda i, j: (i, j, 0)),
            pl.BlockSpec((None, L, small.shape[-1]), lambda i, j: (i, j, 0)),
            pl.BlockSpec((n_heads, L), lambda i, j: (0, i * nc + j)),
            const((SSD_CONV, conv_dim)), const((1, conv_dim)),
            const((1, n_heads)), const((n_heads, 1)),
            const((1, n_heads)), const((n_heads, 1)),
            const((1, d_inner)), const((1, d_inner)),
        ],
        out_specs=pl.BlockSpec((None, L, d_inner), lambda i, j: (i, j, 0)),
        out_shape=jax.ShapeDtypeStruct((b, s, d_inner), BF16),
        scratch_shapes=[pltpu.VMEM((n_heads // 2, SSD_STATE, 2 * SSD_HEAD_DIM), F32),
                        pltpu.VMEM((L, conv_dim), F32),
                        pltpu.VMEM((L, d_inner), F32)],
        compiler_params=_cparams(2), name="ssd",
    )(xbc, xbc_prev, z, small, dtT, conv_w.astype(F32), conv_b.reshape(1, -1).astype(F32),
      dt_bias.reshape(1, -1), dt_bias.reshape(-1, 1), a_log.reshape(1, -1), a_log.reshape(-1, 1),
      jnp.repeat(d_skip, SSD_HEAD_DIM).reshape(1, -1), norm_w.reshape(1, -1))


def _compress_kernel(kr_ref, pos_ref, w1_ref, w2_ref, o_ref):
    nr, half = kr_ref.shape
    kr = kr_ref[...]
    p1 = _dot(kr, w1_ref[0:half, :])
    p2 = _dot(kr, w1_ref[half:2 * half, :])
    pb = _dot(pos_ref[...], w1_ref[...])[0:1]
    hid = p1 + pltpu.roll(p2, nr - 1, 0) + pb
    hid = hid * _sigmoid(hid)
    out = _dot(hid.astype(BF16), w2_ref[...])
    rows = lax.broadcasted_iota(jnp.int32, out.shape, 0)
    o_ref[...] = jnp.where(rows < nr - 1, out, 0.0).astype(o_ref.dtype)


def _compress(kr, pos, w1, w2p):
    two, b, g, nr, half = kr.shape
    hidden = w1.shape[-1]
    return pl.pallas_call(
        _compress_kernel,
        grid=(two, b, g),
        in_specs=[
            pl.BlockSpec((None, None, None, nr, half), lambda t, i, j: (t, i, j, 0, 0)),
            pl.BlockSpec((None, SUBLANES, 2 * half), lambda t, i, j: (t, 0, 0)),
            pl.BlockSpec((None, 2 * half, hidden), lambda t, i, j: (t, 0, 0)),
            pl.BlockSpec((None, hidden, LANES), lambda t, i, j: (t, 0, 0)),
        ],
        out_specs=pl.BlockSpec((None, None, None, nr, LANES), lambda t, i, j: (t, i, j, 0, 0)),
        out_shape=jax.ShapeDtypeStruct((two, b, g, nr, LANES), BF16),
        compiler_params=_cparams(3), name="nsa_compress",
    )(kr, pos, w1, w2p)


def _colmax(a):
    return jnp.max(a, axis=0, keepdims=True)


def _colsum(a):
    return jnp.sum(a, axis=0, keepdims=True)


class _SoftmaxPipe:
    def __init__(self, s_ref, pe_ref, acc_ref, hd):
        self.s, self.pe, self.acc, self.hd = s_ref, pe_ref, acc_ref, hd

    def start(self, first_scores, slot):
        cols = first_scores.shape[1]
        self.s[slot] = first_scores
        self.pe[1 - slot] = jnp.zeros(self.pe.shape[1:], self.pe.dtype)
        self.acc[...] = jnp.zeros(self.acc.shape, F32)
        return _colmax(first_scores), jnp.ones((1, cols), F32), jnp.full((1, cols), NEG_INF, F32)

    def step(self, slot, carry, v_prev, next_scores):
        cm_cur, alpha_prev, m_old = carry
        self.acc[...] = alpha_prev * self.acc[...] + _dot(v_prev, self.pe[1 - slot])
        cm_next = cm_cur
        if next_scores is not None:
            s_next = next_scores()
            self.s[1 - slot] = s_next
            cm_next = _colmax(s_next)
        m_new = jnp.maximum(m_old, cm_cur)
        self.pe[slot] = jnp.exp2(self.s[slot] - m_new).astype(self.pe.dtype)
        return cm_next, jnp.exp2(m_old - m_new), m_new

    def finish(self, slot_last, carry, v_last):
        _, alpha_prev, _ = carry
        acc = alpha_prev * self.acc[...] + _dot(v_last, self.pe[slot_last])
        return acc[0:self.hd] * (1.0 / acc[self.hd:self.hd + 1])

    def shifted_step(self, slot, v_prev, next_scores):
        self.acc[...] += _dot(v_prev, self.pe[1 - slot])
        self.s[1 - slot] = next_scores()
        self.pe[slot] = jnp.exp2(self.s[slot]).astype(self.pe.dtype)


def _nsa_kernel(qT_ref, kc_ref, vcT_ref, ov_ref, ksl_ref, vslT_ref, kw_ref, vwT_ref, cb_ref, tp_ref,
                gl_ref, o_ref, q0_ref, qa_ref, qs_ref, rank_ref, kst_ref, vsa_ref, vwa_ref, s_ref, pe_ref,
                acc_ref, ws_ref, wpe_ref, wacc_ref, *, n_slc, top_k):
    T, R, HD = NSA_TQ, NSA_REP, NSA_HEAD_DIM
    nr = kc_ref.shape[0]
    per_tile = T // CMP_STRIDE
    front = nr - per_tile
    half = LANES // 2
    qi = pl.program_id(2)
    colmax, colsum = _colmax, _colsum

    @pl.when(qi == 0)
    def _():
        for src, dst in ((vslT_ref, vsa_ref), (vwT_ref, vwa_ref)):
            dst[0:HD, :] = src[...]
            dst[HD:, :] = jnp.ones((dst.shape[0] - HD, dst.shape[1]), dst.dtype)
        krow = lax.broadcasted_iota(jnp.int32, (T, LANES), 0)
        klane = lax.broadcasted_iota(jnp.int32, (T, LANES), 1)
        for t0 in range(0, ksl_ref.shape[0], T):
            blk = t0 // SLC_BLOCK + krow // SLC_BLOCK
            tag = jnp.where(klane - half == blk, 1.0, 0.0).astype(BF16)
            kst_ref[t0:t0 + T, :] = jnp.where(klane >= half, tag, ksl_ref[t0:t0 + T, :])

    qT = jnp.concatenate([qT_ref[r] for r in range(R)], axis=1)
    q0_ref[0:HD, :] = qT
    q0_ref[HD:2 * HD, :] = jnp.zeros((HD, R * T), BF16)
    qa_ref[0:HD, :] = qT

    n_win = WINDOW // T

    def win_offset(rel):
        return pl.multiple_of(jnp.maximum(qi - rel, 0) * T, T)

    def win_scores(rel):
        tile = TILE_EDGE if rel == n_win else (TILE_ZERO if rel >= 2 else (TILE_NEAR if rel == 1 else TILE_DIAG))
        if rel > 0:
            tile = jnp.where(qi >= rel, tile, TILE_NONE)
        return _dot(kw_ref[pl.ds(win_offset(rel), T), :], q0_ref[...]) + tp_ref[tile]

    def win_values(rel):
        return vwa_ref[:, pl.ds(win_offset(rel), T)]

    win = _SoftmaxPipe(ws_ref, wpe_ref, wacc_ref, HD)
    rels = list(range(n_win, -1, -1))
    carry = win.start(win_scores(rels[0]), 0)
    for i, rel in enumerate(rels):
        nxt = functools.partial(win_scores, rels[i + 1]) if i + 1 < len(rels) else None
        carry = win.step(i % 2, carry, win_values(rels[max(i - 1, 0)]), nxt)
    o_win = win.finish((len(rels) - 1) % 2, carry, win_values(0))

    off = pl.multiple_of(front - qi * per_tile, SUBLANES)
    lg = _dot(kc_ref[...], q0_ref[...]) + cb_ref[pl.ds(off, nr), :]
    m_c = colmax(lg)
    e = jnp.exp2(lg - m_c)
    p = e * jnp.where(m_c > 0.5 * NEG_INF, 1.0 / colsum(e), 0.0)
    o_cmp = _dot(vcT_ref[...], p.astype(BF16))

    psum = p[:, 0:T]
    for r in range(1, R):
        psum = psum + p[:, r * T:(r + 1) * T]
    imp = sum(_dot(ov_ref[...], pi) for pi in _split3(psum))
    nio = lax.broadcasted_iota(jnp.int32, (half, T), 0)
    tio = lax.broadcasted_iota(jnp.int32, (half, T), 1)
    tb = (qi * T + tio) // SLC_BLOCK
    forced = (nio == 0) | (nio == tb) | (nio == tb - 1)
    imp = jnp.where(forced, FORCE_SCORE, imp)
    imp = jnp.where(nio > tb, NEG_INF, imp)
    n_grp = half // SUBLANES
    grp = [imp[SUBLANES * v:SUBLANES * (v + 1)] for v in range(n_grp)]
    nio8 = lax.broadcasted_iota(jnp.int32, (SUBLANES, T), 0)
    rank_ref[...] = jnp.zeros(rank_ref.shape, F32)
    last_block = (qi * T + T - 1) // SLC_BLOCK
    for mg in range(pl.cdiv(n_slc, SUBLANES)):

        @pl.when(mg * SUBLANES <= last_block)
        def _():
            hits = [jnp.zeros((SUBLANES, T), F32) for _ in range(n_grp)]
            for mm in range(mg * SUBLANES, min((mg + 1) * SUBLANES, n_slc)):
                rowv = imp[mm:mm + 1, :]
                for v in range(n_grp):
                    if SUBLANES * v > mm:
                        hit = jnp.where(rowv >= grp[v], 1.0, 0.0)
                    elif SUBLANES * v + SUBLANES - 1 < mm:
                        hit = jnp.where(rowv > grp[v], 1.0, 0.0)
                    else:
                        hit = jnp.where(nio8 > mm - SUBLANES * v, jnp.where(rowv >= grp[v], 1.0, 0.0),
                                        jnp.where(rowv > grp[v], 1.0, 0.0))
                    hits[v] = hits[v] + hit
            rank_ref[...] += jnp.concatenate(hits, axis=0)

    rank = rank_ref[...]
    pen = jnp.where(rank < top_k, jnp.where(nio <= tb, 0.0, SEL_PENALTY), SEL_PENALTY).astype(BF16)
    qa_ref[HD:2 * HD, :] = jnp.concatenate([pen] * R, axis=1)

    def sel_scores(kj):
        return _dot(kst_ref[pl.ds(pl.multiple_of(kj * T, T), T), :], qa_ref[...])

    def sel_values(kj):
        return vsa_ref[:, pl.ds(pl.multiple_of(kj * T, T), T)]

    n_far = jnp.maximum(qi - 1, 0)
    near = jnp.maximum(qi - 1, 0)
    near_table = jnp.where(qi >= 1, TILE_NEAR, TILE_NONE)
    odd = n_far % 2
    sel = _SoftmaxPipe(s_ref, pe_ref, acc_ref, HD)

    def exact_selected():
        def far_step(kj, slot, carry):
            return sel.step(slot, carry, sel_values(jnp.maximum(kj - 1, 0)), functools.partial(sel_scores, kj + 1))

        carry = sel.start(sel_scores(0), odd)
        carry = lax.cond(odd == 1, lambda c: far_step(0, 1, c), lambda c: c, carry)
        carry = lax.fori_loop(
            0, n_far // 2, lambda j, c: far_step(odd + 2 * j + 1, 1, far_step(odd + 2 * j, 0, c)), carry)
        s_near = s_ref[0] + tp_ref[near_table]
        s_ref[0] = s_near
        carry = (colmax(s_near),) + tuple(carry[1:])
        carry = sel.step(0, carry, sel_values(jnp.maximum(n_far - 1, 0)),
                         lambda: sel_scores(qi) + tp_ref[TILE_DIAG])
        carry = sel.step(1, carry, sel_values(n_far), None)
        return sel.finish(1, carry, sel_values(qi))

    s_d = sel_scores(qi) + tp_ref[TILE_DIAG]
    s_n = sel_scores(near) + tp_ref[near_table]
    shift = jnp.maximum(colmax(s_d), colmax(s_n)).astype(BF16).astype(F32)
    qs_ref[0:HD, :] = qT
    qs_ref[HD:2 * HD, :] = (jnp.concatenate([pen.astype(F32)] * R, axis=1) - shift).astype(BF16)

    def shifted_scores(kj):
        return _dot(kst_ref[pl.ds(pl.multiple_of(kj * T, T), T), :], qs_ref[...])

    def far_shifted(kj, slot):
        sel.shifted_step(slot, sel_values(jnp.maximum(kj - 1, 0)), functools.partial(shifted_scores, kj + 1))

    s_ref[odd] = shifted_scores(0)
    pe_ref[1 - odd] = jnp.zeros(pe_ref.shape[1:], BF16)
    acc_ref[...] = (_dot(sel_values(qi), jnp.exp2(s_d - shift).astype(BF16))
                    + _dot(sel_values(near), jnp.exp2(s_n - shift).astype(BF16)))

    @pl.when(odd == 1)
    def _():
        far_shifted(0, 1)

    def far_pair(j, carry):
        far_shifted(odd + 2 * j, 0)
        far_shifted(odd + 2 * j + 1, 1)
        return carry

    lax.fori_loop(0, n_far // 2, far_pair, 0)
    acc = acc_ref[...] + _dot(sel_values(jnp.maximum(n_far - 1, 0)), pe_ref[1])
    denom = acc[HD:HD + 1]
    unsafe = (jnp.sum(jnp.where(denom < 2.0 ** 100, 0.0, 1.0))
              + jnp.sum(jnp.where(jnp.abs(acc[0:HD]) < 3.0e38, 0.0, 1.0))) > 0.0
    o_slc = lax.cond(unsafe, exact_selected, lambda: acc[0:HD] * (1.0 / denom))

    sg = _sigmoid(gl_ref[...])
    for r in range(R):
        cols = slice(r * T, (r + 1) * T)
        o = (sg[r:r + 1] * o_cmp[:, cols] + sg[R + r:R + r + 1] * o_slc[:, cols]
             + sg[2 * R + r:2 * R + r + 1] * o_win[:, cols])
        o_ref[r * HD:(r + 1) * HD, :] = o.astype(o_ref.dtype)


def _t5_bucket_np(dist):
    n = np.maximum(dist, 0)
    max_exact = REL_BUCKETS // 2
    nf = np.maximum(n, 1).astype(np.float32)
    large = max_exact + (np.log(nf / np.float32(max_exact)) / np.float32(math.log(REL_MAX_DIST / max_exact))
                         * np.float32(REL_BUCKETS - max_exact)).astype(np.int32)
    large = np.minimum(large, REL_BUCKETS - 1)
    return np.where(n < max_exact, n, large).astype(np.int32)


def _nsa_tables(rel_bias, nr):
    T, G, R = NSA_TQ, NSA_KV_HEADS, NSA_REP
    table = rel_bias.astype(F32) * LOG2E
    per_tile = T // CMP_STRIDE
    front = nr - per_tile
    t = np.arange(T)[None, :]

    def lay(a):
        keys = a.shape[0]
        return jnp.transpose(a, (2, 0, 1)).reshape(G, R, keys, T).transpose(0, 2, 1, 3).reshape(G, keys, R * T)

    cend = CMP_STRIDE * (np.arange(front + nr)[:, None] - front) + CMP_BLOCK - 1
    dist = t - cend
    def lookup(d):
        onehot = jax.nn.one_hot(_t5_bucket_np(d), REL_BUCKETS, dtype=F32)
        return jnp.einsum("ktb,bh->kth", onehot, table, precision=lax.Precision.HIGHEST)

    cb = jnp.where((dist >= 0)[..., None], lookup(dist), NEG_INF)
    k = np.arange(T)[:, None]
    assert 2 * T - (T - 1) >= REL_MAX_DIST and WINDOW % T == 0 and WINDOW // T >= 2
    last = table[REL_BUCKETS - 1]
    diag = jnp.where((t - k >= 0)[..., None], lookup(t - k) - last, NEG_INF)
    near = lookup(T + t - k) - last
    none = jnp.full_like(near, NEG_INF)
    edge = jnp.where((k > t)[..., None], jnp.zeros_like(near), NEG_INF)
    tiles = [None] * 5
    tiles[TILE_DIAG], tiles[TILE_NEAR], tiles[TILE_NONE], tiles[TILE_EDGE] = diag, near, none, edge
    tiles[TILE_ZERO] = jnp.zeros_like(near)
    tp = jnp.stack([lay(a) for a in tiles], axis=1)
    return lay(cb), tp


def _overlap_matrix(nr, n_slc):
    half = LANES // 2
    n = np.arange(half)[:, None]
    c = np.arange(nr)[None, :]
    ov = ((CMP_STRIDE * c <= SLC_BLOCK * n + SLC_BLOCK - 1) & (CMP_STRIDE * c + CMP_BLOCK - 1 >= SLC_BLOCK * n)
          & (n < n_slc) & (c < nr - 1))
    return jnp.asarray(ov.astype(np.float32), dtype=BF16)


def _nsa_attention(qT, kc, vcT, ksl, vslT, kw, vwT, cb, tp, glT):
    n_heads, hd, _ = qT.shape
    b, s, _ = ksl.shape
    G, R, T = NSA_KV_HEADS, NSA_REP, NSA_TQ
    nr = kc.shape[-2]
    nq = s // T
    n_slc = s // SLC_BLOCK
    assert n_slc <= LANES // 2 and s % T == 0 and nr % SUBLANES == 0
    ov = _overlap_matrix(nr, n_slc)
    kern = functools.partial(_nsa_kernel, n_slc=n_slc, top_k=min(SLC_TOPK, n_slc))
    k_spec = pl.BlockSpec((None, s, LANES), lambda i, j, k: (i, 0, j))
    vT_spec = pl.BlockSpec((None, hd, s), lambda i, j, k: (j, 0, i))
    ones_rows = 16
    pipe_scratch = [pltpu.VMEM((2, T, R * T), F32), pltpu.VMEM((2, T, R * T), BF16),
                    pltpu.VMEM((hd + ones_rows, R * T), F32)]
    value_scratch = [pltpu.VMEM((hd + ones_rows, s), BF16)] * 2
    return pl.pallas_call(
        kern, grid=(b, G, nq),
        in_specs=[
            pl.BlockSpec((R, hd, T), lambda i, j, k: (j, 0, i * nq + k)),
            pl.BlockSpec((None, None, nr, LANES), lambda i, j, k: (i, j, 0, 0)),
            pl.BlockSpec((None, None, hd, nr), lambda i, j, k: (i, j, 0, 0)),
            pl.BlockSpec(ov.shape, lambda i, j, k: (0, 0)),
            k_spec, vT_spec, k_spec, vT_spec,
            pl.BlockSpec((None,) + cb.shape[1:], lambda i, j, k: (j, 0, 0)),
            pl.BlockSpec((None,) + tp.shape[1:], lambda i, j, k: (j, 0, 0, 0)),
            pl.BlockSpec((None, 16, T), lambda i, j, k: (j, 0, i * nq + k)),
        ],
        out_specs=pl.BlockSpec((R * hd, T), lambda i, j, k: (j, i * nq + k)),
        out_shape=jax.ShapeDtypeStruct((n_heads * hd, b * s), BF16),
        scratch_shapes=[pltpu.VMEM((2 * hd, R * T), BF16)] * 3
        + [pltpu.VMEM((LANES // 2, T), F32), pltpu.VMEM((s, LANES), BF16)]
        + value_scratch + pipe_scratch + pipe_scratch,
        compiler_params=_cparams(3), name="nsa_attention",
    )(qT, kc, vcT, ov, ksl, vslT, kw, vwT, cb, tp, glT)


def _merge_kernel(x_ref, ys_ref, ynT_ref, mg_ref, ws_ref, wn_ref, wo_ref, g_ref, o_ref):
    d = x_ref.shape[1]
    a = _dot(ys_ref[...], ws_ref[...])
    bb = lax.dot_general(ynT_ref[...], wn_ref[...], (((0,), (0,)), ((), ())), preferred_element_type=F32)
    mg = _sigmoid(mg_ref[...].astype(F32))
    mixed = mg[:, :d] * a + mg[:, d:] * bb
    o = _dot(mixed.astype(BF16), wo_ref[...])
    o_ref[...] = x_ref[...] + _rms(o, g_ref[...])


def _merge(x2d, y_ssd, y_nsaT, mg, w_s, w_n, w_o, g, tm=512):
    m, d = x2d.shape
    tm = min(tm, m)
    row = lambda a: pl.BlockSpec((tm, a.shape[1]), lambda i: (i, 0))
    full = lambda a: pl.BlockSpec(a.shape, lambda i: (0, 0))
    g = g.reshape(1, d)
    return pl.pallas_call(
        _merge_kernel, grid=(m // tm,),
        in_specs=[row(x2d), row(y_ssd), pl.BlockSpec((y_nsaT.shape[0], tm), lambda i: (0, i)), row(mg),
                  full(w_s), full(w_n), full(w_o), full(g)],
        out_specs=pl.BlockSpec((tm, d), lambda i: (i, 0)),
        out_shape=jax.ShapeDtypeStruct((m, d), F32),
        compiler_params=_cparams(1), name="merge",
    )(x2d, y_ssd, y_nsaT, mg, w_s, w_n, w_o, g)


def _xattn_kernel(x_ref, kv_ref, gpre_ref, wq_ref, wo_ref, gpost_ref, o_ref):
    x = x_ref[...]
    h = _rms(x, gpre_ref[...]).astype(BF16)
    q = _dot(h, wq_ref[...])
    width = X_HEADS * X_HEAD_DIM
    scale = X_HEAD_DIM ** -0.5
    outs = []
    for hh in range(X_HEADS):
        sl = slice(hh * X_HEAD_DIM, (hh + 1) * X_HEAD_DIM)
        lg = _dot_nt(q[:, sl].astype(BF16), kv_ref[:, sl]) * scale
        e = jnp.exp(lg - jnp.max(lg, axis=-1, keepdims=True))
        p = e / jnp.sum(e, axis=-1, keepdims=True)
        outs.append(_dot(p.astype(BF16), kv_ref[:, width + hh * X_HEAD_DIM:width + (hh + 1) * X_HEAD_DIM]))
    o = _dot(jnp.concatenate(outs, axis=-1).astype(BF16), wo_ref[...])
    o_ref[...] = x + _rms(o, gpost_ref[...])


def _xattn(x3d, kv, g_pre, w_q, w_o, g_post, tm=512):
    b, s, d = x3d.shape
    tm = min(tm, s)
    ml = kv.shape[1]
    full = lambda a: pl.BlockSpec(a.shape, lambda i, j: (0, 0))
    g_pre, g_post = g_pre.reshape(1, d), g_post.reshape(1, d)
    return pl.pallas_call(
        _xattn_kernel, grid=(b, s // tm),
        in_specs=[pl.BlockSpec((None, tm, d), lambda i, j: (i, j, 0)),
                  pl.BlockSpec((None, ml, kv.shape[2]), lambda i, j: (i, 0, 0)),
                  full(g_pre), full(w_q), full(w_o), full(g_post)],
        out_specs=pl.BlockSpec((None, tm, d), lambda i, j: (i, j, 0)),
        out_shape=jax.ShapeDtypeStruct((b, s, d), F32),
        compiler_params=_cparams(2), name="xattn",
    )(x3d, kv, g_pre, w_q, w_o, g_post)


def _mlp_kernel(x_ref, gpre_ref, w1_ref, w2_ref, gpost_ref, o_ref, *, f_chunk):
    x = x_ref[...]
    h = _rms(x, gpre_ref[...]).astype(BF16)
    d_ff = w1_ref.shape[1]
    acc = jnp.zeros(x.shape, F32)
    for f0 in range(0, d_ff, f_chunk):
        u = jnp.maximum(_dot(h, w1_ref[:, f0:f0 + f_chunk]), 0.0)
        acc = acc + _dot((u * u).astype(BF16), w2_ref[f0:f0 + f_chunk, :])
    o_ref[...] = x + _rms(acc, gpost_ref[...])


def _mlp(x2d, g_pre, w1, w2, g_post, tm=512):
    m, d = x2d.shape
    tm = min(tm, m)
    full = lambda a: pl.BlockSpec(a.shape, lambda i: (0, 0))
    g_pre, g_post = g_pre.reshape(1, d), g_post.reshape(1, d)
    return pl.pallas_call(
        functools.partial(_mlp_kernel, f_chunk=1024), grid=(m // tm,),
        in_specs=[pl.BlockSpec((tm, d), lambda i: (i, 0)), full(g_pre), full(w1), full(w2), full(g_post)],
        out_specs=pl.BlockSpec((tm, d), lambda i: (i, 0)),
        out_shape=jax.ShapeDtypeStruct((m, d), F32),
        compiler_params=_cparams(1), name="mlp",
    )(x2d, g_pre, w1, w2, g_post)


def _pad_heads(w, n_heads, hd):
    k = w.shape[0]
    w = w.reshape(k, n_heads, hd)
    return jnp.pad(w, ((0, 0), (0, 0), (0, LANES - hd))).reshape(k, n_heads * LANES)


def _layer(x, mem, w_in, conv_w, conv_b, dt_bias, a_log, d_skip, ssd_norm, cmp_pos, cmp_w1, cmp_w2,
           rel_bias, w_br_ssd, w_br_nsa, w_out, w_xq, w_xkv, w_xo, w_ff1, w_ff2,
           n_mix_pre, n_mix_post, n_x_pre, n_x_post, n_mem, n_ffn_pre, n_ffn_post):
    b, s, d = x.shape
    G, R, HD = NSA_KV_HEADS, NSA_REP, NSA_HEAD_DIM
    d_inner = 2 * d
    n_ssd_heads = d_inner // SSD_HEAD_DIM
    conv_dim = d_inner + 2 * SSD_GROUPS * SSD_STATE
    nsa_w, kv_w = NSA_HEADS * HD, G * HD
    sizes = (d_inner, conv_dim, n_ssd_heads, nsa_w, 6 * kv_w, 3 * NSA_HEADS, 2 * d)
    offs = np.concatenate([[0], np.cumsum(sizes)])
    seg = lambda i: w_in[:, offs[i]:offs[i + 1]]
    w_z, w_xbc, w_dt, w_q, w_kv, w_gate, w_mg = (seg(i) for i in range(7))
    w_small = jnp.pad(jnp.concatenate([w_dt, w_gate], axis=1),
                      ((0, 0), (0, LANES - n_ssd_heads - 3 * NSA_HEADS)))
    w_kv6 = w_kv.reshape(d, 6, kv_w)
    w_cmp = jnp.concatenate([w_kv6[:, 0], w_kv6[:, 1]], axis=1)
    bf = lambda a: a.astype(BF16)

    x2d = x.reshape(b * s, d)
    z, xbc = _norm_matmul(x2d, n_mix_pre, [bf(w_z), bf(w_xbc)], [BF16, BF16])
    w_gl = jnp.transpose(w_gate.reshape(d, 3, G, R), (2, 1, 3, 0)).reshape(G, 3 * R, d)
    w_gl = jnp.pad(w_gl, ((0, 0), (0, 16 - 3 * R), (0, 0))).reshape(G * 16, d)
    small, dtT, glT, qT, kv_cmp, vslT, vwT = _norm_matmul(
        x2d, n_mix_pre,
        [bf(w_small), bf(w_dt.T), bf(w_gl), bf(w_q.T * (HD ** -0.5 * LOG2E)), bf(w_cmp),
         bf(w_kv6[:, 3].T), bf(w_kv6[:, 5].T)],
        [F32, F32, F32, BF16, BF16, BF16, BF16],
        transposed=(False, True, True, True, False, True, True))
    ksl, kw, mg = _norm_matmul(
        x2d, n_mix_pre, [bf(_pad_heads(w_kv6[:, 2], G, HD)), bf(_pad_heads(w_kv6[:, 4], G, HD)), bf(w_mg)],
        [BF16] * 3)

    y_ssd = _ssd_mixer(z.reshape(b, s, -1), xbc.reshape(b, s, -1), small.reshape(b, s, -1), dtT,
                       conv_w, conv_b, dt_bias, a_log, d_skip, ssd_norm)

    nr = s // CMP_STRIDE
    kr = kv_cmp.reshape(b, nr, CMP_STRIDE, 2, G, HD)
    kr = jnp.transpose(kr, (3, 0, 4, 1, 2, 5)).reshape(2, b, G, nr, CMP_STRIDE * HD)
    pos = jnp.broadcast_to(bf(cmp_pos).reshape(2, 1, CMP_BLOCK * HD), (2, SUBLANES, CMP_BLOCK * HD))
    w2p = jnp.pad(bf(cmp_w2), ((0, 0), (0, 0), (0, LANES - HD)))
    kvc = _compress(kr, pos, bf(cmp_w1), w2p)
    vcT = jnp.swapaxes(kvc[1][..., :HD], -1, -2)
    cb, tp = _nsa_tables(rel_bias, nr)
    y_nsa = _nsa_attention(qT.reshape(NSA_HEADS, HD, b * s), kvc[0], vcT,
                           ksl.reshape(b, s, -1), vslT.reshape(G, HD, b * s),
                           kw.reshape(b, s, -1), vwT.reshape(G, HD, b * s), cb, tp,
                           glT.reshape(G, 16, b * s))

    x1 = _merge(x2d, y_ssd.reshape(b * s, -1), y_nsa, mg, bf(w_br_ssd), bf(w_br_nsa), bf(w_out), n_mix_post)

    (kv_mem,) = _norm_matmul(mem.reshape(-1, d), n_mem, [bf(w_xkv)], [BF16])
    x2 = _xattn(x1.reshape(b, s, d), kv_mem.reshape(b, mem.shape[1], -1), n_x_pre, bf(w_xq), bf(w_xo),
                n_x_post)

    x3 = _mlp(x2.reshape(b * s, d), n_ffn_pre, bf(w_ff1), bf(w_ff2), n_ffn_post)
    return x3.reshape(b, s, d)


def kernel(x, mem, w_in, ssd_conv_w, ssd_conv_b, ssd_dt_bias, ssd_a_log, ssd_d_skip, ssd_norm, cmp_pos,
           cmp_w1, cmp_w2, rel_bias, w_br_ssd, w_br_nsa, w_out, w_xq, w_xkv, w_xo, w_ff1, w_ff2,
           norm_mix_pre, norm_mix_post, norm_x_pre, norm_x_post, norm_mem, norm_ffn_pre, norm_ffn_post):
    for l in range(w_in.shape[0]):
        x = _layer(x, mem, w_in[l], ssd_conv_w[l], ssd_conv_b[l], ssd_dt_bias[l], ssd_a_log[l],
                   ssd_d_skip[l], ssd_norm[l], cmp_pos[l], cmp_w1[l], cmp_w2[l], rel_bias,
                   w_br_ssd[l], w_br_nsa[l], w_out[l], w_xq[l], w_xkv[l], w_xo[l], w_ff1[l], w_ff2[l],
                   norm_mix_pre[l], norm_mix_post[l], norm_x_pre[l], norm_x_post[l], norm_mem[l],
                   norm_ffn_pre[l], norm_ffn_post[l])
    return x
```

```python
import functools
import math

import numpy as np
import jax
import jax.numpy as jnp
from jax import lax
from jax.experimental import pallas as pl
from jax.experimental.pallas import tpu as pltpu

F32 = jnp.float32
BF16 = jnp.bfloat16

NORM_EPS = 1e-6
NEG_INF = -1e30
FORCE_SCORE = 1e9
SEL_PENALTY = NEG_INF

LANES = 128
SUBLANES = 8
VMEM_LIMIT = 56 * 1024 * 1024

SSD_HEAD_DIM = 64
SSD_GROUPS = 8
SSD_STATE = 128
SSD_CONV = 4
SSD_CHUNK = 128
NSA_HEADS = 16
NSA_KV_HEADS = 4
NSA_HEAD_DIM = 64
NSA_REP = NSA_HEADS // NSA_KV_HEADS
CMP_BLOCK = 32
CMP_STRIDE = 16
SLC_BLOCK = 64
SLC_TOPK = 16
WINDOW = 512
NSA_TQ = 256
LOG2E = math.log2(math.e)
REL_BUCKETS = 32
REL_MAX_DIST = 128
X_HEADS = 4
X_HEAD_DIM = 128

TILE_DIAG, TILE_NEAR, TILE_NONE, TILE_EDGE, TILE_ZERO = range(5)


def _cparams(n_grid):
    return pltpu.CompilerParams(dimension_semantics=("arbitrary",) * n_grid,
                                vmem_limit_bytes=VMEM_LIMIT)


def _sigmoid(x):
    return jax.nn.sigmoid(x)


def _rms(x, g):
    return x * lax.rsqrt(jnp.mean(x * x, axis=-1, keepdims=True) + NORM_EPS) * g


def _dot(a, b):
    return jnp.dot(a, b, preferred_element_type=F32)


def _dot_nt(a, b):
    return lax.dot_general(a, b, (((1,), (1,)), ((), ())), preferred_element_type=F32)


def _split3(x):
    x1 = x.astype(BF16)
    r1 = x - x1.astype(F32)
    x2 = r1.astype(BF16)
    x3 = (r1 - x2.astype(F32)).astype(BF16)
    return x1, x2, x3


def _norm_matmul_kernel(x_ref, g_ref, *refs, transposed, n_chunk):
    n_out = len(transposed)
    w_refs, o_refs = refs[:n_out], refs[n_out:]
    h = _rms(x_ref[...], g_ref[...]).astype(BF16)
    for w_ref, o_ref, tr in zip(w_refs, o_refs, transposed):
        if tr:
            o_ref[...] = _dot_nt(w_ref[...], h).astype(o_ref.dtype)
            continue
        n = w_ref.shape[1]
        step = min(n, n_chunk)
        for n0 in range(0, n, step):
            o_ref[:, n0:n0 + step] = _dot(h, w_ref[:, n0:n0 + step]).astype(o_ref.dtype)


def _norm_matmul(x2d, g, ws, out_dtypes, transposed=None, tm=512):
    m, k = x2d.shape
    tm = min(tm, m)
    transposed = tuple(transposed) if transposed is not None else (False,) * len(ws)
    in_specs = [pl.BlockSpec((tm, k), lambda i: (i, 0)), pl.BlockSpec((1, k), lambda i: (0, 0))]
    in_specs += [pl.BlockSpec(w.shape, lambda i: (0, 0)) for w in ws]
    out_specs, out_shape = [], []
    for w, dt, tr in zip(ws, out_dtypes, transposed):
        if tr:
            out_specs.append(pl.BlockSpec((w.shape[0], tm), lambda i: (0, i)))
            out_shape.append(jax.ShapeDtypeStruct((w.shape[0], m), dt))
        else:
            out_specs.append(pl.BlockSpec((tm, w.shape[1]), lambda i: (i, 0)))
            out_shape.append(jax.ShapeDtypeStruct((m, w.shape[1]), dt))
    return pl.pallas_call(
        functools.partial(_norm_matmul_kernel, transposed=transposed, n_chunk=1024),
        grid=(m // tm,), in_specs=in_specs, out_specs=out_specs, out_shape=out_shape,
        compiler_params=_cparams(1), name="norm_proj",
    )(x2d, g.reshape(1, k), *ws)


def _softplus(x):
    return jnp.maximum(x, 0.0) + jnp.log(1.0 + jnp.exp(-jnp.abs(x)))


def _ssd_kernel(xbc_ref, prev_ref, z_ref, dt_ref, dtT_ref, cw_ref, cb_ref, dtb_ref, dtbT_ref,
                alog_ref, alogT_ref, dskip_ref, nw_ref, y_ref, state_ref, xc_ref, ybuf_ref,
                *, n_heads, d_inner):
    L, P, N, G = SSD_CHUNK, SSD_HEAD_DIM, SSD_STATE, SSD_GROUPS
    conv_dim = xc_ref.shape[1]
    c = pl.program_id(1)

    @pl.when(c == 0)
    def _():
        state_ref[...] = jnp.zeros_like(state_ref)

    cw = 512
    prev_rows = prev_ref.shape[0]
    row = lax.broadcasted_iota(jnp.int32, (prev_rows, cw), 0)
    for j in range(conv_dim // cw):
        sl = slice(j * cw, (j + 1) * cw)
        cur = xbc_ref[:, sl].astype(F32)
        prev = jnp.where(c == 0, 0.0, prev_ref[:, sl].astype(F32))
        acc = cb_ref[:, sl] + cw_ref[SSD_CONV - 1:SSD_CONV, sl] * cur
        for k in range(1, SSD_CONV):
            rc = pltpu.roll(cur, k, 0)
            rp = pltpu.roll(prev, k, 0)
            head = jnp.where(row < k, rp, rc[:prev_rows])
            shifted = jnp.concatenate([head, rc[prev_rows:]], axis=0)
            acc = acc + cw_ref[SSD_CONV - 1 - k:SSD_CONV - k, sl] * shifted
        xc_ref[:, sl] = acc * _sigmoid(acc)

    dt = _softplus(dt_ref[:, 0:n_heads] + dtb_ref[...])
    a = dt * (-LOG2E * jnp.exp(alog_ref[...]))
    dtT = _softplus(dtT_ref[...] + dtbT_ref[...])
    aT = dtT * (-LOG2E * jnp.exp(alogT_ref[...]))
    ri = lax.broadcasted_iota(jnp.int32, (L, L), 0)
    ci = lax.broadcasted_iota(jnp.int32, (L, L), 1)
    causal = ci <= ri
    tri = jnp.where(causal, 1.0, 0.0).astype(BF16)
    triu = jnp.where(ri <= ci, 1.0, 0.0).astype(BF16)
    a_cs = sum(_dot(tri, ai) for ai in _split3(a))
    a_csT = sum(_dot(ai, triu) for ai in _split3(aT))
    lane = lax.broadcasted_iota(jnp.int32, (L, 2 * P), 1)
    lo_half = lane < P
    lane1 = lax.broadcasted_iota(jnp.int32, (1, 2 * P), 1) < P

    heads_per_group = n_heads // G
    for g in range(G):
        b_g = xc_ref[:, d_inner + g * N:d_inner + (g + 1) * N]
        c_g = xc_ref[:, d_inner + G * N + g * N:d_inner + G * N + (g + 1) * N]
        c_gb = c_g.astype(BF16)
        cb = _dot_nt(c_gb, b_g.astype(BF16))
        b_gT = b_g.T.astype(BF16)
        for pp in range(heads_per_group // 2):
            pair = g * (heads_per_group // 2) + pp
            h0, h1 = 2 * pair, 2 * pair + 1
            xs = xc_ref[:, pair * 2 * P:(pair + 1) * 2 * P]
            dt_pair = jnp.where(lo_half, dt[:, h0:h0 + 1], dt[:, h1:h1 + 1])
            xd = xs * dt_pair
            xd_b = xd.astype(BF16)
            ys = []
            for h in (h0, h1):
                seg = jnp.where(causal, a_cs[:, h:h + 1] - a_csT[h:h + 1, :], NEG_INF)
                mat = (cb * jnp.exp2(seg)).astype(BF16)
                ys.append(_dot(mat, xd_b))
            y = jnp.where(lo_half, ys[0], ys[1])
            cs_pair = jnp.where(lo_half, a_cs[:, h0:h0 + 1], a_cs[:, h1:h1 + 1])
            st = state_ref[pair]
            y = y + _dot(c_gb, st.astype(BF16)) * jnp.exp2(cs_pair)
            tot = jnp.where(lane1, a_csT[h0:h0 + 1, L - 1:L], a_csT[h1:h1 + 1, L - 1:L])
            xdd = (xd * jnp.exp2(tot - cs_pair)).astype(BF16)
            state_ref[pair] = st * jnp.exp2(tot) + _dot(b_gT, xdd)
            ybuf_ref[:, pair * 2 * P:(pair + 1) * 2 * P] = y + xs * dskip_ref[:, pair * 2 * P:(pair + 1) * 2 * P]

    gw = d_inner // G
    for g in range(G):
        sl = slice(g * gw, (g + 1) * gw)
        zz = z_ref[:, sl].astype(F32)
        yg = ybuf_ref[:, sl] * (zz * _sigmoid(zz))
        y_ref[:, sl] = _rms(yg, nw_ref[:, sl]).astype(y_ref.dtype)


def _ssd_mixer(z, xbc, small, dtT, conv_w, conv_b, dt_bias, a_log, d_skip, norm_w):
    b, s, d_inner = z.shape
    conv_dim = xbc.shape[-1]
    n_heads = d_inner // SSD_HEAD_DIM
    L = SSD_CHUNK
    nc = s // L
    prev_rows = 16
    xbc_prev = xbc.reshape(b, s // prev_rows, prev_rows, conv_dim)
    blocks_per_chunk = L // prev_rows
    kern = functools.partial(_ssd_kernel, n_heads=n_heads, d_inner=d_inner)
    const = lambda shape: pl.BlockSpec(shape, lambda i, j: (0,) * len(shape))
    return pl.pallas_call(
        kern, grid=(b, nc),
        in_specs=[
            pl.BlockSpec((None, L, conv_dim), lambda i, j: (i, j, 0)),
            pl.BlockSpec((None, None, prev_rows, conv_dim),
                         lambda i, j: (i, jnp.maximum(j * blocks_per_chunk - 1, 0), 0, 0)),
            pl.BlockSpec((None, L, d_inner), lambda i, j: (i, j, 0)),
            pl.BlockSpec((None, L, small.shape[-1]), lambda i, j: (i, j, 0)),
            pl.BlockSpec((n_heads, L), lambda i, j: (0, i * nc + j)),
            const((SSD_CONV, conv_dim)), const((1, conv_dim)),
            const((1, n_heads)), const((n_heads, 1)),
            const((1, n_heads)), const((n_heads, 1)),
            const((1, d_inner)), const((1, d_inner)),
        ],
        out_specs=pl.BlockSpec((None, L, d_inner), lambda i, j: (i, j, 0)),
        out_shape=jax.ShapeDtypeStruct((b, s, d_inner), BF16),
        scratch_shapes=[pltpu.VMEM((n_heads // 2, SSD_STATE, 2 * SSD_HEAD_DIM), F32),
                        pltpu.VMEM((L, conv_dim), F32),
                        pltpu.VMEM((L, d_inner), F32)],
        compiler_params=_cparams(2), name="ssd",
    )(xbc, xbc_prev, z, small, dtT, conv_w.astype(F32), conv_b.reshape(1, -1).astype(F32),
      dt_bias.reshape(1, -1), dt_bias.reshape(-1, 1), a_log.reshape(1, -1), a_log.reshape(-1, 1),
      jnp.repeat(d_skip, SSD_HEAD_DIM).reshape(1, -1), norm_w.reshape(1, -1))


def _compress_kernel(kr_ref, pos_ref, w1_ref, w2_ref, o_ref):
    nr, half = kr_ref.shape
    kr = kr_ref[...]
    p1 = _dot(kr, w1_ref[0:half, :])
    p2 = _dot(kr, w1_ref[half:2 * half, :])
    pb = _dot(pos_ref[...], w1_ref[...])[0:1]
    hid = p1 + pltpu.roll(p2, nr - 1, 0) + pb
    hid = hid * _sigmoid(hid)
    out = _dot(hid.astype(BF16), w2_ref[...])
    rows = lax.broadcasted_iota(jnp.int32, out.shape, 0)
    o_ref[...] = jnp.where(rows < nr - 1, out, 0.0).astype(o_ref.dtype)


def _compress(kr, pos, w1, w2p):
    two, b, g, nr, half = kr.shape
    hidden = w1.shape[-1]
    return pl.pallas_call(
        _compress_kernel,
        grid=(two, b, g),
        in_specs=[
            pl.BlockSpec((None, None, None, nr, half), lambda t, i, j: (t, i, j, 0, 0)),
            pl.BlockSpec((None, SUBLANES, 2 * half), lambda t, i, j: (t, 0, 0)),
            pl.BlockSpec((None, 2 * half, hidden), lambda t, i, j: (t, 0, 0)),
            pl.BlockSpec((None, hidden, LANES), lambda t, i, j: (t, 0, 0)),
        ],
        out_specs=pl.BlockSpec((None, None, None, nr, LANES), lambda t, i, j: (t, i, j, 0, 0)),
        out_shape=jax.ShapeDtypeStruct((two, b, g, nr, LANES), BF16),
        compiler_params=_cparams(3), name="nsa_compress",
    )(kr, pos, w1, w2p)


def _colmax(a):
    return jnp.max(a, axis=0, keepdims=True)


def _colsum(a):
    return jnp.sum(a, axis=0, keepdims=True)


class _SoftmaxPipe:
    def __init__(self, s_ref, pe_ref, acc_ref, hd):
        self.s, self.pe, self.acc, self.hd = s_ref, pe_ref, acc_ref, hd

    def start(self, first_scores, slot):
        cols = first_scores.shape[1]
        self.s[slot] = first_scores
        self.pe[1 - slot] = jnp.zeros(self.pe.shape[1:], self.pe.dtype)
        self.acc[...] = jnp.zeros(self.acc.shape, F32)
        return _colmax(first_scores), jnp.ones((1, cols), F32), jnp.full((1, cols), NEG_INF, F32)

    def step(self, slot, carry, v_prev, next_scores):
        cm_cur, alpha_prev, m_old = carry
        self.acc[...] = alpha_prev * self.acc[...] + _dot(v_prev, self.pe[1 - slot])
        cm_next = cm_cur
        if next_scores is not None:
            s_next = next_scores()
            self.s[1 - slot] = s_next
            cm_next = _colmax(s_next)
        m_new = jnp.maximum(m_old, cm_cur)
        self.pe[slot] = jnp.exp2(self.s[slot] - m_new).astype(self.pe.dtype)
        return cm_next, jnp.exp2(m_old - m_new), m_new

    def finish(self, slot_last, carry, v_last):
        _, alpha_prev, _ = carry
        acc = alpha_prev * self.acc[...] + _dot(v_last, self.pe[slot_last])
        return acc[0:self.hd] * (1.0 / acc[self.hd:self.hd + 1])

    def shifted_step(self, slot, v_prev, next_scores):
        self.pe[slot] = jnp.exp2(self.s[slot]).astype(self.pe.dtype)
        self.s[1 - slot] = next_scores()
        self.acc[...] += _dot(v_prev, self.pe[1 - slot])


def _nsa_kernel(qT_ref, kc_ref, vcT_ref, ov_ref, ksl_ref, vslT_ref, kw_ref, vwT_ref, cb_ref, tp_ref,
                gl_ref, o_ref, q0_ref, qa_ref, qs_ref, rank_ref, kst_ref, vsa_ref, vwa_ref, s_ref, pe_ref,
                acc_ref, ws_ref, wpe_ref, wacc_ref, *, n_slc, top_k):
    T, R, HD = NSA_TQ, NSA_REP, NSA_HEAD_DIM
    nr = kc_ref.shape[0]
    per_tile = T // CMP_STRIDE
    front = nr - per_tile
    half = LANES // 2
    qi = pl.program_id(2)
    colmax, colsum = _colmax, _colsum

    @pl.when(qi == 0)
    def _():
        for src, dst in ((vslT_ref, vsa_ref), (vwT_ref, vwa_ref)):
            dst[0:HD, :] = src[...]
            dst[HD:, :] = jnp.ones((dst.shape[0] - HD, dst.shape[1]), dst.dtype)
        krow = lax.broadcasted_iota(jnp.int32, (T, LANES), 0)
        klane = lax.broadcasted_iota(jnp.int32, (T, LANES), 1)
        for t0 in range(0, ksl_ref.shape[0], T):
            blk = t0 // SLC_BLOCK + krow // SLC_BLOCK
            tag = jnp.where(klane - half == blk, 1.0, 0.0).astype(BF16)
            kst_ref[t0:t0 + T, :] = jnp.where(klane >= half, tag, ksl_ref[t0:t0 + T, :])

    qT = jnp.concatenate([qT_ref[r] for r in range(R)], axis=1)
    q0_ref[0:HD, :] = qT
    q0_ref[HD:2 * HD, :] = jnp.zeros((HD, R * T), BF16)
    qa_ref[0:HD, :] = qT

    n_win = WINDOW // T

    def win_offset(rel):
        return pl.multiple_of(jnp.maximum(qi - rel, 0) * T, T)

    def win_scores(rel):
        tile = TILE_EDGE if rel == n_win else (TILE_ZERO if rel >= 2 else (TILE_NEAR if rel == 1 else TILE_DIAG))
        if rel > 0:
            tile = jnp.where(qi >= rel, tile, TILE_NONE)
        return _dot(kw_ref[pl.ds(win_offset(rel), T), :], q0_ref[...]) + tp_ref[tile]

    def win_values(rel):
        return vwa_ref[:, pl.ds(win_offset(rel), T)]

    win = _SoftmaxPipe(ws_ref, wpe_ref, wacc_ref, HD)
    rels = list(range(n_win, -1, -1))
    carry = win.start(win_scores(rels[0]), 0)
    for i, rel in enumerate(rels):
        nxt = functools.partial(win_scores, rels[i + 1]) if i + 1 < len(rels) else None
        carry = win.step(i % 2, carry, win_values(rels[max(i - 1, 0)]), nxt)
    o_win = win.finish((len(rels) - 1) % 2, carry, win_values(0))

    off = pl.multiple_of(front - qi * per_tile, SUBLANES)
    lg = _dot(kc_ref[...], q0_ref[...]) + cb_ref[pl.ds(off, nr), :]
    m_c = colmax(lg)
    e = jnp.exp2(lg - m_c)
    p = e * jnp.where(m_c > 0.5 * NEG_INF, 1.0 / colsum(e), 0.0)
    o_cmp = _dot(vcT_ref[...], p.astype(BF16))

    psum = p[:, 0:T]
    for r in range(1, R):
        psum = psum + p[:, r * T:(r + 1) * T]
    imp = sum(_dot(ov_ref[...], pi) for pi in _split3(psum))
    nio = lax.broadcasted_iota(jnp.int32, (half, T), 0)
    tio = lax.broadcasted_iota(jnp.int32, (half, T), 1)
    tb = (qi * T + tio) // SLC_BLOCK
    forced = (nio == 0) | (nio == tb) | (nio == tb - 1)
    imp = jnp.where(forced, FORCE_SCORE, imp)
    imp = jnp.where(nio > tb, NEG_INF, imp)
    n_grp = half // SUBLANES
    grp = [imp[SUBLANES * v:SUBLANES * (v + 1)] for v in range(n_grp)]
    nio8 = lax.broadcasted_iota(jnp.int32, (SUBLANES, T), 0)
    rank_ref[...] = jnp.zeros(rank_ref.shape, F32)
    last_block = (qi * T + T - 1) // SLC_BLOCK
    for mg in range(pl.cdiv(n_slc, SUBLANES)):

        @pl.when(mg * SUBLANES <= last_block)
        def _():
            hits = [jnp.zeros((SUBLANES, T), F32) for _ in range(n_grp)]
            for mm in range(mg * SUBLANES, min((mg + 1) * SUBLANES, n_slc)):
                rowv = imp[mm:mm + 1, :]
                for v in range(n_grp):
                    if SUBLANES * v > mm:
                        hit = jnp.where(rowv >= grp[v], 1.0, 0.0)
                    elif SUBLANES * v + SUBLANES - 1 < mm:
                        hit = jnp.where(rowv > grp[v], 1.0, 0.0)
                    else:
                        hit = jnp.where(nio8 > mm - SUBLANES * v, jnp.where(rowv >= grp[v], 1.0, 0.0),
                                        jnp.where(rowv > grp[v], 1.0, 0.0))
                    hits[v] = hits[v] + hit
            rank_ref[...] += jnp.concatenate(hits, axis=0)

    rank = rank_ref[...]
    pen = jnp.where(rank < top_k, jnp.where(nio <= tb, 0.0, SEL_PENALTY), SEL_PENALTY).astype(BF16)
    qa_ref[HD:2 * HD, :] = jnp.concatenate([pen] * R, axis=1)

    def sel_scores(kj):
        return _dot(kst_ref[pl.ds(pl.multiple_of(kj * T, T), T), :], qa_ref[...])

    def sel_values(kj):
        return vsa_ref[:, pl.ds(pl.multiple_of(kj * T, T), T)]

    n_far = jnp.maximum(qi - 1, 0)
    near = jnp.maximum(qi - 1, 0)
    near_table = jnp.where(qi >= 1, TILE_NEAR, TILE_NONE)
    odd = n_far % 2
    sel = _SoftmaxPipe(s_ref, pe_ref, acc_ref, HD)

    def exact_selected():
        def far_step(kj, slot, carry):
            return sel.step(slot, carry, sel_values(jnp.maximum(kj - 1, 0)), functools.partial(sel_scores, kj + 1))

        carry = sel.start(sel_scores(0), odd)
        carry = lax.cond(odd == 1, lambda c: far_step(0, 1, c), lambda c: c, carry)
        carry = lax.fori_loop(
            0, n_far // 2, lambda j, c: far_step(odd + 2 * j + 1, 1, far_step(odd + 2 * j, 0, c)), carry)
        s_near = s_ref[0] + tp_ref[near_table]
        s_ref[0] = s_near
        carry = (colmax(s_near),) + tuple(carry[1:])
        carry = sel.step(0, carry, sel_values(jnp.maximum(n_far - 1, 0)),
                         lambda: sel_scores(qi) + tp_ref[TILE_DIAG])
        carry = sel.step(1, carry, sel_values(n_far), None)
        return sel.finish(1, carry, sel_values(qi))

    s_d = sel_scores(qi) + tp_ref[TILE_DIAG]
    s_n = sel_scores(near) + tp_ref[near_table]
    shift = jnp.maximum(colmax(s_d), colmax(s_n)).astype(BF16).astype(F32)
    qs_ref[0:HD, :] = qT
    qs_ref[HD:2 * HD, :] = (jnp.concatenate([pen.astype(F32)] * R, axis=1) - shift).astype(BF16)

    def shifted_scores(kj):
        return _dot(kst_ref[pl.ds(pl.multiple_of(kj * T, T), T), :], qs_ref[...])

    def far_shifted(kj, slot):
        sel.shifted_step(slot, sel_values(jnp.maximum(kj - 1, 0)), functools.partial(shifted_scores, kj + 1))

    s_ref[odd] = shifted_scores(0)
    pe_ref[1 - odd] = jnp.zeros(pe_ref.shape[1:], BF16)
    acc_ref[...] = (_dot(sel_values(qi), jnp.exp2(s_d - shift).astype(BF16))
                    + _dot(sel_values(near), jnp.exp2(s_n - shift).astype(BF16)))

    @pl.when(odd == 1)
    def _():
        far_shifted(0, 1)

    def far_pair(j, carry):
        far_shifted(odd + 2 * j, 0)
        far_shifted(odd + 2 * j + 1, 1)
        return carry

    lax.fori_loop(0, n_far // 2, far_pair, 0)
    acc = acc_ref[...] + _dot(sel_values(jnp.maximum(n_far - 1, 0)), pe_ref[1])
    denom = acc[HD:HD + 1]
    unsafe = (jnp.sum(jnp.where(denom < 2.0 ** 100, 0.0, 1.0))
              + jnp.sum(jnp.where(jnp.abs(acc[0:HD]) < 3.0e38, 0.0, 1.0))) > 0.0
    o_slc = lax.cond(unsafe, exact_selected, lambda: acc[0:HD] * (1.0 / denom))

    sg = _sigmoid(gl_ref[...])
    for r in range(R):
        cols = slice(r * T, (r + 1) * T)
        o = (sg[r:r + 1] * o_cmp[:, cols] + sg[R + r:R + r + 1] * o_slc[:, cols]
             + sg[2 * R + r:2 * R + r + 1] * o_win[:, cols])
        o_ref[r * HD:(r + 1) * HD, :] = o.astype(o_ref.dtype)


def _t5_bucket_np(dist):
    n = np.maximum(dist, 0)
    max_exact = REL_BUCKETS // 2
    nf = np.maximum(n, 1).astype(np.float32)
    large = max_exact + (np.log(nf / np.float32(max_exact)) / np.float32(math.log(REL_MAX_DIST / max_exact))
                         * np.float32(REL_BUCKETS - max_exact)).astype(np.int32)
    large = np.minimum(large, REL_BUCKETS - 1)
    return np.where(n < max_exact, n, large).astype(np.int32)


def _nsa_tables(rel_bias, nr):
    T, G, R = NSA_TQ, NSA_KV_HEADS, NSA_REP
    table = rel_bias.astype(F32) * LOG2E
    per_tile = T // CMP_STRIDE
    front = nr - per_tile
    t = np.arange(T)[None, :]

    def lay(a):
        keys = a.shape[0]
        return jnp.transpose(a, (2, 0, 1)).reshape(G, R, keys, T).transpose(0, 2, 1, 3).reshape(G, keys, R * T)

    cend = CMP_STRIDE * (np.arange(front + nr)[:, None] - front) + CMP_BLOCK - 1
    dist = t - cend
    def lookup(d):
        onehot = jax.nn.one_hot(_t5_bucket_np(d), REL_BUCKETS, dtype=F32)
        return jnp.einsum("ktb,bh->kth", onehot, table, precision=lax.Precision.HIGHEST)

    cb = jnp.where((dist >= 0)[..., None], lookup(dist), NEG_INF)
    k = np.arange(T)[:, None]
    assert 2 * T - (T - 1) >= REL_MAX_DIST and WINDOW % T == 0 and WINDOW // T >= 2
    last = table[REL_BUCKETS - 1]
    diag = jnp.where((t - k >= 0)[..., None], lookup(t - k) - last, NEG_INF)
    near = lookup(T + t - k) - last
    none = jnp.full_like(near, NEG_INF)
    edge = jnp.where((k > t)[..., None], jnp.zeros_like(near), NEG_INF)
    tiles = [None] * 5
    tiles[TILE_DIAG], tiles[TILE_NEAR], tiles[TILE_NONE], tiles[TILE_EDGE] = diag, near, none, edge
    tiles[TILE_ZERO] = jnp.zeros_like(near)
    tp = jnp.stack([lay(a) for a in tiles], axis=1)
    return lay(cb), tp


def _overlap_matrix(nr, n_slc):
    half = LANES // 2
    n = np.arange(half)[:, None]
    c = np.arange(nr)[None, :]
    ov = ((CMP_STRIDE * c <= SLC_BLOCK * n + SLC_BLOCK - 1) & (CMP_STRIDE * c + CMP_BLOCK - 1 >= SLC_BLOCK * n)
          & (n < n_slc) & (c < nr - 1))
    return jnp.asarray(ov.astype(np.float32), dtype=BF16)


def _nsa_attention(qT, kc, vcT, ksl, vslT, kw, vwT, cb, tp, glT):
    n_heads, hd, _ = qT.shape
    b, s, _ = ksl.shape
    G, R, T = NSA_KV_HEADS, NSA_REP, NSA_TQ
    nr = kc.shape[-2]
    nq = s // T
    n_slc = s // SLC_BLOCK
    assert n_slc <= LANES // 2 and s % T == 0 and nr % SUBLANES == 0
    ov = _overlap_matrix(nr, n_slc)
    kern = functools.partial(_nsa_kernel, n_slc=n_slc, top_k=min(SLC_TOPK, n_slc))
    k_spec = pl.BlockSpec((None, s, LANES), lambda i, j, k: (i, 0, j))
    vT_spec = pl.BlockSpec((None, hd, s), lambda i, j, k: (j, 0, i))
    ones_rows = 16
    pipe_scratch = [pltpu.VMEM((2, T, R * T), F32), pltpu.VMEM((2, T, R * T), BF16),
                    pltpu.VMEM((hd + ones_rows, R * T), F32)]
    value_scratch = [pltpu.VMEM((hd + ones_rows, s), BF16)] * 2
    return pl.pallas_call(
        kern, grid=(b, G, nq),
        in_specs=[
            pl.BlockSpec((R, hd, T), lambda i, j, k: (j, 0, i * nq + k)),
            pl.BlockSpec((None, None, nr, LANES), lambda i, j, k: (i, j, 0, 0)),
            pl.BlockSpec((None, None, hd, nr), lambda i, j, k: (i, j, 0, 0)),
            pl.BlockSpec(ov.shape, lambda i, j, k: (0, 0)),
            k_spec, vT_spec, k_spec, vT_spec,
            pl.BlockSpec((None,) + cb.shape[1:], lambda i, j, k: (j, 0, 0)),
            pl.BlockSpec((None,) + tp.shape[1:], lambda i, j, k: (j, 0, 0, 0)),
            pl.BlockSpec((None, 16, T), lambda i, j, k: (j, 0, i * nq + k)),
        ],
        out_specs=pl.BlockSpec((R * hd, T), lambda i, j, k: (j, i * nq + k)),
        out_shape=jax.ShapeDtypeStruct((n_heads * hd, b * s), BF16),
        scratch_shapes=[pltpu.VMEM((2 * hd, R * T), BF16)] * 3
        + [pltpu.VMEM((LANES // 2, T), F32), pltpu.VMEM((s, LANES), BF16)]
        + value_scratch + pipe_scratch + pipe_scratch,
        compiler_params=_cparams(3), name="nsa_attention",
    )(qT, kc, vcT, ov, ksl, vslT, kw, vwT, cb, tp, glT)


def _merge_kernel(x_ref, ys_ref, ynT_ref, mg_ref, ws_ref, wn_ref, wo_ref, g_ref, o_ref):
    d = x_ref.shape[1]
    a = _dot(ys_ref[...], ws_ref[...])
    bb = lax.dot_general(ynT_ref[...], wn_ref[...], (((0,), (0,)), ((), ())), preferred_element_type=F32)
    mg = _sigmoid(mg_ref[...].astype(F32))
    mixed = mg[:, :d] * a + mg[:, d:] * bb
    o = _dot(mixed.astype(BF16), wo_ref[...])
    o_ref[...] = x_ref[...] + _rms(o, g_ref[...])


def _merge(x2d, y_ssd, y_nsaT, mg, w_s, w_n, w_o, g, tm=512):
    m, d = x2d.shape
    tm = min(tm, m)
    row = lambda a: pl.BlockSpec((tm, a.shape[1]), lambda i: (i, 0))
    full = lambda a: pl.BlockSpec(a.shape, lambda i: (0, 0))
    g = g.reshape(1, d)
    return pl.pallas_call(
        _merge_kernel, grid=(m // tm,),
        in_specs=[row(x2d), row(y_ssd), pl.BlockSpec((y_nsaT.shape[0], tm), lambda i: (0, i)), row(mg),
                  full(w_s), full(w_n), full(w_o), full(g)],
        out_specs=pl.BlockSpec((tm, d), lambda i: (i, 0)),
        out_shape=jax.ShapeDtypeStruct((m, d), F32),
        compiler_params=_cparams(1), name="merge",
    )(x2d, y_ssd, y_nsaT, mg, w_s, w_n, w_o, g)


def _xattn_kernel(x_ref, kv_ref, gpre_ref, wq_ref, wo_ref, gpost_ref, o_ref):
    x = x_ref[...]
    h = _rms(x, gpre_ref[...]).astype(BF16)
    q = _dot(h, wq_ref[...])
    width = X_HEADS * X_HEAD_DIM
    scale = X_HEAD_DIM ** -0.5
    outs = []
    for hh in range(X_HEADS):
        sl = slice(hh * X_HEAD_DIM, (hh + 1) * X_HEAD_DIM)
        lg = _dot_nt(q[:, sl].astype(BF16), kv_ref[:, sl]) * scale
        e = jnp.exp(lg - jnp.max(lg, axis=-1, keepdims=True))
        p = e / jnp.sum(e, axis=-1, keepdims=True)
        outs.append(_dot(p.astype(BF16), kv_ref[:, width + hh * X_HEAD_DIM:width + (hh + 1) * X_HEAD_DIM]))
    o = _dot(jnp.concatenate(outs, axis=-1).astype(BF16), wo_ref[...])
    o_ref[...] = x + _rms(o, gpost_ref[...])


def _xattn(x3d, kv, g_pre, w_q, w_o, g_post, tm=512):
    b, s, d = x3d.shape
    tm = min(tm, s)
    ml = kv.shape[1]
    full = lambda a: pl.BlockSpec(a.shape, lambda i, j: (0, 0))
    g_pre, g_post = g_pre.reshape(1, d), g_post.reshape(1, d)
    return pl.pallas_call(
        _xattn_kernel, grid=(b, s // tm),
        in_specs=[pl.BlockSpec((None, tm, d), lambda i, j: (i, j, 0)),
                  pl.BlockSpec((None, ml, kv.shape[2]), lambda i, j: (i, 0, 0)),
                  full(g_pre), full(w_q), full(w_o), full(g_post)],
        out_specs=pl.BlockSpec((None, tm, d), lambda i, j: (i, j, 0)),
        out_shape=jax.ShapeDtypeStruct((b, s, d), F32),
        compiler_params=_cparams(2), name="xattn",
    )(x3d, kv, g_pre, w_q, w_o, g_post)


def _mlp_kernel(x_ref, gpre_ref, w1_ref, w2_ref, gpost_ref, o_ref, *, f_chunk):
    x = x_ref[...]
    h = _rms(x, gpre_ref[...]).astype(BF16)
    d_ff = w1_ref.shape[1]
    acc = jnp.zeros(x.shape, F32)
    for f0 in range(0, d_ff, f_chunk):
        u = jnp.maximum(_dot(h, w1_ref[:, f0:f0 + f_chunk]), 0.0)
        acc = acc + _dot((u * u).astype(BF16), w2_ref[f0:f0 + f_chunk, :])
    o_ref[...] = x + _rms(acc, gpost_ref[...])


def _mlp(x2d, g_pre, w1, w2, g_post, tm=512):
    m, d = x2d.shape
    tm = min(tm, m)
    full = lambda a: pl.BlockSpec(a.shape, lambda i: (0, 0))
    g_pre, g_post = g_pre.reshape(1, d), g_post.reshape(1, d)
    return pl.pallas_call(
        functools.partial(_mlp_kernel, f_chunk=1024), grid=(m // tm,),
        in_specs=[pl.BlockSpec((tm, d), lambda i: (i, 0)), full(g_pre), full(w1), full(w2), full(g_post)],
        out_specs=pl.BlockSpec((tm, d), lambda i: (i, 0)),
        out_shape=jax.ShapeDtypeStruct((m, d), F32),
        compiler_params=_cparams(1), name="mlp",
    )(x2d, g_pre, w1, w2, g_post)


def _pad_heads(w, n_heads, hd):
    k = w.shape[0]
    w = w.reshape(k, n_heads, hd)
    return jnp.pad(w, ((0, 0), (0, 0), (0, LANES - hd))).reshape(k, n_heads * LANES)


def _layer(x, mem, w_in, conv_w, conv_b, dt_bias, a_log, d_skip, ssd_norm, cmp_pos, cmp_w1, cmp_w2,
           rel_bias, w_br_ssd, w_br_nsa, w_out, w_xq, w_xkv, w_xo, w_ff1, w_ff2,
           n_mix_pre, n_mix_post, n_x_pre, n_x_post, n_mem, n_ffn_pre, n_ffn_post):
    b, s, d = x.shape
    G, R, HD = NSA_KV_HEADS, NSA_REP, NSA_HEAD_DIM
    d_inner = 2 * d
    n_ssd_heads = d_inner // SSD_HEAD_DIM
    conv_dim = d_inner + 2 * SSD_GROUPS * SSD_STATE
    nsa_w, kv_w = NSA_HEADS * HD, G * HD
    sizes = (d_inner, conv_dim, n_ssd_heads, nsa_w, 6 * kv_w, 3 * NSA_HEADS, 2 * d)
    offs = np.concatenate([[0], np.cumsum(sizes)])
    seg = lambda i: w_in[:, offs[i]:offs[i + 1]]
    w_z, w_xbc, w_dt, w_q, w_kv, w_gate, w_mg = (seg(i) for i in range(7))
    w_small = jnp.pad(jnp.concatenate([w_dt, w_gate], axis=1),
                      ((0, 0), (0, LANES - n_ssd_heads - 3 * NSA_HEADS)))
    w_kv6 = w_kv.reshape(d, 6, kv_w)
    w_cmp = jnp.concatenate([w_kv6[:, 0], w_kv6[:, 1]], axis=1)
    bf = lambda a: a.astype(BF16)

    x2d = x.reshape(b * s, d)
    z, xbc = _norm_matmul(x2d, n_mix_pre, [bf(w_z), bf(w_xbc)], [BF16, BF16])
    w_gl = jnp.transpose(w_gate.reshape(d, 3, G, R), (2, 1, 3, 0)).reshape(G, 3 * R, d)
    w_gl = jnp.pad(w_gl, ((0, 0), (0, 16 - 3 * R), (0, 0))).reshape(G * 16, d)
    small, dtT, glT, qT, kv_cmp, vslT, vwT = _norm_matmul(
        x2d, n_mix_pre,
        [bf(w_small), bf(w_dt.T), bf(w_gl), bf(w_q.T * (HD ** -0.5 * LOG2E)), bf(w_cmp),
         bf(w_kv6[:, 3].T), bf(w_kv6[:, 5].T)],
        [F32, F32, F32, BF16, BF16, BF16, BF16],
        transposed=(False, True, True, True, False, True, True))
    ksl, kw, mg = _norm_matmul(
        x2d, n_mix_pre, [bf(_pad_heads(w_kv6[:, 2], G, HD)), bf(_pad_heads(w_kv6[:, 4], G, HD)), bf(w_mg)],
        [BF16] * 3)

    y_ssd = _ssd_mixer(z.reshape(b, s, -1), xbc.reshape(b, s, -1), small.reshape(b, s, -1), dtT,
                       conv_w, conv_b, dt_bias, a_log, d_skip, ssd_norm)

    nr = s // CMP_STRIDE
    kr = kv_cmp.reshape(b, nr, CMP_STRIDE, 2, G, HD)
    kr = jnp.transpose(kr, (3, 0, 4, 1, 2, 5)).reshape(2, b, G, nr, CMP_STRIDE * HD)
    pos = jnp.broadcast_to(bf(cmp_pos).reshape(2, 1, CMP_BLOCK * HD), (2, SUBLANES, CMP_BLOCK * HD))
    w2p = jnp.pad(bf(cmp_w2), ((0, 0), (0, 0), (0, LANES - HD)))
    kvc = _compress(kr, pos, bf(cmp_w1), w2p)
    vcT = jnp.swapaxes(kvc[1][..., :HD], -1, -2)
    cb, tp = _nsa_tables(rel_bias, nr)
    y_nsa = _nsa_attention(qT.reshape(NSA_HEADS, HD, b * s), kvc[0], vcT,
                           ksl.reshape(b, s, -1), vslT.reshape(G, HD, b * s),
                           kw.reshape(b, s, -1), vwT.reshape(G, HD, b * s), cb, tp,
                           glT.reshape(G, 16, b * s))

    x1 = _merge(x2d, y_ssd.reshape(b * s, -1), y_nsa, mg, bf(w_br_ssd), bf(w_br_nsa), bf(w_out), n_mix_post)

    (kv_mem,) = _norm_matmul(mem.reshape(-1, d), n_mem, [bf(w_xkv)], [BF16])
    x2 = _xattn(x1.reshape(b, s, d), kv_mem.reshape(b, mem.shape[1], -1), n_x_pre, bf(w_xq), bf(w_xo),
                n_x_post)

    x3 = _mlp(x2.reshape(b * s, d), n_ffn_pre, bf(w_ff1), bf(w_ff2), n_ffn_post)
    return x3.reshape(b, s, d)


def kernel(x, mem, w_in, ssd_conv_w, ssd_conv_b, ssd_dt_bias, ssd_a_log, ssd_d_skip, ssd_norm, cmp_pos,
           cmp_w1, cmp_w2, rel_bias, w_br_ssd, w_br_nsa, w_out, w_xq, w_xkv, w_xo, w_ff1, w_ff2,
           norm_mix_pre, norm_mix_post, norm_x_pre, norm_x_post, norm_mem, norm_ffn_pre, norm_ffn_post):
    for l in range(w_in.shape[0]):
        x = _layer(x, mem, w_in[l], ssd_conv_w[l], ssd_conv_b[l], ssd_dt_bias[l], ssd_a_log[l],
                   ssd_d_skip[l], ssd_norm[l], cmp_pos[l], cmp_w1[l], cmp_w2[l], rel_bias,
                   w_br_ssd[l], w_br_nsa[l], w_out[l], w_xq[l], w_xkv[l], w_xo[l], w_ff1[l], w_ff2[l],
                   norm_mix_pre[l], norm_mix_post[l], norm_x_pre[l], norm_x_post[l], norm_mem[l],
                   norm_ffn_pre[l], norm_ffn_post[l])
    return x
```

```python
import functools
import math

import numpy as np
import jax
import jax.numpy as jnp
from jax import lax
from jax.experimental import pallas as pl
from jax.experimental.pallas import tpu as pltpu

F32 = jnp.float32
BF16 = jnp.bfloat16

NORM_EPS = 1e-6
NEG_INF = -1e30
FORCE_SCORE = 1e9
SEL_PENALTY = NEG_INF

LANES = 128
SUBLANES = 8
VMEM_LIMIT = 56 * 1024 * 1024

SSD_HEAD_DIM = 64
SSD_GROUPS = 8
SSD_STATE = 128
SSD_CONV = 4
SSD_CHUNK = 128
NSA_HEADS = 16
NSA_KV_HEADS = 4
NSA_HEAD_DIM = 64
NSA_REP = NSA_HEADS // NSA_KV_HEADS
CMP_BLOCK = 32
CMP_STRIDE = 16
SLC_BLOCK = 64
SLC_TOPK = 16
WINDOW = 512
NSA_TQ = 256
LOG2E = math.log2(math.e)
REL_BUCKETS = 32
REL_MAX_DIST = 128
X_HEADS = 4
X_HEAD_DIM = 128

TILE_DIAG, TILE_NEAR, TILE_NONE, TILE_EDGE, TILE_ZERO = range(5)


def _cparams(n_grid):
    return pltpu.CompilerParams(dimension_semantics=("arbitrary",) * n_grid,
                                vmem_limit_bytes=VMEM_LIMIT)


def _sigmoid(x):
    return jax.nn.sigmoid(x)


def _rms(x, g):
    return x * lax.rsqrt(jnp.mean(x * x, axis=-1, keepdims=True) + NORM_EPS) * g


def _dot(a, b):
    return jnp.dot(a, b, preferred_element_type=F32)


def _dot_nt(a, b):
    return lax.dot_general(a, b, (((1,), (1,)), ((), ())), preferred_element_type=F32)


def _split3(x):
    x1 = x.astype(BF16)
    r1 = x - x1.astype(F32)
    x2 = r1.astype(BF16)
    x3 = (r1 - x2.astype(F32)).astype(BF16)
    return x1, x2, x3


def _norm_matmul_kernel(x_ref, g_ref, *refs, transposed, n_chunk):
    n_out = len(transposed)
    w_refs, o_refs = refs[:n_out], refs[n_out:]
    h = _rms(x_ref[...], g_ref[...]).astype(BF16)
    for w_ref, o_ref, tr in zip(w_refs, o_refs, transposed):
        if tr:
            o_ref[...] = _dot_nt(w_ref[...], h).astype(o_ref.dtype)
            continue
        n = w_ref.shape[1]
        step = min(n, n_chunk)
        for n0 in range(0, n, step):
            o_ref[:, n0:n0 + step] = _dot(h, w_ref[:, n0:n0 + step]).astype(o_ref.dtype)


def _norm_matmul(x2d, g, ws, out_dtypes, transposed=None, tm=512):
    m, k = x2d.shape
    tm = min(tm, m)
    transposed = tuple(transposed) if transposed is not None else (False,) * len(ws)
    in_specs = [pl.BlockSpec((tm, k), lambda i: (i, 0)), pl.BlockSpec((1, k), lambda i: (0, 0))]
    in_specs += [pl.BlockSpec(w.shape, lambda i: (0, 0)) for w in ws]
    out_specs, out_shape = [], []
    for w, dt, tr in zip(ws, out_dtypes, transposed):
        if tr:
            out_specs.append(pl.BlockSpec((w.shape[0], tm), lambda i: (0, i)))
            out_shape.append(jax.ShapeDtypeStruct((w.shape[0], m), dt))
        else:
            out_specs.append(pl.BlockSpec((tm, w.shape[1]), lambda i: (i, 0)))
            out_shape.append(jax.ShapeDtypeStruct((m, w.shape[1]), dt))
    return pl.pallas_call(
        functools.partial(_norm_matmul_kernel, transposed=transposed, n_chunk=1024),
        grid=(m // tm,), in_specs=in_specs, out_specs=out_specs, out_shape=out_shape,
        compiler_params=_cparams(1), name="norm_proj",
    )(x2d, g.reshape(1, k), *ws)


def _softplus(x):
    return jnp.maximum(x, 0.0) + jnp.log(1.0 + jnp.exp(-jnp.abs(x)))


def _ssd_kernel(xbc_ref, prev_ref, z_ref, dt_ref, dtT_ref, cw_ref, cb_ref, dtb_ref, dtbT_ref,
                alog_ref, alogT_ref, dskip_ref, nw_ref, y_ref, state_ref, xc_ref, ybuf_ref,
                *, n_heads, d_inner):
    L, P, N, G = SSD_CHUNK, SSD_HEAD_DIM, SSD_STATE, SSD_GROUPS
    conv_dim = xc_ref.shape[1]
    c = pl.program_id(1)

    @pl.when(c == 0)
    def _():
        state_ref[...] = jnp.zeros_like(state_ref)

    cw = 512
    prev_rows = prev_ref.shape[0]
    row = lax.broadcasted_iota(jnp.int32, (prev_rows, cw), 0)
    for j in range(conv_dim // cw):
        sl = slice(j * cw, (j + 1) * cw)
        cur = xbc_ref[:, sl].astype(F32)
        prev = jnp.where(c == 0, 0.0, prev_ref[:, sl].astype(F32))
        acc = cb_ref[:, sl] + cw_ref[SSD_CONV - 1:SSD_CONV, sl] * cur
        for k in range(1, SSD_CONV):
            rc = pltpu.roll(cur, k, 0)
            rp = pltpu.roll(prev, k, 0)
            head = jnp.where(row < k, rp, rc[:prev_rows])
            shifted = jnp.concatenate([head, rc[prev_rows:]], axis=0)
            acc = acc + cw_ref[SSD_CONV - 1 - k:SSD_CONV - k, sl] * shifted
        xc_ref[:, sl] = acc * _sigmoid(acc)

    dt = _softplus(dt_ref[:, 0:n_heads] + dtb_ref[...])
    a = dt * (-LOG2E * jnp.exp(alog_ref[...]))
    dtT = _softplus(dtT_ref[...] + dtbT_ref[...])
    aT = dtT * (-LOG2E * jnp.exp(alogT_ref[...]))
    ri = lax.broadcasted_iota(jnp.int32, (L, L), 0)
    ci = lax.broadcasted_iota(jnp.int32, (L, L), 1)
    causal = ci <= ri
    tri = jnp.where(causal, 1.0, 0.0).astype(BF16)
    triu = jnp.where(ri <= ci, 1.0, 0.0).astype(BF16)
    a_cs = sum(_dot(tri, ai) for ai in _split3(a))
    a_csT = sum(_dot(ai, triu) for ai in _split3(aT))
    lane = lax.broadcasted_iota(jnp.int32, (L, 2 * P), 1)
    lo_half = lane < P
    lane1 = lax.broadcasted_iota(jnp.int32, (1, 2 * P), 1) < P

    heads_per_group = n_heads // G
    for g in range(G):
        b_g = xc_ref[:, d_inner + g * N:d_inner + (g + 1) * N]
        c_g = xc_ref[:, d_inner + G * N + g * N:d_inner + G * N + (g + 1) * N]
        c_gb = c_g.astype(BF16)
        cb = _dot_nt(c_gb, b_g.astype(BF16))
        b_gT = b_g.T.astype(BF16)
        for pp in range(heads_per_group // 2):
            pair = g * (heads_per_group // 2) + pp
            h0, h1 = 2 * pair, 2 * pair + 1
            xs = xc_ref[:, pair * 2 * P:(pair + 1) * 2 * P]
            dt_pair = jnp.where(lo_half, dt[:, h0:h0 + 1], dt[:, h1:h1 + 1])
            xd = xs * dt_pair
            xd_b = xd.astype(BF16)
            ys = []
            for h in (h0, h1):
                seg = jnp.where(causal, a_cs[:, h:h + 1] - a_csT[h:h + 1, :], NEG_INF)
                mat = (cb * jnp.exp2(seg)).astype(BF16)
                ys.append(_dot(mat, xd_b))
            y = jnp.where(lo_half, ys[0], ys[1])
            cs_pair = jnp.where(lo_half, a_cs[:, h0:h0 + 1], a_cs[:, h1:h1 + 1])
            st = state_ref[pair]
            y = y + _dot(c_gb, st.astype(BF16)) * jnp.exp2(cs_pair)
            tot = jnp.where(lane1, a_csT[h0:h0 + 1, L - 1:L], a_csT[h1:h1 + 1, L - 1:L])
            xdd = (xd * jnp.exp2(tot - cs_pair)).astype(BF16)
            state_ref[pair] = st * jnp.exp2(tot) + _dot(b_gT, xdd)
            ybuf_ref[:, pair * 2 * P:(pair + 1) * 2 * P] = y + xs * dskip_ref[:, pair * 2 * P:(pair + 1) * 2 * P]

    gw = d_inner // G
    for g in range(G):
        sl = slice(g * gw, (g + 1) * gw)
        zz = z_ref[:, sl].astype(F32)
        yg = ybuf_ref[:, sl] * (zz * _sigmoid(zz))
        y_ref[:, sl] = _rms(yg, nw_ref[:, sl]).astype(y_ref.dtype)


def _ssd_mixer(z, xbc, small, dtT, conv_w, conv_b, dt_bias, a_log, d_skip, norm_w):
    b, s, d_inner = z.shape
    conv_dim = xbc.shape[-1]
    n_heads = d_inner // SSD_HEAD_DIM
    L = SSD_CHUNK
    nc = s // L
    prev_rows = 16
    xbc_prev = xbc.reshape(b, s // prev_rows, prev_rows, conv_dim)
    blocks_per_chunk = L // prev_rows
    kern = functools.partial(_ssd_kernel, n_heads=n_heads, d_inner=d_inner)
    const = lambda shape: pl.BlockSpec(shape, lambda i, j: (0,) * len(shape))
    return pl.pallas_call(
        kern, grid=(b, nc),
        in_specs=[
            pl.BlockSpec((None, L, conv_dim), lambda i, j: (i, j, 0)),
            pl.BlockSpec((None, None, prev_rows, conv_dim),
                         lambda i, j: (i, jnp.maximum(j * blocks_per_chunk - 1, 0), 0, 0)),
            pl.BlockSpec((None, L, d_inner), lambda i, j: (i, j, 0)),
            pl.BlockSpec((None, L, small.shape[-1]), lambda i, j: (i, j, 0)),
            pl.BlockSpec((n_heads, L), lambda i, j: (0, i * nc + j)),
            const((SSD_CONV, conv_dim)), const((1, conv_dim)),
            const((1, n_heads)), const((n_heads, 1)),
            const((1, n_heads)), const((n_heads, 1)),
            const((1, d_inner)), const((1, d_inner)),
        ],
        out_specs=pl.BlockSpec((None, L, d_inner), lambda i, j: (i, j, 0)),
        out_shape=jax.ShapeDtypeStruct((b, s, d_inner), BF16),
        scratch_shapes=[pltpu.VMEM((n_heads // 2, SSD_STATE, 2 * SSD_HEAD_DIM), F32),
                        pltpu.VMEM((L, conv_dim), F32),
                        pltpu.VMEM((L, d_inner), F32)],
        compiler_params=_cparams(2), name="ssd",
    )(xbc, xbc_prev, z, small, dtT, conv_w.astype(F32), conv_b.reshape(1, -1).astype(F32),
      dt_bias.reshape(1, -1), dt_bias.reshape(-1, 1), a_log.reshape(1, -1), a_log.reshape(-1, 1),
      jnp.repeat(d_skip, SSD_HEAD_DIM).reshape(1, -1), norm_w.reshape(1, -1))


def _compress_kernel(kr_ref, pos_ref, w1_ref, w2_ref, o_ref):
    nr, half = kr_ref.shape
    kr = kr_ref[...]
    p1 = _dot(kr, w1_ref[0:half, :])
    p2 = _dot(kr, w1_ref[half:2 * half, :])
    pb = _dot(pos_ref[...], w1_ref[...])[0:1]
    hid = p1 + pltpu.roll(p2, nr - 1, 0) + pb
    hid = hid * _sigmoid(hid)
    out = _dot(hid.astype(BF16), w2_ref[...])
    rows = lax.broadcasted_iota(jnp.int32, out.shape, 0)
    o_ref[...] = jnp.where(rows < nr - 1, out, 0.0).astype(o_ref.dtype)


def _compress(kr, pos, w1, w2p):
    two, b, g, nr, half = kr.shape
    hidden = w1.shape[-1]
    return pl.pallas_call(
        _compress_kernel,
        grid=(two, b, g),
        in_specs=[
            pl.BlockSpec((None, None, None, nr, half), lambda t, i, j: (t, i, j, 0, 0)),
            pl.BlockSpec((None, SUBLANES, 2 * half), lambda t, i, j: (t, 0, 0)),
            pl.BlockSpec((None, 2 * half, hidden), lambda t, i, j: (t, 0, 0)),
            pl.BlockSpec((None, hidden, LANES), lambda t, i, j: (t, 0, 0)),
        ],
        out_specs=pl.BlockSpec((None, None, None, nr, LANES), lambda t, i, j: (t, i, j, 0, 0)),
        out_shape=jax.ShapeDtypeStruct((two, b, g, nr, LANES), BF16),
        compiler_params=_cparams(3), name="nsa_compress",
    )(kr, pos, w1, w2p)


def _colmax(a):
    return jnp.max(a, axis=0, keepdims=True)


def _colsum(a):
    return jnp.sum(a, axis=0, keepdims=True)


class _SoftmaxPipe:
    def __init__(self, s_ref, pe_ref, acc_ref, hd):
        self.s, self.pe, self.acc, self.hd = s_ref, pe_ref, acc_ref, hd

    def start(self, first_scores, slot):
        cols = first_scores.shape[1]
        self.s[slot] = first_scores
        self.pe[1 - slot] = jnp.zeros(self.pe.shape[1:], self.pe.dtype)
        self.acc[...] = jnp.zeros(self.acc.shape, F32)
        return _colmax(first_scores), jnp.ones((1, cols), F32), jnp.full((1, cols), NEG_INF, F32)

    def step(self, slot, carry, v_prev, next_scores):
        cm_cur, alpha_prev, m_old = carry
        m_new = jnp.maximum(m_old, cm_cur)
        self.pe[slot] = jnp.exp2(self.s[slot] - m_new).astype(self.pe.dtype)
        cm_next = cm_cur
        if next_scores is not None:
            s_next = next_scores()
            self.s[1 - slot] = s_next
            cm_next = _colmax(s_next)
        self.acc[...] = alpha_prev * self.acc[...] + _dot(v_prev, self.pe[1 - slot])
        return cm_next, jnp.exp2(m_old - m_new), m_new

    def finish(self, slot_last, carry, v_last):
        _, alpha_prev, _ = carry
        acc = alpha_prev * self.acc[...] + _dot(v_last, self.pe[slot_last])
        return acc[0:self.hd] * (1.0 / acc[self.hd:self.hd + 1])

    def shifted_step(self, slot, v_prev, next_scores):
        self.pe[slot] = jnp.exp2(self.s[slot]).astype(self.pe.dtype)
        self.s[1 - slot] = next_scores()
        self.acc[...] += _dot(v_prev, self.pe[1 - slot])


def _nsa_kernel(qT_ref, kc_ref, vcT_ref, ov_ref, ksl_ref, vslT_ref, kw_ref, vwT_ref, cb_ref, tp_ref,
                gl_ref, o_ref, q0_ref, qa_ref, qs_ref, rank_ref, kst_ref, vsa_ref, vwa_ref, s_ref, pe_ref,
                acc_ref, ws_ref, wpe_ref, wacc_ref, *, n_slc, top_k):
    T, R, HD = NSA_TQ, NSA_REP, NSA_HEAD_DIM
    nr = kc_ref.shape[0]
    per_tile = T // CMP_STRIDE
    front = nr - per_tile
    half = LANES // 2
    qi = pl.program_id(2)
    colmax, colsum = _colmax, _colsum

    @pl.when(qi == 0)
    def _():
        for src, dst in ((vslT_ref, vsa_ref), (vwT_ref, vwa_ref)):
            dst[0:HD, :] = src[...]
            dst[HD:, :] = jnp.ones((dst.shape[0] - HD, dst.shape[1]), dst.dtype)
        krow = lax.broadcasted_iota(jnp.int32, (T, LANES), 0)
        klane = lax.broadcasted_iota(jnp.int32, (T, LANES), 1)
        for t0 in range(0, ksl_ref.shape[0], T):
            blk = t0 // SLC_BLOCK + krow // SLC_BLOCK
            tag = jnp.where(klane - half == blk, 1.0, 0.0).astype(BF16)
            kst_ref[t0:t0 + T, :] = jnp.where(klane >= half, tag, ksl_ref[t0:t0 + T, :])

    qT = jnp.concatenate([qT_ref[r] for r in range(R)], axis=1)
    q0_ref[0:HD, :] = qT
    q0_ref[HD:2 * HD, :] = jnp.zeros((HD, R * T), BF16)
    qa_ref[0:HD, :] = qT

    n_win = WINDOW // T

    def win_offset(rel):
        return pl.multiple_of(jnp.maximum(qi - rel, 0) * T, T)

    def win_scores(rel):
        tile = TILE_EDGE if rel == n_win else (TILE_ZERO if rel >= 2 else (TILE_NEAR if rel == 1 else TILE_DIAG))
        if rel > 0:
            tile = jnp.where(qi >= rel, tile, TILE_NONE)
        return _dot(kw_ref[pl.ds(win_offset(rel), T), :], q0_ref[...]) + tp_ref[tile]

    def win_values(rel):
        return vwa_ref[:, pl.ds(win_offset(rel), T)]

    win = _SoftmaxPipe(ws_ref, wpe_ref, wacc_ref, HD)
    rels = list(range(n_win, -1, -1))
    carry = win.start(win_scores(rels[0]), 0)
    for i, rel in enumerate(rels):
        nxt = functools.partial(win_scores, rels[i + 1]) if i + 1 < len(rels) else None
        carry = win.step(i % 2, carry, win_values(rels[max(i - 1, 0)]), nxt)
    o_win = win.finish((len(rels) - 1) % 2, carry, win_values(0))

    off = pl.multiple_of(front - qi * per_tile, SUBLANES)
    lg = _dot(kc_ref[...], q0_ref[...]) + cb_ref[pl.ds(off, nr), :]
    m_c = colmax(lg)
    e = jnp.exp2(lg - m_c)
    p = e * jnp.where(m_c > 0.5 * NEG_INF, 1.0 / colsum(e), 0.0)
    o_cmp = _dot(vcT_ref[...], p.astype(BF16))

    psum = p[:, 0:T]
    for r in range(1, R):
        psum = psum + p[:, r * T:(r + 1) * T]
    imp = sum(_dot(ov_ref[...], pi) for pi in _split3(psum))
    nio = lax.broadcasted_iota(jnp.int32, (half, T), 0)
    tio = lax.broadcasted_iota(jnp.int32, (half, T), 1)
    tb = (qi * T + tio) // SLC_BLOCK
    forced = (nio == 0) | (nio == tb) | (nio == tb - 1)
    imp = jnp.where(forced, FORCE_SCORE, imp)
    imp = jnp.where(nio > tb, NEG_INF, imp)
    n_grp = half // SUBLANES
    grp = [imp[SUBLANES * v:SUBLANES * (v + 1)] for v in range(n_grp)]
    nio8 = lax.broadcasted_iota(jnp.int32, (SUBLANES, T), 0)
    rank_ref[...] = jnp.zeros(rank_ref.shape, F32)
    last_block = (qi * T + T - 1) // SLC_BLOCK
    for mg in range(pl.cdiv(n_slc, SUBLANES)):

        @pl.when(mg * SUBLANES <= last_block)
        def _():
            hits = [jnp.zeros((SUBLANES, T), F32) for _ in range(n_grp)]
            for mm in range(mg * SUBLANES, min((mg + 1) * SUBLANES, n_slc)):
                rowv = imp[mm:mm + 1, :]
                for v in range(n_grp):
                    if SUBLANES * v > mm:
                        hit = jnp.where(rowv >= grp[v], 1.0, 0.0)
                    elif SUBLANES * v + SUBLANES - 1 < mm:
                        hit = jnp.where(rowv > grp[v], 1.0, 0.0)
                    else:
                        hit = jnp.where(nio8 > mm - SUBLANES * v, jnp.where(rowv >= grp[v], 1.0, 0.0),
                                        jnp.where(rowv > grp[v], 1.0, 0.0))
                    hits[v] = hits[v] + hit
            rank_ref[...] += jnp.concatenate(hits, axis=0)

    rank = rank_ref[...]
    pen = jnp.where(rank < top_k, jnp.where(nio <= tb, 0.0, SEL_PENALTY), SEL_PENALTY).astype(BF16)
    qa_ref[HD:2 * HD, :] = jnp.concatenate([pen] * R, axis=1)

    def sel_scores(kj):
        return _dot(kst_ref[pl.ds(pl.multiple_of(kj * T, T), T), :], qa_ref[...])

    def sel_values(kj):
        return vsa_ref[:, pl.ds(pl.multiple_of(kj * T, T), T)]

    n_far = jnp.maximum(qi - 1, 0)
    near = jnp.maximum(qi - 1, 0)
    near_table = jnp.where(qi >= 1, TILE_NEAR, TILE_NONE)
    odd = n_far % 2
    sel = _SoftmaxPipe(s_ref, pe_ref, acc_ref, HD)

    def exact_selected():
        def far_step(kj, slot, carry):
            return sel.step(slot, carry, sel_values(jnp.maximum(kj - 1, 0)), functools.partial(sel_scores, kj + 1))

        carry = sel.start(sel_scores(0), odd)
        carry = lax.cond(odd == 1, lambda c: far_step(0, 1, c), lambda c: c, carry)
        carry = lax.fori_loop(
            0, n_far // 2, lambda j, c: far_step(odd + 2 * j + 1, 1, far_step(odd + 2 * j, 0, c)), carry)
        s_near = s_ref[0] + tp_ref[near_table]
        s_ref[0] = s_near
        carry = (colmax(s_near),) + tuple(carry[1:])
        carry = sel.step(0, carry, sel_values(jnp.maximum(n_far - 1, 0)),
                         lambda: sel_scores(qi) + tp_ref[TILE_DIAG])
        carry = sel.step(1, carry, sel_values(n_far), None)
        return sel.finish(1, carry, sel_values(qi))

    s_d = sel_scores(qi) + tp_ref[TILE_DIAG]
    s_n = sel_scores(near) + tp_ref[near_table]
    shift = jnp.maximum(colmax(s_d), colmax(s_n)).astype(BF16).astype(F32)
    qs_ref[0:HD, :] = qT
    qs_ref[HD:2 * HD, :] = (jnp.concatenate([pen.astype(F32)] * R, axis=1) - shift).astype(BF16)

    def shifted_scores(kj):
        return _dot(kst_ref[pl.ds(pl.multiple_of(kj * T, T), T), :], qs_ref[...])

    def far_shifted(kj, slot):
        sel.shifted_step(slot, sel_values(jnp.maximum(kj - 1, 0)), functools.partial(shifted_scores, kj + 1))

    s_ref[odd] = shifted_scores(0)
    pe_ref[1 - odd] = jnp.zeros(pe_ref.shape[1:], BF16)
    acc_ref[...] = (_dot(sel_values(qi), jnp.exp2(s_d - shift).astype(BF16))
                    + _dot(sel_values(near), jnp.exp2(s_n - shift).astype(BF16)))

    @pl.when(odd == 1)
    def _():
        far_shifted(0, 1)

    def far_pair(j, carry):
        far_shifted(odd + 2 * j, 0)
        far_shifted(odd + 2 * j + 1, 1)
        return carry

    lax.fori_loop(0, n_far // 2, far_pair, 0)
    acc = acc_ref[...] + _dot(sel_values(jnp.maximum(n_far - 1, 0)), pe_ref[1])
    denom = acc[HD:HD + 1]
    unsafe = (jnp.sum(jnp.where(denom < 2.0 ** 100, 0.0, 1.0))
              + jnp.sum(jnp.where(jnp.abs(acc[0:HD]) < 3.0e38, 0.0, 1.0))) > 0.0
    o_slc = lax.cond(unsafe, exact_selected, lambda: acc[0:HD] * (1.0 / denom))

    sg = _sigmoid(gl_ref[...])
    for r in range(R):
        cols = slice(r * T, (r + 1) * T)
        o = (sg[r:r + 1] * o_cmp[:, cols] + sg[R + r:R + r + 1] * o_slc[:, cols]
             + sg[2 * R + r:2 * R + r + 1] * o_win[:, cols])
        o_ref[r * HD:(r + 1) * HD, :] = o.astype(o_ref.dtype)


def _t5_bucket_np(dist):
    n = np.maximum(dist, 0)
    max_exact = REL_BUCKETS // 2
    nf = np.maximum(n, 1).astype(np.float32)
    large = max_exact + (np.log(nf / np.float32(max_exact)) / np.float32(math.log(REL_MAX_DIST / max_exact))
                         * np.float32(REL_BUCKETS - max_exact)).astype(np.int32)
    large = np.minimum(large, REL_BUCKETS - 1)
    return np.where(n < max_exact, n, large).astype(np.int32)


def _nsa_tables(rel_bias, nr):
    T, G, R = NSA_TQ, NSA_KV_HEADS, NSA_REP
    table = rel_bias.astype(F32) * LOG2E
    per_tile = T // CMP_STRIDE
    front = nr - per_tile
    t = np.arange(T)[None, :]

    def lay(a):
        keys = a.shape[0]
        return jnp.transpose(a, (2, 0, 1)).reshape(G, R, keys, T).transpose(0, 2, 1, 3).reshape(G, keys, R * T)

    cend = CMP_STRIDE * (np.arange(front + nr)[:, None] - front) + CMP_BLOCK - 1
    dist = t - cend
    def lookup(d):
        onehot = jax.nn.one_hot(_t5_bucket_np(d), REL_BUCKETS, dtype=F32)
        return jnp.einsum("ktb,bh->kth", onehot, table, precision=lax.Precision.HIGHEST)

    cb = jnp.where((dist >= 0)[..., None], lookup(dist), NEG_INF)
    k = np.arange(T)[:, None]
    assert 2 * T - (T - 1) >= REL_MAX_DIST and WINDOW % T == 0 and WINDOW // T >= 2
    last = table[REL_BUCKETS - 1]
    diag = jnp.where((t - k >= 0)[..., None], lookup(t - k) - last, NEG_INF)
    near = lookup(T + t - k) - last
    none = jnp.full_like(near, NEG_INF)
    edge = jnp.where((k > t)[..., None], jnp.zeros_like(near), NEG_INF)
    tiles = [None] * 5
    tiles[TILE_DIAG], tiles[TILE_NEAR], tiles[TILE_NONE], tiles[TILE_EDGE] = diag, near, none, edge
    tiles[TILE_ZERO] = jnp.zeros_like(near)
    tp = jnp.stack([lay(a) for a in tiles], axis=1)
    return lay(cb), tp


def _overlap_matrix(nr, n_slc):
    half = LANES // 2
    n = np.arange(half)[:, None]
    c = np.arange(nr)[None, :]
    ov = ((CMP_STRIDE * c <= SLC_BLOCK * n + SLC_BLOCK - 1) & (CMP_STRIDE * c + CMP_BLOCK - 1 >= SLC_BLOCK * n)
          & (n < n_slc) & (c < nr - 1))
    return jnp.asarray(ov.astype(np.float32), dtype=BF16)


def _nsa_attention(qT, kc, vcT, ksl, vslT, kw, vwT, cb, tp, glT):
    n_heads, hd, _ = qT.shape
    b, s, _ = ksl.shape
    G, R, T = NSA_KV_HEADS, NSA_REP, NSA_TQ
    nr = kc.shape[-2]
    nq = s // T
    n_slc = s // SLC_BLOCK
    assert n_slc <= LANES // 2 and s % T == 0 and nr % SUBLANES == 0
    ov = _overlap_matrix(nr, n_slc)
    kern = functools.partial(_nsa_kernel, n_slc=n_slc, top_k=min(SLC_TOPK, n_slc))
    k_spec = pl.BlockSpec((None, s, LANES), lambda i, j, k: (i, 0, j))
    vT_spec = pl.BlockSpec((None, hd, s), lambda i, j, k: (j, 0, i))
    ones_rows = 16
    pipe_scratch = [pltpu.VMEM((2, T, R * T), F32), pltpu.VMEM((2, T, R * T), BF16),
                    pltpu.VMEM((hd + ones_rows, R * T), F32)]
    value_scratch = [pltpu.VMEM((hd + ones_rows, s), BF16)] * 2
    return pl.pallas_call(
        kern, grid=(b, G, nq),
        in_specs=[
            pl.BlockSpec((R, hd, T), lambda i, j, k: (j, 0, i * nq + k)),
            pl.BlockSpec((None, None, nr, LANES), lambda i, j, k: (i, j, 0, 0)),
            pl.BlockSpec((None, None, hd, nr), lambda i, j, k: (i, j, 0, 0)),
            pl.BlockSpec(ov.shape, lambda i, j, k: (0, 0)),
            k_spec, vT_spec, k_spec, vT_spec,
            pl.BlockSpec((None,) + cb.shape[1:], lambda i, j, k: (j, 0, 0)),
            pl.BlockSpec((None,) + tp.shape[1:], lambda i, j, k: (j, 0, 0, 0)),
            pl.BlockSpec((None, 16, T), lambda i, j, k: (j, 0, i * nq + k)),
        ],
        out_specs=pl.BlockSpec((R * hd, T), lambda i, j, k: (j, i * nq + k)),
        out_shape=jax.ShapeDtypeStruct((n_heads * hd, b * s), BF16),
        scratch_shapes=[pltpu.VMEM((2 * hd, R * T), BF16)] * 3
        + [pltpu.VMEM((LANES // 2, T), F32), pltpu.VMEM((s, LANES), BF16)]
        + value_scratch + pipe_scratch + pipe_scratch,
        compiler_params=_cparams(3), name="nsa_attention",
    )(qT, kc, vcT, ov, ksl, vslT, kw, vwT, cb, tp, glT)


def _merge_kernel(x_ref, ys_ref, ynT_ref, mg_ref, ws_ref, wn_ref, wo_ref, g_ref, o_ref):
    d = x_ref.shape[1]
    a = _dot(ys_ref[...], ws_ref[...])
    bb = lax.dot_general(ynT_ref[...], wn_ref[...], (((0,), (0,)), ((), ())), preferred_element_type=F32)
    mg = _sigmoid(mg_ref[...].astype(F32))
    mixed = mg[:, :d] * a + mg[:, d:] * bb
    o = _dot(mixed.astype(BF16), wo_ref[...])
    o_ref[...] = x_ref[...] + _rms(o, g_ref[...])


def _merge(x2d, y_ssd, y_nsaT, mg, w_s, w_n, w_o, g, tm=512):
    m, d = x2d.shape
    tm = min(tm, m)
    row = lambda a: pl.BlockSpec((tm, a.shape[1]), lambda i: (i, 0))
    full = lambda a: pl.BlockSpec(a.shape, lambda i: (0, 0))
    g = g.reshape(1, d)
    return pl.pallas_call(
        _merge_kernel, grid=(m // tm,),
        in_specs=[row(x2d), row(y_ssd), pl.BlockSpec((y_nsaT.shape[0], tm), lambda i: (0, i)), row(mg),
                  full(w_s), full(w_n), full(w_o), full(g)],
        out_specs=pl.BlockSpec((tm, d), lambda i: (i, 0)),
        out_shape=jax.ShapeDtypeStruct((m, d), F32),
        compiler_params=_cparams(1), name="merge",
    )(x2d, y_ssd, y_nsaT, mg, w_s, w_n, w_o, g)


def _xattn_kernel(x_ref, kv_ref, gpre_ref, wq_ref, wo_ref, gpost_ref, o_ref):
    x = x_ref[...]
    h = _rms(x, gpre_ref[...]).astype(BF16)
    q = _dot(h, wq_ref[...])
    width = X_HEADS * X_HEAD_DIM
    scale = X_HEAD_DIM ** -0.5
    outs = []
    for hh in range(X_HEADS):
        sl = slice(hh * X_HEAD_DIM, (hh + 1) * X_HEAD_DIM)
        lg = _dot_nt(q[:, sl].astype(BF16), kv_ref[:, sl]) * scale
        e = jnp.exp(lg - jnp.max(lg, axis=-1, keepdims=True))
        p = e / jnp.sum(e, axis=-1, keepdims=True)
        outs.append(_dot(p.astype(BF16), kv_ref[:, width + hh * X_HEAD_DIM:width + (hh + 1) * X_HEAD_DIM]))
    o = _dot(jnp.concatenate(outs, axis=-1).astype(BF16), wo_ref[...])
    o_ref[...] = x + _rms(o, gpost_ref[...])


def _xattn(x3d, kv, g_pre, w_q, w_o, g_post, tm=512):
    b, s, d = x3d.shape
    tm = min(tm, s)
    ml = kv.shape[1]
    full = lambda a: pl.BlockSpec(a.shape, lambda i, j: (0, 0))
    g_pre, g_post = g_pre.reshape(1, d), g_post.reshape(1, d)
    return pl.pallas_call(
        _xattn_kernel, grid=(b, s // tm),
        in_specs=[pl.BlockSpec((None, tm, d), lambda i, j: (i, j, 0)),
                  pl.BlockSpec((None, ml, kv.shape[2]), lambda i, j: (i, 0, 0)),
                  full(g_pre), full(w_q), full(w_o), full(g_post)],
        out_specs=pl.BlockSpec((None, tm, d), lambda i, j: (i, j, 0)),
        out_shape=jax.ShapeDtypeStruct((b, s, d), F32),
        compiler_params=_cparams(2), name="xattn",
    )(x3d, kv, g_pre, w_q, w_o, g_post)


def _mlp_kernel(x_ref, gpre_ref, w1_ref, w2_ref, gpost_ref, o_ref, *, f_chunk):
    x = x_ref[...]
    h = _rms(x, gpre_ref[...]).astype(BF16)
    d_ff = w1_ref.shape[1]
    acc = jnp.zeros(x.shape, F32)
    for f0 in range(0, d_ff, f_chunk):
        u = jnp.maximum(_dot(h, w1_ref[:, f0:f0 + f_chunk]), 0.0)
        acc = acc + _dot((u * u).astype(BF16), w2_ref[f0:f0 + f_chunk, :])
    o_ref[...] = x + _rms(acc, gpost_ref[...])


def _mlp(x2d, g_pre, w1, w2, g_post, tm=512):
    m, d = x2d.shape
    tm = min(tm, m)
    full = lambda a: pl.BlockSpec(a.shape, lambda i: (0, 0))
    g_pre, g_post = g_pre.reshape(1, d), g_post.reshape(1, d)
    return pl.pallas_call(
        functools.partial(_mlp_kernel, f_chunk=1024), grid=(m // tm,),
        in_specs=[pl.BlockSpec((tm, d), lambda i: (i, 0)), full(g_pre), full(w1), full(w2), full(g_post)],
        out_specs=pl.BlockSpec((tm, d), lambda i: (i, 0)),
        out_shape=jax.ShapeDtypeStruct((m, d), F32),
        compiler_params=_cparams(1), name="mlp",
    )(x2d, g_pre, w1, w2, g_post)


def _pad_heads(w, n_heads, hd):
    k = w.shape[0]
    w = w.reshape(k, n_heads, hd)
    return jnp.pad(w, ((0, 0), (0, 0), (0, LANES - hd))).reshape(k, n_heads * LANES)


def _layer(x, mem, w_in, conv_w, conv_b, dt_bias, a_log, d_skip, ssd_norm, cmp_pos, cmp_w1, cmp_w2,
           rel_bias, w_br_ssd, w_br_nsa, w_out, w_xq, w_xkv, w_xo, w_ff1, w_ff2,
           n_mix_pre, n_mix_post, n_x_pre, n_x_post, n_mem, n_ffn_pre, n_ffn_post):
    b, s, d = x.shape
    G, R, HD = NSA_KV_HEADS, NSA_REP, NSA_HEAD_DIM
    d_inner = 2 * d
    n_ssd_heads = d_inner // SSD_HEAD_DIM
    conv_dim = d_inner + 2 * SSD_GROUPS * SSD_STATE
    nsa_w, kv_w = NSA_HEADS * HD, G * HD
    sizes = (d_inner, conv_dim, n_ssd_heads, nsa_w, 6 * kv_w, 3 * NSA_HEADS, 2 * d)
    offs = np.concatenate([[0], np.cumsum(sizes)])
    seg = lambda i: w_in[:, offs[i]:offs[i + 1]]
    w_z, w_xbc, w_dt, w_q, w_kv, w_gate, w_mg = (seg(i) for i in range(7))
    w_small = jnp.pad(jnp.concatenate([w_dt, w_gate], axis=1),
                      ((0, 0), (0, LANES - n_ssd_heads - 3 * NSA_HEADS)))
    w_kv6 = w_kv.reshape(d, 6, kv_w)
    w_cmp = jnp.concatenate([w_kv6[:, 0], w_kv6[:, 1]], axis=1)
    bf = lambda a: a.astype(BF16)

    x2d = x.reshape(b * s, d)
    z, xbc = _norm_matmul(x2d, n_mix_pre, [bf(w_z), bf(w_xbc)], [BF16, BF16])
    w_gl = jnp.transpose(w_gate.reshape(d, 3, G, R), (2, 1, 3, 0)).reshape(G, 3 * R, d)
    w_gl = jnp.pad(w_gl, ((0, 0), (0, 16 - 3 * R), (0, 0))).reshape(G * 16, d)
    small, dtT, glT, qT, kv_cmp, vslT, vwT = _norm_matmul(
        x2d, n_mix_pre,
        [bf(w_small), bf(w_dt.T), bf(w_gl), bf(w_q.T * (HD ** -0.5 * LOG2E)), bf(w_cmp),
         bf(w_kv6[:, 3].T), bf(w_kv6[:, 5].T)],
        [F32, F32, F32, BF16, BF16, BF16, BF16],
        transposed=(False, True, True, True, False, True, True))
    ksl, kw, mg = _norm_matmul(
        x2d, n_mix_pre, [bf(_pad_heads(w_kv6[:, 2], G, HD)), bf(_pad_heads(w_kv6[:, 4], G, HD)), bf(w_mg)],
        [BF16] * 3)

    y_ssd = _ssd_mixer(z.reshape(b, s, -1), xbc.reshape(b, s, -1), small.reshape(b, s, -1), dtT,
                       conv_w, conv_b, dt_bias, a_log, d_skip, ssd_norm)

    nr = s // CMP_STRIDE
    kr = kv_cmp.reshape(b, nr, CMP_STRIDE, 2, G, HD)
    kr = jnp.transpose(kr, (3, 0, 4, 1, 2, 5)).reshape(2, b, G, nr, CMP_STRIDE * HD)
    pos = jnp.broadcast_to(bf(cmp_pos).reshape(2, 1, CMP_BLOCK * HD), (2, SUBLANES, CMP_BLOCK * HD))
    w2p = jnp.pad(bf(cmp_w2), ((0, 0), (0, 0), (0, LANES - HD)))
    kvc = _compress(kr, pos, bf(cmp_w1), w2p)
    vcT = jnp.swapaxes(kvc[1][..., :HD], -1, -2)
    cb, tp = _nsa_tables(rel_bias, nr)
    y_nsa = _nsa_attention(qT.reshape(NSA_HEADS, HD, b * s), kvc[0], vcT,
                           ksl.reshape(b, s, -1), vslT.reshape(G, HD, b * s),
                           kw.reshape(b, s, -1), vwT.reshape(G, HD, b * s), cb, tp,
                           glT.reshape(G, 16, b * s))

    x1 = _merge(x2d, y_ssd.reshape(b * s, -1), y_nsa, mg, bf(w_br_ssd), bf(w_br_nsa), bf(w_out), n_mix_post)

    (kv_mem,) = _norm_matmul(mem.reshape(-1, d), n_mem, [bf(w_xkv)], [BF16])
    x2 = _xattn(x1.reshape(b, s, d), kv_mem.reshape(b, mem.shape[1], -1), n_x_pre, bf(w_xq), bf(w_xo),
                n_x_post)

    x3 = _mlp(x2.reshape(b * s, d), n_ffn_pre, bf(w_ff1), bf(w_ff2), n_ffn_post)
    return x3.reshape(b, s, d)


def kernel(x, mem, w_in, ssd_conv_w, ssd_conv_b, ssd_dt_bias, ssd_a_log, ssd_d_skip, ssd_norm, cmp_pos,
           cmp_w1, cmp_w2, rel_bias, w_br_ssd, w_br_nsa, w_out, w_xq, w_xkv, w_xo, w_ff1, w_ff2,
           norm_mix_pre, norm_mix_post, norm_x_pre, norm_x_post, norm_mem, norm_ffn_pre, norm_ffn_post):
    for l in range(w_in.shape[0]):
        x = _layer(x, mem, w_in[l], ssd_conv_w[l], ssd_conv_b[l], ssd_dt_bias[l], ssd_a_log[l],
                   ssd_d_skip[l], ssd_norm[l], cmp_pos[l], cmp_w1[l], cmp_w2[l], rel_bias,
                   w_br_ssd[l], w_br_nsa[l], w_out[l], w_xq[l], w_xkv[l], w_xo[l], w_ff1[l], w_ff2[l],
                   norm_mix_pre[l], norm_mix_post[l], norm_x_pre[l], norm_x_post[l], norm_mem[l],
                   norm_ffn_pre[l], norm_ffn_post[l])
    return x
```

```python
import functools
import math

import numpy as np
import jax
import jax.numpy as jnp
from jax import lax
from jax.experimental import pallas as pl
from jax.experimental.pallas import tpu as pltpu

F32 = jnp.float32
BF16 = jnp.bfloat16

NORM_EPS = 1e-6
NEG_INF = -1e30
FORCE_SCORE = 1e9
SEL_PENALTY = NEG_INF

LANES = 128
SUBLANES = 8
VMEM_LIMIT = 56 * 1024 * 1024

SSD_HEAD_DIM = 64
SSD_GROUPS = 8
SSD_STATE = 128
SSD_CONV = 4
SSD_CHUNK = 128
NSA_HEADS = 16
NSA_KV_HEADS = 4
NSA_HEAD_DIM = 64
NSA_REP = NSA_HEADS // NSA_KV_HEADS
CMP_BLOCK = 32
CMP_STRIDE = 16
SLC_BLOCK = 64
SLC_TOPK = 16
WINDOW = 512
NSA_TQ = 256
LOG2E = math.log2(math.e)
REL_BUCKETS = 32
REL_MAX_DIST = 128
X_HEADS = 4
X_HEAD_DIM = 128

TILE_DIAG, TILE_NEAR, TILE_NONE, TILE_EDGE, TILE_ZERO = range(5)


def _cparams(n_grid):
    return pltpu.CompilerParams(dimension_semantics=("arbitrary",) * n_grid,
                                vmem_limit_bytes=VMEM_LIMIT)


def _sigmoid(x):
    return jax.nn.sigmoid(x)


def _rms(x, g):
    return x * lax.rsqrt(jnp.mean(x * x, axis=-1, keepdims=True) + NORM_EPS) * g


def _dot(a, b):
    return jnp.dot(a, b, preferred_element_type=F32)


def _dot_nt(a, b):
    return lax.dot_general(a, b, (((1,), (1,)), ((), ())), preferred_element_type=F32)


def _split3(x):
    x1 = x.astype(BF16)
    r1 = x - x1.astype(F32)
    x2 = r1.astype(BF16)
    x3 = (r1 - x2.astype(F32)).astype(BF16)
    return x1, x2, x3


def _norm_matmul_kernel(x_ref, g_ref, *refs, transposed, n_chunk):
    n_out = len(transposed)
    w_refs, o_refs = refs[:n_out], refs[n_out:]
    h = _rms(x_ref[...], g_ref[...]).astype(BF16)
    for w_ref, o_ref, tr in zip(w_refs, o_refs, transposed):
        if tr:
            o_ref[...] = _dot_nt(w_ref[...], h).astype(o_ref.dtype)
            continue
        n = w_ref.shape[1]
        step = min(n, n_chunk)
        for n0 in range(0, n, step):
            o_ref[:, n0:n0 + step] = _dot(h, w_ref[:, n0:n0 + step]).astype(o_ref.dtype)


def _norm_matmul(x2d, g, ws, out_dtypes, transposed=None, tm=512):
    m, k = x2d.shape
    tm = min(tm, m)
    transposed = tuple(transposed) if transposed is not None else (False,) * len(ws)
    in_specs = [pl.BlockSpec((tm, k), lambda i: (i, 0)), pl.BlockSpec((1, k), lambda i: (0, 0))]
    in_specs += [pl.BlockSpec(w.shape, lambda i: (0, 0)) for w in ws]
    out_specs, out_shape = [], []
    for w, dt, tr in zip(ws, out_dtypes, transposed):
        if tr:
            out_specs.append(pl.BlockSpec((w.shape[0], tm), lambda i: (0, i)))
            out_shape.append(jax.ShapeDtypeStruct((w.shape[0], m), dt))
        else:
            out_specs.append(pl.BlockSpec((tm, w.shape[1]), lambda i: (i, 0)))
            out_shape.append(jax.ShapeDtypeStruct((m, w.shape[1]), dt))
    return pl.pallas_call(
        functools.partial(_norm_matmul_kernel, transposed=transposed, n_chunk=1024),
        grid=(m // tm,), in_specs=in_specs, out_specs=out_specs, out_shape=out_shape,
        compiler_params=_cparams(1), name="norm_proj",
    )(x2d, g.reshape(1, k), *ws)


def _softplus(x):
    return jnp.maximum(x, 0.0) + jnp.log(1.0 + jnp.exp(-jnp.abs(x)))


def _ssd_kernel(xbc_ref, prev_ref, z_ref, dt_ref, dtT_ref, cw_ref, cb_ref, dtb_ref, dtbT_ref,
                alog_ref, alogT_ref, dskip_ref, nw_ref, y_ref, state_ref, xc_ref, ybuf_ref,
                *, n_heads, d_inner):
    L, P, N, G = SSD_CHUNK, SSD_HEAD_DIM, SSD_STATE, SSD_GROUPS
    conv_dim = xc_ref.shape[1]
    c = pl.program_id(1)

    @pl.when(c == 0)
    def _():
        state_ref[...] = jnp.zeros_like(state_ref)

    cw = 512
    prev_rows = prev_ref.shape[0]
    row = lax.broadcasted_iota(jnp.int32, (prev_rows, cw), 0)
    for j in range(conv_dim // cw):
        sl = slice(j * cw, (j + 1) * cw)
        cur = xbc_ref[:, sl].astype(F32)
        prev = jnp.where(c == 0, 0.0, prev_ref[:, sl].astype(F32))
        acc = cb_ref[:, sl] + cw_ref[SSD_CONV - 1:SSD_CONV, sl] * cur
        for k in range(1, SSD_CONV):
            rc = pltpu.roll(cur, k, 0)
            rp = pltpu.roll(prev, k, 0)
            head = jnp.where(row < k, rp, rc[:prev_rows])
            shifted = jnp.concatenate([head, rc[prev_rows:]], axis=0)
            acc = acc + cw_ref[SSD_CONV - 1 - k:SSD_CONV - k, sl] * shifted
        xc_ref[:, sl] = acc * _sigmoid(acc)

    dt = _softplus(dt_ref[:, 0:n_heads] + dtb_ref[...])
    a = dt * (-LOG2E * jnp.exp(alog_ref[...]))
    dtT = _softplus(dtT_ref[...] + dtbT_ref[...])
    aT = dtT * (-LOG2E * jnp.exp(alogT_ref[...]))
    ri = lax.broadcasted_iota(jnp.int32, (L, L), 0)
    ci = lax.broadcasted_iota(jnp.int32, (L, L), 1)
    causal = ci <= ri
    tri = jnp.where(causal, 1.0, 0.0).astype(BF16)
    triu = jnp.where(ri <= ci, 1.0, 0.0).astype(BF16)
    a_cs = sum(_dot(tri, ai) for ai in _split3(a))
    a_csT = sum(_dot(ai, triu) for ai in _split3(aT))
    lane = lax.broadcasted_iota(jnp.int32, (L, 2 * P), 1)
    lo_half = lane < P
    lane1 = lax.broadcasted_iota(jnp.int32, (1, 2 * P), 1) < P

    heads_per_group = n_heads // G
    for g in range(G):
        b_g = xc_ref[:, d_inner + g * N:d_inner + (g + 1) * N]
        c_g = xc_ref[:, d_inner + G * N + g * N:d_inner + G * N + (g + 1) * N]
        c_gb = c_g.astype(BF16)
        cb = _dot_nt(c_gb, b_g.astype(BF16))
        b_gT = b_g.T.astype(BF16)
        for pp in range(heads_per_group // 2):
            pair = g * (heads_per_group // 2) + pp
            h0, h1 = 2 * pair, 2 * pair + 1
            xs = xc_ref[:, pair * 2 * P:(pair + 1) * 2 * P]
            dt_pair = jnp.where(lo_half, dt[:, h0:h0 + 1], dt[:, h1:h1 + 1])
            xd = xs * dt_pair
            xd_b = xd.astype(BF16)
            ys = []
            for h in (h0, h1):
                seg = jnp.where(causal, a_cs[:, h:h + 1] - a_csT[h:h + 1, :], NEG_INF)
                mat = (cb * jnp.exp2(seg)).astype(BF16)
                ys.append(_dot(mat, xd_b))
            y = jnp.where(lo_half, ys[0], ys[1])
            cs_pair = jnp.where(lo_half, a_cs[:, h0:h0 + 1], a_cs[:, h1:h1 + 1])
            st = state_ref[pair]
            y = y + _dot(c_gb, st.astype(BF16)) * jnp.exp2(cs_pair)
            tot = jnp.where(lane1, a_csT[h0:h0 + 1, L - 1:L], a_csT[h1:h1 + 1, L - 1:L])
            xdd = (xd * jnp.exp2(tot - cs_pair)).astype(BF16)
            state_ref[pair] = st * jnp.exp2(tot) + _dot(b_gT, xdd)
            ybuf_ref[:, pair * 2 * P:(pair + 1) * 2 * P] = y + xs * dskip_ref[:, pair * 2 * P:(pair + 1) * 2 * P]

    gw = d_inner // G
    for g in range(G):
        sl = slice(g * gw, (g + 1) * gw)
        zz = z_ref[:, sl].astype(F32)
        yg = ybuf_ref[:, sl] * (zz * _sigmoid(zz))
        y_ref[:, sl] = _rms(yg, nw_ref[:, sl]).astype(y_ref.dtype)


def _ssd_mixer(z, xbc, small, dtT, conv_w, conv_b, dt_bias, a_log, d_skip, norm_w):
    b, s, d_inner = z.shape
    conv_dim = xbc.shape[-1]
    n_heads = d_inner // SSD_HEAD_DIM
    L = SSD_CHUNK
    nc = s // L
    prev_rows = 16
    xbc_prev = xbc.reshape(b, s // prev_rows, prev_rows, conv_dim)
    blocks_per_chunk = L // prev_rows
    kern = functools.partial(_ssd_kernel, n_heads=n_heads, d_inner=d_inner)
    const = lambda shape: pl.BlockSpec(shape, lambda i, j: (0,) * len(shape))
    return pl.pallas_call(
        kern, grid=(b, nc),
        in_specs=[
            pl.BlockSpec((None, L, conv_dim), lambda i, j: (i, j, 0)),
            pl.BlockSpec((None, None, prev_rows, conv_dim),
                         lambda i, j: (i, jnp.maximum(j * blocks_per_chunk - 1, 0), 0, 0)),
            pl.BlockSpec((None, L, d_inner), lambda i, j: (i, j, 0)),
            pl.BlockSpec((None, L, small.shape[-1]), lambda i, j: (i, j, 0)),
            pl.BlockSpec((n_heads, L), lambda i, j: (0, i * nc + j)),
            const((SSD_CONV, conv_dim)), const((1, conv_dim)),
            const((1, n_heads)), const((n_heads, 1)),
            const((1, n_heads)), const((n_heads, 1)),
            const((1, d_inner)), const((1, d_inner)),
        ],
        out_specs=pl.BlockSpec((None, L, d_inner), lambda i, j: (i, j, 0)),
        out_shape=jax.ShapeDtypeStruct((b, s, d_inner), BF16),
        scratch_shapes=[pltpu.VMEM((n_heads // 2, SSD_STATE, 2 * SSD_HEAD_DIM), F32),
                        pltpu.VMEM((L, conv_dim), F32),
                        pltpu.VMEM((L, d_inner), F32)],
        compiler_params=_cparams(2), name="ssd",
    )(xbc, xbc_prev, z, small, dtT, conv_w.astype(F32), conv_b.reshape(1, -1).astype(F32),
      dt_bias.reshape(1, -1), dt_bias.reshape(-1, 1), a_log.reshape(1, -1), a_log.reshape(-1, 1),
      jnp.repeat(d_skip, SSD_HEAD_DIM).reshape(1, -1), norm_w.reshape(1, -1))


def _compress_kernel(kr_ref, pos_ref, w1_ref, w2_ref, o_ref):
    nr, half = kr_ref.shape
    kr = kr_ref[...]
    p1 = _dot(kr, w1_ref[0:half, :])
    p2 = _dot(kr, w1_ref[half:2 * half, :])
    pb = _dot(pos_ref[...], w1_ref[...])[0:1]
    hid = p1 + pltpu.roll(p2, nr - 1, 0) + pb
    hid = hid * _sigmoid(hid)
    out = _dot(hid.astype(BF16), w2_ref[...])
    rows = lax.broadcasted_iota(jnp.int32, out.shape, 0)
    o_ref[...] = jnp.where(rows < nr - 1, out, 0.0).astype(o_ref.dtype)


def _compress(kr, pos, w1, w2p):
    two, b, g, nr, half = kr.shape
    hidden = w1.shape[-1]
    return pl.pallas_call(
        _compress_kernel,
        grid=(two, b, g),
        in_specs=[
            pl.BlockSpec((None, None, None, nr, half), lambda t, i, j: (t, i, j, 0, 0)),
            pl.BlockSpec((None, SUBLANES, 2 * half), lambda t, i, j: (t, 0, 0)),
            pl.BlockSpec((None, 2 * half, hidden), lambda t, i, j: (t, 0, 0)),
            pl.BlockSpec((None, hidden, LANES), lambda t, i, j: (t, 0, 0)),
        ],
        out_specs=pl.BlockSpec((None, None, None, nr, LANES), lambda t, i, j: (t, i, j, 0, 0)),
        out_shape=jax.ShapeDtypeStruct((two, b, g, nr, LANES), BF16),
        compiler_params=_cparams(3), name="nsa_compress",
    )(kr, pos, w1, w2p)


def _colmax(a):
    return jnp.max(a, axis=0, keepdims=True)


def _colsum(a):
    return jnp.sum(a, axis=0, keepdims=True)


class _SoftmaxPipe:
    def __init__(self, s_ref, pe_ref, acc_ref, hd):
        self.s, self.pe, self.acc, self.hd = s_ref, pe_ref, acc_ref, hd

    def start(self, first_scores, slot):
        cols = first_scores.shape[1]
        self.s[slot] = first_scores
        self.pe[1 - slot] = jnp.zeros(self.pe.shape[1:], self.pe.dtype)
        self.acc[...] = jnp.zeros(self.acc.shape, F32)
        return _colmax(first_scores), jnp.ones((1, cols), F32), jnp.full((1, cols), NEG_INF, F32)

    def step(self, slot, carry, v_prev, next_scores):
        cm_cur, alpha_prev, m_old = carry
        self.acc[...] = alpha_prev * self.acc[...] + _dot(v_prev, self.pe[1 - slot])
        cm_next = cm_cur
        if next_scores is not None:
            s_next = next_scores()
            self.s[1 - slot] = s_next
            cm_next = _colmax(s_next)
        m_new = jnp.maximum(m_old, cm_cur)
        self.pe[slot] = jnp.exp2(self.s[slot] - m_new).astype(self.pe.dtype)
        return cm_next, jnp.exp2(m_old - m_new), m_new

    def finish(self, slot_last, carry, v_last):
        _, alpha_prev, _ = carry
        acc = alpha_prev * self.acc[...] + _dot(v_last, self.pe[slot_last])
        return acc[0:self.hd] * (1.0 / acc[self.hd:self.hd + 1])

    def shifted_step(self, slot, v_prev, next_scores):
        self.pe[slot] = jnp.exp2(self.s[slot]).astype(self.pe.dtype)
        self.s[1 - slot] = next_scores()
        self.acc[...] += _dot(v_prev, self.pe[1 - slot])


def _nsa_kernel(qT_ref, kc_ref, vcT_ref, ov_ref, ksl_ref, vslT_ref, kw_ref, vwT_ref, cb_ref, tp_ref,
                gl_ref, o_ref, q0_ref, qa_ref, qs_ref, rank_ref, kst_ref, vsa_ref, vwa_ref, s_ref, pe_ref,
                acc_ref, ws_ref, wpe_ref, wacc_ref, *, n_slc, top_k):
    T, R, HD = NSA_TQ, NSA_REP, NSA_HEAD_DIM
    nr = kc_ref.shape[0]
    per_tile = T // CMP_STRIDE
    front = nr - per_tile
    half = LANES // 2
    qi = pl.program_id(2)
    colmax, colsum = _colmax, _colsum

    @pl.when(qi == 0)
    def _():
        for src, dst in ((vslT_ref, vsa_ref), (vwT_ref, vwa_ref)):
            dst[0:HD, :] = src[...]
            dst[HD:, :] = jnp.ones((dst.shape[0] - HD, dst.shape[1]), dst.dtype)
        krow = lax.broadcasted_iota(jnp.int32, (T, LANES), 0)
        klane = lax.broadcasted_iota(jnp.int32, (T, LANES), 1)
        for t0 in range(0, ksl_ref.shape[0], T):
            blk = t0 // SLC_BLOCK + krow // SLC_BLOCK
            tag = jnp.where(klane - half == blk, 1.0, 0.0).astype(BF16)
            kst_ref[t0:t0 + T, :] = jnp.where(klane >= half, tag, ksl_ref[t0:t0 + T, :])

    qT = jnp.concatenate([qT_ref[r] for r in range(R)], axis=1)
    q0_ref[0:HD, :] = qT
    q0_ref[HD:2 * HD, :] = jnp.zeros((HD, R * T), BF16)
    qa_ref[0:HD, :] = qT

    n_win = WINDOW // T

    def win_offset(rel):
        return pl.multiple_of(jnp.maximum(qi - rel, 0) * T, T)

    def win_scores(rel):
        tile = TILE_EDGE if rel == n_win else (TILE_ZERO if rel >= 2 else (TILE_NEAR if rel == 1 else TILE_DIAG))
        if rel > 0:
            tile = jnp.where(qi >= rel, tile, TILE_NONE)
        return _dot(kw_ref[pl.ds(win_offset(rel), T), :], q0_ref[...]) + tp_ref[tile]

    def win_values(rel):
        return vwa_ref[:, pl.ds(win_offset(rel), T)]

    win = _SoftmaxPipe(ws_ref, wpe_ref, wacc_ref, HD)
    rels = list(range(n_win, -1, -1))
    carry = win.start(win_scores(rels[0]), 0)
    for i, rel in enumerate(rels):
        nxt = functools.partial(win_scores, rels[i + 1]) if i + 1 < len(rels) else None
        carry = win.step(i % 2, carry, win_values(rels[max(i - 1, 0)]), nxt)
    o_win = win.finish((len(rels) - 1) % 2, carry, win_values(0))

    off = pl.multiple_of(front - qi * per_tile, SUBLANES)
    lg = _dot(kc_ref[...], q0_ref[...]) + cb_ref[pl.ds(off, nr), :]
    m_c = colmax(lg)
    e = jnp.exp2(lg - m_c)
    p = e * jnp.where(m_c > 0.5 * NEG_INF, 1.0 / colsum(e), 0.0)
    o_cmp = _dot(vcT_ref[...], p.astype(BF16))

    psum = p[:, 0:T]
    for r in range(1, R):
        psum = psum + p[:, r * T:(r + 1) * T]
    imp = sum(_dot(ov_ref[...], pi) for pi in _split3(psum))
    nio = lax.broadcasted_iota(jnp.int32, (half, T), 0)
    tio = lax.broadcasted_iota(jnp.int32, (half, T), 1)
    tb = (qi * T + tio) // SLC_BLOCK
    forced = (nio == 0) | (nio == tb) | (nio == tb - 1)
    imp = jnp.where(forced, FORCE_SCORE, imp)
    imp = jnp.where(nio > tb, NEG_INF, imp)
    n_grp = half // SUBLANES
    grp = [imp[SUBLANES * v:SUBLANES * (v + 1)] for v in range(n_grp)]
    nio8 = lax.broadcasted_iota(jnp.int32, (SUBLANES, T), 0)
    rank_ref[...] = jnp.zeros(rank_ref.shape, F32)
    last_block = (qi * T + T - 1) // SLC_BLOCK
    for mg in range(pl.cdiv(n_slc, SUBLANES)):

        @pl.when(mg * SUBLANES <= last_block)
        def _():
            hits = [jnp.zeros((SUBLANES, T), F32) for _ in range(n_grp)]
            for mm in range(mg * SUBLANES, min((mg + 1) * SUBLANES, n_slc)):
                rowv = imp[mm:mm + 1, :]
                for v in range(n_grp):
                    if SUBLANES * v > mm:
                        hit = jnp.where(rowv >= grp[v], 1.0, 0.0)
                    elif SUBLANES * v + SUBLANES - 1 < mm:
                        hit = jnp.where(rowv > grp[v], 1.0, 0.0)
                    else:
                        hit = jnp.where(nio8 > mm - SUBLANES * v, jnp.where(rowv >= grp[v], 1.0, 0.0),
                                        jnp.where(rowv > grp[v], 1.0, 0.0))
                    hits[v] = hits[v] + hit
            rank_ref[...] += jnp.concatenate(hits, axis=0)

    rank = rank_ref[...]
    pen = jnp.where(rank < top_k, jnp.where(nio <= tb, 0.0, SEL_PENALTY), SEL_PENALTY).astype(BF16)
    qa_ref[HD:2 * HD, :] = jnp.concatenate([pen] * R, axis=1)

    def sel_scores(kj):
        return _dot(kst_ref[pl.ds(pl.multiple_of(kj * T, T), T), :], qa_ref[...])

    def sel_values(kj):
        return vsa_ref[:, pl.ds(pl.multiple_of(kj * T, T), T)]

    n_far = jnp.maximum(qi - 1, 0)
    near = jnp.maximum(qi - 1, 0)
    near_table = jnp.where(qi >= 1, TILE_NEAR, TILE_NONE)
    odd = n_far % 2
    sel = _SoftmaxPipe(s_ref, pe_ref, acc_ref, HD)

    def exact_selected():
        def far_step(kj, slot, carry):
            return sel.step(slot, carry, sel_values(jnp.maximum(kj - 1, 0)), functools.partial(sel_scores, kj + 1))

        carry = sel.start(sel_scores(0), odd)
        carry = lax.cond(odd == 1, lambda c: far_step(0, 1, c), lambda c: c, carry)
        carry = lax.fori_loop(
            0, n_far // 2, lambda j, c: far_step(odd + 2 * j + 1, 1, far_step(odd + 2 * j, 0, c)), carry)
        s_near = s_ref[0] + tp_ref[near_table]
        s_ref[0] = s_near
        carry = (colmax(s_near),) + tuple(carry[1:])
        carry = sel.step(0, carry, sel_values(jnp.maximum(n_far - 1, 0)),
                         lambda: sel_scores(qi) + tp_ref[TILE_DIAG])
        carry = sel.step(1, carry, sel_values(n_far), None)
        return sel.finish(1, carry, sel_values(qi))

    s_d = sel_scores(qi) + tp_ref[TILE_DIAG]
    shift = colmax(s_d).astype(BF16).astype(F32)
    qs_ref[0:HD, :] = qT
    qs_ref[HD:2 * HD, :] = (jnp.concatenate([pen.astype(F32)] * R, axis=1) - shift).astype(BF16)

    def shifted_scores(kj):
        return _dot(kst_ref[pl.ds(pl.multiple_of(kj * T, T), T), :], qs_ref[...])

    def far_shifted(kj, slot):
        sel.shifted_step(slot, sel_values(jnp.maximum(kj - 1, 0)), functools.partial(shifted_scores, kj + 1))

    s_ref[odd] = shifted_scores(0)
    pe_ref[1 - odd] = jnp.zeros(pe_ref.shape[1:], BF16)
    acc_ref[...] = _dot(sel_values(qi), jnp.exp2(s_d - shift).astype(BF16))

    @pl.when(odd == 1)
    def _():
        far_shifted(0, 1)

    def far_pair(j, carry):
        far_shifted(odd + 2 * j, 0)
        far_shifted(odd + 2 * j + 1, 1)
        return carry

    lax.fori_loop(0, n_far // 2, far_pair, 0)
    pe_near = jnp.exp2(s_ref[0] + tp_ref[near_table]).astype(BF16)
    acc = (acc_ref[...] + _dot(sel_values(jnp.maximum(n_far - 1, 0)), pe_ref[1])
           + _dot(sel_values(near), pe_near))
    denom = acc[HD:HD + 1]
    unsafe = (jnp.sum(jnp.where(denom < 2.0 ** 100, 0.0, 1.0))
              + jnp.sum(jnp.where(jnp.abs(acc[0:HD]) < 3.0e38, 0.0, 1.0))) > 0.0
    o_slc = lax.cond(unsafe, exact_selected, lambda: acc[0:HD] * (1.0 / denom))

    sg = _sigmoid(gl_ref[...])
    for r in range(R):
        cols = slice(r * T, (r + 1) * T)
        o = (sg[r:r + 1] * o_cmp[:, cols] + sg[R + r:R + r + 1] * o_slc[:, cols]
             + sg[2 * R + r:2 * R + r + 1] * o_win[:, cols])
        o_ref[r * HD:(r + 1) * HD, :] = o.astype(o_ref.dtype)


def _t5_bucket_np(dist):
    n = np.maximum(dist, 0)
    max_exact = REL_BUCKETS // 2
    nf = np.maximum(n, 1).astype(np.float32)
    large = max_exact + (np.log(nf / np.float32(max_exact)) / np.float32(math.log(REL_MAX_DIST / max_exact))
                         * np.float32(REL_BUCKETS - max_exact)).astype(np.int32)
    large = np.minimum(large, REL_BUCKETS - 1)
    return np.where(n < max_exact, n, large).astype(np.int32)


def _nsa_tables(rel_bias, nr):
    T, G, R = NSA_TQ, NSA_KV_HEADS, NSA_REP
    table = rel_bias.astype(F32) * LOG2E
    per_tile = T // CMP_STRIDE
    front = nr - per_tile
    t = np.arange(T)[None, :]

    def lay(a):
        keys = a.shape[0]
        return jnp.transpose(a, (2, 0, 1)).reshape(G, R, keys, T).transpose(0, 2, 1, 3).reshape(G, keys, R * T)

    cend = CMP_STRIDE * (np.arange(front + nr)[:, None] - front) + CMP_BLOCK - 1
    dist = t - cend
    def lookup(d):
        onehot = jax.nn.one_hot(_t5_bucket_np(d), REL_BUCKETS, dtype=F32)
        return jnp.einsum("ktb,bh->kth", onehot, table, precision=lax.Precision.HIGHEST)

    cb = jnp.where((dist >= 0)[..., None], lookup(dist), NEG_INF)
    k = np.arange(T)[:, None]
    assert 2 * T - (T - 1) >= REL_MAX_DIST and WINDOW % T == 0 and WINDOW // T >= 2
    last = table[REL_BUCKETS - 1]
    diag = jnp.where((t - k >= 0)[..., None], lookup(t - k) - last, NEG_INF)
    near = lookup(T + t - k) - last
    none = jnp.full_like(near, NEG_INF)
    edge = jnp.where((k > t)[..., None], jnp.zeros_like(near), NEG_INF)
    tiles = [None] * 5
    tiles[TILE_DIAG], tiles[TILE_NEAR], tiles[TILE_NONE], tiles[TILE_EDGE] = diag, near, none, edge
    tiles[TILE_ZERO] = jnp.zeros_like(near)
    tp = jnp.stack([lay(a) for a in tiles], axis=1)
    return lay(cb), tp


def _overlap_matrix(nr, n_slc):
    half = LANES // 2
    n = np.arange(half)[:, None]
    c = np.arange(nr)[None, :]
    ov = ((CMP_STRIDE * c <= SLC_BLOCK * n + SLC_BLOCK - 1) & (CMP_STRIDE * c + CMP_BLOCK - 1 >= SLC_BLOCK * n)
          & (n < n_slc) & (c < nr - 1))
    return jnp.asarray(ov.astype(np.float32), dtype=BF16)


def _nsa_attention(qT, kc, vcT, ksl, vslT, kw, vwT, cb, tp, glT):
    n_heads, hd, _ = qT.shape
    b, s, _ = ksl.shape
    G, R, T = NSA_KV_HEADS, NSA_REP, NSA_TQ
    nr = kc.shape[-2]
    nq = s // T
    n_slc = s // SLC_BLOCK
    assert n_slc <= LANES // 2 and s % T == 0 and nr % SUBLANES == 0
    ov = _overlap_matrix(nr, n_slc)
    kern = functools.partial(_nsa_kernel, n_slc=n_slc, top_k=min(SLC_TOPK, n_slc))
    k_spec = pl.BlockSpec((None, s, LANES), lambda i, j, k: (i, 0, j))
    vT_spec = pl.BlockSpec((None, hd, s), lambda i, j, k: (j, 0, i))
    ones_rows = 16
    pipe_scratch = [pltpu.VMEM((2, T, R * T), F32), pltpu.VMEM((2, T, R * T), BF16),
                    pltpu.VMEM((hd + ones_rows, R * T), F32)]
    value_scratch = [pltpu.VMEM((hd + ones_rows, s), BF16)] * 2
    return pl.pallas_call(
        kern, grid=(b, G, nq),
        in_specs=[
            pl.BlockSpec((R, hd, T), lambda i, j, k: (j, 0, i * nq + k)),
            pl.BlockSpec((None, None, nr, LANES), lambda i, j, k: (i, j, 0, 0)),
            pl.BlockSpec((None, None, hd, nr), lambda i, j, k: (i, j, 0, 0)),
            pl.BlockSpec(ov.shape, lambda i, j, k: (0, 0)),
            k_spec, vT_spec, k_spec, vT_spec,
            pl.BlockSpec((None,) + cb.shape[1:], lambda i, j, k: (j, 0, 0)),
            pl.BlockSpec((None,) + tp.shape[1:], lambda i, j, k: (j, 0, 0, 0)),
            pl.BlockSpec((None, 16, T), lambda i, j, k: (j, 0, i * nq + k)),
        ],
        out_specs=pl.BlockSpec((R * hd, T), lambda i, j, k: (j, i * nq + k)),
        out_shape=jax.ShapeDtypeStruct((n_heads * hd, b * s), BF16),
        scratch_shapes=[pltpu.VMEM((2 * hd, R * T), BF16)] * 3
        + [pltpu.VMEM((LANES // 2, T), F32), pltpu.VMEM((s, LANES), BF16)]
        + value_scratch + pipe_scratch + pipe_scratch,
        compiler_params=_cparams(3), name="nsa_attention",
    )(qT, kc, vcT, ov, ksl, vslT, kw, vwT, cb, tp, glT)


def _merge_kernel(x_ref, ys_ref, ynT_ref, mg_ref, ws_ref, wn_ref, wo_ref, g_ref, o_ref):
    d = x_ref.shape[1]
    a = _dot(ys_ref[...], ws_ref[...])
    bb = lax.dot_general(ynT_ref[...], wn_ref[...], (((0,), (0,)), ((), ())), preferred_element_type=F32)
    mg = _sigmoid(mg_ref[...].astype(F32))
    mixed = mg[:, :d] * a + mg[:, d:] * bb
    o = _dot(mixed.astype(BF16), wo_ref[...])
    o_ref[...] = x_ref[...] + _rms(o, g_ref[...])


def _merge(x2d, y_ssd, y_nsaT, mg, w_s, w_n, w_o, g, tm=512):
    m, d = x2d.shape
    tm = min(tm, m)
    row = lambda a: pl.BlockSpec((tm, a.shape[1]), lambda i: (i, 0))
    full = lambda a: pl.BlockSpec(a.shape, lambda i: (0, 0))
    g = g.reshape(1, d)
    return pl.pallas_call(
        _merge_kernel, grid=(m // tm,),
        in_specs=[row(x2d), row(y_ssd), pl.BlockSpec((y_nsaT.shape[0], tm), lambda i: (0, i)), row(mg),
                  full(w_s), full(w_n), full(w_o), full(g)],
        out_specs=pl.BlockSpec((tm, d), lambda i: (i, 0)),
        out_shape=jax.ShapeDtypeStruct((m, d), F32),
        compiler_params=_cparams(1), name="merge",
    )(x2d, y_ssd, y_nsaT, mg, w_s, w_n, w_o, g)


def _xattn_kernel(x_ref, kv_ref, gpre_ref, wq_ref, wo_ref, gpost_ref, o_ref):
    x = x_ref[...]
    h = _rms(x, gpre_ref[...]).astype(BF16)
    q = _dot(h, wq_ref[...])
    width = X_HEADS * X_HEAD_DIM
    scale = X_HEAD_DIM ** -0.5
    outs = []
    for hh in range(X_HEADS):
        sl = slice(hh * X_HEAD_DIM, (hh + 1) * X_HEAD_DIM)
        lg = _dot_nt(q[:, sl].astype(BF16), kv_ref[:, sl]) * scale
        e = jnp.exp(lg - jnp.max(lg, axis=-1, keepdims=True))
        p = e / jnp.sum(e, axis=-1, keepdims=True)
        outs.append(_dot(p.astype(BF16), kv_ref[:, width + hh * X_HEAD_DIM:width + (hh + 1) * X_HEAD_DIM]))
    o = _dot(jnp.concatenate(outs, axis=-1).astype(BF16), wo_ref[...])
    o_ref[...] = x + _rms(o, gpost_ref[...])


def _xattn(x3d, kv, g_pre, w_q, w_o, g_post, tm=512):
    b, s, d = x3d.shape
    tm = min(tm, s)
    ml = kv.shape[1]
    full = lambda a: pl.BlockSpec(a.shape, lambda i, j: (0, 0))
    g_pre, g_post = g_pre.reshape(1, d), g_post.reshape(1, d)
    return pl.pallas_call(
        _xattn_kernel, grid=(b, s // tm),
        in_specs=[pl.BlockSpec((None, tm, d), lambda i, j: (i, j, 0)),
                  pl.BlockSpec((None, ml, kv.shape[2]), lambda i, j: (i, 0, 0)),
                  full(g_pre), full(w_q), full(w_o), full(g_post)],
        out_specs=pl.BlockSpec((None, tm, d), lambda i, j: (i, j, 0)),
        out_shape=jax.ShapeDtypeStruct((b, s, d), F32),
        compiler_params=_cparams(2), name="xattn",
    )(x3d, kv, g_pre, w_q, w_o, g_post)


def _mlp_kernel(x_ref, gpre_ref, w1_ref, w2_ref, gpost_ref, o_ref, *, f_chunk):
    x = x_ref[...]
    h = _rms(x, gpre_ref[...]).astype(BF16)
    d_ff = w1_ref.shape[1]
    acc = jnp.zeros(x.shape, F32)
    for f0 in range(0, d_ff, f_chunk):
        u = jnp.maximum(_dot(h, w1_ref[:, f0:f0 + f_chunk]), 0.0)
        acc = acc + _dot((u * u).astype(BF16), w2_ref[f0:f0 + f_chunk, :])
    o_ref[...] = x + _rms(acc, gpost_ref[...])


def _mlp(x2d, g_pre, w1, w2, g_post, tm=512):
    m, d = x2d.shape
    tm = min(tm, m)
    full = lambda a: pl.BlockSpec(a.shape, lambda i: (0, 0))
    g_pre, g_post = g_pre.reshape(1, d), g_post.reshape(1, d)
    return pl.pallas_call(
        functools.partial(_mlp_kernel, f_chunk=1024), grid=(m // tm,),
        in_specs=[pl.BlockSpec((tm, d), lambda i: (i, 0)), full(g_pre), full(w1), full(w2), full(g_post)],
        out_specs=pl.BlockSpec((tm, d), lambda i: (i, 0)),
        out_shape=jax.ShapeDtypeStruct((m, d), F32),
        compiler_params=_cparams(1), name="mlp",
    )(x2d, g_pre, w1, w2, g_post)


def _pad_heads(w, n_heads, hd):
    k = w.shape[0]
    w = w.reshape(k, n_heads, hd)
    return jnp.pad(w, ((0, 0), (0, 0), (0, LANES - hd))).reshape(k, n_heads * LANES)


def _layer(x, mem, w_in, conv_w, conv_b, dt_bias, a_log, d_skip, ssd_norm, cmp_pos, cmp_w1, cmp_w2,
           rel_bias, w_br_ssd, w_br_nsa, w_out, w_xq, w_xkv, w_xo, w_ff1, w_ff2,
           n_mix_pre, n_mix_post, n_x_pre, n_x_post, n_mem, n_ffn_pre, n_ffn_post):
    b, s, d = x.shape
    G, R, HD = NSA_KV_HEADS, NSA_REP, NSA_HEAD_DIM
    d_inner = 2 * d
    n_ssd_heads = d_inner // SSD_HEAD_DIM
    conv_dim = d_inner + 2 * SSD_GROUPS * SSD_STATE
    nsa_w, kv_w = NSA_HEADS * HD, G * HD
    sizes = (d_inner, conv_dim, n_ssd_heads, nsa_w, 6 * kv_w, 3 * NSA_HEADS, 2 * d)
    offs = np.concatenate([[0], np.cumsum(sizes)])
    seg = lambda i: w_in[:, offs[i]:offs[i + 1]]
    w_z, w_xbc, w_dt, w_q, w_kv, w_gate, w_mg = (seg(i) for i in range(7))
    w_small = jnp.pad(jnp.concatenate([w_dt, w_gate], axis=1),
                      ((0, 0), (0, LANES - n_ssd_heads - 3 * NSA_HEADS)))
    w_kv6 = w_kv.reshape(d, 6, kv_w)
    w_cmp = jnp.concatenate([w_kv6[:, 0], w_kv6[:, 1]], axis=1)
    bf = lambda a: a.astype(BF16)

    x2d = x.reshape(b * s, d)
    z, xbc = _norm_matmul(x2d, n_mix_pre, [bf(w_z), bf(w_xbc)], [BF16, BF16])
    w_gl = jnp.transpose(w_gate.reshape(d, 3, G, R), (2, 1, 3, 0)).reshape(G, 3 * R, d)
    w_gl = jnp.pad(w_gl, ((0, 0), (0, 16 - 3 * R), (0, 0))).reshape(G * 16, d)
    small, dtT, glT, qT, kv_cmp, vslT, vwT = _norm_matmul(
        x2d, n_mix_pre,
        [bf(w_small), bf(w_dt.T), bf(w_gl), bf(w_q.T * (HD ** -0.5 * LOG2E)), bf(w_cmp),
         bf(w_kv6[:, 3].T), bf(w_kv6[:, 5].T)],
        [F32, F32, F32, BF16, BF16, BF16, BF16],
        transposed=(False, True, True, True, False, True, True))
    ksl, kw, mg = _norm_matmul(
        x2d, n_mix_pre, [bf(_pad_heads(w_kv6[:, 2], G, HD)), bf(_pad_heads(w_kv6[:, 4], G, HD)), bf(w_mg)],
        [BF16] * 3)

    y_ssd = _ssd_mixer(z.reshape(b, s, -1), xbc.reshape(b, s, -1), small.reshape(b, s, -1), dtT,
                       conv_w, conv_b, dt_bias, a_log, d_skip, ssd_norm)

    nr = s // CMP_STRIDE
    kr = kv_cmp.reshape(b, nr, CMP_STRIDE, 2, G, HD)
    kr = jnp.transpose(kr, (3, 0, 4, 1, 2, 5)).reshape(2, b, G, nr, CMP_STRIDE * HD)
    pos = jnp.broadcast_to(bf(cmp_pos).reshape(2, 1, CMP_BLOCK * HD), (2, SUBLANES, CMP_BLOCK * HD))
    w2p = jnp.pad(bf(cmp_w2), ((0, 0), (0, 0), (0, LANES - HD)))
    kvc = _compress(kr, pos, bf(cmp_w1), w2p)
    vcT = jnp.swapaxes(kvc[1][..., :HD], -1, -2)
    cb, tp = _nsa_tables(rel_bias, nr)
    y_nsa = _nsa_attention(qT.reshape(NSA_HEADS, HD, b * s), kvc[0], vcT,
                           ksl.reshape(b, s, -1), vslT.reshape(G, HD, b * s),
                           kw.reshape(b, s, -1), vwT.reshape(G, HD, b * s), cb, tp,
                           glT.reshape(G, 16, b * s))

    x1 = _merge(x2d, y_ssd.reshape(b * s, -1), y_nsa, mg, bf(w_br_ssd), bf(w_br_nsa), bf(w_out), n_mix_post)

    (kv_mem,) = _norm_matmul(mem.reshape(-1, d), n_mem, [bf(w_xkv)], [BF16])
    x2 = _xattn(x1.reshape(b, s, d), kv_mem.reshape(b, mem.shape[1], -1), n_x_pre, bf(w_xq), bf(w_xo),
                n_x_post)

    x3 = _mlp(x2.reshape(b * s, d), n_ffn_pre, bf(w_ff1), bf(w_ff2), n_ffn_post)
    return x3.reshape(b, s, d)


def kernel(x, mem, w_in, ssd_conv_w, ssd_conv_b, ssd_dt_bias, ssd_a_log, ssd_d_skip, ssd_norm, cmp_pos,
           cmp_w1, cmp_w2, rel_bias, w_br_ssd, w_br_nsa, w_out, w_xq, w_xkv, w_xo, w_ff1, w_ff2,
           norm_mix_pre, norm_mix_post, norm_x_pre, norm_x_post, norm_mem, norm_ffn_pre, norm_ffn_post):
    for l in range(w_in.shape[0]):
        x = _layer(x, mem, w_in[l], ssd_conv_w[l], ssd_conv_b[l], ssd_dt_bias[l], ssd_a_log[l],
                   ssd_d_skip[l], ssd_norm[l], cmp_pos[l], cmp_w1[l], cmp_w2[l], rel_bias,
                   w_br_ssd[l], w_br_nsa[l], w_out[l], w_xq[l], w_xkv[l], w_xo[l], w_ff1[l], w_ff2[l],
                   norm_mix_pre[l], norm_mix_post[l], norm_x_pre[l], norm_x_post[l], norm_mem[l],
                   norm_ffn_pre[l], norm_ffn_post[l])
    return x
```

```python
import functools
import math

import numpy as np
import jax
import jax.numpy as jnp
from jax import lax
from jax.experimental import pallas as pl
from jax.experimental.pallas import tpu as pltpu

F32 = jnp.float32
BF16 = jnp.bfloat16

NORM_EPS = 1e-6
NEG_INF = -1e30
FORCE_SCORE = 1e9
SEL_PENALTY = NEG_INF

LANES = 128
SUBLANES = 8
VMEM_LIMIT = 56 * 1024 * 1024

SSD_HEAD_DIM = 64
SSD_GROUPS = 8
SSD_STATE = 128
SSD_CONV = 4
SSD_CHUNK = 128
NSA_HEADS = 16
NSA_KV_HEADS = 4
NSA_HEAD_DIM = 64
NSA_REP = NSA_HEADS // NSA_KV_HEADS
CMP_BLOCK = 32
CMP_STRIDE = 16
SLC_BLOCK = 64
SLC_TOPK = 16
WINDOW = 512
NSA_TQ = 256
LOG2E = math.log2(math.e)
REL_BUCKETS = 32
REL_MAX_DIST = 128
X_HEADS = 4
X_HEAD_DIM = 128

TILE_DIAG, TILE_NEAR, TILE_NONE, TILE_EDGE, TILE_ZERO = range(5)


def _cparams(n_grid):
    return pltpu.CompilerParams(dimension_semantics=("arbitrary",) * n_grid,
                                vmem_limit_bytes=VMEM_LIMIT)


def _sigmoid(x):
    return jax.nn.sigmoid(x)


def _rms(x, g):
    return x * lax.rsqrt(jnp.mean(x * x, axis=-1, keepdims=True) + NORM_EPS) * g


def _dot(a, b):
    return jnp.dot(a, b, preferred_element_type=F32)


def _dot_nt(a, b):
    return lax.dot_general(a, b, (((1,), (1,)), ((), ())), preferred_element_type=F32)


def _split3(x):
    x1 = x.astype(BF16)
    r1 = x - x1.astype(F32)
    x2 = r1.astype(BF16)
    x3 = (r1 - x2.astype(F32)).astype(BF16)
    return x1, x2, x3


def _norm_matmul_kernel(x_ref, g_ref, *refs, transposed, n_chunk):
    n_out = len(transposed)
    w_refs, o_refs = refs[:n_out], refs[n_out:]
    h = _rms(x_ref[...], g_ref[...]).astype(BF16)
    for w_ref, o_ref, tr in zip(w_refs, o_refs, transposed):
        if tr:
            o_ref[...] = _dot_nt(w_ref[...], h).astype(o_ref.dtype)
            continue
        n = w_ref.shape[1]
        step = min(n, n_chunk)
        for n0 in range(0, n, step):
            o_ref[:, n0:n0 + step] = _dot(h, w_ref[:, n0:n0 + step]).astype(o_ref.dtype)


def _norm_matmul(x2d, g, ws, out_dtypes, transposed=None, tm=512):
    m, k = x2d.shape
    tm = min(tm, m)
    transposed = tuple(transposed) if transposed is not None else (False,) * len(ws)
    in_specs = [pl.BlockSpec((tm, k), lambda i: (i, 0)), pl.BlockSpec((1, k), lambda i: (0, 0))]
    in_specs += [pl.BlockSpec(w.shape, lambda i: (0, 0)) for w in ws]
    out_specs, out_shape = [], []
    for w, dt, tr in zip(ws, out_dtypes, transposed):
        if tr:
            out_specs.append(pl.BlockSpec((w.shape[0], tm), lambda i: (0, i)))
            out_shape.append(jax.ShapeDtypeStruct((w.shape[0], m), dt))
        else:
            out_specs.append(pl.BlockSpec((tm, w.shape[1]), lambda i: (i, 0)))
            out_shape.append(jax.ShapeDtypeStruct((m, w.shape[1]), dt))
    return pl.pallas_call(
        functools.partial(_norm_matmul_kernel, transposed=transposed, n_chunk=1024),
        grid=(m // tm,), in_specs=in_specs, out_specs=out_specs, out_shape=out_shape,
        compiler_params=_cparams(1), name="norm_proj",
    )(x2d, g.reshape(1, k), *ws)


def _softplus(x):
    return jnp.maximum(x, 0.0) + jnp.log(1.0 + jnp.exp(-jnp.abs(x)))


def _ssd_kernel(xbc_ref, prev_ref, z_ref, dt_ref, dtT_ref, cw_ref, cb_ref, dtb_ref, dtbT_ref,
                alog_ref, alogT_ref, dskip_ref, nw_ref, y_ref, state_ref, xc_ref, ybuf_ref,
                *, n_heads, d_inner):
    L, P, N, G = SSD_CHUNK, SSD_HEAD_DIM, SSD_STATE, SSD_GROUPS
    conv_dim = xc_ref.shape[1]
    c = pl.program_id(1)

    @pl.when(c == 0)
    def _():
        state_ref[...] = jnp.zeros_like(state_ref)

    cw = 512
    prev_rows = prev_ref.shape[0]
    row = lax.broadcasted_iota(jnp.int32, (prev_rows, cw), 0)
    for j in range(conv_dim // cw):
        sl = slice(j * cw, (j + 1) * cw)
        cur = xbc_ref[:, sl].astype(F32)
        prev = jnp.where(c == 0, 0.0, prev_ref[:, sl].astype(F32))
        acc = cb_ref[:, sl] + cw_ref[SSD_CONV - 1:SSD_CONV, sl] * cur
        for k in range(1, SSD_CONV):
            rc = pltpu.roll(cur, k, 0)
            rp = pltpu.roll(prev, k, 0)
            head = jnp.where(row < k, rp, rc[:prev_rows])
            shifted = jnp.concatenate([head, rc[prev_rows:]], axis=0)
            acc = acc + cw_ref[SSD_CONV - 1 - k:SSD_CONV - k, sl] * shifted
        xc_ref[:, sl] = acc * _sigmoid(acc)

    dt = _softplus(dt_ref[:, 0:n_heads] + dtb_ref[...])
    a = dt * (-LOG2E * jnp.exp(alog_ref[...]))
    dtT = _softplus(dtT_ref[...] + dtbT_ref[...])
    aT = dtT * (-LOG2E * jnp.exp(alogT_ref[...]))
    ri = lax.broadcasted_iota(jnp.int32, (L, L), 0)
    ci = lax.broadcasted_iota(jnp.int32, (L, L), 1)
    causal = ci <= ri
    tri = jnp.where(causal, 1.0, 0.0).astype(BF16)
    triu = jnp.where(ri <= ci, 1.0, 0.0).astype(BF16)
    a_cs = sum(_dot(tri, ai) for ai in _split3(a))
    a_csT = sum(_dot(ai, triu) for ai in _split3(aT))
    lane = lax.broadcasted_iota(jnp.int32, (L, 2 * P), 1)
    lo_half = lane < P
    lane1 = lax.broadcasted_iota(jnp.int32, (1, 2 * P), 1) < P

    heads_per_group = n_heads // G
    for g in range(G):
        b_g = xc_ref[:, d_inner + g * N:d_inner + (g + 1) * N]
        c_g = xc_ref[:, d_inner + G * N + g * N:d_inner + G * N + (g + 1) * N]
        c_gb = c_g.astype(BF16)
        cb = _dot_nt(c_gb, b_g.astype(BF16))
        b_gT = b_g.T.astype(BF16)
        for pp in range(heads_per_group // 2):
            pair = g * (heads_per_group // 2) + pp
            h0, h1 = 2 * pair, 2 * pair + 1
            xs = xc_ref[:, pair * 2 * P:(pair + 1) * 2 * P]
            dt_pair = jnp.where(lo_half, dt[:, h0:h0 + 1], dt[:, h1:h1 + 1])
            xd = xs * dt_pair
            xd_b = xd.astype(BF16)
            ys = []
            for h in (h0, h1):
                seg = jnp.where(causal, a_cs[:, h:h + 1] - a_csT[h:h + 1, :], NEG_INF)
                mat = (cb * jnp.exp2(seg)).astype(BF16)
                ys.append(_dot(mat, xd_b))
            y = jnp.where(lo_half, ys[0], ys[1])
            cs_pair = jnp.where(lo_half, a_cs[:, h0:h0 + 1], a_cs[:, h1:h1 + 1])
            st = state_ref[pair]
            y = y + _dot(c_gb, st.astype(BF16)) * jnp.exp2(cs_pair)
            tot = jnp.where(lane1, a_csT[h0:h0 + 1, L - 1:L], a_csT[h1:h1 + 1, L - 1:L])
            xdd = (xd * jnp.exp2(tot - cs_pair)).astype(BF16)
            state_ref[pair] = st * jnp.exp2(tot) + _dot(b_gT, xdd)
            ybuf_ref[:, pair * 2 * P:(pair + 1) * 2 * P] = y + xs * dskip_ref[:, pair * 2 * P:(pair + 1) * 2 * P]

    gw = d_inner // G
    for g in range(G):
        sl = slice(g * gw, (g + 1) * gw)
        zz = z_ref[:, sl].astype(F32)
        yg = ybuf_ref[:, sl] * (zz * _sigmoid(zz))
        y_ref[:, sl] = _rms(yg, nw_ref[:, sl]).astype(y_ref.dtype)


def _ssd_mixer(z, xbc, small, dtT, conv_w, conv_b, dt_bias, a_log, d_skip, norm_w):
    b, s, d_inner = z.shape
    conv_dim = xbc.shape[-1]
    n_heads = d_inner // SSD_HEAD_DIM
    L = SSD_CHUNK
    nc = s // L
    prev_rows = 16
    xbc_prev = xbc.reshape(b, s // prev_rows, prev_rows, conv_dim)
    blocks_per_chunk = L // prev_rows
    kern = functools.partial(_ssd_kernel, n_heads=n_heads, d_inner=d_inner)
    const = lambda shape: pl.BlockSpec(shape, lambda i, j: (0,) * len(shape))
    return pl.pallas_call(
        kern, grid=(b, nc),
        in_specs=[
            pl.BlockSpec((None, L, conv_dim), lambda i, j: (i, j, 0)),
            pl.BlockSpec((None, None, prev_rows, conv_dim),
                         lambda i, j: (i, jnp.maximum(j * blocks_per_chunk - 1, 0), 0, 0)),
            pl.BlockSpec((None, L, d_inner), lambda i, j: (i, j, 0)),
            pl.BlockSpec((None, L, small.shape[-1]), lambda i, j: (i, j, 0)),
            pl.BlockSpec((n_heads, L), lambda i, j: (0, i * nc + j)),
            const((SSD_CONV, conv_dim)), const((1, conv_dim)),
            const((1, n_heads)), const((n_heads, 1)),
            const((1, n_heads)), const((n_heads, 1)),
            const((1, d_inner)), const((1, d_inner)),
        ],
        out_specs=pl.BlockSpec((None, L, d_inner), lambda i, j: (i, j, 0)),
        out_shape=jax.ShapeDtypeStruct((b, s, d_inner), BF16),
        scratch_shapes=[pltpu.VMEM((n_heads // 2, SSD_STATE, 2 * SSD_HEAD_DIM), F32),
                        pltpu.VMEM((L, conv_dim), F32),
                        pltpu.VMEM((L, d_inner), F32)],
        compiler_params=_cparams(2), name="ssd",
    )(xbc, xbc_prev, z, small, dtT, conv_w.astype(F32), conv_b.reshape(1, -1).astype(F32),
      dt_bias.reshape(1, -1), dt_bias.reshape(-1, 1), a_log.reshape(1, -1), a_log.reshape(-1, 1),
      jnp.repeat(d_skip, SSD_HEAD_DIM).reshape(1, -1), norm_w.reshape(1, -1))


def _compress_kernel(kr_ref, pos_ref, w1_ref, w2_ref, o_ref):
    nr, half = kr_ref.shape
    kr = kr_ref[...]
    p1 = _dot(kr, w1_ref[0:half, :])
    p2 = _dot(kr, w1_ref[half:2 * half, :])
    pb = _dot(pos_ref[...], w1_ref[...])[0:1]
    hid = p1 + pltpu.roll(p2, nr - 1, 0) + pb
    hid = hid * _sigmoid(hid)
    out = _dot(hid.astype(BF16), w2_ref[...])
    rows = lax.broadcasted_iota(jnp.int32, out.shape, 0)
    o_ref[...] = jnp.where(rows < nr - 1, out, 0.0).astype(o_ref.dtype)


def _compress(kr, pos, w1, w2p):
    two, b, g, nr, half = kr.shape
    hidden = w1.shape[-1]
    return pl.pallas_call(
        _compress_kernel,
        grid=(two, b, g),
        in_specs=[
            pl.BlockSpec((None, None, None, nr, half), lambda t, i, j: (t, i, j, 0, 0)),
            pl.BlockSpec((None, SUBLANES, 2 * half), lambda t, i, j: (t, 0, 0)),
            pl.BlockSpec((None, 2 * half, hidden), lambda t, i, j: (t, 0, 0)),
            pl.BlockSpec((None, hidden, LANES), lambda t, i, j: (t, 0, 0)),
        ],
        out_specs=pl.BlockSpec((None, None, None, nr, LANES), lambda t, i, j: (t, i, j, 0, 0)),
        out_shape=jax.ShapeDtypeStruct((two, b, g, nr, LANES), BF16),
        compiler_params=_cparams(3), name="nsa_compress",
    )(kr, pos, w1, w2p)


def _colmax(a):
    return jnp.max(a, axis=0, keepdims=True)


def _colsum(a):
    return jnp.sum(a, axis=0, keepdims=True)


class _SoftmaxPipe:
    def __init__(self, s_ref, pe_ref, acc_ref, hd):
        self.s, self.pe, self.acc, self.hd = s_ref, pe_ref, acc_ref, hd

    def start(self, first_scores, slot):
        cols = first_scores.shape[1]
        self.s[slot] = first_scores
        self.pe[1 - slot] = jnp.zeros(self.pe.shape[1:], self.pe.dtype)
        self.acc[...] = jnp.zeros(self.acc.shape, F32)
        return _colmax(first_scores), jnp.ones((1, cols), F32), jnp.full((1, cols), NEG_INF, F32)

    def step(self, slot, carry, v_prev, next_scores):
        cm_cur, alpha_prev, m_old = carry
        self.acc[...] = alpha_prev * self.acc[...] + _dot(v_prev, self.pe[1 - slot])
        cm_next = cm_cur
        if next_scores is not None:
            s_next = next_scores()
            self.s[1 - slot] = s_next
            cm_next = _colmax(s_next)
        m_new = jnp.maximum(m_old, cm_cur)
        self.pe[slot] = jnp.exp2(self.s[slot] - m_new).astype(self.pe.dtype)
        return cm_next, jnp.exp2(m_old - m_new), m_new

    def finish(self, slot_last, carry, v_last):
        _, alpha_prev, _ = carry
        acc = alpha_prev * self.acc[...] + _dot(v_last, self.pe[slot_last])
        return acc[0:self.hd] * (1.0 / acc[self.hd:self.hd + 1])

    def shifted_step(self, slot, v_prev, next_scores):
        self.pe[slot] = jnp.exp2(self.s[slot]).astype(self.pe.dtype)
        self.s[1 - slot] = next_scores()
        self.acc[...] += _dot(v_prev, self.pe[1 - slot])


def _nsa_kernel(qT_ref, kc_ref, vcT_ref, ov_ref, ksl_ref, vslT_ref, kw_ref, vwT_ref, cb_ref, tp_ref,
                gl_ref, o_ref, q0_ref, qa_ref, qs_ref, qw_ref, rank_ref, kst_ref, kwt_ref, vsa_ref, vwa_ref,
                s_ref, pe_ref, acc_ref, ws_ref, wpe_ref, wacc_ref, *, n_slc, top_k):
    T, R, HD = NSA_TQ, NSA_REP, NSA_HEAD_DIM
    nr = kc_ref.shape[0]
    per_tile = T // CMP_STRIDE
    front = nr - per_tile
    half = LANES // 2
    qi = pl.program_id(2)
    colmax, colsum = _colmax, _colsum

    @pl.when(qi == 0)
    def _():
        for src, dst in ((vslT_ref, vsa_ref), (vwT_ref, vwa_ref)):
            dst[0:HD, :] = src[...]
            dst[HD:, :] = jnp.ones((dst.shape[0] - HD, dst.shape[1]), dst.dtype)
        krow = lax.broadcasted_iota(jnp.int32, (T, LANES), 0)
        klane = lax.broadcasted_iota(jnp.int32, (T, LANES), 1)
        for t0 in range(0, ksl_ref.shape[0], T):
            blk = t0 // SLC_BLOCK + krow // SLC_BLOCK
            tag = jnp.where(klane - half == blk, 1.0, 0.0).astype(BF16)
            kst_ref[t0:t0 + T, :] = jnp.where(klane >= half, tag, ksl_ref[t0:t0 + T, :])
            kwt_ref[t0:t0 + T, :] = jnp.where(klane == half, jnp.ones((T, LANES), BF16), kw_ref[t0:t0 + T, :])

    qT = jnp.concatenate([qT_ref[r] for r in range(R)], axis=1)
    q0_ref[0:HD, :] = qT
    q0_ref[HD:2 * HD, :] = jnp.zeros((HD, R * T), BF16)
    qa_ref[0:HD, :] = qT

    n_win = WINDOW // T

    def win_offset(rel):
        return pl.multiple_of(jnp.maximum(qi - rel, 0) * T, T)

    def win_scores(rel):
        tile = TILE_EDGE if rel == n_win else (TILE_ZERO if rel >= 2 else (TILE_NEAR if rel == 1 else TILE_DIAG))
        if rel > 0:
            tile = jnp.where(qi >= rel, tile, TILE_NONE)
        return _dot(kw_ref[pl.ds(win_offset(rel), T), :], q0_ref[...]) + tp_ref[tile]

    def win_values(rel):
        return vwa_ref[:, pl.ds(win_offset(rel), T)]

    def exact_window():
        win = _SoftmaxPipe(ws_ref, wpe_ref, wacc_ref, HD)
        rels = list(range(n_win, -1, -1))
        carry = win.start(win_scores(rels[0]), 0)
        for i, rel in enumerate(rels):
            nxt = functools.partial(win_scores, rels[i + 1]) if i + 1 < len(rels) else None
            carry = win.step(i % 2, carry, win_values(rels[max(i - 1, 0)]), nxt)
        return win.finish((len(rels) - 1) % 2, carry, win_values(0))

    w_d = win_scores(0)
    w_shift = colmax(w_d).astype(BF16).astype(F32)
    qw_ref[0:HD, :] = qT
    first_row = lax.broadcasted_iota(jnp.int32, (HD, R * T), 0) == 0
    qw_ref[HD:2 * HD, :] = jnp.where(first_row, -w_shift, 0.0).astype(BF16)
    w_acc = _dot(win_values(0), jnp.exp2(w_d - w_shift).astype(BF16))
    for rel in range(1, n_win + 1):
        tile = TILE_EDGE if rel == n_win else (TILE_ZERO if rel >= 2 else TILE_NEAR)
        tile = jnp.where(qi >= rel, tile, TILE_NONE)
        w_s = _dot(kwt_ref[pl.ds(win_offset(rel), T), :], qw_ref[...]) + tp_ref[tile]
        w_acc = w_acc + _dot(win_values(rel), jnp.exp2(w_s).astype(BF16))
    w_den = w_acc[HD:HD + 1]
    w_unsafe = (jnp.sum(jnp.where(w_den < 2.0 ** 100, 0.0, 1.0))
                + jnp.sum(jnp.where(jnp.abs(w_acc[0:HD]) < 3.0e38, 0.0, 1.0))) > 0.0
    o_win = lax.cond(w_unsafe, exact_window, lambda: w_acc[0:HD] * (1.0 / w_den))

    off = pl.multiple_of(front - qi * per_tile, SUBLANES)
    lg = _dot(kc_ref[...], q0_ref[...]) + cb_ref[pl.ds(off, nr), :]
    m_c = colmax(lg)
    e = jnp.exp2(lg - m_c)
    p = e * jnp.where(m_c > 0.5 * NEG_INF, 1.0 / colsum(e), 0.0)
    o_cmp = _dot(vcT_ref[...], p.astype(BF16))

    psum = p[:, 0:T]
    for r in range(1, R):
        psum = psum + p[:, r * T:(r + 1) * T]
    imp = sum(_dot(ov_ref[...], pi) for pi in _split3(psum))
    nio = lax.broadcasted_iota(jnp.int32, (half, T), 0)
    tio = lax.broadcasted_iota(jnp.int32, (half, T), 1)
    tb = (qi * T + tio) // SLC_BLOCK
    forced = (nio == 0) | (nio == tb) | (nio == tb - 1)
    imp = jnp.where(forced, FORCE_SCORE, imp)
    imp = jnp.where(nio > tb, NEG_INF, imp)
    n_grp = half // SUBLANES
    grp = [imp[SUBLANES * v:SUBLANES * (v + 1)] for v in range(n_grp)]
    nio8 = lax.broadcasted_iota(jnp.int32, (SUBLANES, T), 0)
    rank_ref[...] = jnp.zeros(rank_ref.shape, F32)
    last_block = (qi * T + T - 1) // SLC_BLOCK
    for mg in range(pl.cdiv(n_slc, SUBLANES)):

        @pl.when(mg * SUBLANES <= last_block)
        def _():
            hits = [jnp.zeros((SUBLANES, T), F32) for _ in range(n_grp)]
            for mm in range(mg * SUBLANES, min((mg + 1) * SUBLANES, n_slc)):
                rowv = imp[mm:mm + 1, :]
                for v in range(n_grp):
                    if SUBLANES * v > mm:
                        hit = jnp.where(rowv >= grp[v], 1.0, 0.0)
                    elif SUBLANES * v + SUBLANES - 1 < mm:
                        hit = jnp.where(rowv > grp[v], 1.0, 0.0)
                    else:
                        hit = jnp.where(nio8 > mm - SUBLANES * v, jnp.where(rowv >= grp[v], 1.0, 0.0),
                                        jnp.where(rowv > grp[v], 1.0, 0.0))
                    hits[v] = hits[v] + hit
            rank_ref[...] += jnp.concatenate(hits, axis=0)

    rank = rank_ref[...]
    pen = jnp.where(rank < top_k, jnp.where(nio <= tb, 0.0, SEL_PENALTY), SEL_PENALTY).astype(BF16)
    qa_ref[HD:2 * HD, :] = jnp.concatenate([pen] * R, axis=1)

    def sel_scores(kj):
        return _dot(kst_ref[pl.ds(pl.multiple_of(kj * T, T), T), :], qa_ref[...])

    def sel_values(kj):
        return vsa_ref[:, pl.ds(pl.multiple_of(kj * T, T), T)]

    n_far = jnp.maximum(qi - 1, 0)
    near = jnp.maximum(qi - 1, 0)
    near_table = jnp.where(qi >= 1, TILE_NEAR, TILE_NONE)
    odd = n_far % 2
    sel = _SoftmaxPipe(s_ref, pe_ref, acc_ref, HD)

    def exact_selected():
        def far_step(kj, slot, carry):
            return sel.step(slot, carry, sel_values(jnp.maximum(kj - 1, 0)), functools.partial(sel_scores, kj + 1))

        carry = sel.start(sel_scores(0), odd)
        carry = lax.cond(odd == 1, lambda c: far_step(0, 1, c), lambda c: c, carry)
        carry = lax.fori_loop(
            0, n_far // 2, lambda j, c: far_step(odd + 2 * j + 1, 1, far_step(odd + 2 * j, 0, c)), carry)
        s_near = s_ref[0] + tp_ref[near_table]
        s_ref[0] = s_near
        carry = (colmax(s_near),) + tuple(carry[1:])
        carry = sel.step(0, carry, sel_values(jnp.maximum(n_far - 1, 0)),
                         lambda: sel_scores(qi) + tp_ref[TILE_DIAG])
        carry = sel.step(1, carry, sel_values(n_far), None)
        return sel.finish(1, carry, sel_values(qi))

    s_d = sel_scores(qi) + tp_ref[TILE_DIAG]
    shift = colmax(s_d).astype(BF16).astype(F32)
    qs_ref[0:HD, :] = qT
    qs_ref[HD:2 * HD, :] = (jnp.concatenate([pen.astype(F32)] * R, axis=1) - shift).astype(BF16)

    def shifted_scores(kj):
        return _dot(kst_ref[pl.ds(pl.multiple_of(kj * T, T), T), :], qs_ref[...])

    def far_shifted(kj, slot):
        sel.shifted_step(slot, sel_values(jnp.maximum(kj - 1, 0)), functools.partial(shifted_scores, kj + 1))

    s_ref[odd] = shifted_scores(0)
    pe_ref[1 - odd] = jnp.zeros(pe_ref.shape[1:], BF16)
    acc_ref[...] = _dot(sel_values(qi), jnp.exp2(s_d - shift).astype(BF16))

    @pl.when(odd == 1)
    def _():
        far_shifted(0, 1)

    def far_pair(j, carry):
        far_shifted(odd + 2 * j, 0)
        far_shifted(odd + 2 * j + 1, 1)
        return carry

    lax.fori_loop(0, n_far // 2, far_pair, 0)
    pe_near = jnp.exp2(s_ref[0] + tp_ref[near_table]).astype(BF16)
    acc = (acc_ref[...] + _dot(sel_values(jnp.maximum(n_far - 1, 0)), pe_ref[1])
           + _dot(sel_values(near), pe_near))
    denom = acc[HD:HD + 1]
    unsafe = (jnp.sum(jnp.where(denom < 2.0 ** 100, 0.0, 1.0))
              + jnp.sum(jnp.where(jnp.abs(acc[0:HD]) < 3.0e38, 0.0, 1.0))) > 0.0
    o_slc = lax.cond(unsafe, exact_selected, lambda: acc[0:HD] * (1.0 / denom))

    sg = _sigmoid(gl_ref[...])
    for r in range(R):
        cols = slice(r * T, (r + 1) * T)
        o = (sg[r:r + 1] * o_cmp[:, cols] + sg[R + r:R + r + 1] * o_slc[:, cols]
             + sg[2 * R + r:2 * R + r + 1] * o_win[:, cols])
        o_ref[r * HD:(r + 1) * HD, :] = o.astype(o_ref.dtype)


def _t5_bucket_np(dist):
    n = np.maximum(dist, 0)
    max_exact = REL_BUCKETS // 2
    nf = np.maximum(n, 1).astype(np.float32)
    large = max_exact + (np.log(nf / np.float32(max_exact)) / np.float32(math.log(REL_MAX_DIST / max_exact))
                         * np.float32(REL_BUCKETS - max_exact)).astype(np.int32)
    large = np.minimum(large, REL_BUCKETS - 1)
    return np.where(n < max_exact, n, large).astype(np.int32)


def _nsa_tables(rel_bias, nr):
    T, G, R = NSA_TQ, NSA_KV_HEADS, NSA_REP
    table = rel_bias.astype(F32) * LOG2E
    per_tile = T // CMP_STRIDE
    front = nr - per_tile
    t = np.arange(T)[None, :]

    def lay(a):
        keys = a.shape[0]
        return jnp.transpose(a, (2, 0, 1)).reshape(G, R, keys, T).transpose(0, 2, 1, 3).reshape(G, keys, R * T)

    cend = CMP_STRIDE * (np.arange(front + nr)[:, None] - front) + CMP_BLOCK - 1
    dist = t - cend
    def lookup(d):
        onehot = jax.nn.one_hot(_t5_bucket_np(d), REL_BUCKETS, dtype=F32)
        return jnp.einsum("ktb,bh->kth", onehot, table, precision=lax.Precision.HIGHEST)

    cb = jnp.where((dist >= 0)[..., None], lookup(dist), NEG_INF)
    k = np.arange(T)[:, None]
    assert 2 * T - (T - 1) >= REL_MAX_DIST and WINDOW % T == 0 and WINDOW // T >= 2
    last = table[REL_BUCKETS - 1]
    diag = jnp.where((t - k >= 0)[..., None], lookup(t - k) - last, NEG_INF)
    near = lookup(T + t - k) - last
    none = jnp.full_like(near, NEG_INF)
    edge = jnp.where((k > t)[..., None], jnp.zeros_like(near), NEG_INF)
    tiles = [None] * 5
    tiles[TILE_DIAG], tiles[TILE_NEAR], tiles[TILE_NONE], tiles[TILE_EDGE] = diag, near, none, edge
    tiles[TILE_ZERO] = jnp.zeros_like(near)
    tp = jnp.stack([lay(a) for a in tiles], axis=1)
    return lay(cb), tp


def _overlap_matrix(nr, n_slc):
    half = LANES // 2
    n = np.arange(half)[:, None]
    c = np.arange(nr)[None, :]
    ov = ((CMP_STRIDE * c <= SLC_BLOCK * n + SLC_BLOCK - 1) & (CMP_STRIDE * c + CMP_BLOCK - 1 >= SLC_BLOCK * n)
          & (n < n_slc) & (c < nr - 1))
    return jnp.asarray(ov.astype(np.float32), dtype=BF16)


def _nsa_attention(qT, kc, vcT, ksl, vslT, kw, vwT, cb, tp, glT):
    n_heads, hd, _ = qT.shape
    b, s, _ = ksl.shape
    G, R, T = NSA_KV_HEADS, NSA_REP, NSA_TQ
    nr = kc.shape[-2]
    nq = s // T
    n_slc = s // SLC_BLOCK
    assert n_slc <= LANES // 2 and s % T == 0 and nr % SUBLANES == 0
    ov = _overlap_matrix(nr, n_slc)
    kern = functools.partial(_nsa_kernel, n_slc=n_slc, top_k=min(SLC_TOPK, n_slc))
    k_spec = pl.BlockSpec((None, s, LANES), lambda i, j, k: (i, 0, j))
    vT_spec = pl.BlockSpec((None, hd, s), lambda i, j, k: (j, 0, i))
    ones_rows = 16
    pipe_scratch = [pltpu.VMEM((2, T, R * T), F32), pltpu.VMEM((2, T, R * T), BF16),
                    pltpu.VMEM((hd + ones_rows, R * T), F32)]
    value_scratch = [pltpu.VMEM((hd + ones_rows, s), BF16)] * 2
    return pl.pallas_call(
        kern, grid=(b, G, nq),
        in_specs=[
            pl.BlockSpec((R, hd, T), lambda i, j, k: (j, 0, i * nq + k)),
            pl.BlockSpec((None, None, nr, LANES), lambda i, j, k: (i, j, 0, 0)),
            pl.BlockSpec((None, None, hd, nr), lambda i, j, k: (i, j, 0, 0)),
            pl.BlockSpec(ov.shape, lambda i, j, k: (0, 0)),
            k_spec, vT_spec, k_spec, vT_spec,
            pl.BlockSpec((None,) + cb.shape[1:], lambda i, j, k: (j, 0, 0)),
            pl.BlockSpec((None,) + tp.shape[1:], lambda i, j, k: (j, 0, 0, 0)),
            pl.BlockSpec((None, 16, T), lambda i, j, k: (j, 0, i * nq + k)),
        ],
        out_specs=pl.BlockSpec((R * hd, T), lambda i, j, k: (j, i * nq + k)),
        out_shape=jax.ShapeDtypeStruct((n_heads * hd, b * s), BF16),
        scratch_shapes=[pltpu.VMEM((2 * hd, R * T), BF16)] * 4
        + [pltpu.VMEM((LANES // 2, T), F32)] + [pltpu.VMEM((s, LANES), BF16)] * 2
        + value_scratch + pipe_scratch + pipe_scratch,
        compiler_params=_cparams(3), name="nsa_attention",
    )(qT, kc, vcT, ov, ksl, vslT, kw, vwT, cb, tp, glT)


def _merge_kernel(x_ref, ys_ref, ynT_ref, mg_ref, ws_ref, wn_ref, wo_ref, g_ref, o_ref):
    d = x_ref.shape[1]
    a = _dot(ys_ref[...], ws_ref[...])
    bb = lax.dot_general(ynT_ref[...], wn_ref[...], (((0,), (0,)), ((), ())), preferred_element_type=F32)
    mg = _sigmoid(mg_ref[...].astype(F32))
    mixed = mg[:, :d] * a + mg[:, d:] * bb
    o = _dot(mixed.astype(BF16), wo_ref[...])
    o_ref[...] = x_ref[...] + _rms(o, g_ref[...])


def _merge(x2d, y_ssd, y_nsaT, mg, w_s, w_n, w_o, g, tm=512):
    m, d = x2d.shape
    tm = min(tm, m)
    row = lambda a: pl.BlockSpec((tm, a.shape[1]), lambda i: (i, 0))
    full = lambda a: pl.BlockSpec(a.shape, lambda i: (0, 0))
    g = g.reshape(1, d)
    return pl.pallas_call(
        _merge_kernel, grid=(m // tm,),
        in_specs=[row(x2d), row(y_ssd), pl.BlockSpec((y_nsaT.shape[0], tm), lambda i: (0, i)), row(mg),
                  full(w_s), full(w_n), full(w_o), full(g)],
        out_specs=pl.BlockSpec((tm, d), lambda i: (i, 0)),
        out_shape=jax.ShapeDtypeStruct((m, d), F32),
        compiler_params=_cparams(1), name="merge",
    )(x2d, y_ssd, y_nsaT, mg, w_s, w_n, w_o, g)


def _xattn_kernel(x_ref, kv_ref, gpre_ref, wq_ref, wo_ref, gpost_ref, o_ref):
    x = x_ref[...]
    h = _rms(x, gpre_ref[...]).astype(BF16)
    q = _dot(h, wq_ref[...])
    width = X_HEADS * X_HEAD_DIM
    scale = X_HEAD_DIM ** -0.5
    outs = []
    for hh in range(X_HEADS):
        sl = slice(hh * X_HEAD_DIM, (hh + 1) * X_HEAD_DIM)
        lg = _dot_nt(q[:, sl].astype(BF16), kv_ref[:, sl]) * scale
        e = jnp.exp(lg - jnp.max(lg, axis=-1, keepdims=True))
        p = e / jnp.sum(e, axis=-1, keepdims=True)
        outs.append(_dot(p.astype(BF16), kv_ref[:, width + hh * X_HEAD_DIM:width + (hh + 1) * X_HEAD_DIM]))
    o = _dot(jnp.concatenate(outs, axis=-1).astype(BF16), wo_ref[...])
    o_ref[...] = x + _rms(o, gpost_ref[...])


def _xattn(x3d, kv, g_pre, w_q, w_o, g_post, tm=512):
    b, s, d = x3d.shape
    tm = min(tm, s)
    ml = kv.shape[1]
    full = lambda a: pl.BlockSpec(a.shape, lambda i, j: (0, 0))
    g_pre, g_post = g_pre.reshape(1, d), g_post.reshape(1, d)
    return pl.pallas_call(
        _xattn_kernel, grid=(b, s // tm),
        in_specs=[pl.BlockSpec((None, tm, d), lambda i, j: (i, j, 0)),
                  pl.BlockSpec((None, ml, kv.shape[2]), lambda i, j: (i, 0, 0)),
                  full(g_pre), full(w_q), full(w_o), full(g_post)],
        out_specs=pl.BlockSpec((None, tm, d), lambda i, j: (i, j, 0)),
        out_shape=jax.ShapeDtypeStruct((b, s, d), F32),
        compiler_params=_cparams(2), name="xattn",
    )(x3d, kv, g_pre, w_q, w_o, g_post)


def _mlp_kernel(x_ref, gpre_ref, w1_ref, w2_ref, gpost_ref, o_ref, *, f_chunk):
    x = x_ref[...]
    h = _rms(x, gpre_ref[...]).astype(BF16)
    d_ff = w1_ref.shape[1]
    acc = jnp.zeros(x.shape, F32)
    for f0 in range(0, d_ff, f_chunk):
        u = jnp.maximum(_dot(h, w1_ref[:, f0:f0 + f_chunk]), 0.0)
        acc = acc + _dot((u * u).astype(BF16), w2_ref[f0:f0 + f_chunk, :])
    o_ref[...] = x + _rms(acc, gpost_ref[...])


def _mlp(x2d, g_pre, w1, w2, g_post, tm=512):
    m, d = x2d.shape
    tm = min(tm, m)
    full = lambda a: pl.BlockSpec(a.shape, lambda i: (0, 0))
    g_pre, g_post = g_pre.reshape(1, d), g_post.reshape(1, d)
    return pl.pallas_call(
        functools.partial(_mlp_kernel, f_chunk=1024), grid=(m // tm,),
        in_specs=[pl.BlockSpec((tm, d), lambda i: (i, 0)), full(g_pre), full(w1), full(w2), full(g_post)],
        out_specs=pl.BlockSpec((tm, d), lambda i: (i, 0)),
        out_shape=jax.ShapeDtypeStruct((m, d), F32),
        compiler_params=_cparams(1), name="mlp",
    )(x2d, g_pre, w1, w2, g_post)


def _pad_heads(w, n_heads, hd):
    k = w.shape[0]
    w = w.reshape(k, n_heads, hd)
    return jnp.pad(w, ((0, 0), (0, 0), (0, LANES - hd))).reshape(k, n_heads * LANES)


def _layer(x, mem, w_in, conv_w, conv_b, dt_bias, a_log, d_skip, ssd_norm, cmp_pos, cmp_w1, cmp_w2,
           rel_bias, w_br_ssd, w_br_nsa, w_out, w_xq, w_xkv, w_xo, w_ff1, w_ff2,
           n_mix_pre, n_mix_post, n_x_pre, n_x_post, n_mem, n_ffn_pre, n_ffn_post):
    b, s, d = x.shape
    G, R, HD = NSA_KV_HEADS, NSA_REP, NSA_HEAD_DIM
    d_inner = 2 * d
    n_ssd_heads = d_inner // SSD_HEAD_DIM
    conv_dim = d_inner + 2 * SSD_GROUPS * SSD_STATE
    nsa_w, kv_w = NSA_HEADS * HD, G * HD
    sizes = (d_inner, conv_dim, n_ssd_heads, nsa_w, 6 * kv_w, 3 * NSA_HEADS, 2 * d)
    offs = np.concatenate([[0], np.cumsum(sizes)])
    seg = lambda i: w_in[:, offs[i]:offs[i + 1]]
    w_z, w_xbc, w_dt, w_q, w_kv, w_gate, w_mg = (seg(i) for i in range(7))
    w_small = jnp.pad(jnp.concatenate([w_dt, w_gate], axis=1),
                      ((0, 0), (0, LANES - n_ssd_heads - 3 * NSA_HEADS)))
    w_kv6 = w_kv.reshape(d, 6, kv_w)
    w_cmp = jnp.concatenate([w_kv6[:, 0], w_kv6[:, 1]], axis=1)
    bf = lambda a: a.astype(BF16)

    x2d = x.reshape(b * s, d)
    z, xbc = _norm_matmul(x2d, n_mix_pre, [bf(w_z), bf(w_xbc)], [BF16, BF16])
    w_gl = jnp.transpose(w_gate.reshape(d, 3, G, R), (2, 1, 3, 0)).reshape(G, 3 * R, d)
    w_gl = jnp.pad(w_gl, ((0, 0), (0, 16 - 3 * R), (0, 0))).reshape(G * 16, d)
    small, dtT, glT, qT, kv_cmp, vslT, vwT = _norm_matmul(
        x2d, n_mix_pre,
        [bf(w_small), bf(w_dt.T), bf(w_gl), bf(w_q.T * (HD ** -0.5 * LOG2E)), bf(w_cmp),
         bf(w_kv6[:, 3].T), bf(w_kv6[:, 5].T)],
        [F32, F32, F32, BF16, BF16, BF16, BF16],
        transposed=(False, True, True, True, False, True, True))
    ksl, kw, mg = _norm_matmul(
        x2d, n_mix_pre, [bf(_pad_heads(w_kv6[:, 2], G, HD)), bf(_pad_heads(w_kv6[:, 4], G, HD)), bf(w_mg)],
        [BF16] * 3)

    y_ssd = _ssd_mixer(z.reshape(b, s, -1), xbc.reshape(b, s, -1), small.reshape(b, s, -1), dtT,
                       conv_w, conv_b, dt_bias, a_log, d_skip, ssd_norm)

    nr = s // CMP_STRIDE
    kr = kv_cmp.reshape(b, nr, CMP_STRIDE, 2, G, HD)
    kr = jnp.transpose(kr, (3, 0, 4, 1, 2, 5)).reshape(2, b, G, nr, CMP_STRIDE * HD)
    pos = jnp.broadcast_to(bf(cmp_pos).reshape(2, 1, CMP_BLOCK * HD), (2, SUBLANES, CMP_BLOCK * HD))
    w2p = jnp.pad(bf(cmp_w2), ((0, 0), (0, 0), (0, LANES - HD)))
    kvc = _compress(kr, pos, bf(cmp_w1), w2p)
    vcT = jnp.swapaxes(kvc[1][..., :HD], -1, -2)
    cb, tp = _nsa_tables(rel_bias, nr)
    y_nsa = _nsa_attention(qT.reshape(NSA_HEADS, HD, b * s), kvc[0], vcT,
                           ksl.reshape(b, s, -1), vslT.reshape(G, HD, b * s),
                           kw.reshape(b, s, -1), vwT.reshape(G, HD, b * s), cb, tp,
                           glT.reshape(G, 16, b * s))

    x1 = _merge(x2d, y_ssd.reshape(b * s, -1), y_nsa, mg, bf(w_br_ssd), bf(w_br_nsa), bf(w_out), n_mix_post)

    (kv_mem,) = _norm_matmul(mem.reshape(-1, d), n_mem, [bf(w_xkv)], [BF16])
    x2 = _xattn(x1.reshape(b, s, d), kv_mem.reshape(b, mem.shape[1], -1), n_x_pre, bf(w_xq), bf(w_xo),
                n_x_post)

    x3 = _mlp(x2.reshape(b * s, d), n_ffn_pre, bf(w_ff1), bf(w_ff2), n_ffn_post)
    return x3.reshape(b, s, d)


def kernel(x, mem, w_in, ssd_conv_w, ssd_conv_b, ssd_dt_bias, ssd_a_log, ssd_d_skip, ssd_norm, cmp_pos,
           cmp_w1, cmp_w2, rel_bias, w_br_ssd, w_br_nsa, w_out, w_xq, w_xkv, w_xo, w_ff1, w_ff2,
           norm_mix_pre, norm_mix_post, norm_x_pre, norm_x_post, norm_mem, norm_ffn_pre, norm_ffn_post):
    for l in range(w_in.shape[0]):
        x = _layer(x, mem, w_in[l], ssd_conv_w[l], ssd_conv_b[l], ssd_dt_bias[l], ssd_a_log[l],
                   ssd_d_skip[l], ssd_norm[l], cmp_pos[l], cmp_w1[l], cmp_w2[l], rel_bias,
                   w_br_ssd[l], w_br_nsa[l], w_out[l], w_xq[l], w_xkv[l], w_xo[l], w_ff1[l], w_ff2[l],
                   norm_mix_pre[l], norm_mix_post[l], norm_x_pre[l], norm_x_post[l], norm_mem[l],
                   norm_ffn_pre[l], norm_ffn_post[l])
    return x
```

```python
import functools
import math

import numpy as np
import jax
import jax.numpy as jnp
from jax import lax
from jax.experimental import pallas as pl
from jax.experimental.pallas import tpu as pltpu

F32 = jnp.float32
BF16 = jnp.bfloat16

NORM_EPS = 1e-6
NEG_INF = -1e30
FORCE_SCORE = 1e9
SEL_PENALTY = NEG_INF

LANES = 128
SUBLANES = 8
VMEM_LIMIT = 56 * 1024 * 1024

SSD_HEAD_DIM = 64
SSD_GROUPS = 8
SSD_STATE = 128
SSD_CONV = 4
SSD_CHUNK = 128
NSA_HEADS = 16
NSA_KV_HEADS = 4
NSA_HEAD_DIM = 64
NSA_REP = NSA_HEADS // NSA_KV_HEADS
CMP_BLOCK = 32
CMP_STRIDE = 16
SLC_BLOCK = 64
SLC_TOPK = 16
WINDOW = 512
NSA_TQ = 256
LOG2E = math.log2(math.e)
REL_BUCKETS = 32
REL_MAX_DIST = 128
X_HEADS = 4
X_HEAD_DIM = 128

TILE_DIAG, TILE_NEAR, TILE_NONE, TILE_EDGE, TILE_ZERO = range(5)


def _cparams(n_grid):
    return pltpu.CompilerParams(dimension_semantics=("arbitrary",) * n_grid,
                                vmem_limit_bytes=VMEM_LIMIT)


def _sigmoid(x):
    return jax.nn.sigmoid(x)


def _rms(x, g):
    return x * lax.rsqrt(jnp.mean(x * x, axis=-1, keepdims=True) + NORM_EPS) * g


def _dot(a, b):
    return jnp.dot(a, b, preferred_element_type=F32)


def _dot_nt(a, b):
    return lax.dot_general(a, b, (((1,), (1,)), ((), ())), preferred_element_type=F32)


def _split3(x):
    x1 = x.astype(BF16)
    r1 = x - x1.astype(F32)
    x2 = r1.astype(BF16)
    x3 = (r1 - x2.astype(F32)).astype(BF16)
    return x1, x2, x3


def _norm_matmul_kernel(x_ref, g_ref, *refs, transposed, n_chunk):
    n_out = len(transposed)
    w_refs, o_refs = refs[:n_out], refs[n_out:]
    h = _rms(x_ref[...], g_ref[...]).astype(BF16)
    for w_ref, o_ref, tr in zip(w_refs, o_refs, transposed):
        if tr:
            o_ref[...] = _dot_nt(w_ref[...], h).astype(o_ref.dtype)
            continue
        n = w_ref.shape[1]
        step = min(n, n_chunk)
        for n0 in range(0, n, step):
            o_ref[:, n0:n0 + step] = _dot(h, w_ref[:, n0:n0 + step]).astype(o_ref.dtype)


def _norm_matmul(x2d, g, ws, out_dtypes, transposed=None, tm=512):
    m, k = x2d.shape
    tm = min(tm, m)
    transposed = tuple(transposed) if transposed is not None else (False,) * len(ws)
    in_specs = [pl.BlockSpec((tm, k), lambda i: (i, 0)), pl.BlockSpec((1, k), lambda i: (0, 0))]
    in_specs += [pl.BlockSpec(w.shape, lambda i: (0, 0)) for w in ws]
    out_specs, out_shape = [], []
    for w, dt, tr in zip(ws, out_dtypes, transposed):
        if tr:
            out_specs.append(pl.BlockSpec((w.shape[0], tm), lambda i: (0, i)))
            out_shape.append(jax.ShapeDtypeStruct((w.shape[0], m), dt))
        else:
            out_specs.append(pl.BlockSpec((tm, w.shape[1]), lambda i: (i, 0)))
            out_shape.append(jax.ShapeDtypeStruct((m, w.shape[1]), dt))
    return pl.pallas_call(
        functools.partial(_norm_matmul_kernel, transposed=transposed, n_chunk=1024),
        grid=(m // tm,), in_specs=in_specs, out_specs=out_specs, out_shape=out_shape,
        compiler_params=_cparams(1), name="norm_proj",
    )(x2d, g.reshape(1, k), *ws)


def _softplus(x):
    return jnp.maximum(x, 0.0) + jnp.log(1.0 + jnp.exp(-jnp.abs(x)))


def _ssd_kernel(xbc_ref, prev_ref, z_ref, dt_ref, dtT_ref, cw_ref, cb_ref, dtb_ref, dtbT_ref,
                alog_ref, alogT_ref, dskip_ref, nw_ref, y_ref, state_ref, xc_ref, ybuf_ref,
                *, n_heads, d_inner):
    L, P, N, G = SSD_CHUNK, SSD_HEAD_DIM, SSD_STATE, SSD_GROUPS
    conv_dim = xc_ref.shape[1]
    c = pl.program_id(1)

    @pl.when(c == 0)
    def _():
        state_ref[...] = jnp.zeros_like(state_ref)

    cw = 512
    prev_rows = prev_ref.shape[0]
    row = lax.broadcasted_iota(jnp.int32, (prev_rows, cw), 0)
    for j in range(conv_dim // cw):
        sl = slice(j * cw, (j + 1) * cw)
        cur = xbc_ref[:, sl].astype(F32)
        prev = jnp.where(c == 0, 0.0, prev_ref[:, sl].astype(F32))
        acc = cb_ref[:, sl] + cw_ref[SSD_CONV - 1:SSD_CONV, sl] * cur
        for k in range(1, SSD_CONV):
            rc = pltpu.roll(cur, k, 0)
            rp = pltpu.roll(prev, k, 0)
            head = jnp.where(row < k, rp, rc[:prev_rows])
            shifted = jnp.concatenate([head, rc[prev_rows:]], axis=0)
            acc = acc + cw_ref[SSD_CONV - 1 - k:SSD_CONV - k, sl] * shifted
        xc_ref[:, sl] = acc * _sigmoid(acc)

    dt = _softplus(dt_ref[:, 0:n_heads] + dtb_ref[...])
    a = dt * (-LOG2E * jnp.exp(alog_ref[...]))
    dtT = _softplus(dtT_ref[...] + dtbT_ref[...])
    aT = dtT * (-LOG2E * jnp.exp(alogT_ref[...]))
    ri = lax.broadcasted_iota(jnp.int32, (L, L), 0)
    ci = lax.broadcasted_iota(jnp.int32, (L, L), 1)
    causal = ci <= ri
    tri = jnp.where(causal, 1.0, 0.0).astype(BF16)
    triu = jnp.where(ri <= ci, 1.0, 0.0).astype(BF16)
    a_cs = sum(_dot(tri, ai) for ai in _split3(a))
    a_csT = sum(_dot(ai, triu) for ai in _split3(aT))
    lane = lax.broadcasted_iota(jnp.int32, (L, 2 * P), 1)
    lo_half = lane < P
    lane1 = lax.broadcasted_iota(jnp.int32, (1, 2 * P), 1) < P

    heads_per_group = n_heads // G
    for g in range(G):
        b_g = xc_ref[:, d_inner + g * N:d_inner + (g + 1) * N]
        c_g = xc_ref[:, d_inner + G * N + g * N:d_inner + G * N + (g + 1) * N]
        c_gb = c_g.astype(BF16)
        cb = _dot_nt(c_gb, b_g.astype(BF16))
        b_gT = b_g.T.astype(BF16)
        for pp in range(heads_per_group // 2):
            pair = g * (heads_per_group // 2) + pp
            h0, h1 = 2 * pair, 2 * pair + 1
            xs = xc_ref[:, pair * 2 * P:(pair + 1) * 2 * P]
            dt_pair = jnp.where(lo_half, dt[:, h0:h0 + 1], dt[:, h1:h1 + 1])
            xd = xs * dt_pair
            xd_b = xd.astype(BF16)
            ys = []
            for h in (h0, h1):
                seg = jnp.where(causal, a_cs[:, h:h + 1] - a_csT[h:h + 1, :], NEG_INF)
                mat = (cb * jnp.exp2(seg)).astype(BF16)
                ys.append(_dot(mat, xd_b))
            y = jnp.where(lo_half, ys[0], ys[1])
            cs_pair = jnp.where(lo_half, a_cs[:, h0:h0 + 1], a_cs[:, h1:h1 + 1])
            st = state_ref[pair]
            y = y + _dot(c_gb, st.astype(BF16)) * jnp.exp2(cs_pair)
            tot = jnp.where(lane1, a_csT[h0:h0 + 1, L - 1:L], a_csT[h1:h1 + 1, L - 1:L])
            xdd = (xd * jnp.exp2(tot - cs_pair)).astype(BF16)
            state_ref[pair] = st * jnp.exp2(tot) + _dot(b_gT, xdd)
            ybuf_ref[:, pair * 2 * P:(pair + 1) * 2 * P] = y + xs * dskip_ref[:, pair * 2 * P:(pair + 1) * 2 * P]

    gw = d_inner // G
    for g in range(G):
        sl = slice(g * gw, (g + 1) * gw)
        zz = z_ref[:, sl].astype(F32)
        yg = ybuf_ref[:, sl] * (zz * _sigmoid(zz))
        y_ref[:, sl] = _rms(yg, nw_ref[:, sl]).astype(y_ref.dtype)


def _ssd_mixer(z, xbc, small, dtT, conv_w, conv_b, dt_bias, a_log, d_skip, norm_w):
    b, s, d_inner = z.shape
    conv_dim = xbc.shape[-1]
    n_heads = d_inner // SSD_HEAD_DIM
    L = SSD_CHUNK
    nc = s // L
    prev_rows = 16
    xbc_prev = xbc.reshape(b, s // prev_rows, prev_rows, conv_dim)
    blocks_per_chunk = L // prev_rows
    kern = functools.partial(_ssd_kernel, n_heads=n_heads, d_inner=d_inner)
    const = lambda shape: pl.BlockSpec(shape, lambda i, j: (0,) * len(shape))
    return pl.pallas_call(
        kern, grid=(b, nc),
        in_specs=[
            pl.BlockSpec((None, L, conv_dim), lambda i, j: (i, j, 0)),
            pl.BlockSpec((None, None, prev_rows, conv_dim),
                         lambda i, j: (i, jnp.maximum(j * blocks_per_chunk - 1, 0), 0, 0)),
            pl.BlockSpec((None, L, d_inner), lambda i, j: (i, j, 0)),
            pl.BlockSpec((None, L, small.shape[-1]), lambda i, j: (i, j, 0)),
            pl.BlockSpec((n_heads, L), lambda i, j: (0, i * nc + j)),
            const((SSD_CONV, conv_dim)), const((1, conv_dim)),
            const((1, n_heads)), const((n_heads, 1)),
            const((1, n_heads)), const((n_heads, 1)),
            const((1, d_inner)), const((1, d_inner)),
        ],
        out_specs=pl.BlockSpec((None, L, d_inner), lambda i, j: (i, j, 0)),
        out_shape=jax.ShapeDtypeStruct((b, s, d_inner), BF16),
        scratch_shapes=[pltpu.VMEM((n_heads // 2, SSD_STATE, 2 * SSD_HEAD_DIM), F32),
                        pltpu.VMEM((L, conv_dim), F32),
                        pltpu.VMEM((L, d_inner), F32)],
        compiler_params=_cparams(2), name="ssd",
    )(xbc, xbc_prev, z, small, dtT, conv_w.astype(F32), conv_b.reshape(1, -1).astype(F32),
      dt_bias.reshape(1, -1), dt_bias.reshape(-1, 1), a_log.reshape(1, -1), a_log.reshape(-1, 1),
      jnp.repeat(d_skip, SSD_HEAD_DIM).reshape(1, -1), norm_w.reshape(1, -1))


def _compress_kernel(kr_ref, pos_ref, w1_ref, w2_ref, o_ref):
    nr, half = kr_ref.shape
    kr = kr_ref[...]
    p1 = _dot(kr, w1_ref[0:half, :])
    p2 = _dot(kr, w1_ref[half:2 * half, :])
    pb = _dot(pos_ref[...], w1_ref[...])[0:1]
    hid = p1 + pltpu.roll(p2, nr - 1, 0) + pb
    hid = hid * _sigmoid(hid)
    out = _dot(hid.astype(BF16), w2_ref[...])
    rows = lax.broadcasted_iota(jnp.int32, out.shape, 0)
    o_ref[...] = jnp.where(rows < nr - 1, out, 0.0).astype(o_ref.dtype)


def _compress(kr, pos, w1, w2p):
    two, b, g, nr, half = kr.shape
    hidden = w1.shape[-1]
    return pl.pallas_call(
        _compress_kernel,
        grid=(two, b, g),
        in_specs=[
            pl.BlockSpec((None, None, None, nr, half), lambda t, i, j: (t, i, j, 0, 0)),
            pl.BlockSpec((None, SUBLANES, 2 * half), lambda t, i, j: (t, 0, 0)),
            pl.BlockSpec((None, 2 * half, hidden), lambda t, i, j: (t, 0, 0)),
            pl.BlockSpec((None, hidden, LANES), lambda t, i, j: (t, 0, 0)),
        ],
        out_specs=pl.BlockSpec((None, None, None, nr, LANES), lambda t, i, j: (t, i, j, 0, 0)),
        out_shape=jax.ShapeDtypeStruct((two, b, g, nr, LANES), BF16),
        compiler_params=_cparams(3), name="nsa_compress",
    )(kr, pos, w1, w2p)


def _colmax(a):
    return jnp.max(a, axis=0, keepdims=True)


def _colsum(a):
    return jnp.sum(a, axis=0, keepdims=True)


class _SoftmaxPipe:
    def __init__(self, s_ref, pe_ref, acc_ref, hd):
        self.s, self.pe, self.acc, self.hd = s_ref, pe_ref, acc_ref, hd

    def start(self, first_scores, slot):
        cols = first_scores.shape[1]
        self.s[slot] = first_scores
        self.pe[1 - slot] = jnp.zeros(self.pe.shape[1:], self.pe.dtype)
        self.acc[...] = jnp.zeros(self.acc.shape, F32)
        return _colmax(first_scores), jnp.ones((1, cols), F32), jnp.full((1, cols), NEG_INF, F32)

    def step(self, slot, carry, v_prev, next_scores):
        cm_cur, alpha_prev, m_old = carry
        self.acc[...] = alpha_prev * self.acc[...] + _dot(v_prev, self.pe[1 - slot])
        cm_next = cm_cur
        if next_scores is not None:
            s_next = next_scores()
            self.s[1 - slot] = s_next
            cm_next = _colmax(s_next)
        m_new = jnp.maximum(m_old, cm_cur)
        self.pe[slot] = jnp.exp2(self.s[slot] - m_new).astype(self.pe.dtype)
        return cm_next, jnp.exp2(m_old - m_new), m_new

    def finish(self, slot_last, carry, v_last):
        _, alpha_prev, _ = carry
        acc = alpha_prev * self.acc[...] + _dot(v_last, self.pe[slot_last])
        return acc[0:self.hd] * (1.0 / acc[self.hd:self.hd + 1])

    def shifted_step(self, slot, v_prev, next_scores):
        self.pe[slot] = jnp.exp2(self.s[slot]).astype(self.pe.dtype)
        if next_scores is not None:
            self.s[1 - slot] = next_scores()
        self.acc[...] += _dot(v_prev, self.pe[1 - slot])


def _nsa_kernel(qT_ref, kc_ref, vcT_ref, ov_ref, ksl_ref, vslT_ref, kw_ref, vwT_ref, cb_ref, tp_ref,
                gl_ref, o_ref, q0_ref, qa_ref, qs_ref, qw_ref, rank_ref, kst_ref, kwt_ref, vsa_ref, vwa_ref,
                s_ref, pe_ref, acc_ref, ws_ref, wpe_ref, wacc_ref, *, n_slc, top_k):
    T, R, HD = NSA_TQ, NSA_REP, NSA_HEAD_DIM
    nr = kc_ref.shape[0]
    per_tile = T // CMP_STRIDE
    front = nr - per_tile
    half = LANES // 2
    qi = pl.program_id(2)
    colmax, colsum = _colmax, _colsum

    @pl.when(qi == 0)
    def _():
        for src, dst in ((vslT_ref, vsa_ref), (vwT_ref, vwa_ref)):
            dst[0:HD, :] = src[...]
            dst[HD:, :] = jnp.ones((dst.shape[0] - HD, dst.shape[1]), dst.dtype)
        krow = lax.broadcasted_iota(jnp.int32, (T, LANES), 0)
        klane = lax.broadcasted_iota(jnp.int32, (T, LANES), 1)
        for t0 in range(0, ksl_ref.shape[0], T):
            blk = t0 // SLC_BLOCK + krow // SLC_BLOCK
            tag = jnp.where(klane - half == blk, 1.0, 0.0).astype(BF16)
            kst_ref[t0:t0 + T, :] = jnp.where(klane >= half, tag, ksl_ref[t0:t0 + T, :])
            kwt_ref[t0:t0 + T, :] = jnp.where(klane == half, jnp.ones((T, LANES), BF16), kw_ref[t0:t0 + T, :])

    qT = jnp.concatenate([qT_ref[r] for r in range(R)], axis=1)
    q0_ref[0:HD, :] = qT
    q0_ref[HD:2 * HD, :] = jnp.zeros((HD, R * T), BF16)
    qa_ref[0:HD, :] = qT

    n_win = WINDOW // T

    def win_offset(rel):
        return pl.multiple_of(jnp.maximum(qi - rel, 0) * T, T)

    def win_scores(rel):
        tile = TILE_EDGE if rel == n_win else (TILE_ZERO if rel >= 2 else (TILE_NEAR if rel == 1 else TILE_DIAG))
        if rel > 0:
            tile = jnp.where(qi >= rel, tile, TILE_NONE)
        return _dot(kw_ref[pl.ds(win_offset(rel), T), :], q0_ref[...]) + tp_ref[tile]

    def win_values(rel):
        return vwa_ref[:, pl.ds(win_offset(rel), T)]

    win = _SoftmaxPipe(ws_ref, wpe_ref, wacc_ref, HD)

    def exact_window():
        rels = list(range(n_win, -1, -1))
        carry = win.start(win_scores(rels[0]), 0)
        for i, rel in enumerate(rels):
            nxt = functools.partial(win_scores, rels[i + 1]) if i + 1 < len(rels) else None
            carry = win.step(i % 2, carry, win_values(rels[max(i - 1, 0)]), nxt)
        return win.finish((len(rels) - 1) % 2, carry, win_values(0))

    w_d = win_scores(0)
    w_shift = colmax(w_d).astype(BF16).astype(F32)
    qw_ref[0:HD, :] = qT
    first_row = lax.broadcasted_iota(jnp.int32, (HD, R * T), 0) == 0
    qw_ref[HD:2 * HD, :] = jnp.where(first_row, -w_shift, 0.0).astype(BF16)

    def shifted_win_scores(rel):
        tile = TILE_EDGE if rel == n_win else (TILE_ZERO if rel >= 2 else TILE_NEAR)
        tile = jnp.where(qi >= rel, tile, TILE_NONE)
        return _dot(kwt_ref[pl.ds(win_offset(rel), T), :], qw_ref[...]) + tp_ref[tile]

    ws_ref[0] = shifted_win_scores(1)
    wpe_ref[1] = jnp.zeros(wpe_ref.shape[1:], BF16)
    wacc_ref[...] = _dot(win_values(0), jnp.exp2(w_d - w_shift).astype(BF16))
    for i, rel in enumerate(range(1, n_win + 1)):
        nxt = functools.partial(shifted_win_scores, rel + 1) if rel < n_win else None
        win.shifted_step(i % 2, win_values(max(rel - 1, 1)), nxt)
    w_acc = wacc_ref[...] + _dot(win_values(n_win), wpe_ref[(n_win - 1) % 2])
    w_den = w_acc[HD:HD + 1]
    w_unsafe = (jnp.sum(jnp.where(w_den < 2.0 ** 100, 0.0, 1.0))
                + jnp.sum(jnp.where(jnp.abs(w_acc[0:HD]) < 3.0e38, 0.0, 1.0))) > 0.0
    o_win = lax.cond(w_unsafe, exact_window, lambda: w_acc[0:HD] * (1.0 / w_den))

    off = pl.multiple_of(front - qi * per_tile, SUBLANES)
    lg = _dot(kc_ref[...], q0_ref[...]) + cb_ref[pl.ds(off, nr), :]
    m_c = colmax(lg)
    e = jnp.exp2(lg - m_c)
    p = e * jnp.where(m_c > 0.5 * NEG_INF, 1.0 / colsum(e), 0.0)
    o_cmp = _dot(vcT_ref[...], p.astype(BF16))

    psum = p[:, 0:T]
    for r in range(1, R):
        psum = psum + p[:, r * T:(r + 1) * T]
    imp = sum(_dot(ov_ref[...], pi) for pi in _split3(psum))
    nio = lax.broadcasted_iota(jnp.int32, (half, T), 0)
    tio = lax.broadcasted_iota(jnp.int32, (half, T), 1)
    tb = (qi * T + tio) // SLC_BLOCK
    forced = (nio == 0) | (nio == tb) | (nio == tb - 1)
    imp = jnp.where(forced, FORCE_SCORE, imp)
    imp = jnp.where(nio > tb, NEG_INF, imp)
    n_grp = half // SUBLANES
    grp = [imp[SUBLANES * v:SUBLANES * (v + 1)] for v in range(n_grp)]
    nio8 = lax.broadcasted_iota(jnp.int32, (SUBLANES, T), 0)
    rank_ref[...] = jnp.zeros(rank_ref.shape, F32)
    last_block = (qi * T + T - 1) // SLC_BLOCK
    for mg in range(pl.cdiv(n_slc, SUBLANES)):

        @pl.when(mg * SUBLANES <= last_block)
        def _():
            hits = [jnp.zeros((SUBLANES, T), F32) for _ in range(n_grp)]
            for mm in range(mg * SUBLANES, min((mg + 1) * SUBLANES, n_slc)):
                rowv = imp[mm:mm + 1, :]
                for v in range(n_grp):
                    if SUBLANES * v > mm:
                        hit = jnp.where(rowv >= grp[v], 1.0, 0.0)
                    elif SUBLANES * v + SUBLANES - 1 < mm:
                        hit = jnp.where(rowv > grp[v], 1.0, 0.0)
                    else:
                        hit = jnp.where(nio8 > mm - SUBLANES * v, jnp.where(rowv >= grp[v], 1.0, 0.0),
                                        jnp.where(rowv > grp[v], 1.0, 0.0))
                    hits[v] = hits[v] + hit
            rank_ref[...] += jnp.concatenate(hits, axis=0)

    rank = rank_ref[...]
    pen = jnp.where(rank < top_k, jnp.where(nio <= tb, 0.0, SEL_PENALTY), SEL_PENALTY).astype(BF16)
    qa_ref[HD:2 * HD, :] = jnp.concatenate([pen] * R, axis=1)

    def sel_scores(kj):
        return _dot(kst_ref[pl.ds(pl.multiple_of(kj * T, T), T), :], qa_ref[...])

    def sel_values(kj):
        return vsa_ref[:, pl.ds(pl.multiple_of(kj * T, T), T)]

    n_far = jnp.maximum(qi - 1, 0)
    near = jnp.maximum(qi - 1, 0)
    near_table = jnp.where(qi >= 1, TILE_NEAR, TILE_NONE)
    odd = n_far % 2
    sel = _SoftmaxPipe(s_ref, pe_ref, acc_ref, HD)

    def exact_selected():
        def far_step(kj, slot, carry):
            return sel.step(slot, carry, sel_values(jnp.maximum(kj - 1, 0)), functools.partial(sel_scores, kj + 1))

        carry = sel.start(sel_scores(0), odd)
        carry = lax.cond(odd == 1, lambda c: far_step(0, 1, c), lambda c: c, carry)
        carry = lax.fori_loop(
            0, n_far // 2, lambda j, c: far_step(odd + 2 * j + 1, 1, far_step(odd + 2 * j, 0, c)), carry)
        s_near = s_ref[0] + tp_ref[near_table]
        s_ref[0] = s_near
        carry = (colmax(s_near),) + tuple(carry[1:])
        carry = sel.step(0, carry, sel_values(jnp.maximum(n_far - 1, 0)),
                         lambda: sel_scores(qi) + tp_ref[TILE_DIAG])
        carry = sel.step(1, carry, sel_values(n_far), None)
        return sel.finish(1, carry, sel_values(qi))

    s_d = sel_scores(qi) + tp_ref[TILE_DIAG]
    shift = colmax(s_d).astype(BF16).astype(F32)
    qs_ref[0:HD, :] = qT
    qs_ref[HD:2 * HD, :] = (jnp.concatenate([pen.astype(F32)] * R, axis=1) - shift).astype(BF16)

    def shifted_scores(kj):
        return _dot(kst_ref[pl.ds(pl.multiple_of(kj * T, T), T), :], qs_ref[...])

    def far_shifted(kj, slot):
        sel.shifted_step(slot, sel_values(jnp.maximum(kj - 1, 0)), functools.partial(shifted_scores, kj + 1))

    s_ref[odd] = shifted_scores(0)
    pe_ref[1 - odd] = jnp.zeros(pe_ref.shape[1:], BF16)
    acc_ref[...] = _dot(sel_values(qi), jnp.exp2(s_d - shift).astype(BF16))

    @pl.when(odd == 1)
    def _():
        far_shifted(0, 1)

    def far_pair(j, carry):
        far_shifted(odd + 2 * j, 0)
        far_shifted(odd + 2 * j + 1, 1)
        return carry

    lax.fori_loop(0, n_far // 2, far_pair, 0)
    pe_near = jnp.exp2(s_ref[0] + tp_ref[near_table]).astype(BF16)
    acc = (acc_ref[...] + _dot(sel_values(jnp.maximum(n_far - 1, 0)), pe_ref[1])
           + _dot(sel_values(near), pe_near))
    denom = acc[HD:HD + 1]
    unsafe = (jnp.sum(jnp.where(denom < 2.0 ** 100, 0.0, 1.0))
              + jnp.sum(jnp.where(jnp.abs(acc[0:HD]) < 3.0e38, 0.0, 1.0))) > 0.0
    o_slc = lax.cond(unsafe, exact_selected, lambda: acc[0:HD] * (1.0 / denom))

    sg = _sigmoid(gl_ref[...])
    for r in range(R):
        cols = slice(r * T, (r + 1) * T)
        o = (sg[r:r + 1] * o_cmp[:, cols] + sg[R + r:R + r + 1] * o_slc[:, cols]
             + sg[2 * R + r:2 * R + r + 1] * o_win[:, cols])
        o_ref[r * HD:(r + 1) * HD, :] = o.astype(o_ref.dtype)


def _t5_bucket_np(dist):
    n = np.maximum(dist, 0)
    max_exact = REL_BUCKETS // 2
    nf = np.maximum(n, 1).astype(np.float32)
    large = max_exact + (np.log(nf / np.float32(max_exact)) / np.float32(math.log(REL_MAX_DIST / max_exact))
                         * np.float32(REL_BUCKETS - max_exact)).astype(np.int32)
    large = np.minimum(large, REL_BUCKETS - 1)
    return np.where(n < max_exact, n, large).astype(np.int32)


def _nsa_tables(rel_bias, nr):
    T, G, R = NSA_TQ, NSA_KV_HEADS, NSA_REP
    table = rel_bias.astype(F32) * LOG2E
    per_tile = T // CMP_STRIDE
    front = nr - per_tile
    t = np.arange(T)[None, :]

    def lay(a):
        keys = a.shape[0]
        return jnp.transpose(a, (2, 0, 1)).reshape(G, R, keys, T).transpose(0, 2, 1, 3).reshape(G, keys, R * T)

    cend = CMP_STRIDE * (np.arange(front + nr)[:, None] - front) + CMP_BLOCK - 1
    dist = t - cend
    def lookup(d):
        onehot = jax.nn.one_hot(_t5_bucket_np(d), REL_BUCKETS, dtype=F32)
        return jnp.einsum("ktb,bh->kth", onehot, table, precision=lax.Precision.HIGHEST)

    cb = jnp.where((dist >= 0)[..., None], lookup(dist), NEG_INF)
    k = np.arange(T)[:, None]
    assert 2 * T - (T - 1) >= REL_MAX_DIST and WINDOW % T == 0 and WINDOW // T >= 2
    last = table[REL_BUCKETS - 1]
    diag = jnp.where((t - k >= 0)[..., None], lookup(t - k) - last, NEG_INF)
    near = lookup(T + t - k) - last
    none = jnp.full_like(near, NEG_INF)
    edge = jnp.where((k > t)[..., None], jnp.zeros_like(near), NEG_INF)
    tiles = [None] * 5
    tiles[TILE_DIAG], tiles[TILE_NEAR], tiles[TILE_NONE], tiles[TILE_EDGE] = diag, near, none, edge
    tiles[TILE_ZERO] = jnp.zeros_like(near)
    tp = jnp.stack([lay(a) for a in tiles], axis=1)
    return lay(cb), tp


def _overlap_matrix(nr, n_slc):
    half = LANES // 2
    n = np.arange(half)[:, None]
    c = np.arange(nr)[None, :]
    ov = ((CMP_STRIDE * c <= SLC_BLOCK * n + SLC_BLOCK - 1) & (CMP_STRIDE * c + CMP_BLOCK - 1 >= SLC_BLOCK * n)
          & (n < n_slc) & (c < nr - 1))
    return jnp.asarray(ov.astype(np.float32), dtype=BF16)


def _nsa_attention(qT, kc, vcT, ksl, vslT, kw, vwT, cb, tp, glT):
    n_heads, hd, _ = qT.shape
    b, s, _ = ksl.shape
    G, R, T = NSA_KV_HEADS, NSA_REP, NSA_TQ
    nr = kc.shape[-2]
    nq = s // T
    n_slc = s // SLC_BLOCK
    assert n_slc <= LANES // 2 and s % T == 0 and nr % SUBLANES == 0
    ov = _overlap_matrix(nr, n_slc)
    kern = functools.partial(_nsa_kernel, n_slc=n_slc, top_k=min(SLC_TOPK, n_slc))
    k_spec = pl.BlockSpec((None, s, LANES), lambda i, j, k: (i, 0, j))
    vT_spec = pl.BlockSpec((None, hd, s), lambda i, j, k: (j, 0, i))
    ones_rows = 16
    pipe_scratch = [pltpu.VMEM((2, T, R * T), F32), pltpu.VMEM((2, T, R * T), BF16),
                    pltpu.VMEM((hd + ones_rows, R * T), F32)]
    value_scratch = [pltpu.VMEM((hd + ones_rows, s), BF16)] * 2
    return pl.pallas_call(
        kern, grid=(b, G, nq),
        in_specs=[
            pl.BlockSpec((R, hd, T), lambda i, j, k: (j, 0, i * nq + k)),
            pl.BlockSpec((None, None, nr, LANES), lambda i, j, k: (i, j, 0, 0)),
            pl.BlockSpec((None, None, hd, nr), lambda i, j, k: (i, j, 0, 0)),
            pl.BlockSpec(ov.shape, lambda i, j, k: (0, 0)),
            k_spec, vT_spec, k_spec, vT_spec,
            pl.BlockSpec((None,) + cb.shape[1:], lambda i, j, k: (j, 0, 0)),
            pl.BlockSpec((None,) + tp.shape[1:], lambda i, j, k: (j, 0, 0, 0)),
            pl.BlockSpec((None, 16, T), lambda i, j, k: (j, 0, i * nq + k)),
        ],
        out_specs=pl.BlockSpec((R * hd, T), lambda i, j, k: (j, i * nq + k)),
        out_shape=jax.ShapeDtypeStruct((n_heads * hd, b * s), BF16),
        scratch_shapes=[pltpu.VMEM((2 * hd, R * T), BF16)] * 4
        + [pltpu.VMEM((LANES // 2, T), F32)] + [pltpu.VMEM((s, LANES), BF16)] * 2
        + value_scratch + pipe_scratch + pipe_scratch,
        compiler_params=_cparams(3), name="nsa_attention",
    )(qT, kc, vcT, ov, ksl, vslT, kw, vwT, cb, tp, glT)


def _merge_kernel(x_ref, ys_ref, ynT_ref, mg_ref, ws_ref, wn_ref, wo_ref, g_ref, o_ref):
    d = x_ref.shape[1]
    a = _dot(ys_ref[...], ws_ref[...])
    bb = lax.dot_general(ynT_ref[...], wn_ref[...], (((0,), (0,)), ((), ())), preferred_element_type=F32)
    mg = _sigmoid(mg_ref[...].astype(F32))
    mixed = mg[:, :d] * a + mg[:, d:] * bb
    o = _dot(mixed.astype(BF16), wo_ref[...])
    o_ref[...] = x_ref[...] + _rms(o, g_ref[...])


def _merge(x2d, y_ssd, y_nsaT, mg, w_s, w_n, w_o, g, tm=512):
    m, d = x2d.shape
    tm = min(tm, m)
    row = lambda a: pl.BlockSpec((tm, a.shape[1]), lambda i: (i, 0))
    full = lambda a: pl.BlockSpec(a.shape, lambda i: (0, 0))
    g = g.reshape(1, d)
    return pl.pallas_call(
        _merge_kernel, grid=(m // tm,),
        in_specs=[row(x2d), row(y_ssd), pl.BlockSpec((y_nsaT.shape[0], tm), lambda i: (0, i)), row(mg),
                  full(w_s), full(w_n), full(w_o), full(g)],
        out_specs=pl.BlockSpec((tm, d), lambda i: (i, 0)),
        out_shape=jax.ShapeDtypeStruct((m, d), F32),
        compiler_params=_cparams(1), name="merge",
    )(x2d, y_ssd, y_nsaT, mg, w_s, w_n, w_o, g)


def _xattn_kernel(x_ref, kv_ref, gpre_ref, wq_ref, wo_ref, gpost_ref, o_ref):
    x = x_ref[...]
    h = _rms(x, gpre_ref[...]).astype(BF16)
    q = _dot(h, wq_ref[...])
    width = X_HEADS * X_HEAD_DIM
    scale = X_HEAD_DIM ** -0.5
    outs = []
    for hh in range(X_HEADS):
        sl = slice(hh * X_HEAD_DIM, (hh + 1) * X_HEAD_DIM)
        lg = _dot_nt(q[:, sl].astype(BF16), kv_ref[:, sl]) * scale
        e = jnp.exp(lg - jnp.max(lg, axis=-1, keepdims=True))
        p = e / jnp.sum(e, axis=-1, keepdims=True)
        outs.append(_dot(p.astype(BF16), kv_ref[:, width + hh * X_HEAD_DIM:width + (hh + 1) * X_HEAD_DIM]))
    o = _dot(jnp.concatenate(outs, axis=-1).astype(BF16), wo_ref[...])
    o_ref[...] = x + _rms(o, gpost_ref[...])


def _xattn(x3d, kv, g_pre, w_q, w_o, g_post, tm=512):
    b, s, d = x3d.shape
    tm = min(tm, s)
    ml = kv.shape[1]
    full = lambda a: pl.BlockSpec(a.shape, lambda i, j: (0, 0))
    g_pre, g_post = g_pre.reshape(1, d), g_post.reshape(1, d)
    return pl.pallas_call(
        _xattn_kernel, grid=(b, s // tm),
        in_specs=[pl.BlockSpec((None, tm, d), lambda i, j: (i, j, 0)),
                  pl.BlockSpec((None, ml, kv.shape[2]), lambda i, j: (i, 0, 0)),
                  full(g_pre), full(w_q), full(w_o), full(g_post)],
        out_specs=pl.BlockSpec((None, tm, d), lambda i, j: (i, j, 0)),
        out_shape=jax.ShapeDtypeStruct((b, s, d), F32),
        compiler_params=_cparams(2), name="xattn",
    )(x3d, kv, g_pre, w_q, w_o, g_post)


def _mlp_kernel(x_ref, gpre_ref, w1_ref, w2_ref, gpost_ref, o_ref, *, f_chunk):
    x = x_ref[...]
    h = _rms(x, gpre_ref[...]).astype(BF16)
    d_ff = w1_ref.shape[1]
    acc = jnp.zeros(x.shape, F32)
    for f0 in range(0, d_ff, f_chunk):
        u = jnp.maximum(_dot(h, w1_ref[:, f0:f0 + f_chunk]), 0.0)
        acc = acc + _dot((u * u).astype(BF16), w2_ref[f0:f0 + f_chunk, :])
    o_ref[...] = x + _rms(acc, gpost_ref[...])


def _mlp(x2d, g_pre, w1, w2, g_post, tm=512):
    m, d = x2d.shape
    tm = min(tm, m)
    full = lambda a: pl.BlockSpec(a.shape, lambda i: (0, 0))
    g_pre, g_post = g_pre.reshape(1, d), g_post.reshape(1, d)
    return pl.pallas_call(
        functools.partial(_mlp_kernel, f_chunk=1024), grid=(m // tm,),
        in_specs=[pl.BlockSpec((tm, d), lambda i: (i, 0)), full(g_pre), full(w1), full(w2), full(g_post)],
        out_specs=pl.BlockSpec((tm, d), lambda i: (i, 0)),
        out_shape=jax.ShapeDtypeStruct((m, d), F32),
        compiler_params=_cparams(1), name="mlp",
    )(x2d, g_pre, w1, w2, g_post)


def _pad_heads(w, n_heads, hd):
    k = w.shape[0]
    w = w.reshape(k, n_heads, hd)
    return jnp.pad(w, ((0, 0), (0, 0), (0, LANES - hd))).reshape(k, n_heads * LANES)


def _layer(x, mem, w_in, conv_w, conv_b, dt_bias, a_log, d_skip, ssd_norm, cmp_pos, cmp_w1, cmp_w2,
           rel_bias, w_br_ssd, w_br_nsa, w_out, w_xq, w_xkv, w_xo, w_ff1, w_ff2,
           n_mix_pre, n_mix_post, n_x_pre, n_x_post, n_mem, n_ffn_pre, n_ffn_post):
    b, s, d = x.shape
    G, R, HD = NSA_KV_HEADS, NSA_REP, NSA_HEAD_DIM
    d_inner = 2 * d
    n_ssd_heads = d_inner // SSD_HEAD_DIM
    conv_dim = d_inner + 2 * SSD_GROUPS * SSD_STATE
    nsa_w, kv_w = NSA_HEADS * HD, G * HD
    sizes = (d_inner, conv_dim, n_ssd_heads, nsa_w, 6 * kv_w, 3 * NSA_HEADS, 2 * d)
    offs = np.concatenate([[0], np.cumsum(sizes)])
    seg = lambda i: w_in[:, offs[i]:offs[i + 1]]
    w_z, w_xbc, w_dt, w_q, w_kv, w_gate, w_mg = (seg(i) for i in range(7))
    w_small = jnp.pad(jnp.concatenate([w_dt, w_gate], axis=1),
                      ((0, 0), (0, LANES - n_ssd_heads - 3 * NSA_HEADS)))
    w_kv6 = w_kv.reshape(d, 6, kv_w)
    w_cmp = jnp.concatenate([w_kv6[:, 0], w_kv6[:, 1]], axis=1)
    bf = lambda a: a.astype(BF16)

    x2d = x.reshape(b * s, d)
    z, xbc = _norm_matmul(x2d, n_mix_pre, [bf(w_z), bf(w_xbc)], [BF16, BF16])
    w_gl = jnp.transpose(w_gate.reshape(d, 3, G, R), (2, 1, 3, 0)).reshape(G, 3 * R, d)
    w_gl = jnp.pad(w_gl, ((0, 0), (0, 16 - 3 * R), (0, 0))).reshape(G * 16, d)
    small, dtT, glT, qT, kv_cmp, vslT, vwT = _norm_matmul(
        x2d, n_mix_pre,
        [bf(w_small), bf(w_dt.T), bf(w_gl), bf(w_q.T * (HD ** -0.5 * LOG2E)), bf(w_cmp),
         bf(w_kv6[:, 3].T), bf(w_kv6[:, 5].T)],
        [F32, F32, F32, BF16, BF16, BF16, BF16],
        transposed=(False, True, True, True, False, True, True))
    ksl, kw, mg = _norm_matmul(
        x2d, n_mix_pre, [bf(_pad_heads(w_kv6[:, 2], G, HD)), bf(_pad_heads(w_kv6[:, 4], G, HD)), bf(w_mg)],
        [BF16] * 3)

    y_ssd = _ssd_mixer(z.reshape(b, s, -1), xbc.reshape(b, s, -1), small.reshape(b, s, -1), dtT,
                       conv_w, conv_b, dt_bias, a_log, d_skip, ssd_norm)

    nr = s // CMP_STRIDE
    kr = kv_cmp.reshape(b, nr, CMP_STRIDE, 2, G, HD)
    kr = jnp.transpose(kr, (3, 0, 4, 1, 2, 5)).reshape(2, b, G, nr, CMP_STRIDE * HD)
    pos = jnp.broadcast_to(bf(cmp_pos).reshape(2, 1, CMP_BLOCK * HD), (2, SUBLANES, CMP_BLOCK * HD))
    w2p = jnp.pad(bf(cmp_w2), ((0, 0), (0, 0), (0, LANES - HD)))
    kvc = _compress(kr, pos, bf(cmp_w1), w2p)
    vcT = jnp.swapaxes(kvc[1][..., :HD], -1, -2)
    cb, tp = _nsa_tables(rel_bias, nr)
    y_nsa = _nsa_attention(qT.reshape(NSA_HEADS, HD, b * s), kvc[0], vcT,
                           ksl.reshape(b, s, -1), vslT.reshape(G, HD, b * s),
                           kw.reshape(b, s, -1), vwT.reshape(G, HD, b * s), cb, tp,
                           glT.reshape(G, 16, b * s))

    x1 = _merge(x2d, y_ssd.reshape(b * s, -1), y_nsa, mg, bf(w_br_ssd), bf(w_br_nsa), bf(w_out), n_mix_post)

    (kv_mem,) = _norm_matmul(mem.reshape(-1, d), n_mem, [bf(w_xkv)], [BF16])
    x2 = _xattn(x1.reshape(b, s, d), kv_mem.reshape(b, mem.shape[1], -1), n_x_pre, bf(w_xq), bf(w_xo),
                n_x_post)

    x3 = _mlp(x2.reshape(b * s, d), n_ffn_pre, bf(w_ff1), bf(w_ff2), n_ffn_post)
    return x3.reshape(b, s, d)


def kernel(x, mem, w_in, ssd_conv_w, ssd_conv_b, ssd_dt_bias, ssd_a_log, ssd_d_skip, ssd_norm, cmp_pos,
           cmp_w1, cmp_w2, rel_bias, w_br_ssd, w_br_nsa, w_out, w_xq, w_xkv, w_xo, w_ff1, w_ff2,
           norm_mix_pre, norm_mix_post, norm_x_pre, norm_x_post, norm_mem, norm_ffn_pre, norm_ffn_post):
    for l in range(w_in.shape[0]):
        x = _layer(x, mem, w_in[l], ssd_conv_w[l], ssd_conv_b[l], ssd_dt_bias[l], ssd_a_log[l],
                   ssd_d_skip[l], ssd_norm[l], cmp_pos[l], cmp_w1[l], cmp_w2[l], rel_bias,
                   w_br_ssd[l], w_br_nsa[l], w_out[l], w_xq[l], w_xkv[l], w_xo[l], w_ff1[l], w_ff2[l],
                   norm_mix_pre[l], norm_mix_post[l], norm_x_pre[l], norm_x_post[l], norm_mem[l],
                   norm_ffn_pre[l], norm_ffn_post[l])
    return x
```

```python
import functools
import math

import numpy as np
import jax
import jax.numpy as jnp
from jax import lax
from jax.experimental import pallas as pl
from jax.experimental.pallas import tpu as pltpu

F32 = jnp.float32
BF16 = jnp.bfloat16

NORM_EPS = 1e-6
NEG_INF = -1e30
FORCE_SCORE = 1e9
SEL_PENALTY = NEG_INF

LANES = 128
SUBLANES = 8
VMEM_LIMIT = 56 * 1024 * 1024

SSD_HEAD_DIM = 64
SSD_GROUPS = 8
SSD_STATE = 128
SSD_CONV = 4
SSD_CHUNK = 128
NSA_HEADS = 16
NSA_KV_HEADS = 4
NSA_HEAD_DIM = 64
NSA_REP = NSA_HEADS // NSA_KV_HEADS
CMP_BLOCK = 32
CMP_STRIDE = 16
SLC_BLOCK = 64
SLC_TOPK = 16
WINDOW = 512
NSA_TQ = 256
LOG2E = math.log2(math.e)
REL_BUCKETS = 32
REL_MAX_DIST = 128
X_HEADS = 4
X_HEAD_DIM = 128

TILE_DIAG, TILE_NEAR, TILE_NONE, TILE_EDGE, TILE_ZERO = range(5)


def _cparams(n_grid):
    return pltpu.CompilerParams(dimension_semantics=("arbitrary",) * n_grid,
                                vmem_limit_bytes=VMEM_LIMIT)


def _sigmoid(x):
    return jax.nn.sigmoid(x)


def _rms(x, g):
    return x * lax.rsqrt(jnp.mean(x * x, axis=-1, keepdims=True) + NORM_EPS) * g


def _dot(a, b):
    return jnp.dot(a, b, preferred_element_type=F32)


def _dot_nt(a, b):
    return lax.dot_general(a, b, (((1,), (1,)), ((), ())), preferred_element_type=F32)


def _split3(x):
    x1 = x.astype(BF16)
    r1 = x - x1.astype(F32)
    x2 = r1.astype(BF16)
    x3 = (r1 - x2.astype(F32)).astype(BF16)
    return x1, x2, x3


def _norm_matmul_kernel(x_ref, g_ref, *refs, transposed, n_chunk):
    n_out = len(transposed)
    w_refs, o_refs = refs[:n_out], refs[n_out:]
    h = _rms(x_ref[...], g_ref[...]).astype(BF16)
    for w_ref, o_ref, tr in zip(w_refs, o_refs, transposed):
        if tr:
            o_ref[...] = _dot_nt(w_ref[...], h).astype(o_ref.dtype)
            continue
        n = w_ref.shape[1]
        step = min(n, n_chunk)
        for n0 in range(0, n, step):
            o_ref[:, n0:n0 + step] = _dot(h, w_ref[:, n0:n0 + step]).astype(o_ref.dtype)


def _norm_matmul(x2d, g, ws, out_dtypes, transposed=None, tm=512):
    m, k = x2d.shape
    tm = min(tm, m)
    transposed = tuple(transposed) if transposed is not None else (False,) * len(ws)
    in_specs = [pl.BlockSpec((tm, k), lambda i: (i, 0)), pl.BlockSpec((1, k), lambda i: (0, 0))]
    in_specs += [pl.BlockSpec(w.shape, lambda i: (0, 0)) for w in ws]
    out_specs, out_shape = [], []
    for w, dt, tr in zip(ws, out_dtypes, transposed):
        if tr:
            out_specs.append(pl.BlockSpec((w.shape[0], tm), lambda i: (0, i)))
            out_shape.append(jax.ShapeDtypeStruct((w.shape[0], m), dt))
        else:
            out_specs.append(pl.BlockSpec((tm, w.shape[1]), lambda i: (i, 0)))
            out_shape.append(jax.ShapeDtypeStruct((m, w.shape[1]), dt))
    return pl.pallas_call(
        functools.partial(_norm_matmul_kernel, transposed=transposed, n_chunk=1024),
        grid=(m // tm,), in_specs=in_specs, out_specs=out_specs, out_shape=out_shape,
        compiler_params=_cparams(1), name="norm_proj",
    )(x2d, g.reshape(1, k), *ws)


def _softplus(x):
    return jnp.maximum(x, 0.0) + jnp.log(1.0 + jnp.exp(-jnp.abs(x)))


def _ssd_kernel(xbc_ref, prev_ref, z_ref, dt_ref, dtT_ref, cw_ref, cb_ref, dtb_ref, dtbT_ref,
                alog_ref, alogT_ref, dskip_ref, nw_ref, y_ref, state_ref, xc_ref, ybuf_ref,
                *, n_heads, d_inner):
    L, P, N, G = SSD_CHUNK, SSD_HEAD_DIM, SSD_STATE, SSD_GROUPS
    conv_dim = xc_ref.shape[1]
    c = pl.program_id(1)

    @pl.when(c == 0)
    def _():
        state_ref[...] = jnp.zeros_like(state_ref)

    cw = 512
    prev_rows = prev_ref.shape[0]
    row = lax.broadcasted_iota(jnp.int32, (prev_rows, cw), 0)
    for j in range(conv_dim // cw):
        sl = slice(j * cw, (j + 1) * cw)
        cur = xbc_ref[:, sl].astype(F32)
        prev = jnp.where(c == 0, 0.0, prev_ref[:, sl].astype(F32))
        acc = cb_ref[:, sl] + cw_ref[SSD_CONV - 1:SSD_CONV, sl] * cur
        for k in range(1, SSD_CONV):
            rc = pltpu.roll(cur, k, 0)
            rp = pltpu.roll(prev, k, 0)
            head = jnp.where(row < k, rp, rc[:prev_rows])
            shifted = jnp.concatenate([head, rc[prev_rows:]], axis=0)
            acc = acc + cw_ref[SSD_CONV - 1 - k:SSD_CONV - k, sl] * shifted
        xc_ref[:, sl] = acc * _sigmoid(acc)

    dt = _softplus(dt_ref[:, 0:n_heads] + dtb_ref[...])
    a = dt * (-LOG2E * jnp.exp(alog_ref[...]))
    dtT = _softplus(dtT_ref[...] + dtbT_ref[...])
    aT = dtT * (-LOG2E * jnp.exp(alogT_ref[...]))
    ri = lax.broadcasted_iota(jnp.int32, (L, L), 0)
    ci = lax.broadcasted_iota(jnp.int32, (L, L), 1)
    causal = ci <= ri
    tri = jnp.where(causal, 1.0, 0.0).astype(BF16)
    triu = jnp.where(ri <= ci, 1.0, 0.0).astype(BF16)
    a_cs = sum(_dot(tri, ai) for ai in _split3(a))
    a_csT = sum(_dot(ai, triu) for ai in _split3(aT))
    lane = lax.broadcasted_iota(jnp.int32, (L, 2 * P), 1)
    lo_half = lane < P
    lane1 = lax.broadcasted_iota(jnp.int32, (1, 2 * P), 1) < P

    heads_per_group = n_heads // G
    for g in range(G):
        b_g = xc_ref[:, d_inner + g * N:d_inner + (g + 1) * N]
        c_g = xc_ref[:, d_inner + G * N + g * N:d_inner + G * N + (g + 1) * N]
        c_gb = c_g.astype(BF16)
        cb = _dot_nt(c_gb, b_g.astype(BF16))
        b_gT = b_g.T.astype(BF16)
        for pp in range(heads_per_group // 2):
            pair = g * (heads_per_group // 2) + pp
            h0, h1 = 2 * pair, 2 * pair + 1
            xs = xc_ref[:, pair * 2 * P:(pair + 1) * 2 * P]
            dt_pair = jnp.where(lo_half, dt[:, h0:h0 + 1], dt[:, h1:h1 + 1])
            xd = xs * dt_pair
            xd_b = xd.astype(BF16)
            ys = []
            for h in (h0, h1):
                seg = jnp.where(causal, a_cs[:, h:h + 1] - a_csT[h:h + 1, :], NEG_INF)
                mat = (cb * jnp.exp2(seg)).astype(BF16)
                ys.append(_dot(mat, xd_b))
            y = jnp.where(lo_half, ys[0], ys[1])
            cs_pair = jnp.where(lo_half, a_cs[:, h0:h0 + 1], a_cs[:, h1:h1 + 1])
            st = state_ref[pair]
            y = y + _dot(c_gb, st.astype(BF16)) * jnp.exp2(cs_pair)
            tot = jnp.where(lane1, a_csT[h0:h0 + 1, L - 1:L], a_csT[h1:h1 + 1, L - 1:L])
            xdd = (xd * jnp.exp2(tot - cs_pair)).astype(BF16)
            state_ref[pair] = st * jnp.exp2(tot) + _dot(b_gT, xdd)
            ybuf_ref[:, pair * 2 * P:(pair + 1) * 2 * P] = y + xs * dskip_ref[:, pair * 2 * P:(pair + 1) * 2 * P]

    gw = d_inner // G
    for g in range(G):
        sl = slice(g * gw, (g + 1) * gw)
        zz = z_ref[:, sl].astype(F32)
        yg = ybuf_ref[:, sl] * (zz * _sigmoid(zz))
        y_ref[:, sl] = _rms(yg, nw_ref[:, sl]).astype(y_ref.dtype)


def _ssd_mixer(z, xbc, small, dtT, conv_w, conv_b, dt_bias, a_log, d_skip, norm_w):
    b, s, d_inner = z.shape
    conv_dim = xbc.shape[-1]
    n_heads = d_inner // SSD_HEAD_DIM
    L = SSD_CHUNK
    nc = s // L
    prev_rows = 16
    xbc_prev = xbc.reshape(b, s // prev_rows, prev_rows, conv_dim)
    blocks_per_chunk = L // prev_rows
    kern = functools.partial(_ssd_kernel, n_heads=n_heads, d_inner=d_inner)
    const = lambda shape: pl.BlockSpec(shape, lambda i, j: (0,) * len(shape))
    return pl.pallas_call(
        kern, grid=(b, nc),
        in_specs=[
            pl.BlockSpec((None, L, conv_dim), lambda i, j: (i, j, 0)),
            pl.BlockSpec((None, None, prev_rows, conv_dim),
                         lambda i, j: (i, jnp.maximum(j * blocks_per_chunk - 1, 0), 0, 0)),
            pl.BlockSpec((None, L, d_inner), lambda i, j: (i, j, 0)),
            pl.BlockSpec((None, L, small.shape[-1]), lambda i, j: (i, j, 0)),
            pl.BlockSpec((n_heads, L), lambda i, j: (0, i * nc + j)),
            const((SSD_CONV, conv_dim)), const((1, conv_dim)),
            const((1, n_heads)), const((n_heads, 1)),
            const((1, n_heads)), const((n_heads, 1)),
            const((1, d_inner)), const((1, d_inner)),
        ],
        out_specs=pl.BlockSpec((None, L, d_inner), lambda i, j: (i, j, 0)),
        out_shape=jax.ShapeDtypeStruct((b, s, d_inner), BF16),
        scratch_shapes=[pltpu.VMEM((n_heads // 2, SSD_STATE, 2 * SSD_HEAD_DIM), F32),
                        pltpu.VMEM((L, conv_dim), F32),
                        pltpu.VMEM((L, d_inner), F32)],
        compiler_params=_cparams(2), name="ssd",
    )(xbc, xbc_prev, z, small, dtT, conv_w.astype(F32), conv_b.reshape(1, -1).astype(F32),
      dt_bias.reshape(1, -1), dt_bias.reshape(-1, 1), a_log.reshape(1, -1), a_log.reshape(-1, 1),
      jnp.repeat(d_skip, SSD_HEAD_DIM).reshape(1, -1), norm_w.reshape(1, -1))


def _compress_kernel(kr_ref, pos_ref, w1_ref, w2_ref, o_ref):
    nr, half = kr_ref.shape
    kr = kr_ref[...]
    p1 = _dot(kr, w1_ref[0:half, :])
    p2 = _dot(kr, w1_ref[half:2 * half, :])
    pb = _dot(pos_ref[...], w1_ref[...])[0:1]
    hid = p1 + pltpu.roll(p2, nr - 1, 0) + pb
    hid = hid * _sigmoid(hid)
    out = _dot(hid.astype(BF16), w2_ref[...])
    rows = lax.broadcasted_iota(jnp.int32, out.shape, 0)
    o_ref[...] = jnp.where(rows < nr - 1, out, 0.0).astype(o_ref.dtype)


def _compress(kr, pos, w1, w2p):
    two, b, g, nr, half = kr.shape
    hidden = w1.shape[-1]
    return pl.pallas_call(
        _compress_kernel,
        grid=(two, b, g),
        in_specs=[
            pl.BlockSpec((None, None, None, nr, half), lambda t, i, j: (t, i, j, 0, 0)),
            pl.BlockSpec((None, SUBLANES, 2 * half), lambda t, i, j: (t, 0, 0)),
            pl.BlockSpec((None, 2 * half, hidden), lambda t, i, j: (t, 0, 0)),
            pl.BlockSpec((None, hidden, LANES), lambda t, i, j: (t, 0, 0)),
        ],
        out_specs=pl.BlockSpec((None, None, None, nr, LANES), lambda t, i, j: (t, i, j, 0, 0)),
        out_shape=jax.ShapeDtypeStruct((two, b, g, nr, LANES), BF16),
        compiler_params=_cparams(3), name="nsa_compress",
    )(kr, pos, w1, w2p)


def _colmax(a):
    return jnp.max(a, axis=0, keepdims=True)


def _colsum(a):
    return jnp.sum(a, axis=0, keepdims=True)


class _SoftmaxPipe:
    def __init__(self, s_ref, pe_ref, acc_ref, hd):
        self.s, self.pe, self.acc, self.hd = s_ref, pe_ref, acc_ref, hd

    def start(self, first_scores, slot):
        cols = first_scores.shape[1]
        self.s[slot] = first_scores
        self.pe[1 - slot] = jnp.zeros(self.pe.shape[1:], self.pe.dtype)
        self.acc[...] = jnp.zeros(self.acc.shape, F32)
        return _colmax(first_scores), jnp.ones((1, cols), F32), jnp.full((1, cols), NEG_INF, F32)

    def step(self, slot, carry, v_prev, next_scores):
        cm_cur, alpha_prev, m_old = carry
        self.acc[...] = alpha_prev * self.acc[...] + _dot(v_prev, self.pe[1 - slot])
        cm_next = cm_cur
        if next_scores is not None:
            s_next = next_scores()
            self.s[1 - slot] = s_next
            cm_next = _colmax(s_next)
        m_new = jnp.maximum(m_old, cm_cur)
        self.pe[slot] = jnp.exp2(self.s[slot] - m_new).astype(self.pe.dtype)
        return cm_next, jnp.exp2(m_old - m_new), m_new

    def finish(self, slot_last, carry, v_last):
        _, alpha_prev, _ = carry
        acc = alpha_prev * self.acc[...] + _dot(v_last, self.pe[slot_last])
        return acc[0:self.hd] * (1.0 / acc[self.hd:self.hd + 1])

    def shifted_step(self, slot, v_prev, next_scores):
        self.pe[slot] = jnp.exp2(self.s[slot]).astype(self.pe.dtype)
        self.s[1 - slot] = next_scores()
        self.acc[...] += _dot(v_prev, self.pe[1 - slot])


def _nsa_kernel(qT_ref, kc_ref, vcT_ref, ov_ref, ksl_ref, vslT_ref, kw_ref, vwT_ref, cb_ref, tp_ref,
                gl_ref, o_ref, q0_ref, qa_ref, qs_ref, rank_ref, kst_ref, vsa_ref, vwa_ref, s_ref, pe_ref,
                acc_ref, ws_ref, wpe_ref, wacc_ref, *, n_slc, top_k):
    T, R, HD = NSA_TQ, NSA_REP, NSA_HEAD_DIM
    nr = kc_ref.shape[0]
    per_tile = T // CMP_STRIDE
    front = nr - per_tile
    half = LANES // 2
    qi = pl.program_id(2)
    colmax, colsum = _colmax, _colsum

    @pl.when(qi == 0)
    def _():
        for src, dst in ((vslT_ref, vsa_ref), (vwT_ref, vwa_ref)):
            dst[0:HD, :] = src[...]
            dst[HD:, :] = jnp.ones((dst.shape[0] - HD, dst.shape[1]), dst.dtype)
        krow = lax.broadcasted_iota(jnp.int32, (T, LANES), 0)
        klane = lax.broadcasted_iota(jnp.int32, (T, LANES), 1)
        for t0 in range(0, ksl_ref.shape[0], T):
            blk = t0 // SLC_BLOCK + krow // SLC_BLOCK
            tag = jnp.where(klane - half == blk, 1.0, 0.0).astype(BF16)
            kst_ref[t0:t0 + T, :] = jnp.where(klane >= half, tag, ksl_ref[t0:t0 + T, :])

    qT = jnp.concatenate([qT_ref[r] for r in range(R)], axis=1)
    q0_ref[0:HD, :] = qT
    q0_ref[HD:2 * HD, :] = jnp.zeros((HD, R * T), BF16)
    qa_ref[0:HD, :] = qT

    n_win = WINDOW // T

    def win_offset(rel):
        return pl.multiple_of(jnp.maximum(qi - rel, 0) * T, T)

    def win_scores(rel):
        tile = TILE_EDGE if rel == n_win else (TILE_ZERO if rel >= 2 else (TILE_NEAR if rel == 1 else TILE_DIAG))
        if rel > 0:
            tile = jnp.where(qi >= rel, tile, TILE_NONE)
        return _dot(kw_ref[pl.ds(win_offset(rel), T), :], q0_ref[...]) + tp_ref[tile]

    def win_values(rel):
        return vwa_ref[:, pl.ds(win_offset(rel), T)]

    win = _SoftmaxPipe(ws_ref, wpe_ref, wacc_ref, HD)
    rels = list(range(n_win, -1, -1))
    carry = win.start(win_scores(rels[0]), 0)
    for i, rel in enumerate(rels):
        nxt = functools.partial(win_scores, rels[i + 1]) if i + 1 < len(rels) else None
        carry = win.step(i % 2, carry, win_values(rels[max(i - 1, 0)]), nxt)
    o_win = win.finish((len(rels) - 1) % 2, carry, win_values(0))

    off = pl.multiple_of(front - qi * per_tile, SUBLANES)
    lg = _dot(kc_ref[...], q0_ref[...]) + cb_ref[pl.ds(off, nr), :]
    m_c = colmax(lg)
    e = jnp.exp2(lg - m_c)
    p = e * jnp.where(m_c > 0.5 * NEG_INF, 1.0 / colsum(e), 0.0)
    o_cmp = _dot(vcT_ref[...], p.astype(BF16))
    sg = _sigmoid(gl_ref[...])
    o_partial = [sg[r:r + 1] * o_cmp[:, r * T:(r + 1) * T] + sg[2 * R + r:2 * R + r + 1] * o_win[:, r * T:(r + 1) * T]
                 for r in range(R)]

    psum = p[:, 0:T]
    for r in range(1, R):
        psum = psum + p[:, r * T:(r + 1) * T]
    imp = sum(_dot(ov_ref[...], pi) for pi in _split3(psum))
    nio = lax.broadcasted_iota(jnp.int32, (half, T), 0)
    tio = lax.broadcasted_iota(jnp.int32, (half, T), 1)
    tb = (qi * T + tio) // SLC_BLOCK
    forced = (nio == 0) | (nio == tb) | (nio == tb - 1)
    imp = jnp.where(forced, FORCE_SCORE, imp)
    imp = jnp.where(nio > tb, NEG_INF, imp)
    n_grp = half // SUBLANES
    grp = [imp[SUBLANES * v:SUBLANES * (v + 1)] for v in range(n_grp)]
    nio8 = lax.broadcasted_iota(jnp.int32, (SUBLANES, T), 0)
    rank_ref[...] = jnp.zeros(rank_ref.shape, F32)
    last_block = (qi * T + T - 1) // SLC_BLOCK
    for mg in range(pl.cdiv(n_slc, SUBLANES)):

        @pl.when(mg * SUBLANES <= last_block)
        def _():
            hits = [jnp.zeros((SUBLANES, T), F32) for _ in range(n_grp)]
            for mm in range(mg * SUBLANES, min((mg + 1) * SUBLANES, n_slc)):
                rowv = imp[mm:mm + 1, :]
                for v in range(n_grp):
                    if SUBLANES * v > mm:
                        hit = jnp.where(rowv >= grp[v], 1.0, 0.0)
                    elif SUBLANES * v + SUBLANES - 1 < mm:
                        hit = jnp.where(rowv > grp[v], 1.0, 0.0)
                    else:
                        hit = jnp.where(nio8 > mm - SUBLANES * v, jnp.where(rowv >= grp[v], 1.0, 0.0),
                                        jnp.where(rowv > grp[v], 1.0, 0.0))
                    hits[v] = hits[v] + hit
            rank_ref[...] += jnp.concatenate(hits, axis=0)

    rank = rank_ref[...]
    pen = jnp.where(rank < top_k, jnp.where(nio <= tb, 0.0, SEL_PENALTY), SEL_PENALTY).astype(BF16)
    qa_ref[HD:2 * HD, :] = jnp.concatenate([pen] * R, axis=1)

    def sel_scores(kj):
        return _dot(kst_ref[pl.ds(pl.multiple_of(kj * T, T), T), :], qa_ref[...])

    def sel_values(kj):
        return vsa_ref[:, pl.ds(pl.multiple_of(kj * T, T), T)]

    n_far = jnp.maximum(qi - 1, 0)
    near = jnp.maximum(qi - 1, 0)
    near_table = jnp.where(qi >= 1, TILE_NEAR, TILE_NONE)
    odd = n_far % 2
    sel = _SoftmaxPipe(s_ref, pe_ref, acc_ref, HD)

    def exact_selected():
        def far_step(kj, slot, carry):
            return sel.step(slot, carry, sel_values(jnp.maximum(kj - 1, 0)), functools.partial(sel_scores, kj + 1))

        carry = sel.start(sel_scores(0), odd)
        carry = lax.cond(odd == 1, lambda c: far_step(0, 1, c), lambda c: c, carry)
        carry = lax.fori_loop(
            0, n_far // 2, lambda j, c: far_step(odd + 2 * j + 1, 1, far_step(odd + 2 * j, 0, c)), carry)
        s_near = s_ref[0] + tp_ref[near_table]
        s_ref[0] = s_near
        carry = (colmax(s_near),) + tuple(carry[1:])
        carry = sel.step(0, carry, sel_values(jnp.maximum(n_far - 1, 0)),
                         lambda: sel_scores(qi) + tp_ref[TILE_DIAG])
        carry = sel.step(1, carry, sel_values(n_far), None)
        return sel.finish(1, carry, sel_values(qi))

    s_d = sel_scores(qi) + tp_ref[TILE_DIAG]
    shift = colmax(s_d).astype(BF16).astype(F32)
    qs_ref[0:HD, :] = qT
    qs_ref[HD:2 * HD, :] = (jnp.concatenate([pen.astype(F32)] * R, axis=1) - shift).astype(BF16)

    def shifted_scores(kj):
        return _dot(kst_ref[pl.ds(pl.multiple_of(kj * T, T), T), :], qs_ref[...])

    def far_shifted(kj, slot):
        sel.shifted_step(slot, sel_values(jnp.maximum(kj - 1, 0)), functools.partial(shifted_scores, kj + 1))

    s_ref[odd] = shifted_scores(0)
    pe_ref[1 - odd] = jnp.zeros(pe_ref.shape[1:], BF16)
    acc_ref[...] = _dot(sel_values(qi), jnp.exp2(s_d - shift).astype(BF16))

    @pl.when(odd == 1)
    def _():
        far_shifted(0, 1)

    def far_pair(j, carry):
        far_shifted(odd + 2 * j, 0)
        far_shifted(odd + 2 * j + 1, 1)
        return carry

    lax.fori_loop(0, n_far // 2, far_pair, 0)
    pe_near = jnp.exp2(s_ref[0] + tp_ref[near_table]).astype(BF16)
    acc = (acc_ref[...] + _dot(sel_values(jnp.maximum(n_far - 1, 0)), pe_ref[1])
           + _dot(sel_values(near), pe_near))
    denom = acc[HD:HD + 1]
    unsafe = (jnp.sum(jnp.where(denom < 2.0 ** 100, 0.0, 1.0))
              + jnp.sum(jnp.where(jnp.abs(acc[0:HD]) < 3.0e38, 0.0, 1.0))) > 0.0
    o_slc = lax.cond(unsafe, exact_selected, lambda: acc[0:HD] * (1.0 / denom))

    for r in range(R):
        o = o_partial[r] + sg[R + r:R + r + 1] * o_slc[:, r * T:(r + 1) * T]
        o_ref[r * HD:(r + 1) * HD, :] = o.astype(o_ref.dtype)


def _t5_bucket_np(dist):
    n = np.maximum(dist, 0)
    max_exact = REL_BUCKETS // 2
    nf = np.maximum(n, 1).astype(np.float32)
    large = max_exact + (np.log(nf / np.float32(max_exact)) / np.float32(math.log(REL_MAX_DIST / max_exact))
                         * np.float32(REL_BUCKETS - max_exact)).astype(np.int32)
    large = np.minimum(large, REL_BUCKETS - 1)
    return np.where(n < max_exact, n, large).astype(np.int32)


def _nsa_tables(rel_bias, nr):
    T, G, R = NSA_TQ, NSA_KV_HEADS, NSA_REP
    table = rel_bias.astype(F32) * LOG2E
    per_tile = T // CMP_STRIDE
    front = nr - per_tile
    t = np.arange(T)[None, :]

    def lay(a):
        keys = a.shape[0]
        return jnp.transpose(a, (2, 0, 1)).reshape(G, R, keys, T).transpose(0, 2, 1, 3).reshape(G, keys, R * T)

    cend = CMP_STRIDE * (np.arange(front + nr)[:, None] - front) + CMP_BLOCK - 1
    dist = t - cend
    def lookup(d):
        onehot = jax.nn.one_hot(_t5_bucket_np(d), REL_BUCKETS, dtype=F32)
        return jnp.einsum("ktb,bh->kth", onehot, table, precision=lax.Precision.HIGHEST)

    cb = jnp.where((dist >= 0)[..., None], lookup(dist), NEG_INF)
    k = np.arange(T)[:, None]
    assert 2 * T - (T - 1) >= REL_MAX_DIST and WINDOW % T == 0 and WINDOW // T >= 2
    last = table[REL_BUCKETS - 1]
    diag = jnp.where((t - k >= 0)[..., None], lookup(t - k) - last, NEG_INF)
    near = lookup(T + t - k) - last
    none = jnp.full_like(near, NEG_INF)
    edge = jnp.where((k > t)[..., None], jnp.zeros_like(near), NEG_INF)
    tiles = [None] * 5
    tiles[TILE_DIAG], tiles[TILE_NEAR], tiles[TILE_NONE], tiles[TILE_EDGE] = diag, near, none, edge
    tiles[TILE_ZERO] = jnp.zeros_like(near)
    tp = jnp.stack([lay(a) for a in tiles], axis=1)
    return lay(cb), tp


def _overlap_matrix(nr, n_slc):
    half = LANES // 2
    n = np.arange(half)[:, None]
    c = np.arange(nr)[None, :]
    ov = ((CMP_STRIDE * c <= SLC_BLOCK * n + SLC_BLOCK - 1) & (CMP_STRIDE * c + CMP_BLOCK - 1 >= SLC_BLOCK * n)
          & (n < n_slc) & (c < nr - 1))
    return jnp.asarray(ov.astype(np.float32), dtype=BF16)


def _nsa_attention(qT, kc, vcT, ksl, vslT, kw, vwT, cb, tp, glT):
    n_heads, hd, _ = qT.shape
    b, s, _ = ksl.shape
    G, R, T = NSA_KV_HEADS, NSA_REP, NSA_TQ
    nr = kc.shape[-2]
    nq = s // T
    n_slc = s // SLC_BLOCK
    assert n_slc <= LANES // 2 and s % T == 0 and nr % SUBLANES == 0
    ov = _overlap_matrix(nr, n_slc)
    kern = functools.partial(_nsa_kernel, n_slc=n_slc, top_k=min(SLC_TOPK, n_slc))
    k_spec = pl.BlockSpec((None, s, LANES), lambda i, j, k: (i, 0, j))
    vT_spec = pl.BlockSpec((None, hd, s), lambda i, j, k: (j, 0, i))
    ones_rows = 16
    pipe_scratch = [pltpu.VMEM((2, T, R * T), F32), pltpu.VMEM((2, T, R * T), BF16),
                    pltpu.VMEM((hd + ones_rows, R * T), F32)]
    value_scratch = [pltpu.VMEM((hd + ones_rows, s), BF16)] * 2
    return pl.pallas_call(
        kern, grid=(b, G, nq),
        in_specs=[
            pl.BlockSpec((R, hd, T), lambda i, j, k: (j, 0, i * nq + k)),
            pl.BlockSpec((None, None, nr, LANES), lambda i, j, k: (i, j, 0, 0)),
            pl.BlockSpec((None, None, hd, nr), lambda i, j, k: (i, j, 0, 0)),
            pl.BlockSpec(ov.shape, lambda i, j, k: (0, 0)),
            k_spec, vT_spec, k_spec, vT_spec,
            pl.BlockSpec((None,) + cb.shape[1:], lambda i, j, k: (j, 0, 0)),
            pl.BlockSpec((None,) + tp.shape[1:], lambda i, j, k: (j, 0, 0, 0)),
            pl.BlockSpec((None, 16, T), lambda i, j, k: (j, 0, i * nq + k)),
        ],
        out_specs=pl.BlockSpec((R * hd, T), lambda i, j, k: (j, i * nq + k)),
        out_shape=jax.ShapeDtypeStruct((n_heads * hd, b * s), BF16),
        scratch_shapes=[pltpu.VMEM((2 * hd, R * T), BF16)] * 3
        + [pltpu.VMEM((LANES // 2, T), F32), pltpu.VMEM((s, LANES), BF16)]
        + value_scratch + pipe_scratch + pipe_scratch,
        compiler_params=_cparams(3), name="nsa_attention",
    )(qT, kc, vcT, ov, ksl, vslT, kw, vwT, cb, tp, glT)


def _merge_kernel(x_ref, ys_ref, ynT_ref, mg_ref, ws_ref, wn_ref, wo_ref, g_ref, o_ref):
    d = x_ref.shape[1]
    a = _dot(ys_ref[...], ws_ref[...])
    bb = lax.dot_general(ynT_ref[...], wn_ref[...], (((0,), (0,)), ((), ())), preferred_element_type=F32)
    mg = _sigmoid(mg_ref[...].astype(F32))
    mixed = mg[:, :d] * a + mg[:, d:] * bb
    o = _dot(mixed.astype(BF16), wo_ref[...])
    o_ref[...] = x_ref[...] + _rms(o, g_ref[...])


def _merge(x2d, y_ssd, y_nsaT, mg, w_s, w_n, w_o, g, tm=512):
    m, d = x2d.shape
    tm = min(tm, m)
    row = lambda a: pl.BlockSpec((tm, a.shape[1]), lambda i: (i, 0))
    full = lambda a: pl.BlockSpec(a.shape, lambda i: (0, 0))
    g = g.reshape(1, d)
    return pl.pallas_call(
        _merge_kernel, grid=(m // tm,),
        in_specs=[row(x2d), row(y_ssd), pl.BlockSpec((y_nsaT.shape[0], tm), lambda i: (0, i)), row(mg),
                  full(w_s), full(w_n), full(w_o), full(g)],
        out_specs=pl.BlockSpec((tm, d), lambda i: (i, 0)),
        out_shape=jax.ShapeDtypeStruct((m, d), F32),
        compiler_params=_cparams(1), name="merge",
    )(x2d, y_ssd, y_nsaT, mg, w_s, w_n, w_o, g)


def _xattn_kernel(x_ref, kv_ref, gpre_ref, wq_ref, wo_ref, gpost_ref, o_ref):
    x = x_ref[...]
    h = _rms(x, gpre_ref[...]).astype(BF16)
    q = _dot(h, wq_ref[...])
    width = X_HEADS * X_HEAD_DIM
    scale = X_HEAD_DIM ** -0.5
    outs = []
    for hh in range(X_HEADS):
        sl = slice(hh * X_HEAD_DIM, (hh + 1) * X_HEAD_DIM)
        lg = _dot_nt(q[:, sl].astype(BF16), kv_ref[:, sl]) * scale
        e = jnp.exp(lg - jnp.max(lg, axis=-1, keepdims=True))
        p = e / jnp.sum(e, axis=-1, keepdims=True)
        outs.append(_dot(p.astype(BF16), kv_ref[:, width + hh * X_HEAD_DIM:width + (hh + 1) * X_HEAD_DIM]))
    o = _dot(jnp.concatenate(outs, axis=-1).astype(BF16), wo_ref[...])
    o_ref[...] = x + _rms(o, gpost_ref[...])


def _xattn(x3d, kv, g_pre, w_q, w_o, g_post, tm=512):
    b, s, d = x3d.shape
    tm = min(tm, s)
    ml = kv.shape[1]
    full = lambda a: pl.BlockSpec(a.shape, lambda i, j: (0, 0))
    g_pre, g_post = g_pre.reshape(1, d), g_post.reshape(1, d)
    return pl.pallas_call(
        _xattn_kernel, grid=(b, s // tm),
        in_specs=[pl.BlockSpec((None, tm, d), lambda i, j: (i, j, 0)),
                  pl.BlockSpec((None, ml, kv.shape[2]), lambda i, j: (i, 0, 0)),
                  full(g_pre), full(w_q), full(w_o), full(g_post)],
        out_specs=pl.BlockSpec((None, tm, d), lambda i, j: (i, j, 0)),
        out_shape=jax.ShapeDtypeStruct((b, s, d), F32),
        compiler_params=_cparams(2), name="xattn",
    )(x3d, kv, g_pre, w_q, w_o, g_post)


def _mlp_kernel(x_ref, gpre_ref, w1_ref, w2_ref, gpost_ref, o_ref, *, f_chunk):
    x = x_ref[...]
    h = _rms(x, gpre_ref[...]).astype(BF16)
    d_ff = w1_ref.shape[1]
    acc = jnp.zeros(x.shape, F32)
    for f0 in range(0, d_ff, f_chunk):
        u = jnp.maximum(_dot(h, w1_ref[:, f0:f0 + f_chunk]), 0.0)
        acc = acc + _dot((u * u).astype(BF16), w2_ref[f0:f0 + f_chunk, :])
    o_ref[...] = x + _rms(acc, gpost_ref[...])


def _mlp(x2d, g_pre, w1, w2, g_post, tm=512):
    m, d = x2d.shape
    tm = min(tm, m)
    full = lambda a: pl.BlockSpec(a.shape, lambda i: (0, 0))
    g_pre, g_post = g_pre.reshape(1, d), g_post.reshape(1, d)
    return pl.pallas_call(
        functools.partial(_mlp_kernel, f_chunk=1024), grid=(m // tm,),
        in_specs=[pl.BlockSpec((tm, d), lambda i: (i, 0)), full(g_pre), full(w1), full(w2), full(g_post)],
        out_specs=pl.BlockSpec((tm, d), lambda i: (i, 0)),
        out_shape=jax.ShapeDtypeStruct((m, d), F32),
        compiler_params=_cparams(1), name="mlp",
    )(x2d, g_pre, w1, w2, g_post)


def _pad_heads(w, n_heads, hd):
    k = w.shape[0]
    w = w.reshape(k, n_heads, hd)
    return jnp.pad(w, ((0, 0), (0, 0), (0, LANES - hd))).reshape(k, n_heads * LANES)


def _layer(x, mem, w_in, conv_w, conv_b, dt_bias, a_log, d_skip, ssd_norm, cmp_pos, cmp_w1, cmp_w2,
           rel_bias, w_br_ssd, w_br_nsa, w_out, w_xq, w_xkv, w_xo, w_ff1, w_ff2,
           n_mix_pre, n_mix_post, n_x_pre, n_x_post, n_mem, n_ffn_pre, n_ffn_post):
    b, s, d = x.shape
    G, R, HD = NSA_KV_HEADS, NSA_REP, NSA_HEAD_DIM
    d_inner = 2 * d
    n_ssd_heads = d_inner // SSD_HEAD_DIM
    conv_dim = d_inner + 2 * SSD_GROUPS * SSD_STATE
    nsa_w, kv_w = NSA_HEADS * HD, G * HD
    sizes = (d_inner, conv_dim, n_ssd_heads, nsa_w, 6 * kv_w, 3 * NSA_HEADS, 2 * d)
    offs = np.concatenate([[0], np.cumsum(sizes)])
    seg = lambda i: w_in[:, offs[i]:offs[i + 1]]
    w_z, w_xbc, w_dt, w_q, w_kv, w_gate, w_mg = (seg(i) for i in range(7))
    w_small = jnp.pad(jnp.concatenate([w_dt, w_gate], axis=1),
                      ((0, 0), (0, LANES - n_ssd_heads - 3 * NSA_HEADS)))
    w_kv6 = w_kv.reshape(d, 6, kv_w)
    w_cmp = jnp.concatenate([w_kv6[:, 0], w_kv6[:, 1]], axis=1)
    bf = lambda a: a.astype(BF16)

    x2d = x.reshape(b * s, d)
    z, xbc = _norm_matmul(x2d, n_mix_pre, [bf(w_z), bf(w_xbc)], [BF16, BF16])
    w_gl = jnp.transpose(w_gate.reshape(d, 3, G, R), (2, 1, 3, 0)).reshape(G, 3 * R, d)
    w_gl = jnp.pad(w_gl, ((0, 0), (0, 16 - 3 * R), (0, 0))).reshape(G * 16, d)
    small, dtT, glT, qT, kv_cmp, vslT, vwT = _norm_matmul(
        x2d, n_mix_pre,
        [bf(w_small), bf(w_dt.T), bf(w_gl), bf(w_q.T * (HD ** -0.5 * LOG2E)), bf(w_cmp),
         bf(w_kv6[:, 3].T), bf(w_kv6[:, 5].T)],
        [F32, F32, F32, BF16, BF16, BF16, BF16],
        transposed=(False, True, True, True, False, True, True))
    ksl, kw, mg = _norm_matmul(
        x2d, n_mix_pre, [bf(_pad_heads(w_kv6[:, 2], G, HD)), bf(_pad_heads(w_kv6[:, 4], G, HD)), bf(w_mg)],
        [BF16] * 3)

    y_ssd = _ssd_mixer(z.reshape(b, s, -1), xbc.reshape(b, s, -1), small.reshape(b, s, -1), dtT,
                       conv_w, conv_b, dt_bias, a_log, d_skip, ssd_norm)

    nr = s // CMP_STRIDE
    kr = kv_cmp.reshape(b, nr, CMP_STRIDE, 2, G, HD)
    kr = jnp.transpose(kr, (3, 0, 4, 1, 2, 5)).reshape(2, b, G, nr, CMP_STRIDE * HD)
    pos = jnp.broadcast_to(bf(cmp_pos).reshape(2, 1, CMP_BLOCK * HD), (2, SUBLANES, CMP_BLOCK * HD))
    w2p = jnp.pad(bf(cmp_w2), ((0, 0), (0, 0), (0, LANES - HD)))
    kvc = _compress(kr, pos, bf(cmp_w1), w2p)
    vcT = jnp.swapaxes(kvc[1][..., :HD], -1, -2)
    cb, tp = _nsa_tables(rel_bias, nr)
    y_nsa = _nsa_attention(qT.reshape(NSA_HEADS, HD, b * s), kvc[0], vcT,
                           ksl.reshape(b, s, -1), vslT.reshape(G, HD, b * s),
                           kw.reshape(b, s, -1), vwT.reshape(G, HD, b * s), cb, tp,
                           glT.reshape(G, 16, b * s))

    x1 = _merge(x2d, y_ssd.reshape(b * s, -1), y_nsa, mg, bf(w_br_ssd), bf(w_br_nsa), bf(w_out), n_mix_post)

    (kv_mem,) = _norm_matmul(mem.reshape(-1, d), n_mem, [bf(w_xkv)], [BF16])
    x2 = _xattn(x1.reshape(b, s, d), kv_mem.reshape(b, mem.shape[1], -1), n_x_pre, bf(w_xq), bf(w_xo),
                n_x_post)

    x3 = _mlp(x2.reshape(b * s, d), n_ffn_pre, bf(w_ff1), bf(w_ff2), n_ffn_post)
    return x3.reshape(b, s, d)


def kernel(x, mem, w_in, ssd_conv_w, ssd_conv_b, ssd_dt_bias, ssd_a_log, ssd_d_skip, ssd_norm, cmp_pos,
           cmp_w1, cmp_w2, rel_bias, w_br_ssd, w_br_nsa, w_out, w_xq, w_xkv, w_xo, w_ff1, w_ff2,
           norm_mix_pre, norm_mix_post, norm_x_pre, norm_x_post, norm_mem, norm_ffn_pre, norm_ffn_post):
    for l in range(w_in.shape[0]):
        x = _layer(x, mem, w_in[l], ssd_conv_w[l], ssd_conv_b[l], ssd_dt_bias[l], ssd_a_log[l],
                   ssd_d_skip[l], ssd_norm[l], cmp_pos[l], cmp_w1[l], cmp_w2[l], rel_bias,
                   w_br_ssd[l], w_br_nsa[l], w_out[l], w_xq[l], w_xkv[l], w_xo[l], w_ff1[l], w_ff2[l],
                   norm_mix_pre[l], norm_mix_post[l], norm_x_pre[l], norm_x_post[l], norm_mem[l],
                   norm_ffn_pre[l], norm_ffn_post[l])
    return x
```

```python
import functools
import math

import numpy as np
import jax
import jax.numpy as jnp
from jax import lax
from jax.experimental import pallas as pl
from jax.experimental.pallas import tpu as pltpu

F32 = jnp.float32
BF16 = jnp.bfloat16

NORM_EPS = 1e-6
NEG_INF = -1e30
FORCE_SCORE = 1e9
SEL_PENALTY = NEG_INF

LANES = 128
SUBLANES = 8
VMEM_LIMIT = 56 * 1024 * 1024

SSD_HEAD_DIM = 64
SSD_GROUPS = 8
SSD_STATE = 128
SSD_CONV = 4
SSD_CHUNK = 128
NSA_HEADS = 16
NSA_KV_HEADS = 4
NSA_HEAD_DIM = 64
NSA_REP = NSA_HEADS // NSA_KV_HEADS
CMP_BLOCK = 32
CMP_STRIDE = 16
SLC_BLOCK = 64
SLC_TOPK = 16
WINDOW = 512
NSA_TQ = 256
LOG2E = math.log2(math.e)
REL_BUCKETS = 32
REL_MAX_DIST = 128
X_HEADS = 4
X_HEAD_DIM = 128

TILE_DIAG, TILE_NEAR, TILE_NONE, TILE_EDGE, TILE_ZERO = range(5)


def _cparams(n_grid):
    return pltpu.CompilerParams(dimension_semantics=("arbitrary",) * n_grid,
                                vmem_limit_bytes=VMEM_LIMIT)


def _sigmoid(x):
    return jax.nn.sigmoid(x)


def _rms(x, g):
    return x * lax.rsqrt(jnp.mean(x * x, axis=-1, keepdims=True) + NORM_EPS) * g


def _dot(a, b):
    return jnp.dot(a, b, preferred_element_type=F32)


def _dot_nt(a, b):
    return lax.dot_general(a, b, (((1,), (1,)), ((), ())), preferred_element_type=F32)


def _split3(x):
    x1 = x.astype(BF16)
    r1 = x - x1.astype(F32)
    x2 = r1.astype(BF16)
    x3 = (r1 - x2.astype(F32)).astype(BF16)
    return x1, x2, x3


def _norm_matmul_kernel(x_ref, g_ref, *refs, transposed, n_chunk):
    n_out = len(transposed)
    w_refs, o_refs = refs[:n_out], refs[n_out:]
    h = _rms(x_ref[...], g_ref[...]).astype(BF16)
    for w_ref, o_ref, tr in zip(w_refs, o_refs, transposed):
        if tr:
            o_ref[...] = _dot_nt(w_ref[...], h).astype(o_ref.dtype)
            continue
        n = w_ref.shape[1]
        step = min(n, n_chunk)
        for n0 in range(0, n, step):
            o_ref[:, n0:n0 + step] = _dot(h, w_ref[:, n0:n0 + step]).astype(o_ref.dtype)


def _norm_matmul(x2d, g, ws, out_dtypes, transposed=None, tm=512):
    m, k = x2d.shape
    tm = min(tm, m)
    transposed = tuple(transposed) if transposed is not None else (False,) * len(ws)
    in_specs = [pl.BlockSpec((tm, k), lambda i: (i, 0)), pl.BlockSpec((1, k), lambda i: (0, 0))]
    in_specs += [pl.BlockSpec(w.shape, lambda i: (0, 0)) for w in ws]
    out_specs, out_shape = [], []
    for w, dt, tr in zip(ws, out_dtypes, transposed):
        if tr:
            out_specs.append(pl.BlockSpec((w.shape[0], tm), lambda i: (0, i)))
            out_shape.append(jax.ShapeDtypeStruct((w.shape[0], m), dt))
        else:
            out_specs.append(pl.BlockSpec((tm, w.shape[1]), lambda i: (i, 0)))
            out_shape.append(jax.ShapeDtypeStruct((m, w.shape[1]), dt))
    return pl.pallas_call(
        functools.partial(_norm_matmul_kernel, transposed=transposed, n_chunk=1024),
        grid=(m // tm,), in_specs=in_specs, out_specs=out_specs, out_shape=out_shape,
        compiler_params=_cparams(1), name="norm_proj",
    )(x2d, g.reshape(1, k), *ws)


def _softplus(x):
    return jnp.maximum(x, 0.0) + jnp.log(1.0 + jnp.exp(-jnp.abs(x)))


def _ssd_kernel(xbc_ref, prev_ref, z_ref, dt_ref, dtT_ref, cw_ref, cb_ref, dtb_ref, dtbT_ref,
                alog_ref, alogT_ref, dskip_ref, nw_ref, y_ref, state_ref, xc_ref, ybuf_ref,
                *, n_heads, d_inner):
    L, P, N, G = SSD_CHUNK, SSD_HEAD_DIM, SSD_STATE, SSD_GROUPS
    conv_dim = xc_ref.shape[1]
    c = pl.program_id(1)

    @pl.when(c == 0)
    def _():
        state_ref[...] = jnp.zeros_like(state_ref)

    cw = 512
    prev_rows = prev_ref.shape[0]
    row = lax.broadcasted_iota(jnp.int32, (prev_rows, cw), 0)
    for j in range(conv_dim // cw):
        sl = slice(j * cw, (j + 1) * cw)
        cur = xbc_ref[:, sl].astype(F32)
        prev = jnp.where(c == 0, 0.0, prev_ref[:, sl].astype(F32))
        acc = cb_ref[:, sl] + cw_ref[SSD_CONV - 1:SSD_CONV, sl] * cur
        for k in range(1, SSD_CONV):
            rc = pltpu.roll(cur, k, 0)
            rp = pltpu.roll(prev, k, 0)
            head = jnp.where(row < k, rp, rc[:prev_rows])
            shifted = jnp.concatenate([head, rc[prev_rows:]], axis=0)
            acc = acc + cw_ref[SSD_CONV - 1 - k:SSD_CONV - k, sl] * shifted
        xc_ref[:, sl] = acc * _sigmoid(acc)

    dt = _softplus(dt_ref[:, 0:n_heads] + dtb_ref[...])
    a = dt * (-LOG2E * jnp.exp(alog_ref[...]))
    dtT = _softplus(dtT_ref[...] + dtbT_ref[...])
    aT = dtT * (-LOG2E * jnp.exp(alogT_ref[...]))
    ri = lax.broadcasted_iota(jnp.int32, (L, L), 0)
    ci = lax.broadcasted_iota(jnp.int32, (L, L), 1)
    causal = ci <= ri
    tri = jnp.where(causal, 1.0, 0.0).astype(BF16)
    triu = jnp.where(ri <= ci, 1.0, 0.0).astype(BF16)
    a_cs = sum(_dot(tri, ai) for ai in _split3(a))
    a_csT = sum(_dot(ai, triu) for ai in _split3(aT))
    lane = lax.broadcasted_iota(jnp.int32, (L, 2 * P), 1)
    lo_half = lane < P
    lane1 = lax.broadcasted_iota(jnp.int32, (1, 2 * P), 1) < P

    heads_per_group = n_heads // G
    for g in range(G):
        b_g = xc_ref[:, d_inner + g * N:d_inner + (g + 1) * N]
        c_g = xc_ref[:, d_inner + G * N + g * N:d_inner + G * N + (g + 1) * N]
        c_gb = c_g.astype(BF16)
        cb = _dot_nt(c_gb, b_g.astype(BF16))
        b_gT = b_g.T.astype(BF16)
        for pp in range(heads_per_group // 2):
            pair = g * (heads_per_group // 2) + pp
            h0, h1 = 2 * pair, 2 * pair + 1
            xs = xc_ref[:, pair * 2 * P:(pair + 1) * 2 * P]
            dt_pair = jnp.where(lo_half, dt[:, h0:h0 + 1], dt[:, h1:h1 + 1])
            xd = xs * dt_pair
            xd_b = xd.astype(BF16)
            ys = []
            for h in (h0, h1):
                seg = jnp.where(causal, a_cs[:, h:h + 1] - a_csT[h:h + 1, :], NEG_INF)
                mat = (cb * jnp.exp2(seg)).astype(BF16)
                ys.append(_dot(mat, xd_b))
            y = jnp.where(lo_half, ys[0], ys[1])
            cs_pair = jnp.where(lo_half, a_cs[:, h0:h0 + 1], a_cs[:, h1:h1 + 1])
            st = state_ref[pair]
            y = y + _dot(c_gb, st.astype(BF16)) * jnp.exp2(cs_pair)
            tot = jnp.where(lane1, a_csT[h0:h0 + 1, L - 1:L], a_csT[h1:h1 + 1, L - 1:L])
            xdd = (xd * jnp.exp2(tot - cs_pair)).astype(BF16)
            state_ref[pair] = st * jnp.exp2(tot) + _dot(b_gT, xdd)
            ybuf_ref[:, pair * 2 * P:(pair + 1) * 2 * P] = y + xs * dskip_ref[:, pair * 2 * P:(pair + 1) * 2 * P]

    gw = d_inner // G
    for g in range(G):
        sl = slice(g * gw, (g + 1) * gw)
        zz = z_ref[:, sl].astype(F32)
        yg = ybuf_ref[:, sl] * (zz * _sigmoid(zz))
        y_ref[:, sl] = _rms(yg, nw_ref[:, sl]).astype(y_ref.dtype)


def _ssd_mixer(z, xbc, small, dtT, conv_w, conv_b, dt_bias, a_log, d_skip, norm_w):
    b, s, d_inner = z.shape
    conv_dim = xbc.shape[-1]
    n_heads = d_inner // SSD_HEAD_DIM
    L = SSD_CHUNK
    nc = s // L
    prev_rows = 16
    xbc_prev = xbc.reshape(b, s // prev_rows, prev_rows, conv_dim)
    blocks_per_chunk = L // prev_rows
    kern = functools.partial(_ssd_kernel, n_heads=n_heads, d_inner=d_inner)
    const = lambda shape: pl.BlockSpec(shape, lambda i, j: (0,) * len(shape))
    return pl.pallas_call(
        kern, grid=(b, nc),
        in_specs=[
            pl.BlockSpec((None, L, conv_dim), lambda i, j: (i, j, 0)),
            pl.BlockSpec((None, None, prev_rows, conv_dim),
                         lambda i, j: (i, jnp.maximum(j * blocks_per_chunk - 1, 0), 0, 0)),
            pl.BlockSpec((None, L, d_inner), lambda i, j: (i, j, 0)),
            pl.BlockSpec((None, L, small.shape[-1]), lambda i, j: (i, j, 0)),
            pl.BlockSpec((n_heads, L), lambda i, j: (0, i * nc + j)),
            const((SSD_CONV, conv_dim)), const((1, conv_dim)),
            const((1, n_heads)), const((n_heads, 1)),
            const((1, n_heads)), const((n_heads, 1)),
            const((1, d_inner)), const((1, d_inner)),
        ],
        out_specs=pl.BlockSpec((None, L, d_inner), lambda i, j: (i, j, 0)),
        out_shape=jax.ShapeDtypeStruct((b, s, d_inner), BF16),
        scratch_shapes=[pltpu.VMEM((n_heads // 2, SSD_STATE, 2 * SSD_HEAD_DIM), F32),
                        pltpu.VMEM((L, conv_dim), F32),
                        pltpu.VMEM((L, d_inner), F32)],
        compiler_params=_cparams(2), name="ssd",
    )(xbc, xbc_prev, z, small, dtT, conv_w.astype(F32), conv_b.reshape(1, -1).astype(F32),
      dt_bias.reshape(1, -1), dt_bias.reshape(-1, 1), a_log.reshape(1, -1), a_log.reshape(-1, 1),
      jnp.repeat(d_skip, SSD_HEAD_DIM).reshape(1, -1), norm_w.reshape(1, -1))


def _compress_kernel(kr_ref, pos_ref, w1_ref, w2_ref, o_ref):
    nr, half = kr_ref.shape
    kr = kr_ref[...]
    p1 = _dot(kr, w1_ref[0:half, :])
    p2 = _dot(kr, w1_ref[half:2 * half, :])
    pb = _dot(pos_ref[...], w1_ref[...])[0:1]
    hid = p1 + pltpu.roll(p2, nr - 1, 0) + pb
    hid = hid * _sigmoid(hid)
    out = _dot(hid.astype(BF16), w2_ref[...])
    rows = lax.broadcasted_iota(jnp.int32, out.shape, 0)
    o_ref[...] = jnp.where(rows < nr - 1, out, 0.0).astype(o_ref.dtype)


def _compress(kr, pos, w1, w2p):
    two, b, g, nr, half = kr.shape
    hidden = w1.shape[-1]
    return pl.pallas_call(
        _compress_kernel,
        grid=(two, b, g),
        in_specs=[
            pl.BlockSpec((None, None, None, nr, half), lambda t, i, j: (t, i, j, 0, 0)),
            pl.BlockSpec((None, SUBLANES, 2 * half), lambda t, i, j: (t, 0, 0)),
            pl.BlockSpec((None, 2 * half, hidden), lambda t, i, j: (t, 0, 0)),
            pl.BlockSpec((None, hidden, LANES), lambda t, i, j: (t, 0, 0)),
        ],
        out_specs=pl.BlockSpec((None, None, None, nr, LANES), lambda t, i, j: (t, i, j, 0, 0)),
        out_shape=jax.ShapeDtypeStruct((two, b, g, nr, LANES), BF16),
        compiler_params=_cparams(3), name="nsa_compress",
    )(kr, pos, w1, w2p)


def _colmax(a):
    return jnp.max(a, axis=0, keepdims=True)


def _colsum(a):
    return jnp.sum(a, axis=0, keepdims=True)


class _SoftmaxPipe:
    def __init__(self, s_ref, pe_ref, acc_ref, hd):
        self.s, self.pe, self.acc, self.hd = s_ref, pe_ref, acc_ref, hd

    def start(self, first_scores, slot):
        cols = first_scores.shape[1]
        self.s[slot] = first_scores
        self.pe[1 - slot] = jnp.zeros(self.pe.shape[1:], self.pe.dtype)
        self.acc[...] = jnp.zeros(self.acc.shape, F32)
        return _colmax(first_scores), jnp.ones((1, cols), F32), jnp.full((1, cols), NEG_INF, F32)

    def step(self, slot, carry, v_prev, next_scores):
        cm_cur, alpha_prev, m_old = carry
        self.acc[...] = alpha_prev * self.acc[...] + _dot(v_prev, self.pe[1 - slot])
        cm_next = cm_cur
        if next_scores is not None:
            s_next = next_scores()
            self.s[1 - slot] = s_next
            cm_next = _colmax(s_next)
        m_new = jnp.maximum(m_old, cm_cur)
        self.pe[slot] = jnp.exp2(self.s[slot] - m_new).astype(self.pe.dtype)
        return cm_next, jnp.exp2(m_old - m_new), m_new

    def finish(self, slot_last, carry, v_last):
        _, alpha_prev, _ = carry
        acc = alpha_prev * self.acc[...] + _dot(v_last, self.pe[slot_last])
        return acc[0:self.hd] * (1.0 / acc[self.hd:self.hd + 1])

    def shifted_step(self, slot, v_prev, next_scores):
        self.pe[slot] = jnp.exp2(self.s[slot]).astype(self.pe.dtype)
        self.s[1 - slot] = next_scores()
        self.acc[...] += _dot(v_prev, self.pe[1 - slot])


def _nsa_kernel(qT_ref, kc_ref, vcT_ref, ov_ref, ksl_ref, vslT_ref, kw_ref, vwT_ref, cb_ref, tp_ref,
                gl_ref, o_ref, q0_ref, qa_ref, qs_ref, rank_ref, kst_ref, vsa_ref, vwa_ref, s_ref, pe_ref,
                acc_ref, ws_ref, wpe_ref, wacc_ref, *, n_slc, top_k):
    T, R, HD = NSA_TQ, NSA_REP, NSA_HEAD_DIM
    nr = kc_ref.shape[0]
    per_tile = T // CMP_STRIDE
    front = nr - per_tile
    half = LANES // 2
    qi = pl.program_id(2)
    colmax, colsum = _colmax, _colsum

    @pl.when(qi == 0)
    def _():
        for src, dst in ((vslT_ref, vsa_ref), (vwT_ref, vwa_ref)):
            dst[0:HD, :] = src[...]
            dst[HD:, :] = jnp.ones((dst.shape[0] - HD, dst.shape[1]), dst.dtype)
        krow = lax.broadcasted_iota(jnp.int32, (T, LANES), 0)
        klane = lax.broadcasted_iota(jnp.int32, (T, LANES), 1)
        for t0 in range(0, ksl_ref.shape[0], T):
            blk = t0 // SLC_BLOCK + krow // SLC_BLOCK
            tag = jnp.where(klane - half == blk, 1.0, 0.0).astype(BF16)
            kst_ref[t0:t0 + T, :] = jnp.where(klane >= half, tag, ksl_ref[t0:t0 + T, :])

    qT = jnp.concatenate([qT_ref[r] for r in range(R)], axis=1)
    q0_ref[0:HD, :] = qT
    q0_ref[HD:2 * HD, :] = jnp.zeros((HD, R * T), BF16)
    qa_ref[0:HD, :] = qT

    n_win = WINDOW // T

    def win_offset(rel):
        return pl.multiple_of(jnp.maximum(qi - rel, 0) * T, T)

    def win_scores(rel):
        tile = TILE_EDGE if rel == n_win else (TILE_ZERO if rel >= 2 else (TILE_NEAR if rel == 1 else TILE_DIAG))
        if rel > 0:
            tile = jnp.where(qi >= rel, tile, TILE_NONE)
        return _dot(kw_ref[pl.ds(win_offset(rel), T), :], q0_ref[...]) + tp_ref[tile]

    def win_values(rel):
        return vwa_ref[:, pl.ds(win_offset(rel), T)]

    win = _SoftmaxPipe(ws_ref, wpe_ref, wacc_ref, HD)
    rels = list(range(n_win, -1, -1))
    carry = win.start(win_scores(rels[0]), 0)
    for i, rel in enumerate(rels):
        nxt = functools.partial(win_scores, rels[i + 1]) if i + 1 < len(rels) else None
        carry = win.step(i % 2, carry, win_values(rels[max(i - 1, 0)]), nxt)
    o_win = win.finish((len(rels) - 1) % 2, carry, win_values(0))

    off = pl.multiple_of(front - qi * per_tile, SUBLANES)
    lg = _dot(kc_ref[...], q0_ref[...]) + cb_ref[pl.ds(off, nr), :]
    m_c = colmax(lg)
    e = jnp.exp2(lg - m_c)
    p = e * jnp.where(m_c > 0.5 * NEG_INF, 1.0 / colsum(e), 0.0)
    o_cmp = _dot(vcT_ref[...], p.astype(BF16))
    sg = _sigmoid(gl_ref[...])
    o_partial = [sg[r:r + 1] * o_cmp[:, r * T:(r + 1) * T] + sg[2 * R + r:2 * R + r + 1] * o_win[:, r * T:(r + 1) * T]
                 for r in range(R)]

    psum = p[:, 0:T]
    for r in range(1, R):
        psum = psum + p[:, r * T:(r + 1) * T]
    imp = sum(_dot(ov_ref[...], pi) for pi in _split3(psum))
    nio = lax.broadcasted_iota(jnp.int32, (half, T), 0)
    tio = lax.broadcasted_iota(jnp.int32, (half, T), 1)
    tb = (qi * T + tio) // SLC_BLOCK
    forced = (nio == 0) | (nio == tb) | (nio == tb - 1)
    imp = jnp.where(forced, FORCE_SCORE, imp)
    imp = jnp.where(nio > tb, NEG_INF, imp)
    n_grp = half // SUBLANES
    grp = [imp[SUBLANES * v:SUBLANES * (v + 1)] for v in range(n_grp)]
    nio8 = lax.broadcasted_iota(jnp.int32, (SUBLANES, T), 0)
    def count_outranking(mg):
        hits = [jnp.zeros((SUBLANES, T), F32) for _ in range(n_grp)]
        for mm in range(mg * SUBLANES, min((mg + 1) * SUBLANES, n_slc)):
            rowv = imp[mm:mm + 1, :]
            for v in range(n_grp):
                if SUBLANES * v > mm:
                    hit = jnp.where(rowv >= grp[v], 1.0, 0.0)
                elif SUBLANES * v + SUBLANES - 1 < mm:
                    hit = jnp.where(rowv > grp[v], 1.0, 0.0)
                else:
                    hit = jnp.where(nio8 > mm - SUBLANES * v, jnp.where(rowv >= grp[v], 1.0, 0.0),
                                    jnp.where(rowv > grp[v], 1.0, 0.0))
                hits[v] = hits[v] + hit
        return jnp.concatenate(hits, axis=0)

    rank_ref[...] = count_outranking(0)
    last_block = (qi * T + T - 1) // SLC_BLOCK
    for mg in range(1, pl.cdiv(n_slc, SUBLANES)):

        @pl.when(mg * SUBLANES <= last_block)
        def _():
            rank_ref[...] += count_outranking(mg)

    rank = rank_ref[...]
    pen = jnp.where(rank < top_k, jnp.where(nio <= tb, 0.0, SEL_PENALTY), SEL_PENALTY).astype(BF16)
    qa_ref[HD:2 * HD, :] = jnp.concatenate([pen] * R, axis=1)

    def sel_scores(kj):
        return _dot(kst_ref[pl.ds(pl.multiple_of(kj * T, T), T), :], qa_ref[...])

    def sel_values(kj):
        return vsa_ref[:, pl.ds(pl.multiple_of(kj * T, T), T)]

    n_far = jnp.maximum(qi - 1, 0)
    near = jnp.maximum(qi - 1, 0)
    near_table = jnp.where(qi >= 1, TILE_NEAR, TILE_NONE)
    odd = n_far % 2
    sel = _SoftmaxPipe(s_ref, pe_ref, acc_ref, HD)

    def exact_selected():
        def far_step(kj, slot, carry):
            return sel.step(slot, carry, sel_values(jnp.maximum(kj - 1, 0)), functools.partial(sel_scores, kj + 1))

        carry = sel.start(sel_scores(0), odd)
        carry = lax.cond(odd == 1, lambda c: far_step(0, 1, c), lambda c: c, carry)
        carry = lax.fori_loop(
            0, n_far // 2, lambda j, c: far_step(odd + 2 * j + 1, 1, far_step(odd + 2 * j, 0, c)), carry)
        s_near = s_ref[0] + tp_ref[near_table]
        s_ref[0] = s_near
        carry = (colmax(s_near),) + tuple(carry[1:])
        carry = sel.step(0, carry, sel_values(jnp.maximum(n_far - 1, 0)),
                         lambda: sel_scores(qi) + tp_ref[TILE_DIAG])
        carry = sel.step(1, carry, sel_values(n_far), None)
        return sel.finish(1, carry, sel_values(qi))

    s_d = sel_scores(qi) + tp_ref[TILE_DIAG]
    shift = colmax(s_d).astype(BF16).astype(F32)
    qs_ref[0:HD, :] = qT
    qs_ref[HD:2 * HD, :] = (jnp.concatenate([pen.astype(F32)] * R, axis=1) - shift).astype(BF16)

    def shifted_scores(kj):
        return _dot(kst_ref[pl.ds(pl.multiple_of(kj * T, T), T), :], qs_ref[...])

    def far_shifted(kj, slot):
        sel.shifted_step(slot, sel_values(jnp.maximum(kj - 1, 0)), functools.partial(shifted_scores, kj + 1))

    s_ref[odd] = shifted_scores(0)
    pe_ref[1 - odd] = jnp.zeros(pe_ref.shape[1:], BF16)
    acc_ref[...] = _dot(sel_values(qi), jnp.exp2(s_d - shift).astype(BF16))

    @pl.when(odd == 1)
    def _():
        far_shifted(0, 1)

    def far_pair(j, carry):
        far_shifted(odd + 2 * j, 0)
        far_shifted(odd + 2 * j + 1, 1)
        return carry

    lax.fori_loop(0, n_far // 2, far_pair, 0)
    pe_near = jnp.exp2(s_ref[0] + tp_ref[near_table]).astype(BF16)
    acc = (acc_ref[...] + _dot(sel_values(jnp.maximum(n_far - 1, 0)), pe_ref[1])
           + _dot(sel_values(near), pe_near))
    denom = acc[HD:HD + 1]
    unsafe = (jnp.sum(jnp.where(denom < 2.0 ** 100, 0.0, 1.0))
              + jnp.sum(jnp.where(jnp.abs(acc[0:HD]) < 3.0e38, 0.0, 1.0))) > 0.0
    o_slc = lax.cond(unsafe, exact_selected, lambda: acc[0:HD] * (1.0 / denom))

    for r in range(R):
        o = o_partial[r] + sg[R + r:R + r + 1] * o_slc[:, r * T:(r + 1) * T]
        o_ref[r * HD:(r + 1) * HD, :] = o.astype(o_ref.dtype)


def _t5_bucket_np(dist):
    n = np.maximum(dist, 0)
    max_exact = REL_BUCKETS // 2
    nf = np.maximum(n, 1).astype(np.float32)
    large = max_exact + (np.log(nf / np.float32(max_exact)) / np.float32(math.log(REL_MAX_DIST / max_exact))
                         * np.float32(REL_BUCKETS - max_exact)).astype(np.int32)
    large = np.minimum(large, REL_BUCKETS - 1)
    return np.where(n < max_exact, n, large).astype(np.int32)


def _nsa_tables(rel_bias, nr):
    T, G, R = NSA_TQ, NSA_KV_HEADS, NSA_REP
    table = rel_bias.astype(F32) * LOG2E
    per_tile = T // CMP_STRIDE
    front = nr - per_tile
    t = np.arange(T)[None, :]

    def lay(a):
        keys = a.shape[0]
        return jnp.transpose(a, (2, 0, 1)).reshape(G, R, keys, T).transpose(0, 2, 1, 3).reshape(G, keys, R * T)

    cend = CMP_STRIDE * (np.arange(front + nr)[:, None] - front) + CMP_BLOCK - 1
    dist = t - cend
    def lookup(d):
        onehot = jax.nn.one_hot(_t5_bucket_np(d), REL_BUCKETS, dtype=F32)
        return jnp.einsum("ktb,bh->kth", onehot, table, precision=lax.Precision.HIGHEST)

    cb = jnp.where((dist >= 0)[..., None], lookup(dist), NEG_INF)
    k = np.arange(T)[:, None]
    assert 2 * T - (T - 1) >= REL_MAX_DIST and WINDOW % T == 0 and WINDOW // T >= 2
    last = table[REL_BUCKETS - 1]
    diag = jnp.where((t - k >= 0)[..., None], lookup(t - k) - last, NEG_INF)
    near = lookup(T + t - k) - last
    none = jnp.full_like(near, NEG_INF)
    edge = jnp.where((k > t)[..., None], jnp.zeros_like(near), NEG_INF)
    tiles = [None] * 5
    tiles[TILE_DIAG], tiles[TILE_NEAR], tiles[TILE_NONE], tiles[TILE_EDGE] = diag, near, none, edge
    tiles[TILE_ZERO] = jnp.zeros_like(near)
    tp = jnp.stack([lay(a) for a in tiles], axis=1)
    return lay(cb), tp


def _overlap_matrix(nr, n_slc):
    half = LANES // 2
    n = np.arange(half)[:, None]
    c = np.arange(nr)[None, :]
    ov = ((CMP_STRIDE * c <= SLC_BLOCK * n + SLC_BLOCK - 1) & (CMP_STRIDE * c + CMP_BLOCK - 1 >= SLC_BLOCK * n)
          & (n < n_slc) & (c < nr - 1))
    return jnp.asarray(ov.astype(np.float32), dtype=BF16)


def _nsa_attention(qT, kc, vcT, ksl, vslT, kw, vwT, cb, tp, glT):
    n_heads, hd, _ = qT.shape
    b, s, _ = ksl.shape
    G, R, T = NSA_KV_HEADS, NSA_REP, NSA_TQ
    nr = kc.shape[-2]
    nq = s // T
    n_slc = s // SLC_BLOCK
    assert n_slc <= LANES // 2 and s % T == 0 and nr % SUBLANES == 0
    ov = _overlap_matrix(nr, n_slc)
    kern = functools.partial(_nsa_kernel, n_slc=n_slc, top_k=min(SLC_TOPK, n_slc))
    k_spec = pl.BlockSpec((None, s, LANES), lambda i, j, k: (i, 0, j))
    vT_spec = pl.BlockSpec((None, hd, s), lambda i, j, k: (j, 0, i))
    ones_rows = 16
    pipe_scratch = [pltpu.VMEM((2, T, R * T), F32), pltpu.VMEM((2, T, R * T), BF16),
                    pltpu.VMEM((hd + ones_rows, R * T), F32)]
    value_scratch = [pltpu.VMEM((hd + ones_rows, s), BF16)] * 2
    return pl.pallas_call(
        kern, grid=(b, G, nq),
        in_specs=[
            pl.BlockSpec((R, hd, T), lambda i, j, k: (j, 0, i * nq + k)),
            pl.BlockSpec((None, None, nr, LANES), lambda i, j, k: (i, j, 0, 0)),
            pl.BlockSpec((None, None, hd, nr), lambda i, j, k: (i, j, 0, 0)),
            pl.BlockSpec(ov.shape, lambda i, j, k: (0, 0)),
            k_spec, vT_spec, k_spec, vT_spec,
            pl.BlockSpec((None,) + cb.shape[1:], lambda i, j, k: (j, 0, 0)),
            pl.BlockSpec((None,) + tp.shape[1:], lambda i, j, k: (j, 0, 0, 0)),
            pl.BlockSpec((None, 16, T), lambda i, j, k: (j, 0, i * nq + k)),
        ],
        out_specs=pl.BlockSpec((R * hd, T), lambda i, j, k: (j, i * nq + k)),
        out_shape=jax.ShapeDtypeStruct((n_heads * hd, b * s), BF16),
        scratch_shapes=[pltpu.VMEM((2 * hd, R * T), BF16)] * 3
        + [pltpu.VMEM((LANES // 2, T), F32), pltpu.VMEM((s, LANES), BF16)]
        + value_scratch + pipe_scratch + pipe_scratch,
        compiler_params=_cparams(3), name="nsa_attention",
    )(qT, kc, vcT, ov, ksl, vslT, kw, vwT, cb, tp, glT)


def _merge_kernel(x_ref, ys_ref, ynT_ref, mg_ref, ws_ref, wn_ref, wo_ref, g_ref, o_ref):
    d = x_ref.shape[1]
    a = _dot(ys_ref[...], ws_ref[...])
    bb = lax.dot_general(ynT_ref[...], wn_ref[...], (((0,), (0,)), ((), ())), preferred_element_type=F32)
    mg = _sigmoid(mg_ref[...].astype(F32))
    mixed = mg[:, :d] * a + mg[:, d:] * bb
    o = _dot(mixed.astype(BF16), wo_ref[...])
    o_ref[...] = x_ref[...] + _rms(o, g_ref[...])


def _merge(x2d, y_ssd, y_nsaT, mg, w_s, w_n, w_o, g, tm=512):
    m, d = x2d.shape
    tm = min(tm, m)
    row = lambda a: pl.BlockSpec((tm, a.shape[1]), lambda i: (i, 0))
    full = lambda a: pl.BlockSpec(a.shape, lambda i: (0, 0))
    g = g.reshape(1, d)
    return pl.pallas_call(
        _merge_kernel, grid=(m // tm,),
        in_specs=[row(x2d), row(y_ssd), pl.BlockSpec((y_nsaT.shape[0], tm), lambda i: (0, i)), row(mg),
                  full(w_s), full(w_n), full(w_o), full(g)],
        out_specs=pl.BlockSpec((tm, d), lambda i: (i, 0)),
        out_shape=jax.ShapeDtypeStruct((m, d), F32),
        compiler_params=_cparams(1), name="merge",
    )(x2d, y_ssd, y_nsaT, mg, w_s, w_n, w_o, g)


def _xattn_kernel(x_ref, kv_ref, gpre_ref, wq_ref, wo_ref, gpost_ref, o_ref):
    x = x_ref[...]
    h = _rms(x, gpre_ref[...]).astype(BF16)
    q = _dot(h, wq_ref[...])
    width = X_HEADS * X_HEAD_DIM
    scale = X_HEAD_DIM ** -0.5
    outs = []
    for hh in range(X_HEADS):
        sl = slice(hh * X_HEAD_DIM, (hh + 1) * X_HEAD_DIM)
        lg = _dot_nt(q[:, sl].astype(BF16), kv_ref[:, sl]) * scale
        e = jnp.exp(lg - jnp.max(lg, axis=-1, keepdims=True))
        p = e / jnp.sum(e, axis=-1, keepdims=True)
        outs.append(_dot(p.astype(BF16), kv_ref[:, width + hh * X_HEAD_DIM:width + (hh + 1) * X_HEAD_DIM]))
    o = _dot(jnp.concatenate(outs, axis=-1).astype(BF16), wo_ref[...])
    o_ref[...] = x + _rms(o, gpost_ref[...])


def _xattn(x3d, kv, g_pre, w_q, w_o, g_post, tm=512):
    b, s, d = x3d.shape
    tm = min(tm, s)
    ml = kv.shape[1]
    full = lambda a: pl.BlockSpec(a.shape, lambda i, j: (0, 0))
    g_pre, g_post = g_pre.reshape(1, d), g_post.reshape(1, d)
    return pl.pallas_call(
        _xattn_kernel, grid=(b, s // tm),
        in_specs=[pl.BlockSpec((None, tm, d), lambda i, j: (i, j, 0)),
                  pl.BlockSpec((None, ml, kv.shape[2]), lambda i, j: (i, 0, 0)),
                  full(g_pre), full(w_q), full(w_o), full(g_post)],
        out_specs=pl.BlockSpec((None, tm, d), lambda i, j: (i, j, 0)),
        out_shape=jax.ShapeDtypeStruct((b, s, d), F32),
        compiler_params=_cparams(2), name="xattn",
    )(x3d, kv, g_pre, w_q, w_o, g_post)


def _mlp_kernel(x_ref, gpre_ref, w1_ref, w2_ref, gpost_ref, o_ref, *, f_chunk):
    x = x_ref[...]
    h = _rms(x, gpre_ref[...]).astype(BF16)
    d_ff = w1_ref.shape[1]
    acc = jnp.zeros(x.shape, F32)
    for f0 in range(0, d_ff, f_chunk):
        u = jnp.maximum(_dot(h, w1_ref[:, f0:f0 + f_chunk]), 0.0)
        acc = acc + _dot((u * u).astype(BF16), w2_ref[f0:f0 + f_chunk, :])
    o_ref[...] = x + _rms(acc, gpost_ref[...])


def _mlp(x2d, g_pre, w1, w2, g_post, tm=512):
    m, d = x2d.shape
    tm = min(tm, m)
    full = lambda a: pl.BlockSpec(a.shape, lambda i: (0, 0))
    g_pre, g_post = g_pre.reshape(1, d), g_post.reshape(1, d)
    return pl.pallas_call(
        functools.partial(_mlp_kernel, f_chunk=1024), grid=(m // tm,),
        in_specs=[pl.BlockSpec((tm, d), lambda i: (i, 0)), full(g_pre), full(w1), full(w2), full(g_post)],
        out_specs=pl.BlockSpec((tm, d), lambda i: (i, 0)),
        out_shape=jax.ShapeDtypeStruct((m, d), F32),
        compiler_params=_cparams(1), name="mlp",
    )(x2d, g_pre, w1, w2, g_post)


def _pad_heads(w, n_heads, hd):
    k = w.shape[0]
    w = w.reshape(k, n_heads, hd)
    return jnp.pad(w, ((0, 0), (0, 0), (0, LANES - hd))).reshape(k, n_heads * LANES)


def _layer(x, mem, w_in, conv_w, conv_b, dt_bias, a_log, d_skip, ssd_norm, cmp_pos, cmp_w1, cmp_w2,
           rel_bias, w_br_ssd, w_br_nsa, w_out, w_xq, w_xkv, w_xo, w_ff1, w_ff2,
           n_mix_pre, n_mix_post, n_x_pre, n_x_post, n_mem, n_ffn_pre, n_ffn_post):
    b, s, d = x.shape
    G, R, HD = NSA_KV_HEADS, NSA_REP, NSA_HEAD_DIM
    d_inner = 2 * d
    n_ssd_heads = d_inner // SSD_HEAD_DIM
    conv_dim = d_inner + 2 * SSD_GROUPS * SSD_STATE
    nsa_w, kv_w = NSA_HEADS * HD, G * HD
    sizes = (d_inner, conv_dim, n_ssd_heads, nsa_w, 6 * kv_w, 3 * NSA_HEADS, 2 * d)
    offs = np.concatenate([[0], np.cumsum(sizes)])
    seg = lambda i: w_in[:, offs[i]:offs[i + 1]]
    w_z, w_xbc, w_dt, w_q, w_kv, w_gate, w_mg = (seg(i) for i in range(7))
    w_small = jnp.pad(jnp.concatenate([w_dt, w_gate], axis=1),
                      ((0, 0), (0, LANES - n_ssd_heads - 3 * NSA_HEADS)))
    w_kv6 = w_kv.reshape(d, 6, kv_w)
    w_cmp = jnp.concatenate([w_kv6[:, 0], w_kv6[:, 1]], axis=1)
    bf = lambda a: a.astype(BF16)

    x2d = x.reshape(b * s, d)
    z, xbc = _norm_matmul(x2d, n_mix_pre, [bf(w_z), bf(w_xbc)], [BF16, BF16])
    w_gl = jnp.transpose(w_gate.reshape(d, 3, G, R), (2, 1, 3, 0)).reshape(G, 3 * R, d)
    w_gl = jnp.pad(w_gl, ((0, 0), (0, 16 - 3 * R), (0, 0))).reshape(G * 16, d)
    small, dtT, glT, qT, kv_cmp, vslT, vwT = _norm_matmul(
        x2d, n_mix_pre,
        [bf(w_small), bf(w_dt.T), bf(w_gl), bf(w_q.T * (HD ** -0.5 * LOG2E)), bf(w_cmp),
         bf(w_kv6[:, 3].T), bf(w_kv6[:, 5].T)],
        [F32, F32, F32, BF16, BF16, BF16, BF16],
        transposed=(False, True, True, True, False, True, True))
    ksl, kw, mg = _norm_matmul(
        x2d, n_mix_pre, [bf(_pad_heads(w_kv6[:, 2], G, HD)), bf(_pad_heads(w_kv6[:, 4], G, HD)), bf(w_mg)],
        [BF16] * 3)

    y_ssd = _ssd_mixer(z.reshape(b, s, -1), xbc.reshape(b, s, -1), small.reshape(b, s, -1), dtT,
                       conv_w, conv_b, dt_bias, a_log, d_skip, ssd_norm)

    nr = s // CMP_STRIDE
    kr = kv_cmp.reshape(b, nr, CMP_STRIDE, 2, G, HD)
    kr = jnp.transpose(kr, (3, 0, 4, 1, 2, 5)).reshape(2, b, G, nr, CMP_STRIDE * HD)
    pos = jnp.broadcast_to(bf(cmp_pos).reshape(2, 1, CMP_BLOCK * HD), (2, SUBLANES, CMP_BLOCK * HD))
    w2p = jnp.pad(bf(cmp_w2), ((0, 0), (0, 0), (0, LANES - HD)))
    kvc = _compress(kr, pos, bf(cmp_w1), w2p)
    vcT = jnp.swapaxes(kvc[1][..., :HD], -1, -2)
    cb, tp = _nsa_tables(rel_bias, nr)
    y_nsa = _nsa_attention(qT.reshape(NSA_HEADS, HD, b * s), kvc[0], vcT,
                           ksl.reshape(b, s, -1), vslT.reshape(G, HD, b * s),
                           kw.reshape(b, s, -1), vwT.reshape(G, HD, b * s), cb, tp,
                           glT.reshape(G, 16, b * s))

    x1 = _merge(x2d, y_ssd.reshape(b * s, -1), y_nsa, mg, bf(w_br_ssd), bf(w_br_nsa), bf(w_out), n_mix_post)

    (kv_mem,) = _norm_matmul(mem.reshape(-1, d), n_mem, [bf(w_xkv)], [BF16])
    x2 = _xattn(x1.reshape(b, s, d), kv_mem.reshape(b, mem.shape[1], -1), n_x_pre, bf(w_xq), bf(w_xo),
                n_x_post)

    x3 = _mlp(x2.reshape(b * s, d), n_ffn_pre, bf(w_ff1), bf(w_ff2), n_ffn_post)
    return x3.reshape(b, s, d)


def kernel(x, mem, w_in, ssd_conv_w, ssd_conv_b, ssd_dt_bias, ssd_a_log, ssd_d_skip, ssd_norm, cmp_pos,
           cmp_w1, cmp_w2, rel_bias, w_br_ssd, w_br_nsa, w_out, w_xq, w_xkv, w_xo, w_ff1, w_ff2,
           norm_mix_pre, norm_mix_post, norm_x_pre, norm_x_post, norm_mem, norm_ffn_pre, norm_ffn_post):
    for l in range(w_in.shape[0]):
        x = _layer(x, mem, w_in[l], ssd_conv_w[l], ssd_conv_b[l], ssd_dt_bias[l], ssd_a_log[l],
                   ssd_d_skip[l], ssd_norm[l], cmp_pos[l], cmp_w1[l], cmp_w2[l], rel_bias,
                   w_br_ssd[l], w_br_nsa[l], w_out[l], w_xq[l], w_xkv[l], w_xo[l], w_ff1[l], w_ff2[l],
                   norm_mix_pre[l], norm_mix_post[l], norm_x_pre[l], norm_x_post[l], norm_mem[l],
                   norm_ffn_pre[l], norm_ffn_post[l])
    return x
```

```python
import functools
import math

import numpy as np
import jax
import jax.numpy as jnp
from jax import lax
from jax.experimental import pallas as pl
from jax.experimental.pallas import tpu as pltpu

F32 = jnp.float32
BF16 = jnp.bfloat16

NORM_EPS = 1e-6
NEG_INF = -1e30
FORCE_SCORE = 1e9
SEL_PENALTY = NEG_INF

LANES = 128
SUBLANES = 8
VMEM_LIMIT = 56 * 1024 * 1024

SSD_HEAD_DIM = 64
SSD_GROUPS = 8
SSD_STATE = 128
SSD_CONV = 4
SSD_CHUNK = 128
NSA_HEADS = 16
NSA_KV_HEADS = 4
NSA_HEAD_DIM = 64
NSA_REP = NSA_HEADS // NSA_KV_HEADS
CMP_BLOCK = 32
CMP_STRIDE = 16
SLC_BLOCK = 64
SLC_TOPK = 16
WINDOW = 512
NSA_TQ = 256
LOG2E = math.log2(math.e)
REL_BUCKETS = 32
REL_MAX_DIST = 128
X_HEADS = 4
X_HEAD_DIM = 128

TILE_DIAG, TILE_NEAR, TILE_NONE, TILE_EDGE, TILE_ZERO = range(5)


def _cparams(n_grid):
    return pltpu.CompilerParams(dimension_semantics=("arbitrary",) * n_grid,
                                vmem_limit_bytes=VMEM_LIMIT)


def _sigmoid(x):
    return jax.nn.sigmoid(x)


def _rms(x, g):
    return x * lax.rsqrt(jnp.mean(x * x, axis=-1, keepdims=True) + NORM_EPS) * g


def _dot(a, b):
    return jnp.dot(a, b, preferred_element_type=F32)


def _dot_nt(a, b):
    return lax.dot_general(a, b, (((1,), (1,)), ((), ())), preferred_element_type=F32)


def _split3(x):
    x1 = x.astype(BF16)
    r1 = x - x1.astype(F32)
    x2 = r1.astype(BF16)
    x3 = (r1 - x2.astype(F32)).astype(BF16)
    return x1, x2, x3


def _norm_matmul_kernel(x_ref, g_ref, *refs, transposed, n_chunk):
    n_out = len(transposed)
    w_refs, o_refs = refs[:n_out], refs[n_out:]
    h = _rms(x_ref[...], g_ref[...]).astype(BF16)
    for w_ref, o_ref, tr in zip(w_refs, o_refs, transposed):
        if tr:
            o_ref[...] = _dot_nt(w_ref[...], h).astype(o_ref.dtype)
            continue
        n = w_ref.shape[1]
        step = min(n, n_chunk)
        for n0 in range(0, n, step):
            o_ref[:, n0:n0 + step] = _dot(h, w_ref[:, n0:n0 + step]).astype(o_ref.dtype)


def _norm_matmul(x2d, g, ws, out_dtypes, transposed=None, tm=512):
    m, k = x2d.shape
    tm = min(tm, m)
    transposed = tuple(transposed) if transposed is not None else (False,) * len(ws)
    in_specs = [pl.BlockSpec((tm, k), lambda i: (i, 0)), pl.BlockSpec((1, k), lambda i: (0, 0))]
    in_specs += [pl.BlockSpec(w.shape, lambda i: (0, 0)) for w in ws]
    out_specs, out_shape = [], []
    for w, dt, tr in zip(ws, out_dtypes, transposed):
        if tr:
            out_specs.append(pl.BlockSpec((w.shape[0], tm), lambda i: (0, i)))
            out_shape.append(jax.ShapeDtypeStruct((w.shape[0], m), dt))
        else:
            out_specs.append(pl.BlockSpec((tm, w.shape[1]), lambda i: (i, 0)))
            out_shape.append(jax.ShapeDtypeStruct((m, w.shape[1]), dt))
    return pl.pallas_call(
        functools.partial(_norm_matmul_kernel, transposed=transposed, n_chunk=1024),
        grid=(m // tm,), in_specs=in_specs, out_specs=out_specs, out_shape=out_shape,
        compiler_params=_cparams(1), name="norm_proj",
    )(x2d, g.reshape(1, k), *ws)


def _softplus(x):
    return jnp.maximum(x, 0.0) + jnp.log(1.0 + jnp.exp(-jnp.abs(x)))


def _ssd_kernel(xbc_ref, prev_ref, z_ref, dt_ref, dtT_ref, cw_ref, cb_ref, dtb_ref, dtbT_ref,
                alog_ref, alogT_ref, dskip_ref, nw_ref, y_ref, state_ref, xc_ref, ybuf_ref,
                *, n_heads, d_inner):
    L, P, N, G = SSD_CHUNK, SSD_HEAD_DIM, SSD_STATE, SSD_GROUPS
    conv_dim = xc_ref.shape[1]
    c = pl.program_id(1)

    @pl.when(c == 0)
    def _():
        state_ref[...] = jnp.zeros_like(state_ref)

    cw = 512
    prev_rows = prev_ref.shape[0]
    row = lax.broadcasted_iota(jnp.int32, (prev_rows, cw), 0)
    for j in range(conv_dim // cw):
        sl = slice(j * cw, (j + 1) * cw)
        cur = xbc_ref[:, sl].astype(F32)
        prev = jnp.where(c == 0, 0.0, prev_ref[:, sl].astype(F32))
        acc = cb_ref[:, sl] + cw_ref[SSD_CONV - 1:SSD_CONV, sl] * cur
        for k in range(1, SSD_CONV):
            rc = pltpu.roll(cur, k, 0)
            rp = pltpu.roll(prev, k, 0)
            head = jnp.where(row < k, rp, rc[:prev_rows])
            shifted = jnp.concatenate([head, rc[prev_rows:]], axis=0)
            acc = acc + cw_ref[SSD_CONV - 1 - k:SSD_CONV - k, sl] * shifted
        xc_ref[:, sl] = acc * _sigmoid(acc)

    dt = _softplus(dt_ref[:, 0:n_heads] + dtb_ref[...])
    a = dt * (-LOG2E * jnp.exp(alog_ref[...]))
    dtT = _softplus(dtT_ref[...] + dtbT_ref[...])
    aT = dtT * (-LOG2E * jnp.exp(alogT_ref[...]))
    ri = lax.broadcasted_iota(jnp.int32, (L, L), 0)
    ci = lax.broadcasted_iota(jnp.int32, (L, L), 1)
    causal = ci <= ri
    tri = jnp.where(causal, 1.0, 0.0).astype(BF16)
    triu = jnp.where(ri <= ci, 1.0, 0.0).astype(BF16)
    a_cs = sum(_dot(tri, ai) for ai in _split3(a))
    a_csT = sum(_dot(ai, triu) for ai in _split3(aT))
    lane = lax.broadcasted_iota(jnp.int32, (L, 2 * P), 1)
    lo_half = lane < P
    lane1 = lax.broadcasted_iota(jnp.int32, (1, 2 * P), 1) < P

    heads_per_group = n_heads // G
    for g in range(G):
        b_g = xc_ref[:, d_inner + g * N:d_inner + (g + 1) * N]
        c_g = xc_ref[:, d_inner + G * N + g * N:d_inner + G * N + (g + 1) * N]
        c_gb = c_g.astype(BF16)
        cb = _dot_nt(c_gb, b_g.astype(BF16))
        b_gT = b_g.T.astype(BF16)
        for pp in range(heads_per_group // 2):
            pair = g * (heads_per_group // 2) + pp
            h0, h1 = 2 * pair, 2 * pair + 1
            xs = xc_ref[:, pair * 2 * P:(pair + 1) * 2 * P]
            dt_pair = jnp.where(lo_half, dt[:, h0:h0 + 1], dt[:, h1:h1 + 1])
            xd = xs * dt_pair
            xd_b = xd.astype(BF16)
            ys = []
            for h in (h0, h1):
                seg = jnp.where(causal, a_cs[:, h:h + 1] - a_csT[h:h + 1, :], NEG_INF)
                mat = (cb * jnp.exp2(seg)).astype(BF16)
                ys.append(_dot(mat, xd_b))
            y = jnp.where(lo_half, ys[0], ys[1])
            cs_pair = jnp.where(lo_half, a_cs[:, h0:h0 + 1], a_cs[:, h1:h1 + 1])
            st = state_ref[pair]
            y = y + _dot(c_gb, st.astype(BF16)) * jnp.exp2(cs_pair)
            tot = jnp.where(lane1, a_csT[h0:h0 + 1, L - 1:L], a_csT[h1:h1 + 1, L - 1:L])
            xdd = (xd * jnp.exp2(tot - cs_pair)).astype(BF16)
            state_ref[pair] = st * jnp.exp2(tot) + _dot(b_gT, xdd)
            ybuf_ref[:, pair * 2 * P:(pair + 1) * 2 * P] = y + xs * dskip_ref[:, pair * 2 * P:(pair + 1) * 2 * P]

    gw = d_inner // G
    for g in range(G):
        sl = slice(g * gw, (g + 1) * gw)
        zz = z_ref[:, sl].astype(F32)
        yg = ybuf_ref[:, sl] * (zz * _sigmoid(zz))
        y_ref[:, sl] = _rms(yg, nw_ref[:, sl]).astype(y_ref.dtype)


def _ssd_mixer(z, xbc, small, dtT, conv_w, conv_b, dt_bias, a_log, d_skip, norm_w):
    b, s, d_inner = z.shape
    conv_dim = xbc.shape[-1]
    n_heads = d_inner // SSD_HEAD_DIM
    L = SSD_CHUNK
    nc = s // L
    prev_rows = 16
    xbc_prev = xbc.reshape(b, s // prev_rows, prev_rows, conv_dim)
    blocks_per_chunk = L // prev_rows
    kern = functools.partial(_ssd_kernel, n_heads=n_heads, d_inner=d_inner)
    const = lambda shape: pl.BlockSpec(shape, lambda i, j: (0,) * len(shape))
    return pl.pallas_call(
        kern, grid=(b, nc),
        in_specs=[
            pl.BlockSpec((None, L, conv_dim), lambda i, j: (i, j, 0)),
            pl.BlockSpec((None, None, prev_rows, conv_dim),
                         lambda i, j: (i, jnp.maximum(j * blocks_per_chunk - 1, 0), 0, 0)),
            pl.BlockSpec((None, L, d_inner), lambda i, j: (i, j, 0)),
            pl.BlockSpec((None, L, small.shape[-1]), lambda i, j: (i, j, 0)),
            pl.BlockSpec((n_heads, L), lambda i, j: (0, i * nc + j)),
            const((SSD_CONV, conv_dim)), const((1, conv_dim)),
            const((1, n_heads)), const((n_heads, 1)),
            const((1, n_heads)), const((n_heads, 1)),
            const((1, d_inner)), const((1, d_inner)),
        ],
        out_specs=pl.BlockSpec((None, L, d_inner), lambda i, j: (i, j, 0)),
        out_shape=jax.ShapeDtypeStruct((b, s, d_inner), BF16),
        scratch_shapes=[pltpu.VMEM((n_heads // 2, SSD_STATE, 2 * SSD_HEAD_DIM), F32),
                        pltpu.VMEM((L, conv_dim), F32),
                        pltpu.VMEM((L, d_inner), F32)],
        compiler_params=_cparams(2), name="ssd",
    )(xbc, xbc_prev, z, small, dtT, conv_w.astype(F32), conv_b.reshape(1, -1).astype(F32),
      dt_bias.reshape(1, -1), dt_bias.reshape(-1, 1), a_log.reshape(1, -1), a_log.reshape(-1, 1),
      jnp.repeat(d_skip, SSD_HEAD_DIM).reshape(1, -1), norm_w.reshape(1, -1))


def _compress_kernel(kr_ref, pos_ref, w1_ref, w2_ref, o_ref):
    nr, half = kr_ref.shape
    kr = kr_ref[...]
    p1 = _dot(kr, w1_ref[0:half, :])
    p2 = _dot(kr, w1_ref[half:2 * half, :])
    pb = _dot(pos_ref[...], w1_ref[...])[0:1]
    hid = p1 + pltpu.roll(p2, nr - 1, 0) + pb
    hid = hid * _sigmoid(hid)
    out = _dot(hid.astype(BF16), w2_ref[...])
    rows = lax.broadcasted_iota(jnp.int32, out.shape, 0)
    o_ref[...] = jnp.where(rows < nr - 1, out, 0.0).astype(o_ref.dtype)


def _compress(kr, pos, w1, w2p):
    two, b, g, nr, half = kr.shape
    hidden = w1.shape[-1]
    return pl.pallas_call(
        _compress_kernel,
        grid=(two, b, g),
        in_specs=[
            pl.BlockSpec((None, None, None, nr, half), lambda t, i, j: (t, i, j, 0, 0)),
            pl.BlockSpec((None, SUBLANES, 2 * half), lambda t, i, j: (t, 0, 0)),
            pl.BlockSpec((None, 2 * half, hidden), lambda t, i, j: (t, 0, 0)),
            pl.BlockSpec((None, hidden, LANES), lambda t, i, j: (t, 0, 0)),
        ],
        out_specs=pl.BlockSpec((None, None, None, nr, LANES), lambda t, i, j: (t, i, j, 0, 0)),
        out_shape=jax.ShapeDtypeStruct((two, b, g, nr, LANES), BF16),
        compiler_params=_cparams(3), name="nsa_compress",
    )(kr, pos, w1, w2p)


def _colmax(a):
    return jnp.max(a, axis=0, keepdims=True)


def _colsum(a):
    return jnp.sum(a, axis=0, keepdims=True)


class _SoftmaxPipe:
    def __init__(self, s_ref, pe_ref, acc_ref, hd):
        self.s, self.pe, self.acc, self.hd = s_ref, pe_ref, acc_ref, hd

    def start(self, first_scores, slot):
        cols = first_scores.shape[1]
        self.s[slot] = first_scores
        self.pe[1 - slot] = jnp.zeros(self.pe.shape[1:], self.pe.dtype)
        self.acc[...] = jnp.zeros(self.acc.shape, F32)
        return _colmax(first_scores), jnp.ones((1, cols), F32), jnp.full((1, cols), NEG_INF, F32)

    def step(self, slot, carry, v_prev, next_scores):
        cm_cur, alpha_prev, m_old = carry
        self.acc[...] = alpha_prev * self.acc[...] + _dot(v_prev, self.pe[1 - slot])
        cm_next = cm_cur
        if next_scores is not None:
            s_next = next_scores()
            self.s[1 - slot] = s_next
            cm_next = _colmax(s_next)
        m_new = jnp.maximum(m_old, cm_cur)
        self.pe[slot] = jnp.exp2(self.s[slot] - m_new).astype(self.pe.dtype)
        return cm_next, jnp.exp2(m_old - m_new), m_new

    def finish(self, slot_last, carry, v_last):
        _, alpha_prev, _ = carry
        acc = alpha_prev * self.acc[...] + _dot(v_last, self.pe[slot_last])
        return acc[0:self.hd] * (1.0 / acc[self.hd:self.hd + 1])

    def shifted_step(self, slot, v_prev, next_scores):
        self.pe[slot] = jnp.exp2(self.s[slot]).astype(self.pe.dtype)
        self.s[1 - slot] = next_scores()
        self.acc[...] += _dot(v_prev, self.pe[1 - slot])


def _nsa_kernel(qT_ref, kc_ref, vcT_ref, ov_ref, ksl_ref, vslT_ref, kw_ref, vwT_ref, cb_ref, tp_ref,
                gl_ref, o_ref, q0_ref, qa_ref, qs_ref, rank_ref, kst_ref, vsa_ref, vwa_ref, s_ref, pe_ref,
                acc_ref, ws_ref, wpe_ref, wacc_ref, *, n_slc, top_k):
    T, R, HD = NSA_TQ, NSA_REP, NSA_HEAD_DIM
    nr = kc_ref.shape[0]
    per_tile = T // CMP_STRIDE
    front = nr - per_tile
    half = LANES // 2
    qi = pl.program_id(2)
    colmax, colsum = _colmax, _colsum

    @pl.when(qi == 0)
    def _():
        for src, dst in ((vslT_ref, vsa_ref), (vwT_ref, vwa_ref)):
            dst[0:HD, :] = src[...]
            dst[HD:, :] = jnp.ones((dst.shape[0] - HD, dst.shape[1]), dst.dtype)
        krow = lax.broadcasted_iota(jnp.int32, (T, LANES), 0)
        klane = lax.broadcasted_iota(jnp.int32, (T, LANES), 1)
        for t0 in range(0, ksl_ref.shape[0], T):
            blk = t0 // SLC_BLOCK + krow // SLC_BLOCK
            tag = jnp.where(klane - half == blk, 1.0, 0.0).astype(BF16)
            kst_ref[t0:t0 + T, :] = jnp.where(klane >= half, tag, ksl_ref[t0:t0 + T, :])

    qT = jnp.concatenate([qT_ref[r] for r in range(R)], axis=1)
    q0_ref[0:HD, :] = qT
    q0_ref[HD:2 * HD, :] = jnp.zeros((HD, R * T), BF16)
    qa_ref[0:HD, :] = qT

    n_win = WINDOW // T

    def win_offset(rel):
        return pl.multiple_of(jnp.maximum(qi - rel, 0) * T, T)

    def win_scores(rel):
        tile = TILE_EDGE if rel == n_win else (TILE_ZERO if rel >= 2 else (TILE_NEAR if rel == 1 else TILE_DIAG))
        if rel > 0:
            tile = jnp.where(qi >= rel, tile, TILE_NONE)
        return _dot(kw_ref[pl.ds(win_offset(rel), T), :], q0_ref[...]) + tp_ref[tile]

    def win_values(rel):
        return vwa_ref[:, pl.ds(win_offset(rel), T)]

    win = _SoftmaxPipe(ws_ref, wpe_ref, wacc_ref, HD)
    rels = list(range(n_win, -1, -1))
    carry = win.start(win_scores(rels[0]), 0)
    for i, rel in enumerate(rels):
        nxt = functools.partial(win_scores, rels[i + 1]) if i + 1 < len(rels) else None
        carry = win.step(i % 2, carry, win_values(rels[max(i - 1, 0)]), nxt)
    o_win = win.finish((len(rels) - 1) % 2, carry, win_values(0))

    off = pl.multiple_of(front - qi * per_tile, SUBLANES)
    lg = _dot(kc_ref[...], q0_ref[...]) + cb_ref[pl.ds(off, nr), :]
    m_c = colmax(lg)
    e = jnp.exp2(lg - m_c)
    p = e * jnp.where(m_c > 0.5 * NEG_INF, 1.0 / colsum(e), 0.0)
    o_cmp = _dot(vcT_ref[...], p.astype(BF16))
    sg = _sigmoid(gl_ref[...])
    o_partial = [sg[r:r + 1] * o_cmp[:, r * T:(r + 1) * T] + sg[2 * R + r:2 * R + r + 1] * o_win[:, r * T:(r + 1) * T]
                 for r in range(R)]

    psum = p[:, 0:T]
    for r in range(1, R):
        psum = psum + p[:, r * T:(r + 1) * T]
    imp = sum(_dot(ov_ref[...], pi) for pi in _split3(psum))
    nio = lax.broadcasted_iota(jnp.int32, (half, T), 0)
    tio = lax.broadcasted_iota(jnp.int32, (half, T), 1)
    tb = (qi * T + tio) // SLC_BLOCK
    forced = (nio == 0) | (nio == tb) | (nio == tb - 1)
    imp = jnp.where(forced, FORCE_SCORE, imp)
    imp = jnp.where(nio > tb, NEG_INF, imp)
    n_grp = half // SUBLANES
    grp = [imp[SUBLANES * v:SUBLANES * (v + 1)] for v in range(n_grp)]
    nio8 = lax.broadcasted_iota(jnp.int32, (SUBLANES, T), 0)
    rank_ref[...] = jnp.zeros(rank_ref.shape, F32)
    last_block = (qi * T + T - 1) // SLC_BLOCK
    for mg in range(pl.cdiv(n_slc, SUBLANES)):

        @pl.when(mg * SUBLANES <= last_block)
        def _():
            hits = [jnp.zeros((SUBLANES, T), F32) for _ in range(n_grp)]
            for mm in range(mg * SUBLANES, min((mg + 1) * SUBLANES, n_slc)):
                rowv = imp[mm:mm + 1, :]
                for v in range(n_grp):
                    if SUBLANES * v > mm:
                        hit = jnp.where(rowv >= grp[v], 1.0, 0.0)
                    elif SUBLANES * v + SUBLANES - 1 < mm:
                        hit = jnp.where(rowv > grp[v], 1.0, 0.0)
                    else:
                        hit = jnp.where(nio8 > mm - SUBLANES * v, jnp.where(rowv >= grp[v], 1.0, 0.0),
                                        jnp.where(rowv > grp[v], 1.0, 0.0))
                    hits[v] = hits[v] + hit
            rank_ref[...] += jnp.concatenate(hits, axis=0)

    rank = rank_ref[...]
    pen = jnp.where(rank < top_k, jnp.where(nio <= tb, 0.0, SEL_PENALTY), SEL_PENALTY).astype(BF16)
    qa_ref[HD:2 * HD, :] = jnp.concatenate([pen] * R, axis=1)

    def sel_scores(kj):
        return _dot(kst_ref[pl.ds(pl.multiple_of(kj * T, T), T), :], qa_ref[...])

    def sel_values(kj):
        return vsa_ref[:, pl.ds(pl.multiple_of(kj * T, T), T)]

    n_far = jnp.maximum(qi - 1, 0)
    near = jnp.maximum(qi - 1, 0)
    near_table = jnp.where(qi >= 1, TILE_NEAR, TILE_NONE)
    odd = n_far % 2
    sel = _SoftmaxPipe(s_ref, pe_ref, acc_ref, HD)

    def exact_selected():
        def far_step(kj, slot, carry):
            return sel.step(slot, carry, sel_values(jnp.maximum(kj - 1, 0)), functools.partial(sel_scores, kj + 1))

        carry = sel.start(sel_scores(0), odd)
        carry = lax.cond(odd == 1, lambda c: far_step(0, 1, c), lambda c: c, carry)
        carry = lax.fori_loop(
            0, n_far // 2, lambda j, c: far_step(odd + 2 * j + 1, 1, far_step(odd + 2 * j, 0, c)), carry)
        s_near = s_ref[0] + tp_ref[near_table]
        s_ref[0] = s_near
        carry = (colmax(s_near),) + tuple(carry[1:])
        carry = sel.step(0, carry, sel_values(jnp.maximum(n_far - 1, 0)),
                         lambda: sel_scores(qi) + tp_ref[TILE_DIAG])
        carry = sel.step(1, carry, sel_values(n_far), None)
        return sel.finish(1, carry, sel_values(qi))

    s_d = sel_scores(qi) + tp_ref[TILE_DIAG]
    shift = colmax(s_d).astype(BF16).astype(F32)
    qs_ref[0:HD, :] = qT
    qs_ref[HD:2 * HD, :] = (jnp.concatenate([pen.astype(F32)] * R, axis=1) - shift).astype(BF16)

    def shifted_scores(kj):
        return _dot(kst_ref[pl.ds(pl.multiple_of(kj * T, T), T), :], qs_ref[...])

    def far_shifted(kj, slot):
        sel.shifted_step(slot, sel_values(jnp.maximum(kj - 1, 0)), functools.partial(shifted_scores, kj + 1))

    s_ref[odd] = shifted_scores(0)
    pe_ref[1 - odd] = jnp.zeros(pe_ref.shape[1:], BF16)
    acc_ref[...] = _dot(sel_values(qi), jnp.exp2(s_d - shift).astype(BF16))

    @pl.when(odd == 1)
    def _():
        far_shifted(0, 1)

    def far_pair(j, carry):
        far_shifted(odd + 2 * j, 0)
        far_shifted(odd + 2 * j + 1, 1)
        return carry

    lax.fori_loop(0, n_far // 2, far_pair, 0)
    pe_near = jnp.exp2(s_ref[0] + tp_ref[near_table]).astype(BF16)
    acc = (acc_ref[...] + _dot(sel_values(jnp.maximum(n_far - 1, 0)), pe_ref[1])
           + _dot(sel_values(near), pe_near))
    denom = acc[HD:HD + 1]
    unsafe = (jnp.sum(jnp.where(denom < 2.0 ** 100, 0.0, 1.0))
              + jnp.sum(jnp.where(jnp.abs(acc[0:HD]) < 3.0e38, 0.0, 1.0))) > 0.0
    o_slc = lax.cond(unsafe, exact_selected, lambda: acc[0:HD] * (1.0 / denom))

    for r in range(R):
        o = o_partial[r] + sg[R + r:R + r + 1] * o_slc[:, r * T:(r + 1) * T]
        o_ref[r * HD:(r + 1) * HD, :] = o.astype(o_ref.dtype)


def _t5_bucket_np(dist):
    n = np.maximum(dist, 0)
    max_exact = REL_BUCKETS // 2
    nf = np.maximum(n, 1).astype(np.float32)
    large = max_exact + (np.log(nf / np.float32(max_exact)) / np.float32(math.log(REL_MAX_DIST / max_exact))
                         * np.float32(REL_BUCKETS - max_exact)).astype(np.int32)
    large = np.minimum(large, REL_BUCKETS - 1)
    return np.where(n < max_exact, n, large).astype(np.int32)


def _nsa_tables(rel_bias, nr):
    T, G, R = NSA_TQ, NSA_KV_HEADS, NSA_REP
    table = rel_bias.astype(F32) * LOG2E
    per_tile = T // CMP_STRIDE
    front = nr - per_tile
    t = np.arange(T)[None, :]

    def lay(a):
        keys = a.shape[0]
        return jnp.transpose(a, (2, 0, 1)).reshape(G, R, keys, T).transpose(0, 2, 1, 3).reshape(G, keys, R * T)

    cend = CMP_STRIDE * (np.arange(front + nr)[:, None] - front) + CMP_BLOCK - 1
    dist = t - cend
    def lookup(d):
        onehot = jax.nn.one_hot(_t5_bucket_np(d), REL_BUCKETS, dtype=F32)
        return jnp.einsum("ktb,bh->kth", onehot, table, precision=lax.Precision.HIGHEST)

    cb = jnp.where((dist >= 0)[..., None], lookup(dist), NEG_INF)
    k = np.arange(T)[:, None]
    assert 2 * T - (T - 1) >= REL_MAX_DIST and WINDOW % T == 0 and WINDOW // T >= 2
    last = table[REL_BUCKETS - 1]
    diag = jnp.where((t - k >= 0)[..., None], lookup(t - k) - last, NEG_INF)
    near = lookup(T + t - k) - last
    none = jnp.full_like(near, NEG_INF)
    edge = jnp.where((k > t)[..., None], jnp.zeros_like(near), NEG_INF)
    tiles = [None] * 5
    tiles[TILE_DIAG], tiles[TILE_NEAR], tiles[TILE_NONE], tiles[TILE_EDGE] = diag, near, none, edge
    tiles[TILE_ZERO] = jnp.zeros_like(near)
    tp = jnp.stack([lay(a) for a in tiles], axis=1)
    return lay(cb), tp


def _overlap_matrix(nr, n_slc):
    half = LANES // 2
    n = np.arange(half)[:, None]
    c = np.arange(nr)[None, :]
    ov = ((CMP_STRIDE * c <= SLC_BLOCK * n + SLC_BLOCK - 1) & (CMP_STRIDE * c + CMP_BLOCK - 1 >= SLC_BLOCK * n)
          & (n < n_slc) & (c < nr - 1))
    return jnp.asarray(ov.astype(np.float32), dtype=BF16)


def _nsa_attention(qT, kc, vcT, ksl, vslT, kw, vwT, cb, tp, glT):
    n_heads, hd, _ = qT.shape
    b, s, _ = ksl.shape
    G, R, T = NSA_KV_HEADS, NSA_REP, NSA_TQ
    nr = kc.shape[-2]
    nq = s // T
    n_slc = s // SLC_BLOCK
    assert n_slc <= LANES // 2 and s % T == 0 and nr % SUBLANES == 0
    ov = _overlap_matrix(nr, n_slc)
    kern = functools.partial(_nsa_kernel, n_slc=n_slc, top_k=min(SLC_TOPK, n_slc))
    k_spec = pl.BlockSpec((None, s, LANES), lambda i, j, k: (i, 0, j))
    vT_spec = pl.BlockSpec((None, hd, s), lambda i, j, k: (j, 0, i))
    ones_rows = 16
    pipe_scratch = [pltpu.VMEM((2, T, R * T), F32), pltpu.VMEM((2, T, R * T), BF16),
                    pltpu.VMEM((hd + ones_rows, R * T), F32)]
    value_scratch = [pltpu.VMEM((hd + ones_rows, s), BF16)] * 2
    return pl.pallas_call(
        kern, grid=(b, G, nq),
        in_specs=[
            pl.BlockSpec((R, hd, T), lambda i, j, k: (j, 0, i * nq + k)),
            pl.BlockSpec((None, None, nr, LANES), lambda i, j, k: (i, j, 0, 0)),
            pl.BlockSpec((None, None, hd, nr), lambda i, j, k: (i, j, 0, 0)),
            pl.BlockSpec(ov.shape, lambda i, j, k: (0, 0)),
            k_spec, vT_spec, k_spec, vT_spec,
            pl.BlockSpec((None,) + cb.shape[1:], lambda i, j, k: (j, 0, 0)),
            pl.BlockSpec((None,) + tp.shape[1:], lambda i, j, k: (j, 0, 0, 0)),
            pl.BlockSpec((None, 16, T), lambda i, j, k: (j, 0, i * nq + k)),
        ],
        out_specs=pl.BlockSpec((R * hd, T), lambda i, j, k: (j, i * nq + k)),
        out_shape=jax.ShapeDtypeStruct((n_heads * hd, b * s), BF16),
        scratch_shapes=[pltpu.VMEM((2 * hd, R * T), BF16)] * 3
        + [pltpu.VMEM((LANES // 2, T), F32), pltpu.VMEM((s, LANES), BF16)]
        + value_scratch + pipe_scratch + pipe_scratch,
        compiler_params=_cparams(3), name="nsa_attention",
    )(qT, kc, vcT, ov, ksl, vslT, kw, vwT, cb, tp, glT)


def _merge_kernel(x_ref, ys_ref, ynT_ref, mg_ref, ws_ref, wn_ref, wo_ref, g_ref, o_ref):
    d = x_ref.shape[1]
    a = _dot(ys_ref[...], ws_ref[...])
    bb = lax.dot_general(ynT_ref[...], wn_ref[...], (((0,), (0,)), ((), ())), preferred_element_type=F32)
    mg = _sigmoid(mg_ref[...].astype(F32))
    mixed = mg[:, :d] * a + mg[:, d:] * bb
    o = _dot(mixed.astype(BF16), wo_ref[...])
    o_ref[...] = x_ref[...] + _rms(o, g_ref[...])


def _merge(x2d, y_ssd, y_nsaT, mg, w_s, w_n, w_o, g, tm=512):
    m, d = x2d.shape
    tm = min(tm, m)
    row = lambda a: pl.BlockSpec((tm, a.shape[1]), lambda i: (i, 0))
    full = lambda a: pl.BlockSpec(a.shape, lambda i: (0, 0))
    g = g.reshape(1, d)
    return pl.pallas_call(
        _merge_kernel, grid=(m // tm,),
        in_specs=[row(x2d), row(y_ssd), pl.BlockSpec((y_nsaT.shape[0], tm), lambda i: (0, i)), row(mg),
                  full(w_s), full(w_n), full(w_o), full(g)],
        out_specs=pl.BlockSpec((tm, d), lambda i: (i, 0)),
        out_shape=jax.ShapeDtypeStruct((m, d), F32),
        compiler_params=_cparams(1), name="merge",
    )(x2d, y_ssd, y_nsaT, mg, w_s, w_n, w_o, g)


def _xattn_kernel(x_ref, kv_ref, gpre_ref, wq_ref, wo_ref, gpost_ref, o_ref):
    x = x_ref[...]
    h = _rms(x, gpre_ref[...]).astype(BF16)
    q = _dot(h, wq_ref[...])
    width = X_HEADS * X_HEAD_DIM
    scale = X_HEAD_DIM ** -0.5
    outs = []
    for hh in range(X_HEADS):
        sl = slice(hh * X_HEAD_DIM, (hh + 1) * X_HEAD_DIM)
        lg = _dot_nt(q[:, sl].astype(BF16), kv_ref[:, sl]) * scale
        e = jnp.exp(lg - jnp.max(lg, axis=-1, keepdims=True))
        p = e / jnp.sum(e, axis=-1, keepdims=True)
        outs.append(_dot(p.astype(BF16), kv_ref[:, width + hh * X_HEAD_DIM:width + (hh + 1) * X_HEAD_DIM]))
    o = _dot(jnp.concatenate(outs, axis=-1).astype(BF16), wo_ref[...])
    o_ref[...] = x + _rms(o, gpost_ref[...])


def _xattn(x3d, kv, g_pre, w_q, w_o, g_post, tm=512):
    b, s, d = x3d.shape
    tm = min(tm, s)
    ml = kv.shape[1]
    full = lambda a: pl.BlockSpec(a.shape, lambda i, j: (0, 0))
    g_pre, g_post = g_pre.reshape(1, d), g_post.reshape(1, d)
    return pl.pallas_call(
        _xattn_kernel, grid=(b, s // tm),
        in_specs=[pl.BlockSpec((None, tm, d), lambda i, j: (i, j, 0)),
                  pl.BlockSpec((None, ml, kv.shape[2]), lambda i, j: (i, 0, 0)),
                  full(g_pre), full(w_q), full(w_o), full(g_post)],
        out_specs=pl.BlockSpec((None, tm, d), lambda i, j: (i, j, 0)),
        out_shape=jax.ShapeDtypeStruct((b, s, d), F32),
        compiler_params=_cparams(2), name="xattn",
    )(x3d, kv, g_pre, w_q, w_o, g_post)


def _mlp_kernel(x_ref, gpre_ref, w1_ref, w2_ref, gpost_ref, o_ref, *, f_chunk):
    x = x_ref[...]
    h = _rms(x, gpre_ref[...]).astype(BF16)
    d_ff = w1_ref.shape[1]
    acc = jnp.zeros(x.shape, F32)
    for f0 in range(0, d_ff, f_chunk):
        u = jnp.maximum(_dot(h, w1_ref[:, f0:f0 + f_chunk]), 0.0)
        acc = acc + _dot((u * u).astype(BF16), w2_ref[f0:f0 + f_chunk, :])
    o_ref[...] = x + _rms(acc, gpost_ref[...])


def _mlp(x2d, g_pre, w1, w2, g_post, tm=512):
    m, d = x2d.shape
    tm = min(tm, m)
    full = lambda a: pl.BlockSpec(a.shape, lambda i: (0, 0))
    g_pre, g_post = g_pre.reshape(1, d), g_post.reshape(1, d)
    return pl.pallas_call(
        functools.partial(_mlp_kernel, f_chunk=1024), grid=(m // tm,),
        in_specs=[pl.BlockSpec((tm, d), lambda i: (i, 0)), full(g_pre), full(w1), full(w2), full(g_post)],
        out_specs=pl.BlockSpec((tm, d), lambda i: (i, 0)),
        out_shape=jax.ShapeDtypeStruct((m, d), F32),
        compiler_params=_cparams(1), name="mlp",
    )(x2d, g_pre, w1, w2, g_post)


def _xattn_mlp_kernel(x_ref, kv_ref, gxpre_ref, wq_ref, wo_ref, gxpost_ref, gfpre_ref, w1_ref, w2_ref,
                      gfpost_ref, o_ref, x2_ref, *, f_chunk):
    _xattn_kernel(x_ref, kv_ref, gxpre_ref, wq_ref, wo_ref, gxpost_ref, x2_ref)
    _mlp_kernel(x2_ref, gfpre_ref, w1_ref, w2_ref, gfpost_ref, o_ref, f_chunk=f_chunk)


def _xattn_mlp(x3d, kv, gx_pre, w_q, w_o, gx_post, gf_pre, w1, w2, gf_post, tm=512):
    b, s, d = x3d.shape
    tm = min(tm, s)
    ml = kv.shape[1]
    full = lambda a: pl.BlockSpec(a.shape, lambda i, j: (0, 0))
    vec = lambda g: g.reshape(1, d)
    gx_pre, gx_post, gf_pre, gf_post = vec(gx_pre), vec(gx_post), vec(gf_pre), vec(gf_post)
    return pl.pallas_call(
        functools.partial(_xattn_mlp_kernel, f_chunk=1024), grid=(b, s // tm),
        in_specs=[pl.BlockSpec((None, tm, d), lambda i, j: (i, j, 0)),
                  pl.BlockSpec((None, ml, kv.shape[2]), lambda i, j: (i, 0, 0)),
                  full(gx_pre), full(w_q), full(w_o), full(gx_post), full(gf_pre), full(w1), full(w2),
                  full(gf_post)],
        out_specs=pl.BlockSpec((None, tm, d), lambda i, j: (i, j, 0)),
        out_shape=jax.ShapeDtypeStruct((b, s, d), F32),
        scratch_shapes=[pltpu.VMEM((tm, d), F32)],
        compiler_params=_cparams(2), name="xattn_mlp",
    )(x3d, kv, gx_pre, w_q, w_o, gx_post, gf_pre, w1, w2, gf_post)


def _pad_heads(w, n_heads, hd):
    k = w.shape[0]
    w = w.reshape(k, n_heads, hd)
    return jnp.pad(w, ((0, 0), (0, 0), (0, LANES - hd))).reshape(k, n_heads * LANES)


def _layer(x, mem, w_in, conv_w, conv_b, dt_bias, a_log, d_skip, ssd_norm, cmp_pos, cmp_w1, cmp_w2,
           rel_bias, w_br_ssd, w_br_nsa, w_out, w_xq, w_xkv, w_xo, w_ff1, w_ff2,
           n_mix_pre, n_mix_post, n_x_pre, n_x_post, n_mem, n_ffn_pre, n_ffn_post):
    b, s, d = x.shape
    G, R, HD = NSA_KV_HEADS, NSA_REP, NSA_HEAD_DIM
    d_inner = 2 * d
    n_ssd_heads = d_inner // SSD_HEAD_DIM
    conv_dim = d_inner + 2 * SSD_GROUPS * SSD_STATE
    nsa_w, kv_w = NSA_HEADS * HD, G * HD
    sizes = (d_inner, conv_dim, n_ssd_heads, nsa_w, 6 * kv_w, 3 * NSA_HEADS, 2 * d)
    offs = np.concatenate([[0], np.cumsum(sizes)])
    seg = lambda i: w_in[:, offs[i]:offs[i + 1]]
    w_z, w_xbc, w_dt, w_q, w_kv, w_gate, w_mg = (seg(i) for i in range(7))
    w_small = jnp.pad(jnp.concatenate([w_dt, w_gate], axis=1),
                      ((0, 0), (0, LANES - n_ssd_heads - 3 * NSA_HEADS)))
    w_kv6 = w_kv.reshape(d, 6, kv_w)
    w_cmp = jnp.concatenate([w_kv6[:, 0], w_kv6[:, 1]], axis=1)
    bf = lambda a: a.astype(BF16)

    x2d = x.reshape(b * s, d)
    z, xbc = _norm_matmul(x2d, n_mix_pre, [bf(w_z), bf(w_xbc)], [BF16, BF16])
    w_gl = jnp.transpose(w_gate.reshape(d, 3, G, R), (2, 1, 3, 0)).reshape(G, 3 * R, d)
    w_gl = jnp.pad(w_gl, ((0, 0), (0, 16 - 3 * R), (0, 0))).reshape(G * 16, d)
    small, dtT, glT, qT, kv_cmp, vslT, vwT = _norm_matmul(
        x2d, n_mix_pre,
        [bf(w_small), bf(w_dt.T), bf(w_gl), bf(w_q.T * (HD ** -0.5 * LOG2E)), bf(w_cmp),
         bf(w_kv6[:, 3].T), bf(w_kv6[:, 5].T)],
        [F32, F32, F32, BF16, BF16, BF16, BF16],
        transposed=(False, True, True, True, False, True, True))
    ksl, kw, mg = _norm_matmul(
        x2d, n_mix_pre, [bf(_pad_heads(w_kv6[:, 2], G, HD)), bf(_pad_heads(w_kv6[:, 4], G, HD)), bf(w_mg)],
        [BF16] * 3)

    y_ssd = _ssd_mixer(z.reshape(b, s, -1), xbc.reshape(b, s, -1), small.reshape(b, s, -1), dtT,
                       conv_w, conv_b, dt_bias, a_log, d_skip, ssd_norm)

    nr = s // CMP_STRIDE
    kr = kv_cmp.reshape(b, nr, CMP_STRIDE, 2, G, HD)
    kr = jnp.transpose(kr, (3, 0, 4, 1, 2, 5)).reshape(2, b, G, nr, CMP_STRIDE * HD)
    pos = jnp.broadcast_to(bf(cmp_pos).reshape(2, 1, CMP_BLOCK * HD), (2, SUBLANES, CMP_BLOCK * HD))
    w2p = jnp.pad(bf(cmp_w2), ((0, 0), (0, 0), (0, LANES - HD)))
    kvc = _compress(kr, pos, bf(cmp_w1), w2p)
    vcT = jnp.swapaxes(kvc[1][..., :HD], -1, -2)
    cb, tp = _nsa_tables(rel_bias, nr)
    y_nsa = _nsa_attention(qT.reshape(NSA_HEADS, HD, b * s), kvc[0], vcT,
                           ksl.reshape(b, s, -1), vslT.reshape(G, HD, b * s),
                           kw.reshape(b, s, -1), vwT.reshape(G, HD, b * s), cb, tp,
                           glT.reshape(G, 16, b * s))

    x1 = _merge(x2d, y_ssd.reshape(b * s, -1), y_nsa, mg, bf(w_br_ssd), bf(w_br_nsa), bf(w_out), n_mix_post)

    (kv_mem,) = _norm_matmul(mem.reshape(-1, d), n_mem, [bf(w_xkv)], [BF16])
    return _xattn_mlp(x1.reshape(b, s, d), kv_mem.reshape(b, mem.shape[1], -1), n_x_pre, bf(w_xq), bf(w_xo),
                      n_x_post, n_ffn_pre, bf(w_ff1), bf(w_ff2), n_ffn_post)


def kernel(x, mem, w_in, ssd_conv_w, ssd_conv_b, ssd_dt_bias, ssd_a_log, ssd_d_skip, ssd_norm, cmp_pos,
           cmp_w1, cmp_w2, rel_bias, w_br_ssd, w_br_nsa, w_out, w_xq, w_xkv, w_xo, w_ff1, w_ff2,
           norm_mix_pre, norm_mix_post, norm_x_pre, norm_x_post, norm_mem, norm_ffn_pre, norm_ffn_post):
    for l in range(w_in.shape[0]):
        x = _layer(x, mem, w_in[l], ssd_conv_w[l], ssd_conv_b[l], ssd_dt_bias[l], ssd_a_log[l],
                   ssd_d_skip[l], ssd_norm[l], cmp_pos[l], cmp_w1[l], cmp_w2[l], rel_bias,
                   w_br_ssd[l], w_br_nsa[l], w_out[l], w_xq[l], w_xkv[l], w_xo[l], w_ff1[l], w_ff2[l],
                   norm_mix_pre[l], norm_mix_post[l], norm_x_pre[l], norm_x_post[l], norm_mem[l],
                   norm_ffn_pre[l], norm_ffn_post[l])
    return x
```
